```python
import jax, jax.numpy as jnp
from jax import lax
import numpy as np

D_MODEL = 1024
BATCH = 16
SEQ = 256
DEPTH = 1
DEC_BATCH = 4
DEC_SEQ = 4096
PAST_LEN = 256

GRID_W = 64
HEAD_DIM = 64
N_HEADS = 12
N_KV_HEADS = 4
GQA_GROUP = N_HEADS // N_KV_HEADS
ATTN_DIM = N_HEADS * HEAD_DIM
KV_DIM = N_KV_HEADS * HEAD_DIM
F_GROUPS = 4
F_HD = 64
F_DIM = F_GROUPS * F_HD
MIX_DIM = F_DIM + ATTN_DIM
IN_DIM = F_DIM + ATTN_DIM + 2 * KV_DIM
WINDOW = 128
Q_BLOCK = 128
ROPE_THETA = 10000.0
N_EXPERTS = 64
TOP_K = 8
N_EXPERT_GROUPS = 8
TOPK_GROUPS = 4
EXPERT_DIM = 256
SHARED_DIM = 256
ROUTED_SCALE = 2.5
MOE_BLOCK = 128
EPS = 1e-6

kernel_name = 'hybrid_fourier_swa_moe_diffusion_step'


def rms_norm(x, g):
    x32 = x.astype(jnp.float32)
    y = x32 * lax.rsqrt(jnp.mean(x32 * x32, axis=-1, keepdims=True) + EPS)
    return (y * g.astype(jnp.float32)).astype(x.dtype)


def modulation(c, w_mod, b_mod):
    m = jax.nn.silu(c) @ w_mod + b_mod
    return jnp.split(m, 6, axis=-1)


def modulate(x, shift, scale):
    return x * (1 + scale) + shift


def project(x, g, shift, scale, w_in):
    B, T = x.shape[:2]
    p = modulate(rms_norm(x, g), shift, scale) @ w_in
    u = p[..., :F_DIM].reshape(B, T, F_GROUPS, F_HD)
    q = p[..., F_DIM:F_DIM + ATTN_DIM].reshape(B, T, N_HEADS, HEAD_DIM)
    k = p[..., F_DIM + ATTN_DIM:F_DIM + ATTN_DIM + KV_DIM].reshape(B, T, N_KV_HEADS, HEAD_DIM)
    v = p[..., F_DIM + ATTN_DIM + KV_DIM:].reshape(B, T, N_KV_HEADS, HEAD_DIM)
    return u, q, k, v


def fourier_mixer(u, w_fourier):
    B, T = u.shape[:2]
    f = jnp.fft.fft2(u.astype(jnp.float32), axes=(1, 3), norm='ortho').real.astype(u.dtype)
    return jnp.einsum('btgc,gcd->btgd', f, w_fourier).reshape(B, T, F_DIM)


def axial_rope(x):
    T = x.shape[1]
    rows = T // GRID_W
    row = jnp.repeat(jnp.arange(rows), GRID_W)
    col = jnp.tile(jnp.arange(GRID_W), rows)
    n_freq = HEAD_DIM // 4
    inv = ROPE_THETA ** (-jnp.arange(n_freq, dtype=jnp.float32) / n_freq)
    half = HEAD_DIM // 2

    def rot(xa, pos):
        ang = pos.astype(jnp.float32)[:, None] * inv[None, :]
        c = jnp.cos(ang)[None, :, None, :]
        s = jnp.sin(ang)[None, :, None, :]
        x1, x2 = jnp.split(xa.astype(jnp.float32), 2, axis=-1)
        return jnp.concatenate([x1 * c - x2 * s, x2 * c + x1 * s], axis=-1)

    return jnp.concatenate([rot(x[..., :half], row), rot(x[..., half:], col)], axis=-1).astype(x.dtype)


def to_blocks(q):
    B, T = q.shape[:2]
    nb = T // Q_BLOCK
    return q.reshape(B, nb, Q_BLOCK, N_KV_HEADS, GQA_GROUP, HEAD_DIM).transpose(1, 0, 2, 3, 4, 5)


def from_blocks(o):
    nb, B = o.shape[:2]
    return o.transpose(1, 0, 2, 3, 4, 5).reshape(B, nb * Q_BLOCK, ATTN_DIM)


def sink_logits(sink, B):
    s = sink.astype(jnp.float32).reshape(N_KV_HEADS, GQA_GROUP)[None, :, :, None, None]
    return jnp.broadcast_to(s, (B, N_KV_HEADS, GQA_GROUP, Q_BLOCK, 1))


def context_attention(q, k, v, sink):
    B, S = q.shape[:2]
    scale = HEAD_DIM ** -0.5
    sink_b = sink_logits(sink, B)

    def block(qblk):
        s = jnp.einsum('bqkgd,bjkd->bkgqj', qblk, k).astype(jnp.float32) * scale
        prob = jax.nn.softmax(jnp.concatenate([s, sink_b], axis=-1), axis=-1).astype(v.dtype)
        return jnp.einsum('bkgqj,bjkd->bqkgd', prob[..., :S], v)

    return from_blocks(lax.map(block, to_blocks(q)))


def latent_attention(q, k, v, k_ctx, v_ctx, sink):
    B, T = q.shape[:2]
    C = k_ctx.shape[1]
    nb = T // Q_BLOCK
    kw_len = Q_BLOCK + 2 * WINDOW
    pad = ((0, 0), (WINDOW, WINDOW), (0, 0), (0, 0))
    kp = jnp.pad(k, pad)
    vp = jnp.pad(v, pad)
    scale = HEAD_DIM ** -0.5
    sink_b = sink_logits(sink, B)
    a = jnp.arange(Q_BLOCK)
    j = jnp.arange(kw_len)

    def block(args):
        i, qblk = args
        start = i * Q_BLOCK
        kw = lax.dynamic_slice_in_dim(kp, start, kw_len, axis=1)
        vw = lax.dynamic_slice_in_dim(vp, start, kw_len, axis=1)
        kpos = start - WINDOW + j
        qpos = start + a
        valid = (kpos >= 0)[None, :] & (kpos < T)[None, :] & (jnp.abs(kpos[None, :] - qpos[:, None]) <= WINDOW)
        s_w = jnp.einsum('bqkgd,bjkd->bkgqj', qblk, kw).astype(jnp.float32) * scale
        s_w = jnp.where(valid[None, None, None], s_w, -jnp.inf)
        s_c = jnp.einsum('bqkgd,bjkd->bkgqj', qblk, k_ctx).astype(jnp.float32) * scale
        prob = jax.nn.softmax(jnp.concatenate([s_w, s_c, sink_b], axis=-1), axis=-1).astype(v.dtype)
        o_w = jnp.einsum('bkgqj,bjkd->bqkgd', prob[..., :kw_len], vw)
        o_c = jnp.einsum('bkgqj,bjkd->bqkgd', prob[..., kw_len:kw_len + C], v_ctx)
        return o_w + o_c

    return from_blocks(lax.map(block, (jnp.arange(nb), to_blocks(q))))


def route(xf, w_router, router_bias):
    N = xf.shape[0]
    s = jax.nn.sigmoid((xf @ w_router).astype(jnp.float32))
    sc = s + router_bias.astype(jnp.float32)
    per_group = N_EXPERTS // N_EXPERT_GROUPS
    gscore = lax.top_k(sc.reshape(N, N_EXPERT_GROUPS, per_group), 2)[0].sum(-1)
    _, gidx = lax.top_k(gscore, TOPK_GROUPS)
    gmask = jax.nn.one_hot(gidx, N_EXPERT_GROUPS, dtype=jnp.float32).sum(1) > 0
    emask = jnp.repeat(gmask, per_group, axis=1)
    _, idx = lax.top_k(jnp.where(emask, sc, -jnp.inf), TOP_K)
    w = jnp.take_along_axis(s, idx, axis=-1)
    w = w / jnp.sum(w, axis=-1, keepdims=True) * ROUTED_SCALE
    return idx, w


def moe_ffn(h, w_router, router_bias, w_gate, w_up, w_down, ws_gate, ws_up, ws_down):
    B, T, D = h.shape
    xf = h.reshape(-1, D)
    N = xf.shape[0]
    idx, gates = route(xf, w_router, router_bias)
    NK = N * TOP_K
    flat_e = idx.reshape(-1)
    flat_tok = jnp.repeat(jnp.arange(N, dtype=jnp.int32), TOP_K)
    flat_g = gates.reshape(-1)
    order = jnp.argsort(flat_e)
    sorted_e = flat_e[order]
    counts = jnp.bincount(flat_e, length=N_EXPERTS)
    starts = jnp.cumsum(counts) - counts
    padded = (counts + MOE_BLOCK - 1) // MOE_BLOCK * MOE_BLOCK
    pad_ends = jnp.cumsum(padded)
    pad_starts = pad_ends - padded
    dest = pad_starts[sorted_e] + jnp.arange(NK) - starts[sorted_e]
    n_blocks = -(-NK // MOE_BLOCK) + N_EXPERTS
    R = n_blocks * MOE_BLOCK
    row_tok = jnp.full((R,), N, jnp.int32).at[dest].set(flat_tok[order])
    row_gate = jnp.zeros((R,), jnp.float32).at[dest].set(flat_g[order])
    block_e = jnp.minimum(jnp.searchsorted(pad_ends, jnp.arange(n_blocks) * MOE_BLOCK, side='right'), N_EXPERTS - 1)
    xpad = jnp.concatenate([xf, jnp.zeros((1, D), xf.dtype)], axis=0)

    def expert_block(args):
        tok, e = args
        xb = xpad[tok]
        hb = jax.nn.silu(xb @ w_gate[e]) * (xb @ w_up[e])
        return hb @ w_down[e]

    rows = lax.map(expert_block, (row_tok.reshape(n_blocks, MOE_BLOCK), block_e))
    routed = jax.ops.segment_sum(rows.reshape(R, D) * row_gate[:, None].astype(xf.dtype), row_tok, num_segments=N + 1)[:N]
    shared = (jax.nn.silu(xf @ ws_gate) * (xf @ ws_up)) @ ws_down
    return (routed + shared).reshape(B, T, D)


def ffn_sublayer(x, shift, scale, gate, p):
    h = modulate(rms_norm(x, p['norm2']), shift, scale)
    return x + gate * moe_ffn(h, p['w_router'], p['router_bias'], p['w_gate'], p['w_up'], p['w_down'],
                              p['ws_gate'], p['ws_up'], p['ws_down'])


def context_layer(x, c_ctx, p):
    sh1, sc1, g1, sh2, sc2, g2 = modulation(c_ctx[None, None, :], p['w_mod'], p['b_mod'])
    u, q, k, v = project(x, p['norm1'], sh1, sc1, p['w_in'])
    mix = jnp.concatenate([fourier_mixer(u, p['w_fourier']), context_attention(q, k, v, p['sink'])], axis=-1)
    x = x + g1 * (mix @ p['w_out'])
    return ffn_sublayer(x, sh2, sc2, g2, p), k, v


def latent_layer(x, c, k_ctx, v_ctx, p):
    sh1, sc1, g1, sh2, sc2, g2 = modulation(c[:, None, :], p['w_mod'], p['b_mod'])
    u, q, k, v = project(x, p['norm1'], sh1, sc1, p['w_in'])
    q = axial_rope(q)
    k = axial_rope(k)
    mix = jnp.concatenate([fourier_mixer(u, p['w_fourier']),
                           latent_attention(q, k, v, k_ctx, v_ctx, p['sink'])], axis=-1)
    x = x + g1 * (mix @ p['w_out'])
    return ffn_sublayer(x, sh2, sc2, g2, p)


def setup_inputs(seed: int = 0) -> dict:
    key = jax.random.key(seed)
    ks = jax.random.split(key, 24)
    f32 = jnp.float32

    def nrm(k, shape, scale):
        return jax.random.normal(k, shape, f32) * scale

    D = D_MODEL
    return {
        'x_prompt': nrm(ks[0], (BATCH, SEQ, D), 1.0),
        'x_sample': nrm(ks[1], (DEC_BATCH, DEC_SEQ, D), 1.0),
        'cache_k': nrm(ks[2], (DEC_BATCH, DEPTH, PAST_LEN, N_KV_HEADS, HEAD_DIM), 1.0),
        'cache_v': nrm(ks[3], (DEC_BATCH, DEPTH, PAST_LEN, N_KV_HEADS, HEAD_DIM), 1.0),
        'c': nrm(ks[4], (DEC_BATCH, D), 1.0),
        'c_ctx': nrm(ks[5], (D,), 1.0),
        'w_mod': nrm(ks[6], (DEPTH, D, 6 * D), 0.5 * D ** -0.5),
        'b_mod': nrm(ks[7], (DEPTH, 6 * D), 0.02),
        'norm1': 1.0 + nrm(ks[8], (DEPTH, D), 0.05),
        'w_in': nrm(ks[9], (DEPTH, D, IN_DIM), D ** -0.5),
        'w_fourier': nrm(ks[10], (DEPTH, F_GROUPS, F_HD, F_HD), F_HD ** -0.5),
        'sink': nrm(ks[11], (DEPTH, N_HEADS), 0.5),
        'w_out': nrm(ks[12], (DEPTH, MIX_DIM, D), MIX_DIM ** -0.5),
        'norm2': 1.0 + nrm(ks[13], (DEPTH, D), 0.05),
        'w_router': nrm(ks[14], (DEPTH, D, N_EXPERTS), D ** -0.5),
        'router_bias': nrm(ks[15], (DEPTH, N_EXPERTS), 0.01),
        'w_gate': nrm(ks[16], (DEPTH, N_EXPERTS, D, EXPERT_DIM), D ** -0.5),
        'w_up': nrm(ks[17], (DEPTH, N_EXPERTS, D, EXPERT_DIM), D ** -0.5),
        'w_down': nrm(ks[18], (DEPTH, N_EXPERTS, EXPERT_DIM, D), EXPERT_DIM ** -0.5),
        'ws_gate': nrm(ks[19], (DEPTH, D, SHARED_DIM), D ** -0.5),
        'ws_up': nrm(ks[20], (DEPTH, D, SHARED_DIM), D ** -0.5),
        'ws_down': nrm(ks[21], (DEPTH, SHARED_DIM, D), SHARED_DIM ** -0.5),
        'norm_f': 1.0 + nrm(ks[22], (D,), 0.05),
    }


def reference(x_prompt, x_sample, cache_k, cache_v, c, c_ctx, w_mod, b_mod, norm1, w_in, w_fourier, sink,
              w_out, norm2, w_router, router_bias, w_gate, w_up, w_down, ws_gate, ws_up, ws_down, norm_f):
    xp = x_prompt
    xs = x_sample
    new_ks = []
    new_vs = []
    for l in range(DEPTH):
        p = {'w_mod': w_mod[l], 'b_mod': b_mod[l], 'norm1': norm1[l], 'w_in': w_in[l],
             'w_fourier': w_fourier[l], 'sink': sink[l], 'w_out': w_out[l], 'norm2': norm2[l],
             'w_router': w_router[l], 'router_bias': router_bias[l], 'w_gate': w_gate[l],
             'w_up': w_up[l], 'w_down': w_down[l], 'ws_gate': ws_gate[l], 'ws_up': ws_up[l],
             'ws_down': ws_down[l]}
        xp, k_l, v_l = context_layer(xp, c_ctx, p)
        new_ks.append(k_l)
        new_vs.append(v_l)
        xs = latent_layer(xs, c, cache_k[:, l], cache_v[:, l], p)
    y_prompt = rms_norm(xp, norm_f)
    y_sample = rms_norm(xs, norm_f)
    new_k = jnp.stack(new_ks, axis=1)
    new_v = jnp.stack(new_vs, axis=1)
    return (y_prompt, y_sample, new_k, new_v)
```

```python
import functools

import numpy as np
import jax
import jax.numpy as jnp
from jax import lax
from jax.experimental import pallas as pl
from jax.experimental.pallas import tpu as pltpu

F32 = jnp.float32
BF16 = jnp.bfloat16

D_MODEL = 1024
GRID_W = 64
HEAD_DIM = 64
N_HEADS = 12
N_KV_HEADS = 4
GQA_GROUP = N_HEADS // N_KV_HEADS
ATTN_DIM = N_HEADS * HEAD_DIM
KV_DIM = N_KV_HEADS * HEAD_DIM
F_GROUPS = 4
F_HD = 64
F_DIM = F_GROUPS * F_HD
IN_DIM = F_DIM + ATTN_DIM + 2 * KV_DIM
WINDOW = 128
Q_BLOCK = 128
ROPE_THETA = 10000.0
N_EXPERTS = 64
TOP_K = 8
N_EXPERT_GROUPS = 8
TOPK_GROUPS = 4
EXPERT_DIM = 256
SHARED_DIM = 256
ROUTED_SCALE = 2.5
EPS = 1e-6

LANES = 128
MOD_ROWS = 8
VMEM_LIMIT = 56 * 1024 * 1024


def _cparams(sem):
    return pltpu.CompilerParams(dimension_semantics=sem, vmem_limit_bytes=VMEM_LIMIT)


def _bdot(a, b):
    return jnp.dot(a.astype(BF16), b.astype(BF16), preferred_element_type=F32)


def _rms(x, g):
    return x * lax.rsqrt(jnp.mean(x * x, axis=-1, keepdims=True) + EPS) * g


def _mod_kernel(c_ref, w_ref, b_ref, o_ref):
    c = c_ref[...]
    a = c * jax.nn.sigmoid(c)
    o_ref[...] = _bdot(a, w_ref[...]) + b_ref[...]


def _modulation(cvec, w_mod, b_mod):
    n = w_mod.shape[1]
    tn = 1024
    return pl.pallas_call(
        _mod_kernel,
        grid=(n // tn,),
        in_specs=[pl.BlockSpec((MOD_ROWS, D_MODEL), lambda j: (0, 0)),
                  pl.BlockSpec((D_MODEL, tn), lambda j: (0, j)),
                  pl.BlockSpec((1, tn), lambda j: (0, j))],
        out_specs=pl.BlockSpec((MOD_ROWS, tn), lambda j: (0, j)),
        out_shape=jax.ShapeDtypeStruct((MOD_ROWS, n), F32),
        compiler_params=_cparams(("parallel",)),
        name="modulation",
    )(cvec, w_mod, b_mod.reshape(1, n))


def _mod_spec(piece, row_fn):
    return pl.BlockSpec((None, None, 1, D_MODEL), lambda *idx: (row_fn(*idx), piece, 0, 0))


def _rope(x, cos, sin_up, sin_dn):
    outs = []
    for j in range(x.shape[1] // LANES):
        xj = x[:, j * LANES:(j + 1) * LANES]
        up = pltpu.roll(xj, LANES - 16, axis=1)
        dn = pltpu.roll(xj, 16, axis=1)
        outs.append(xj * cos + up * sin_up + dn * sin_dn)
    return jnp.concatenate(outs, axis=1)


def _proj_kernel(*refs, latent):
    if latent:
        (x_ref, sh_ref, sc_ref, g_ref, w_ref, dft_ref, cos_ref, sup_ref, sdn_ref,
         wr_ref, wi_ref, q_ref, k_ref, v_ref) = refs
    else:
        (x_ref, sh_ref, sc_ref, g_ref, w_ref, dft_ref,
         wr_ref, wi_ref, q_ref, k_ref, v_ref, kf_ref, vf_ref) = refs
    x = x_ref[...]
    h = _rms(x, g_ref[...]) * (1.0 + sc_ref[...]) + sh_ref[...]
    p = _bdot(h, w_ref[...])
    u = p[:, :F_DIM]
    q = p[:, F_DIM:F_DIM + ATTN_DIM]
    k = p[:, F_DIM + ATTN_DIM:F_DIM + ATTN_DIM + KV_DIM]
    v = p[:, F_DIM + ATTN_DIM + KV_DIM:]
    w = _bdot(u, dft_ref[...])
    wr_ref[...] = w[:, :F_DIM]
    wi_ref[...] = w[:, F_DIM:]
    if latent:
        cos, sup, sdn = cos_ref[...], sup_ref[...], sdn_ref[...]
        q = _rope(q, cos, sup, sdn)
        k = _rope(k, cos, sup, sdn)
    else:
        kf_ref[...] = k
        vf_ref[...] = v
    q_ref[...] = (q * (HEAD_DIM ** -0.5)).astype(BF16)
    k_ref[...] = k.astype(BF16)
    v_ref[...] = v.astype(BF16)


def _project(x2d, mod, norm1, w_in, dft_c, rope_tabs, tm, tiles_per_batch, row_fn):
    n = x2d.shape[0]
    latent = rope_tabs is not None
    tok = lambda w: pl.BlockSpec((tm, w), lambda i: (i, 0))
    full = lambda a: pl.BlockSpec(a.shape, lambda i: (0,) * a.ndim)
    in_specs = [tok(D_MODEL), _mod_spec(0, row_fn), _mod_spec(1, row_fn), full(norm1), full(w_in), full(dft_c)]
    args = [x2d, mod, mod, norm1, w_in, dft_c]
    out_specs = [tok(F_DIM), tok(F_DIM), tok(ATTN_DIM), tok(KV_DIM), tok(KV_DIM)]
    out_shape = [jax.ShapeDtypeStruct((n, F_DIM), F32), jax.ShapeDtypeStruct((n, F_DIM), F32),
                 jax.ShapeDtypeStruct((n, ATTN_DIM), BF16), jax.ShapeDtypeStruct((n, KV_DIM), BF16),
                 jax.ShapeDtypeStruct((n, KV_DIM), BF16)]
    if latent:
        pos = pl.BlockSpec((tm, LANES), lambda i: (i % tiles_per_batch, 0))
        in_specs += [pos, pos, pos]
        args += list(rope_tabs)
    else:
        out_specs += [tok(KV_DIM), tok(KV_DIM)]
        out_shape += [jax.ShapeDtypeStruct((n, KV_DIM), F32), jax.ShapeDtypeStruct((n, KV_DIM), F32)]
    return pl.pallas_call(
        functools.partial(_proj_kernel, latent=latent),
        grid=(n // tm,),
        in_specs=in_specs, out_specs=out_specs, out_shape=out_shape,
        compiler_params=_cparams(("parallel",)),
        name="project_latent" if latent else "project_context",
    )(*args)


def _fourier_kernel(wr_ref, wi_ref, a_ref, b_ref, wf_ref, o_ref, yr_ref, yi_ref, z_ref, *, t1, t2, scale):
    for j in range(t2):
        rows = pl.ds(j, t1, stride=t2)
        xin = jnp.concatenate([wr_ref[rows, :], wi_ref[rows, :]], axis=0)
        y = _bdot(a_ref[j], xin)
        yr_ref[rows, :] = y[:t1]
        yi_ref[rows, :] = y[t1:]
    bm = b_ref[...]
    for k1 in range(t1):
        rows = pl.ds(k1 * t2, t2)
        yin = jnp.concatenate([yr_ref[rows, :], yi_ref[rows, :]], axis=0)
        z_ref[pl.ds(k1, t2, stride=t1), :] = _bdot(bm, yin)
    o_ref[...] = (_bdot(z_ref[...], wf_ref[...]) * scale).astype(BF16)


def _dft_tables(t1, t2):
    t = t1 * t2
    k1 = np.arange(t1)[None, :, None]
    pos = (t2 * np.arange(t1)[None, None, :] + np.arange(t2)[:, None, None])
    ang = 2.0 * np.pi * ((k1 * pos) % t) / t
    c, s = np.cos(ang), np.sin(ang)
    a = np.concatenate([np.concatenate([c, s], axis=2), np.concatenate([-s, c], axis=2)], axis=1)
    ang2 = 2.0 * np.pi * ((np.arange(t2)[:, None] * np.arange(t2)[None, :]) % t2) / t2
    b = np.concatenate([np.cos(ang2), np.sin(ang2)], axis=1)
    return a.astype(np.float32), b.astype(np.float32)


def _fourier(wr, wi, wf_bd, batch, t, t1, t2):
    a_np, b_np = _dft_tables(t1, t2)
    a = jnp.asarray(a_np).astype(BF16)
    b = jnp.asarray(b_np).astype(BF16)
    cw = LANES
    blk = pl.BlockSpec((None, t, cw), lambda bi, ci: (bi, 0, ci))
    return pl.pallas_call(
        functools.partial(_fourier_kernel, t1=t1, t2=t2, scale=float((t * F_HD) ** -0.5)),
        grid=(batch, F_DIM // cw),
        in_specs=[blk, blk,
                  pl.BlockSpec(a.shape, lambda bi, ci: (0, 0, 0)),
                  pl.BlockSpec(b.shape, lambda bi, ci: (0, 0)),
                  pl.BlockSpec((None, cw, cw), lambda bi, ci: (ci, 0, 0))],
        out_specs=blk,
        out_shape=jax.ShapeDtypeStruct((batch, t, F_DIM), BF16),
        scratch_shapes=[pltpu.VMEM((t, cw), F32), pltpu.VMEM((t, cw), F32), pltpu.VMEM((t, cw), F32)],
        compiler_params=_cparams(("parallel", "parallel")),
        name="fourier_%d" % t,
    )(wr.reshape(batch, t, F_DIM), wi.reshape(batch, t, F_DIM), a, b, wf_bd)


def _attn_kernel(*refs, windowed, n_blocks):
    if windowed:
        q_ref, kp_ref, kc_ref, kn_ref, vp_ref, vc_ref, vn_ref, kx_ref, vx_ref, sink_ref, o_ref = refs
    else:
        q_ref, kx_ref, vx_ref, sink_ref, o_ref = refs
    q = q_ref[...]
    n_ctx = kx_ref.shape[0]
    if windowed:
        i = pl.program_id(1)
        a = lax.broadcasted_iota(jnp.int32, (Q_BLOCK, Q_BLOCK), 0)
        j = lax.broadcasted_iota(jnp.int32, (Q_BLOCK, Q_BLOCK), 1)
        prev_ok = (j >= a) & (i > 0)
        next_ok = (j <= a) & (i < n_blocks - 1)
        true_w = jnp.ones((Q_BLOCK, Q_BLOCK), jnp.bool_)
        true_c = jnp.ones((Q_BLOCK, n_ctx), jnp.bool_)
        mask1 = jnp.concatenate([prev_ok, true_w, next_ok, true_c], axis=1)
        mask = jnp.concatenate([mask1] * GQA_GROUP, axis=0)
    outs = []
    for kv in range(N_KV_HEADS):
        hs = slice(kv * HEAD_DIM, (kv + 1) * HEAD_DIM)
        qs = jnp.concatenate(
            [q[:, (kv * GQA_GROUP + g) * HEAD_DIM:(kv * GQA_GROUP + g + 1) * HEAD_DIM] for g in range(GQA_GROUP)],
            axis=0)
        if windowed:
            kcat = jnp.concatenate([kp_ref[:, hs], kc_ref[:, hs], kn_ref[:, hs], kx_ref[:, hs]], axis=0)
            vcat = jnp.concatenate([vp_ref[:, hs], vc_ref[:, hs], vn_ref[:, hs], vx_ref[:, hs]], axis=0)
        else:
            kcat = kx_ref[:, hs]
            vcat = vx_ref[:, hs]
        s = lax.dot_general(qs, kcat, (((1,), (1,)), ((), ())), preferred_element_type=F32)
        if windowed:
            s = jnp.where(mask, s, -jnp.inf)
        sk = sink_ref[kv]
        m = jnp.maximum(jnp.max(s, axis=1, keepdims=True), sk)
        p = jnp.exp(s - m)
        den = jnp.sum(p, axis=1, keepdims=True) + jnp.exp(sk - m)
        o = jnp.dot(p.astype(BF16), vcat, preferred_element_type=F32) / den
        for g in range(GQA_GROUP):
            outs.append(o[g * Q_BLOCK:(g + 1) * Q_BLOCK])
    o_ref[...] = jnp.concatenate(outs, axis=1).astype(BF16)


def _attention(q, k, v, kx, vx, sink_col, batch, t, windowed):
    nb = t // Q_BLOCK
    n_ctx = kx.shape[1]
    qspec = pl.BlockSpec((None, Q_BLOCK, ATTN_DIM), lambda b, i: (b, i, 0))
    xspec = pl.BlockSpec((None, n_ctx, KV_DIM), lambda b, i: (b, 0, 0))
    sspec = pl.BlockSpec(sink_col.shape, lambda b, i: (0, 0, 0))
    if windowed:
        prev = pl.BlockSpec((None, Q_BLOCK, KV_DIM), lambda b, i: (b, jnp.maximum(i - 1, 0), 0))
        cur = pl.BlockSpec((None, Q_BLOCK, KV_DIM), lambda b, i: (b, i, 0))
        nxt = pl.BlockSpec((None, Q_BLOCK, KV_DIM), lambda b, i: (b, jnp.minimum(i + 1, nb - 1), 0))
        in_specs = [qspec, prev, cur, nxt, prev, cur, nxt, xspec, xspec, sspec]
        args = [q, k, k, k, v, v, v, kx, vx, sink_col]
    else:
        in_specs = [qspec, xspec, xspec, sspec]
        args = [q, kx, vx, sink_col]
    return pl.pallas_call(
        functools.partial(_attn_kernel, windowed=windowed, n_blocks=nb),
        grid=(batch, nb),
        in_specs=in_specs, out_specs=qspec,
        out_shape=jax.ShapeDtypeStruct((batch, t, ATTN_DIM), BF16),
        compiler_params=_cparams(("parallel", "parallel")),
        name="attention_latent" if windowed else "attention_context",
    )(*args)


def _outproj_kernel(x_ref, mf_ref, ma_ref, wof_ref, woa_ref, g1_ref, sh_ref, sc_ref, n2_ref, wr_ref,
                    x1_ref, h_ref, lg_ref):
    o = jnp.dot(mf_ref[...], wof_ref[...], preferred_element_type=F32)
    o = o + jnp.dot(ma_ref[...], woa_ref[...], preferred_element_type=F32)
    x1 = x_ref[...] + g1_ref[...] * o
    x1_ref[...] = x1
    h = (_rms(x1, n2_ref[...]) * (1.0 + sc_ref[...]) + sh_ref[...]).astype(BF16)
    h_ref[...] = h
    lg_ref[...] = lax.dot_general(wr_ref[...], h, (((1,), (1,)), ((), ())), preferred_element_type=F32)


def _outproj(x2d, mixf, mixa, wo_f, wo_a, mod, norm2, w_router_t, tm, row_fn):
    n = x2d.shape[0]
    tok = lambda w: pl.BlockSpec((tm, w), lambda i: (i, 0))
    full = lambda a: pl.BlockSpec(a.shape, lambda i: (0,) * a.ndim)
    return pl.pallas_call(
        _outproj_kernel,
        grid=(n // tm,),
        in_specs=[tok(D_MODEL), tok(F_DIM), tok(ATTN_DIM), full(wo_f), full(wo_a),
                  _mod_spec(2, row_fn), _mod_spec(3, row_fn), _mod_spec(4, row_fn), full(norm2), full(w_router_t)],
        out_specs=[tok(D_MODEL), tok(D_MODEL), pl.BlockSpec((N_EXPERTS, tm), lambda i: (0, i))],
        out_shape=[jax.ShapeDtypeStruct((n, D_MODEL), F32), jax.ShapeDtypeStruct((n, D_MODEL), BF16),
                   jax.ShapeDtypeStruct((N_EXPERTS, n), F32)],
        compiler_params=_cparams(("parallel",)),
        name="outproj",
    )(x2d, mixf, mixa, wo_f, wo_a, mod, mod, mod, norm2, w_router_t)


def _rank_before(vals, n):
    idx = lax.broadcasted_iota(jnp.int32, vals.shape, 0)
    cnt = jnp.zeros(vals.shape, jnp.int32)
    for r in range(n):
        row = vals[r:r + 1, :]
        ge = jnp.where(row >= vals, 1, 0)
        gt = jnp.where(row > vals, 1, 0)
        cnt = cnt + jnp.where(idx > r, ge, gt)
    return cnt


def _route_kernel(lg_ref, bias_ref, gate_ref):
    s = jax.nn.sigmoid(lg_ref[...])
    sc = s + bias_ref[...]
    tn = s.shape[1]
    per = N_EXPERTS // N_EXPERT_GROUPS
    g3 = sc.reshape(N_EXPERT_GROUPS, per, tn)
    member = lax.broadcasted_iota(jnp.int32, g3.shape, 1)
    m1 = jnp.max(g3, axis=1, keepdims=True)
    first = jnp.min(jnp.where(g3 == m1, member, per), axis=1, keepdims=True)
    m2 = jnp.max(jnp.where(member == first, -jnp.inf, g3), axis=1, keepdims=True)
    gscore = (m1 + m2).reshape(N_EXPERT_GROUPS, tn)
    gsel = _rank_before(gscore, N_EXPERT_GROUPS) < TOPK_GROUPS
    emask = jnp.broadcast_to(gsel.reshape(N_EXPERT_GROUPS, 1, tn), g3.shape).reshape(N_EXPERTS, tn)
    masked = jnp.where(emask, sc, -jnp.inf)
    sel = _rank_before(masked, N_EXPERTS) < TOP_K
    w = jnp.where(sel, s, 0.0)
    gate_ref[...] = w / jnp.sum(w, axis=0, keepdims=True) * ROUTED_SCALE


def _route(logits_t, bias_col, tn):
    n = logits_t.shape[1]
    return pl.pallas_call(
        _route_kernel,
        grid=(n // tn,),
        in_specs=[pl.BlockSpec((N_EXPERTS, tn), lambda i: (0, i)),
                  pl.BlockSpec((N_EXPERTS, 1), lambda i: (0, 0))],
        out_specs=pl.BlockSpec((N_EXPERTS, tn), lambda i: (0, i)),
        out_shape=jax.ShapeDtypeStruct((N_EXPERTS, n), F32),
        compiler_params=_cparams(("parallel",)),
        name="route",
    )(logits_t, bias_col)


def _moe_kernel(x1_ref, h_ref, gate_ref, wg_ref, wu_ref, wd_ref, sg_ref, su_ref, sd_ref, g2_ref, nf_ref,
                y_ref, acc_ref):
    e = pl.program_id(1)
    h = h_ref[...]

    def ffn(wg, wu, wd, scale):
        a = jnp.dot(h, wg.astype(BF16), preferred_element_type=F32)
        b = jnp.dot(h, wu.astype(BF16), preferred_element_type=F32)
        hb = a * jax.nn.sigmoid(a) * b
        out = jnp.dot(hb.astype(BF16), wd.astype(BF16), preferred_element_type=F32)
        return out if scale is None else out * scale

    @pl.when(e == 0)
    def _():
        acc_ref[...] = ffn(sg_ref[...], su_ref[...], sd_ref[...], None)

    lane = lax.broadcasted_iota(jnp.int32, gate_ref.shape, 1)
    gcol = jnp.sum(jnp.where(lane == e, gate_ref[...], 0.0), axis=1, keepdims=True)
    acc_ref[...] += ffn(wg_ref[...], wu_ref[...], wd_ref[...], gcol)

    @pl.when(e == N_EXPERTS - 1)
    def _():
        x2 = x1_ref[...] + g2_ref[...] * acc_ref[...]
        y_ref[...] = _rms(x2, nf_ref[...])


def _moe(x1, h2, gates, wg, wu, wd, sg, su, sd, mod, norm_f, tm, row_fn):
    n = x1.shape[0]
    tok = lambda w: pl.BlockSpec((tm, w), lambda i, e: (i, 0))
    full = lambda a: pl.BlockSpec(a.shape, lambda i, e: (0,) * a.ndim)
    ew = lambda a: pl.BlockSpec((None,) + a.shape[1:], lambda i, e: (e, 0, 0))
    return pl.pallas_call(
        _moe_kernel,
        grid=(n // tm, N_EXPERTS),
        in_specs=[tok(D_MODEL), tok(D_MODEL), tok(N_EXPERTS), ew(wg), ew(wu), ew(wd), full(sg), full(su), full(sd),
                  _mod_spec(5, lambda i, e: row_fn(i)), full(norm_f)],
        out_specs=tok(D_MODEL),
        out_shape=jax.ShapeDtypeStruct((n, D_MODEL), F32),
        scratch_shapes=[pltpu.VMEM((tm, D_MODEL), F32)],
        compiler_params=_cparams(("parallel", "arbitrary")),
        name="moe",
    )(x1, h2, gates, wg, wu, wd, sg, su, sd, mod, norm_f)


def _channel_dft():
    ang = 2.0 * np.pi * ((np.arange(F_HD)[:, None] * np.arange(F_HD)[None, :]) % F_HD) / F_HD
    eye = np.eye(F_GROUPS)
    return np.concatenate([np.kron(eye, np.cos(ang)), np.kron(eye, -np.sin(ang))], axis=1).astype(np.float32)


def _rope_tables(t):
    pos = np.arange(t)
    row, col = pos // GRID_W, pos % GRID_W
    n_freq = HEAD_DIM // 4
    inv = ROPE_THETA ** (-np.arange(n_freq, dtype=np.float64) / n_freq)
    lane = np.arange(LANES)
    hd = lane % HEAD_DIM
    within = hd % (HEAD_DIM // 2)
    freq = within % n_freq
    first = within < n_freq
    p = np.where((hd < HEAD_DIM // 2)[None, :], row[:, None], col[:, None]).astype(np.float64)
    ang = p * inv[freq][None, :]
    cos, sin = np.cos(ang), np.sin(ang)
    sin_up = np.where(first[None, :], -sin, 0.0)
    sin_dn = np.where(first[None, :], 0.0, sin)
    return [jnp.asarray(a.astype(np.float32)) for a in (cos, sin_up, sin_dn)]


def _block_diag_pairs(wf):
    per = LANES // F_HD
    z = jnp.zeros((F_HD, F_HD), wf.dtype)
    blocks = []
    for c in range(F_GROUPS // per):
        rows = [jnp.concatenate([wf[c * per + r] if r == cc else z for cc in range(per)], axis=1) for r in range(per)]
        blocks.append(jnp.concatenate(rows, axis=0))
    return jnp.stack(blocks)


def kernel(x_prompt, x_sample, cache_k, cache_v, c, c_ctx, w_mod, b_mod, norm1, w_in, w_fourier, sink,
           w_out, norm2, w_router, router_bias, w_gate, w_up, w_down, ws_gate, ws_up, ws_down, norm_f):
    nb_ctx, t_ctx, _ = x_prompt.shape
    nb_lat, t_lat, _ = x_sample.shape
    l = 0
    cvec = jnp.concatenate([c_ctx[None, :], c, jnp.zeros((MOD_ROWS - 1 - nb_lat, D_MODEL), F32)], axis=0)
    mod = _modulation(cvec, w_mod[l], b_mod[l]).reshape(MOD_ROWS, 6, 1, D_MODEL)

    w_in_b = w_in[l].astype(BF16)
    dft_c = jnp.asarray(_channel_dft()).astype(BF16)
    wf_bd = _block_diag_pairs(w_fourier[l]).astype(BF16)
    wo_f = w_out[l][:F_DIM].astype(BF16)
    wo_a = w_out[l][F_DIM:].astype(BF16)
    w_router_t = w_router[l].T.astype(BF16)
    bias_col = router_bias[l].reshape(N_EXPERTS, 1)
    n1 = norm1[l].reshape(1, D_MODEL)
    n2 = norm2[l].reshape(1, D_MODEL)
    nf = norm_f.reshape(1, D_MODEL)
    sink_col = jnp.broadcast_to(sink[l].reshape(N_KV_HEADS, GQA_GROUP, 1, 1),
                                (N_KV_HEADS, GQA_GROUP, Q_BLOCK, 1)).reshape(N_KV_HEADS, GQA_GROUP * Q_BLOCK, 1)
    wg, wu, wd = w_gate[l].astype(BF16), w_up[l].astype(BF16), w_down[l].astype(BF16)
    sg, su, sd = ws_gate[l].astype(BF16), ws_up[l].astype(BF16), ws_down[l].astype(BF16)

    def layer(x, batch, t, latent, kx, vx, t1, t2, tm, tm_moe):
        n = batch * t
        x2d = x.reshape(n, D_MODEL)
        tiles = t // tm
        row_fn = (lambda i: 1 + i // tiles) if latent else (lambda i: 0)
        rope_tabs = _rope_tables(t) if latent else None
        outs = _project(x2d, mod, n1, w_in_b, dft_c, rope_tabs, tm, tiles, row_fn)
        wr, wi, q, k, v = outs[:5]
        mixf = _fourier(wr, wi, wf_bd, batch, t, t1, t2)
        q3, k3, v3 = (a.reshape(batch, t, -1) for a in (q, k, v))
        if latent:
            mixa = _attention(q3, k3, v3, kx, vx, sink_col, batch, t, True)
        else:
            mixa = _attention(q3, None, None, k3, v3, sink_col, batch, t, False)
        x1, h2, logits_t = _outproj(x2d, mixf.reshape(n, F_DIM), mixa.reshape(n, ATTN_DIM), wo_f, wo_a,
                                    mod, n2, w_router_t, tm, row_fn)
        gates = _route(logits_t, bias_col, 512).T
        moe_tiles = t // tm_moe if latent else 1
        moe_row = (lambda i: 1 + i // moe_tiles) if latent else (lambda i: 0)
        y = _moe(x1, h2, gates, wg, wu, wd, sg, su, sd, mod, nf, tm_moe, moe_row)
        return y.reshape(batch, t, D_MODEL), outs[5:]

    y_prompt, (kf, vf) = layer(x_prompt, nb_ctx, t_ctx, False, None, None, 16, 16, 256, 1024)
    kx = cache_k[:, l].reshape(nb_lat, -1, KV_DIM).astype(BF16)
    vx = cache_v[:, l].reshape(nb_lat, -1, KV_DIM).astype(BF16)
    y_sample, _ = layer(x_sample, nb_lat, t_lat, True, kx, vx, 64, 64, 512, 1024)
    new_k = kf.reshape(nb_ctx, 1, t_ctx, N_KV_HEADS, HEAD_DIM)
    new_v = vf.reshape(nb_ctx, 1, t_ctx, N_KV_HEADS, HEAD_DIM)
    return (y_prompt, y_sample, new_k, new_v)
```

```python
import functools

import numpy as np
import jax
import jax.numpy as jnp
from jax import lax
from jax.experimental import pallas as pl
from jax.experimental.pallas import tpu as pltpu

F32 = jnp.float32
BF16 = jnp.bfloat16

D_MODEL = 1024
GRID_W = 64
HEAD_DIM = 64
N_HEADS = 12
N_KV_HEADS = 4
GQA_GROUP = N_HEADS // N_KV_HEADS
ATTN_DIM = N_HEADS * HEAD_DIM
KV_DIM = N_KV_HEADS * HEAD_DIM
F_GROUPS = 4
F_HD = 64
F_DIM = F_GROUPS * F_HD
IN_DIM = F_DIM + ATTN_DIM + 2 * KV_DIM
WINDOW = 128
Q_BLOCK = 128
ROPE_THETA = 10000.0
N_EXPERTS = 64
TOP_K = 8
N_EXPERT_GROUPS = 8
TOPK_GROUPS = 4
EXPERT_DIM = 256
SHARED_DIM = 256
ROUTED_SCALE = 2.5
EPS = 1e-6

LANES = 128
MOD_ROWS = 8
RUN_ALIGN = 16
PERM_BLOCK = 256
FFN_BLOCK = 512
MOE_CHUNK = 256
VMEM_LIMIT = 56 * 1024 * 1024


def _cparams(sem):
    return pltpu.CompilerParams(dimension_semantics=sem, vmem_limit_bytes=VMEM_LIMIT)


def _bdot(a, b):
    return jnp.dot(a.astype(BF16), b.astype(BF16), preferred_element_type=F32)


def _rms(x, g):
    return x * lax.rsqrt(jnp.mean(x * x, axis=-1, keepdims=True) + EPS) * g


def _mod_kernel(c_ref, w_ref, b_ref, o_ref):
    c = c_ref[...]
    a = c * jax.nn.sigmoid(c)
    o_ref[...] = _bdot(a, w_ref[...]) + b_ref[...]


def _modulation(cvec, w_mod, b_mod):
    n = w_mod.shape[1]
    tn = 1024
    return pl.pallas_call(
        _mod_kernel,
        grid=(n // tn,),
        in_specs=[pl.BlockSpec((MOD_ROWS, D_MODEL), lambda j: (0, 0)),
                  pl.BlockSpec((D_MODEL, tn), lambda j: (0, j)),
                  pl.BlockSpec((1, tn), lambda j: (0, j))],
        out_specs=pl.BlockSpec((MOD_ROWS, tn), lambda j: (0, j)),
        out_shape=jax.ShapeDtypeStruct((MOD_ROWS, n), F32),
        compiler_params=_cparams(("parallel",)),
        name="modulation",
    )(cvec, w_mod, b_mod.reshape(1, n))


def _mod_spec(piece, row_fn):
    return pl.BlockSpec((None, None, 1, D_MODEL), lambda *idx: (row_fn(*idx), piece, 0, 0))


def _rope(x, cos, sin_up, sin_dn):
    outs = []
    for j in range(x.shape[1] // LANES):
        xj = x[:, j * LANES:(j + 1) * LANES]
        up = pltpu.roll(xj, LANES - 16, axis=1)
        dn = pltpu.roll(xj, 16, axis=1)
        outs.append(xj * cos + up * sin_up + dn * sin_dn)
    return jnp.concatenate(outs, axis=1)


def _proj_kernel(*refs, latent):
    if latent:
        (x_ref, sh_ref, sc_ref, g_ref, w_ref, dft_ref, cos_ref, sup_ref, sdn_ref,
         wr_ref, wi_ref, q_ref, k_ref, v_ref) = refs
    else:
        (x_ref, sh_ref, sc_ref, g_ref, w_ref, dft_ref,
         wr_ref, wi_ref, q_ref, k_ref, v_ref, kf_ref, vf_ref) = refs
    x = x_ref[...]
    h = _rms(x, g_ref[...]) * (1.0 + sc_ref[...]) + sh_ref[...]
    p = _bdot(h, w_ref[...])
    u = p[:, :F_DIM]
    q = p[:, F_DIM:F_DIM + ATTN_DIM]
    k = p[:, F_DIM + ATTN_DIM:F_DIM + ATTN_DIM + KV_DIM]
    v = p[:, F_DIM + ATTN_DIM + KV_DIM:]
    w = _bdot(u, dft_ref[...])
    wr_ref[...] = w[:, :F_DIM]
    wi_ref[...] = w[:, F_DIM:]
    if latent:
        cos, sup, sdn = cos_ref[...], sup_ref[...], sdn_ref[...]
        q = _rope(q, cos, sup, sdn)
        k = _rope(k, cos, sup, sdn)
    else:
        kf_ref[...] = k
        vf_ref[...] = v
    q_ref[...] = (q * (HEAD_DIM ** -0.5)).astype(BF16)
    k_ref[...] = k.astype(BF16)
    v_ref[...] = v.astype(BF16)


def _project(x2d, mod, norm1, w_in, dft_c, rope_tabs, tm, tiles_per_batch, row_fn):
    n = x2d.shape[0]
    latent = rope_tabs is not None
    tok = lambda w: pl.BlockSpec((tm, w), lambda i: (i, 0))
    full = lambda a: pl.BlockSpec(a.shape, lambda i: (0,) * a.ndim)
    in_specs = [tok(D_MODEL), _mod_spec(0, row_fn), _mod_spec(1, row_fn), full(norm1), full(w_in), full(dft_c)]
    args = [x2d, mod, mod, norm1, w_in, dft_c]
    out_specs = [tok(F_DIM), tok(F_DIM), tok(ATTN_DIM), tok(KV_DIM), tok(KV_DIM)]
    out_shape = [jax.ShapeDtypeStruct((n, F_DIM), F32), jax.ShapeDtypeStruct((n, F_DIM), F32),
                 jax.ShapeDtypeStruct((n, ATTN_DIM), BF16), jax.ShapeDtypeStruct((n, KV_DIM), BF16),
                 jax.ShapeDtypeStruct((n, KV_DIM), BF16)]
    if latent:
        pos = pl.BlockSpec((tm, LANES), lambda i: (i % tiles_per_batch, 0))
        in_specs += [pos, pos, pos]
        args += list(rope_tabs)
    else:
        out_specs += [tok(KV_DIM), tok(KV_DIM)]
        out_shape += [jax.ShapeDtypeStruct((n, KV_DIM), F32), jax.ShapeDtypeStruct((n, KV_DIM), F32)]
    return pl.pallas_call(
        functools.partial(_proj_kernel, latent=latent),
        grid=(n // tm,),
        in_specs=in_specs, out_specs=out_specs, out_shape=out_shape,
        compiler_params=_cparams(("parallel",)),
        name="project_latent" if latent else "project_context",
    )(*args)


def _fourier_kernel(wr_ref, wi_ref, a_ref, b_ref, wf_ref, o_ref, yr_ref, yi_ref, z_ref, *, t1, t2, scale):
    for j in range(t2):
        rows = pl.ds(j, t1, stride=t2)
        xin = jnp.concatenate([wr_ref[rows, :], wi_ref[rows, :]], axis=0)
        y = _bdot(a_ref[j], xin)
        yr_ref[rows, :] = y[:t1]
        yi_ref[rows, :] = y[t1:]
    bm = b_ref[...]
    for k1 in range(t1):
        rows = pl.ds(k1 * t2, t2)
        yin = jnp.concatenate([yr_ref[rows, :], yi_ref[rows, :]], axis=0)
        z_ref[pl.ds(k1, t2, stride=t1), :] = _bdot(bm, yin)
    o_ref[...] = (_bdot(z_ref[...], wf_ref[...]) * scale).astype(BF16)


def _dft_tables(t1, t2):
    t = t1 * t2
    k1 = np.arange(t1)[None, :, None]
    pos = (t2 * np.arange(t1)[None, None, :] + np.arange(t2)[:, None, None])
    ang = 2.0 * np.pi * ((k1 * pos) % t) / t
    c, s = np.cos(ang), np.sin(ang)
    a = np.concatenate([np.concatenate([c, s], axis=2), np.concatenate([-s, c], axis=2)], axis=1)
    ang2 = 2.0 * np.pi * ((np.arange(t2)[:, None] * np.arange(t2)[None, :]) % t2) / t2
    b = np.concatenate([np.cos(ang2), np.sin(ang2)], axis=1)
    return a.astype(np.float32), b.astype(np.float32)


def _fourier(wr, wi, wf_bd, batch, t, t1, t2):
    a_np, b_np = _dft_tables(t1, t2)
    a = jnp.asarray(a_np).astype(BF16)
    b = jnp.asarray(b_np).astype(BF16)
    cw = LANES
    blk = pl.BlockSpec((None, t, cw), lambda bi, ci: (bi, 0, ci))
    return pl.pallas_call(
        functools.partial(_fourier_kernel, t1=t1, t2=t2, scale=float((t * F_HD) ** -0.5)),
        grid=(batch, F_DIM // cw),
        in_specs=[blk, blk,
                  pl.BlockSpec(a.shape, lambda bi, ci: (0, 0, 0)),
                  pl.BlockSpec(b.shape, lambda bi, ci: (0, 0)),
                  pl.BlockSpec((None, cw, cw), lambda bi, ci: (ci, 0, 0))],
        out_specs=blk,
        out_shape=jax.ShapeDtypeStruct((batch, t, F_DIM), BF16),
        scratch_shapes=[pltpu.VMEM((t, cw), F32), pltpu.VMEM((t, cw), F32), pltpu.VMEM((t, cw), F32)],
        compiler_params=_cparams(("parallel", "parallel")),
        name="fourier_%d" % t,
    )(wr.reshape(batch, t, F_DIM), wi.reshape(batch, t, F_DIM), a, b, wf_bd)


def _attn_kernel(*refs, windowed, n_blocks):
    if windowed:
        q_ref, kp_ref, kc_ref, kn_ref, vp_ref, vc_ref, vn_ref, kx_ref, vx_ref, sink_ref, o_ref = refs
    else:
        q_ref, kx_ref, vx_ref, sink_ref, o_ref = refs
    q = q_ref[...]
    n_ctx = kx_ref.shape[0]
    if windowed:
        i = pl.program_id(1)
        a = lax.broadcasted_iota(jnp.int32, (Q_BLOCK, Q_BLOCK), 0)
        j = lax.broadcasted_iota(jnp.int32, (Q_BLOCK, Q_BLOCK), 1)
        prev_ok = (j >= a) & (i > 0)
        next_ok = (j <= a) & (i < n_blocks - 1)
        true_w = jnp.ones((Q_BLOCK, Q_BLOCK), jnp.bool_)
        true_c = jnp.ones((Q_BLOCK, n_ctx), jnp.bool_)
        mask1 = jnp.concatenate([prev_ok, true_w, next_ok, true_c], axis=1)
        mask = jnp.concatenate([mask1] * GQA_GROUP, axis=0)
    outs = []
    for kv in range(N_KV_HEADS):
        hs = slice(kv * HEAD_DIM, (kv + 1) * HEAD_DIM)
        qs = jnp.concatenate(
            [q[:, (kv * GQA_GROUP + g) * HEAD_DIM:(kv * GQA_GROUP + g + 1) * HEAD_DIM] for g in range(GQA_GROUP)],
            axis=0)
        if windowed:
            kcat = jnp.concatenate([kp_ref[:, hs], kc_ref[:, hs], kn_ref[:, hs], kx_ref[:, hs]], axis=0)
            vcat = jnp.concatenate([vp_ref[:, hs], vc_ref[:, hs], vn_ref[:, hs], vx_ref[:, hs]], axis=0)
        else:
            kcat = kx_ref[:, hs]
            vcat = vx_ref[:, hs]
        s = lax.dot_general(qs, kcat, (((1,), (1,)), ((), ())), preferred_element_type=F32)
        if windowed:
            s = jnp.where(mask, s, -jnp.inf)
        sk = sink_ref[kv]
        m = jnp.maximum(jnp.max(s, axis=1, keepdims=True), sk)
        p = jnp.exp(s - m)
        den = jnp.sum(p, axis=1, keepdims=True) + jnp.exp(sk - m)
        o = jnp.dot(p.astype(BF16), vcat, preferred_element_type=F32) / den
        for g in range(GQA_GROUP):
            outs.append(o[g * Q_BLOCK:(g + 1) * Q_BLOCK])
    o_ref[...] = jnp.concatenate(outs, axis=1).astype(BF16)


def _attention(q, k, v, kx, vx, sink_col, batch, t, windowed):
    nb = t // Q_BLOCK
    n_ctx = kx.shape[1]
    qspec = pl.BlockSpec((None, Q_BLOCK, ATTN_DIM), lambda b, i: (b, i, 0))
    xspec = pl.BlockSpec((None, n_ctx, KV_DIM), lambda b, i: (b, 0, 0))
    sspec = pl.BlockSpec(sink_col.shape, lambda b, i: (0, 0, 0))
    if windowed:
        prev = pl.BlockSpec((None, Q_BLOCK, KV_DIM), lambda b, i: (b, jnp.maximum(i - 1, 0), 0))
        cur = pl.BlockSpec((None, Q_BLOCK, KV_DIM), lambda b, i: (b, i, 0))
        nxt = pl.BlockSpec((None, Q_BLOCK, KV_DIM), lambda b, i: (b, jnp.minimum(i + 1, nb - 1), 0))
        in_specs = [qspec, prev, cur, nxt, prev, cur, nxt, xspec, xspec, sspec]
        args = [q, k, k, k, v, v, v, kx, vx, sink_col]
    else:
        in_specs = [qspec, xspec, xspec, sspec]
        args = [q, kx, vx, sink_col]
    return pl.pallas_call(
        functools.partial(_attn_kernel, windowed=windowed, n_blocks=nb),
        grid=(batch, nb),
        in_specs=in_specs, out_specs=qspec,
        out_shape=jax.ShapeDtypeStruct((batch, t, ATTN_DIM), BF16),
        compiler_params=_cparams(("parallel", "parallel")),
        name="attention_latent" if windowed else "attention_context",
    )(*args)


def _outproj_kernel(x_ref, mf_ref, ma_ref, wof_ref, woa_ref, g1_ref, sh_ref, sc_ref, n2_ref, wr_ref,
                    x1_ref, h_ref, lg_ref):
    o = jnp.dot(mf_ref[...], wof_ref[...], preferred_element_type=F32)
    o = o + jnp.dot(ma_ref[...], woa_ref[...], preferred_element_type=F32)
    x1 = x_ref[...] + g1_ref[...] * o
    x1_ref[...] = x1
    h = (_rms(x1, n2_ref[...]) * (1.0 + sc_ref[...]) + sh_ref[...]).astype(BF16)
    h_ref[...] = h
    lg_ref[...] = lax.dot_general(wr_ref[...], h, (((1,), (1,)), ((), ())), preferred_element_type=F32)


def _outproj(x2d, mixf, mixa, wo_f, wo_a, mod, norm2, w_router_t, tm, row_fn):
    n = x2d.shape[0]
    tok = lambda w: pl.BlockSpec((tm, w), lambda i: (i, 0))
    full = lambda a: pl.BlockSpec(a.shape, lambda i: (0,) * a.ndim)
    return pl.pallas_call(
        _outproj_kernel,
        grid=(n // tm,),
        in_specs=[tok(D_MODEL), tok(F_DIM), tok(ATTN_DIM), full(wo_f), full(wo_a),
                  _mod_spec(2, row_fn), _mod_spec(3, row_fn), _mod_spec(4, row_fn), full(norm2), full(w_router_t)],
        out_specs=[tok(D_MODEL), tok(D_MODEL), pl.BlockSpec((N_EXPERTS, tm), lambda i: (0, i))],
        out_shape=[jax.ShapeDtypeStruct((n, D_MODEL), F32), jax.ShapeDtypeStruct((n, D_MODEL), BF16),
                   jax.ShapeDtypeStruct((N_EXPERTS, n), F32)],
        compiler_params=_cparams(("parallel",)),
        name="outproj",
    )(x2d, mixf, mixa, wo_f, wo_a, mod, mod, mod, norm2, w_router_t)


def _rank_before(vals, n):
    idx = lax.broadcasted_iota(jnp.int32, vals.shape, 0)
    cnt = jnp.zeros(vals.shape, jnp.int32)
    for r in range(n):
        row = vals[r:r + 1, :]
        ge = jnp.where(row >= vals, 1, 0)
        gt = jnp.where(row > vals, 1, 0)
        cnt = cnt + jnp.where(idx > r, ge, gt)
    return cnt


def _route_kernel(lg_ref, bias_ref, lo_ref, up_ref, pos_ref, gs_ref, cnt_ref):
    s = jax.nn.sigmoid(lg_ref[...])
    sc = s + bias_ref[...]
    tn = s.shape[1]
    per = N_EXPERTS // N_EXPERT_GROUPS
    g3 = sc.reshape(N_EXPERT_GROUPS, per, tn)
    member = lax.broadcasted_iota(jnp.int32, g3.shape, 1)
    m1 = jnp.max(g3, axis=1, keepdims=True)
    first = jnp.min(jnp.where(g3 == m1, member, per), axis=1, keepdims=True)
    m2 = jnp.max(jnp.where(member == first, -jnp.inf, g3), axis=1, keepdims=True)
    gscore = (m1 + m2).reshape(N_EXPERT_GROUPS, tn)
    gsel = _rank_before(gscore, N_EXPERT_GROUPS) < TOPK_GROUPS
    emask = jnp.broadcast_to(gsel.reshape(N_EXPERT_GROUPS, 1, tn), g3.shape).reshape(N_EXPERTS, tn)
    masked = jnp.where(emask, sc, -jnp.inf)
    sel = _rank_before(masked, N_EXPERTS) < TOP_K
    w = jnp.where(sel, s, 0.0)
    gate = w / jnp.sum(w, axis=0, keepdims=True) * ROUTED_SCALE

    self = jnp.where(sel, 1.0, 0.0)
    selb = self.astype(BF16)
    cnt = jnp.sum(self, axis=1, keepdims=True)
    pad = jnp.floor((cnt + (RUN_ALIGN - 1)) * (1.0 / RUN_ALIGN)) * RUN_ALIGN
    lo = lo_ref[...]
    soff = jnp.dot(lo, jnp.broadcast_to(pad, (N_EXPERTS, LANES)).astype(BF16), preferred_element_type=F32)[:, :1]
    rank = jnp.dot(selb, up_ref[...], preferred_element_type=F32)
    kidx = jnp.dot(lo, selb, preferred_element_type=F32)
    pos_e = jnp.where(sel, soff + rank, 0.0)
    rows_p, rows_g = [], []
    for k in range(TOP_K):
        m = kidx == k
        rows_p.append(jnp.sum(jnp.where(m, pos_e, 0.0), axis=0, keepdims=True))
        rows_g.append(jnp.sum(jnp.where(m, gate, 0.0), axis=0, keepdims=True))
    pos_ref[...] = jnp.concatenate(rows_p, axis=0).astype(jnp.int32)
    gs_ref[...] = jnp.concatenate(rows_g, axis=0)
    cnt_ref[...] = pad.astype(jnp.int32)


def _route(logits_t, bias_col, tc):
    n = logits_t.shape[1]
    lo = jnp.asarray(np.tril(np.ones((N_EXPERTS, N_EXPERTS), np.float32), -1)).astype(BF16)
    up = jnp.asarray(np.triu(np.ones((tc, tc), np.float32), 1)).astype(BF16)
    slot = pl.BlockSpec((TOP_K, tc), lambda i: (0, i))
    return pl.pallas_call(
        _route_kernel,
        grid=(n // tc,),
        in_specs=[pl.BlockSpec((N_EXPERTS, tc), lambda i: (0, i)),
                  pl.BlockSpec((N_EXPERTS, 1), lambda i: (0, 0)),
                  pl.BlockSpec(lo.shape, lambda i: (0, 0)),
                  pl.BlockSpec(up.shape, lambda i: (0, 0))],
        out_specs=[slot, slot, pl.BlockSpec((None, N_EXPERTS, 1), lambda i: (i, 0, 0))],
        out_shape=[jax.ShapeDtypeStruct((TOP_K, n), jnp.int32), jax.ShapeDtypeStruct((TOP_K, n), F32),
                   jax.ShapeDtypeStruct((n // tc, N_EXPERTS, 1), jnp.int32)],
        compiler_params=_cparams(("parallel",)),
        name="route",
    )(logits_t, bias_col, lo, up)


def _aligned(x):
    return pl.multiple_of(x, RUN_ALIGN)


def _gather_kernel(src_ref, cnt_ref, dst_ref, tot_ref, tail0_ref, tailn_ref,
                   pos_ref, x_ref, xs_ref, z_ref, zero_ref, sem, *, tc, n_chunks):
    c = pl.program_id(0)
    slot = c % 2

    def wait_chunk(ci, s):
        n = _aligned(tot_ref[ci])
        pltpu.make_async_copy(z_ref.at[s, pl.ds(0, n)], xs_ref.at[pl.ds(0, n)], sem.at[s]).wait()

    def tail_copy(e):
        n = _aligned(tailn_ref[e])
        return n, pltpu.make_async_copy(zero_ref.at[pl.ds(0, n)], xs_ref.at[pl.ds(_aligned(tail0_ref[e]), n)],
                                        sem.at[2])

    @pl.when(c == 0)
    def _():
        zero_ref[...] = jnp.zeros(zero_ref.shape, zero_ref.dtype)

        def body(e, carry):
            n, cp = tail_copy(e)

            @pl.when(n > 0)
            def _():
                cp.start()
            return carry
        lax.fori_loop(0, N_EXPERTS, body, 0)

    @pl.when(c >= 2)
    def _():
        wait_chunk(c - 2, slot)

    def rb_body(rb, carry):
        r0 = pl.multiple_of(rb * PERM_BLOCK, PERM_BLOCK)
        rows = r0 + lax.broadcasted_iota(jnp.int32, (PERM_BLOCK, tc), 0)
        onehot = jnp.zeros((PERM_BLOCK, tc), F32)
        for k in range(TOP_K):
            onehot = jnp.where(pos_ref[k:k + 1, :] == rows, 1.0, onehot)
        z_ref[slot, pl.ds(r0, PERM_BLOCK), :] = jnp.dot(
            onehot.astype(BF16), x_ref[...], preferred_element_type=F32).astype(BF16)
        return carry
    lax.fori_loop(0, (tot_ref[c] + PERM_BLOCK - 1) // PERM_BLOCK, rb_body, 0)

    def e_body(e, carry):
        i = c * N_EXPERTS + e
        n = _aligned(cnt_ref[i])

        @pl.when(n > 0)
        def _():
            pltpu.make_async_copy(z_ref.at[slot, pl.ds(_aligned(src_ref[i]), n)],
                                  xs_ref.at[pl.ds(_aligned(dst_ref[i]), n)], sem.at[slot]).start()
        return carry
    lax.fori_loop(0, N_EXPERTS, e_body, 0)

    @pl.when(c == n_chunks - 1)
    def _():
        wait_chunk(c, slot)
        if n_chunks > 1:
            wait_chunk(c - 1, 1 - slot)

        def body(e, carry):
            n, cp = tail_copy(e)

            @pl.when(n > 0)
            def _():
                cp.wait()
            return carry
        lax.fori_loop(0, N_EXPERTS, body, 0)


def _local_cap(tc):
    worst = TOP_K * tc + N_EXPERTS * (RUN_ALIGN - 1)
    return -(-worst // PERM_BLOCK) * PERM_BLOCK


def _gather(tabs, pos, h2, r_max, tc):
    n = h2.shape[0]
    n_chunks = n // tc
    return pl.pallas_call(
        functools.partial(_gather_kernel, tc=tc, n_chunks=n_chunks),
        grid_spec=pltpu.PrefetchScalarGridSpec(
            num_scalar_prefetch=6, grid=(n_chunks,),
            in_specs=[pl.BlockSpec((TOP_K, tc), lambda c, *_: (0, c)),
                      pl.BlockSpec((tc, D_MODEL), lambda c, *_: (c, 0))],
            out_specs=pl.BlockSpec(memory_space=pl.ANY),
            scratch_shapes=[pltpu.VMEM((2, _local_cap(tc), D_MODEL), BF16),
                            pltpu.VMEM((FFN_BLOCK, D_MODEL), BF16),
                            pltpu.SemaphoreType.DMA((3,))]),
        out_shape=jax.ShapeDtypeStruct((r_max, D_MODEL), BF16),
        compiler_params=_cparams(("arbitrary",)),
        name="moe_gather",
    )(tabs["src"], tabs["cnt"], tabs["dst"], tabs["tot"], tabs["tail0"], tabs["tailn"], pos, h2)


def _ffn_kernel(be_ref, nb_ref, x_ref, wg_ref, wu_ref, wd_ref, y_ref):
    @pl.when(pl.program_id(0) < nb_ref[0])
    def _():
        x = x_ref[...]
        a = jnp.dot(x, wg_ref[...], preferred_element_type=F32)
        u = jnp.dot(x, wu_ref[...], preferred_element_type=F32)
        h = (a * jax.nn.sigmoid(a) * u).astype(BF16)
        y_ref[...] = jnp.dot(h, wd_ref[...], preferred_element_type=F32).astype(BF16)


def _ffn(tabs, xs, wg, wu, wd):
    r_max = xs.shape[0]
    rows = pl.BlockSpec((FFN_BLOCK, D_MODEL), lambda b, be, nb: (jnp.minimum(b, nb[0] - 1), 0))
    ew = lambda a: pl.BlockSpec((None,) + a.shape[1:], lambda b, be, nb: (be[b], 0, 0))
    return pl.pallas_call(
        _ffn_kernel,
        grid_spec=pltpu.PrefetchScalarGridSpec(
            num_scalar_prefetch=2, grid=(r_max // FFN_BLOCK,),
            in_specs=[rows, ew(wg), ew(wu), ew(wd)],
            out_specs=rows),
        out_shape=jax.ShapeDtypeStruct((r_max, D_MODEL), BF16),
        compiler_params=_cparams(("arbitrary",)),
        name="moe_ffn",
    )(tabs["blk_e"], tabs["nblk"], xs, wg, wu, wd)


def _combine_kernel(src_ref, cnt_ref, dst_ref, tot_ref,
                    post_ref, gt_ref, x1_ref, h_ref, sg_ref, su_ref, sd_ref, g2_ref, nf_ref, ys_ref,
                    y_ref, ybuf, acc_ref, sem, *, tc, n_chunks):
    c = pl.program_id(0)
    slot = c % 2

    def issue(ci, s):
        def e_body(e, carry):
            i = ci * N_EXPERTS + e
            n = _aligned(cnt_ref[i])

            @pl.when(n > 0)
            def _():
                pltpu.make_async_copy(ys_ref.at[pl.ds(_aligned(dst_ref[i]), n)],
                                      ybuf.at[s, pl.ds(_aligned(src_ref[i]), n)], sem.at[s]).start()
            return carry
        lax.fori_loop(0, N_EXPERTS, e_body, 0)

    @pl.when(c == 0)
    def _():
        ybuf[...] = jnp.zeros(ybuf.shape, ybuf.dtype)
        issue(0, 0)

    @pl.when(c + 1 < n_chunks)
    def _():
        issue(c + 1, 1 - slot)

    h = h_ref[...]
    a = jnp.dot(h, sg_ref[...], preferred_element_type=F32)
    u = jnp.dot(h, su_ref[...], preferred_element_type=F32)
    acc_ref[...] = jnp.dot((a * jax.nn.sigmoid(a) * u).astype(BF16), sd_ref[...], preferred_element_type=F32)

    n = _aligned(tot_ref[c])
    pltpu.make_async_copy(ys_ref.at[pl.ds(0, n)], ybuf.at[slot, pl.ds(0, n)], sem.at[slot]).wait()

    def cb_body(cb, carry):
        r0 = pl.multiple_of(cb * PERM_BLOCK, PERM_BLOCK)
        cols = r0 + lax.broadcasted_iota(jnp.int32, (tc, PERM_BLOCK), 1)
        w = jnp.zeros((tc, PERM_BLOCK), F32)
        for k in range(TOP_K):
            w = jnp.where(post_ref[:, k:k + 1] == cols, gt_ref[:, k:k + 1], w)
        acc_ref[...] += jnp.dot(w.astype(BF16), ybuf[slot, pl.ds(r0, PERM_BLOCK), :], preferred_element_type=F32)
        return carry
    lax.fori_loop(0, (tot_ref[c] + PERM_BLOCK - 1) // PERM_BLOCK, cb_body, 0)

    x2 = x1_ref[...] + g2_ref[...] * acc_ref[...]
    y_ref[...] = _rms(x2, nf_ref[...])


def _combine(tabs, pos_t, gate_t, x1, h2, sg, su, sd, mod, norm_f, ys, tc, row_fn):
    n = x1.shape[0]
    n_chunks = n // tc
    tok = lambda w: pl.BlockSpec((tc, w), lambda c, *_: (c, 0))
    full = lambda a: pl.BlockSpec(a.shape, lambda c, *_: (0,) * a.ndim)
    return pl.pallas_call(
        functools.partial(_combine_kernel, tc=tc, n_chunks=n_chunks),
        grid_spec=pltpu.PrefetchScalarGridSpec(
            num_scalar_prefetch=4, grid=(n_chunks,),
            in_specs=[tok(TOP_K), tok(TOP_K), tok(D_MODEL), tok(D_MODEL), full(sg), full(su), full(sd),
                      _mod_spec(5, lambda c, *_: row_fn(c)), full(norm_f), pl.BlockSpec(memory_space=pl.ANY)],
            out_specs=tok(D_MODEL),
            scratch_shapes=[pltpu.VMEM((2, _local_cap(tc), D_MODEL), BF16),
                            pltpu.VMEM((tc, D_MODEL), F32),
                            pltpu.SemaphoreType.DMA((2,))]),
        out_shape=jax.ShapeDtypeStruct((n, D_MODEL), F32),
        compiler_params=_cparams(("arbitrary",)),
        name="moe_combine",
    )(tabs["src"], tabs["cnt"], tabs["dst"], tabs["tot"], pos_t, gate_t, x1, h2, sg, su, sd, mod, norm_f, ys)


def _dispatch_tables(pad_cnt, n_blocks_max):
    i32 = jnp.int32
    src = jnp.cumsum(pad_cnt, axis=1) - pad_cnt
    tot = jnp.sum(pad_cnt, axis=1)
    used = jnp.sum(pad_cnt, axis=0)
    region = (used + FFN_BLOCK - 1) // FFN_BLOCK * FFN_BLOCK
    region_end = jnp.cumsum(region)
    e_off = region_end - region
    dst = e_off[None, :] + jnp.cumsum(pad_cnt, axis=0) - pad_cnt
    nblk = region_end[-1] // FFN_BLOCK
    ends = region_end // FFN_BLOCK
    blk_e = jnp.minimum(jnp.sum(ends[None, :] <= jnp.arange(n_blocks_max)[:, None], axis=1), N_EXPERTS - 1)
    return {"src": src.reshape(-1).astype(i32), "cnt": pad_cnt.reshape(-1).astype(i32),
            "dst": dst.reshape(-1).astype(i32), "tot": tot.astype(i32),
            "tail0": (e_off + used).astype(i32), "tailn": (region - used).astype(i32),
            "blk_e": blk_e.astype(i32), "nblk": nblk.reshape(1).astype(i32)}


def _moe(x1, h2, logits_t, bias_col, wg, wu, wd, sg, su, sd, mod, norm_f, row_fn):
    n = x1.shape[0]
    tc = MOE_CHUNK
    pos, gslot, pad_cnt = _route(logits_t, bias_col, tc)
    worst_rows = (n // tc) * (TOP_K * tc + N_EXPERTS * (RUN_ALIGN - 1)) + N_EXPERTS * (FFN_BLOCK - RUN_ALIGN)
    n_blocks_max = -(-worst_rows // FFN_BLOCK)
    tabs = _dispatch_tables(pad_cnt[:, :, 0], n_blocks_max)
    xs = _gather(tabs, pos, h2, n_blocks_max * FFN_BLOCK, tc)
    ys = _ffn(tabs, xs, wg, wu, wd)
    return _combine(tabs, pos.T, gslot.T, x1, h2, sg, su, sd, mod, norm_f, ys, tc, row_fn)


def _channel_dft():
    ang = 2.0 * np.pi * ((np.arange(F_HD)[:, None] * np.arange(F_HD)[None, :]) % F_HD) / F_HD
    eye = np.eye(F_GROUPS)
    return np.concatenate([np.kron(eye, np.cos(ang)), np.kron(eye, -np.sin(ang))], axis=1).astype(np.float32)


def _rope_tables(t):
    pos = np.arange(t)
    row, col = pos // GRID_W, pos % GRID_W
    n_freq = HEAD_DIM // 4
    inv = ROPE_THETA ** (-np.arange(n_freq, dtype=np.float64) / n_freq)
    lane = np.arange(LANES)
    hd = lane % HEAD_DIM
    within = hd % (HEAD_DIM // 2)
    freq = within % n_freq
    first = within < n_freq
    p = np.where((hd < HEAD_DIM // 2)[None, :], row[:, None], col[:, None]).astype(np.float64)
    ang = p * inv[freq][None, :]
    cos, sin = np.cos(ang), np.sin(ang)
    sin_up = np.where(first[None, :], -sin, 0.0)
    sin_dn = np.where(first[None, :], 0.0, sin)
    return [jnp.asarray(a.astype(np.float32)) for a in (cos, sin_up, sin_dn)]


def _block_diag_pairs(wf):
    per = LANES // F_HD
    z = jnp.zeros((F_HD, F_HD), wf.dtype)
    blocks = []
    for c in range(F_GROUPS // per):
        rows = [jnp.concatenate([wf[c * per + r] if r == cc else z for cc in range(per)], axis=1) for r in range(per)]
        blocks.append(jnp.concatenate(rows, axis=0))
    return jnp.stack(blocks)


def kernel(x_prompt, x_sample, cache_k, cache_v, c, c_ctx, w_mod, b_mod, norm1, w_in, w_fourier, sink,
           w_out, norm2, w_router, router_bias, w_gate, w_up, w_down, ws_gate, ws_up, ws_down, norm_f):
    nb_ctx, t_ctx, _ = x_prompt.shape
    nb_lat, t_lat, _ = x_sample.shape
    l = 0
    cvec = jnp.concatenate([c_ctx[None, :], c, jnp.zeros((MOD_ROWS - 1 - nb_lat, D_MODEL), F32)], axis=0)
    mod = _modulation(cvec, w_mod[l], b_mod[l]).reshape(MOD_ROWS, 6, 1, D_MODEL)

    w_in_b = w_in[l].astype(BF16)
    dft_c = jnp.asarray(_channel_dft()).astype(BF16)
    wf_bd = _block_diag_pairs(w_fourier[l]).astype(BF16)
    wo_f = w_out[l][:F_DIM].astype(BF16)
    wo_a = w_out[l][F_DIM:].astype(BF16)
    w_router_t = w_router[l].T.astype(BF16)
    bias_col = router_bias[l].reshape(N_EXPERTS, 1)
    n1 = norm1[l].reshape(1, D_MODEL)
    n2 = norm2[l].reshape(1, D_MODEL)
    nf = norm_f.reshape(1, D_MODEL)
    sink_col = jnp.broadcast_to(sink[l].reshape(N_KV_HEADS, GQA_GROUP, 1, 1),
                                (N_KV_HEADS, GQA_GROUP, Q_BLOCK, 1)).reshape(N_KV_HEADS, GQA_GROUP * Q_BLOCK, 1)
    wg, wu, wd = w_gate[l].astype(BF16), w_up[l].astype(BF16), w_down[l].astype(BF16)
    sg, su, sd = ws_gate[l].astype(BF16), ws_up[l].astype(BF16), ws_down[l].astype(BF16)

    def layer(x, batch, t, latent, kx, vx, t1, t2, tm):
        n = batch * t
        x2d = x.reshape(n, D_MODEL)
        tiles = t // tm
        row_fn = (lambda i: 1 + i // tiles) if latent else (lambda i: 0)
        rope_tabs = _rope_tables(t) if latent else None
        outs = _project(x2d, mod, n1, w_in_b, dft_c, rope_tabs, tm, tiles, row_fn)
        wr, wi, q, k, v = outs[:5]
        mixf = _fourier(wr, wi, wf_bd, batch, t, t1, t2)
        q3, k3, v3 = (a.reshape(batch, t, -1) for a in (q, k, v))
        if latent:
            mixa = _attention(q3, k3, v3, kx, vx, sink_col, batch, t, True)
        else:
            mixa = _attention(q3, None, None, k3, v3, sink_col, batch, t, False)
        x1, h2, logits_t = _outproj(x2d, mixf.reshape(n, F_DIM), mixa.reshape(n, ATTN_DIM), wo_f, wo_a,
                                    mod, n2, w_router_t, tm, row_fn)
        chunks = t // MOE_CHUNK if latent else 1
        moe_row = (lambda ci: 1 + ci // chunks) if latent else (lambda ci: 0)
        y = _moe(x1, h2, logits_t, bias_col, wg, wu, wd, sg, su, sd, mod, nf, moe_row)
        return y.reshape(batch, t, D_MODEL), outs[5:]

    y_prompt, (kf, vf) = layer(x_prompt, nb_ctx, t_ctx, False, None, None, 16, 16, 256)
    kx = cache_k[:, l].reshape(nb_lat, -1, KV_DIM).astype(BF16)
    vx = cache_v[:, l].reshape(nb_lat, -1, KV_DIM).astype(BF16)
    y_sample, _ = layer(x_sample, nb_lat, t_lat, True, kx, vx, 64, 64, 512)
    new_k = kf.reshape(nb_ctx, 1, t_ctx, N_KV_HEADS, HEAD_DIM)
    new_v = vf.reshape(nb_ctx, 1, t_ctx, N_KV_HEADS, HEAD_DIM)
    return (y_prompt, y_sample, new_k, new_v)
```

```python
import functools

import numpy as np
import jax
import jax.numpy as jnp
from jax import lax
from jax.experimental import pallas as pl
from jax.experimental.pallas import tpu as pltpu

F32 = jnp.float32
BF16 = jnp.bfloat16

D_MODEL = 1024
GRID_W = 64
HEAD_DIM = 64
N_HEADS = 12
N_KV_HEADS = 4
GQA_GROUP = N_HEADS // N_KV_HEADS
ATTN_DIM = N_HEADS * HEAD_DIM
KV_DIM = N_KV_HEADS * HEAD_DIM
F_GROUPS = 4
F_HD = 64
F_DIM = F_GROUPS * F_HD
IN_DIM = F_DIM + ATTN_DIM + 2 * KV_DIM
WINDOW = 128
Q_BLOCK = 128
ROPE_THETA = 10000.0
N_EXPERTS = 64
TOP_K = 8
N_EXPERT_GROUPS = 8
TOPK_GROUPS = 4
EXPERT_DIM = 256
SHARED_DIM = 256
ROUTED_SCALE = 2.5
EPS = 1e-6

LANES = 128
MOD_ROWS = 8
RUN_ALIGN = 16
PERM_BLOCK = 256
FFN_BLOCK = 512
MOE_CHUNK = 256
PERM_UNROLL = 2
DMA_UNROLL = 8
VMEM_LIMIT = 56 * 1024 * 1024


def _cparams(sem):
    return pltpu.CompilerParams(dimension_semantics=sem, vmem_limit_bytes=VMEM_LIMIT)


def _bdot(a, b):
    return jnp.dot(a.astype(BF16), b.astype(BF16), preferred_element_type=F32)


def _rms(x, g):
    return x * lax.rsqrt(jnp.mean(x * x, axis=-1, keepdims=True) + EPS) * g


def _mod_kernel(c_ref, w_ref, b_ref, o_ref):
    c = c_ref[...]
    a = c * jax.nn.sigmoid(c)
    o_ref[...] = _bdot(a, w_ref[...]) + b_ref[...]


def _modulation(cvec, w_mod, b_mod):
    n = w_mod.shape[1]
    tn = 1024
    return pl.pallas_call(
        _mod_kernel,
        grid=(n // tn,),
        in_specs=[pl.BlockSpec((MOD_ROWS, D_MODEL), lambda j: (0, 0)),
                  pl.BlockSpec((D_MODEL, tn), lambda j: (0, j)),
                  pl.BlockSpec((1, tn), lambda j: (0, j))],
        out_specs=pl.BlockSpec((MOD_ROWS, tn), lambda j: (0, j)),
        out_shape=jax.ShapeDtypeStruct((MOD_ROWS, n), F32),
        compiler_params=_cparams(("parallel",)),
        name="modulation",
    )(cvec, w_mod, b_mod.reshape(1, n))


def _mod_spec(piece, row_fn):
    return pl.BlockSpec((None, None, 1, D_MODEL), lambda *idx: (row_fn(*idx), piece, 0, 0))


def _rope(x, cos, sin_up, sin_dn):
    outs = []
    for j in range(x.shape[1] // LANES):
        xj = x[:, j * LANES:(j + 1) * LANES]
        up = pltpu.roll(xj, LANES - 16, axis=1)
        dn = pltpu.roll(xj, 16, axis=1)
        outs.append(xj * cos + up * sin_up + dn * sin_dn)
    return jnp.concatenate(outs, axis=1)


def _proj_kernel(*refs, latent):
    if latent:
        (x_ref, sh_ref, sc_ref, g_ref, w_ref, dft_ref, cos_ref, sup_ref, sdn_ref,
         wr_ref, wi_ref, q_ref, k_ref, v_ref) = refs
    else:
        (x_ref, sh_ref, sc_ref, g_ref, w_ref, dft_ref,
         wr_ref, wi_ref, q_ref, k_ref, v_ref, kf_ref, vf_ref) = refs
    x = x_ref[...]
    h = _rms(x, g_ref[...]) * (1.0 + sc_ref[...]) + sh_ref[...]
    p = _bdot(h, w_ref[...])
    u = p[:, :F_DIM]
    q = p[:, F_DIM:F_DIM + ATTN_DIM]
    k = p[:, F_DIM + ATTN_DIM:F_DIM + ATTN_DIM + KV_DIM]
    v = p[:, F_DIM + ATTN_DIM + KV_DIM:]
    w = _bdot(u, dft_ref[...])
    wr_ref[...] = w[:, :F_DIM]
    wi_ref[...] = w[:, F_DIM:]
    if latent:
        cos, sup, sdn = cos_ref[...], sup_ref[...], sdn_ref[...]
        q = _rope(q, cos, sup, sdn)
        k = _rope(k, cos, sup, sdn)
    else:
        kf_ref[...] = k
        vf_ref[...] = v
    q_ref[...] = (q * (HEAD_DIM ** -0.5)).astype(BF16)
    k_ref[...] = k.astype(BF16)
    v_ref[...] = v.astype(BF16)


def _project(x2d, mod, norm1, w_in, dft_c, rope_tabs, tm, tiles_per_batch, row_fn):
    n = x2d.shape[0]
    latent = rope_tabs is not None
    tok = lambda w: pl.BlockSpec((tm, w), lambda i: (i, 0))
    full = lambda a: pl.BlockSpec(a.shape, lambda i: (0,) * a.ndim)
    in_specs = [tok(D_MODEL), _mod_spec(0, row_fn), _mod_spec(1, row_fn), full(norm1), full(w_in), full(dft_c)]
    args = [x2d, mod, mod, norm1, w_in, dft_c]
    out_specs = [tok(F_DIM), tok(F_DIM), tok(ATTN_DIM), tok(KV_DIM), tok(KV_DIM)]
    out_shape = [jax.ShapeDtypeStruct((n, F_DIM), F32), jax.ShapeDtypeStruct((n, F_DIM), F32),
                 jax.ShapeDtypeStruct((n, ATTN_DIM), BF16), jax.ShapeDtypeStruct((n, KV_DIM), BF16),
                 jax.ShapeDtypeStruct((n, KV_DIM), BF16)]
    if latent:
        pos = pl.BlockSpec((tm, LANES), lambda i: (i % tiles_per_batch, 0))
        in_specs += [pos, pos, pos]
        args += list(rope_tabs)
    else:
        out_specs += [tok(KV_DIM), tok(KV_DIM)]
        out_shape += [jax.ShapeDtypeStruct((n, KV_DIM), F32), jax.ShapeDtypeStruct((n, KV_DIM), F32)]
    return pl.pallas_call(
        functools.partial(_proj_kernel, latent=latent),
        grid=(n // tm,),
        in_specs=in_specs, out_specs=out_specs, out_shape=out_shape,
        compiler_params=_cparams(("parallel",)),
        name="project_latent" if latent else "project_context",
    )(*args)


def _fourier_kernel(wr_ref, wi_ref, a_ref, b_ref, wf_ref, o_ref, yr_ref, yi_ref, z_ref, *, t1, t2, scale):
    for j in range(t2):
        rows = pl.ds(j, t1, stride=t2)
        xin = jnp.concatenate([wr_ref[rows, :], wi_ref[rows, :]], axis=0)
        y = _bdot(a_ref[j], xin)
        yr_ref[rows, :] = y[:t1]
        yi_ref[rows, :] = y[t1:]
    bm = b_ref[...]
    for k1 in range(t1):
        rows = pl.ds(k1 * t2, t2)
        yin = jnp.concatenate([yr_ref[rows, :], yi_ref[rows, :]], axis=0)
        z_ref[pl.ds(k1, t2, stride=t1), :] = _bdot(bm, yin)
    o_ref[...] = (_bdot(z_ref[...], wf_ref[...]) * scale).astype(BF16)


def _dft_tables(t1, t2):
    t = t1 * t2
    k1 = np.arange(t1)[None, :, None]
    pos = (t2 * np.arange(t1)[None, None, :] + np.arange(t2)[:, None, None])
    ang = 2.0 * np.pi * ((k1 * pos) % t) / t
    c, s = np.cos(ang), np.sin(ang)
    a = np.concatenate([np.concatenate([c, s], axis=2), np.concatenate([-s, c], axis=2)], axis=1)
    ang2 = 2.0 * np.pi * ((np.arange(t2)[:, None] * np.arange(t2)[None, :]) % t2) / t2
    b = np.concatenate([np.cos(ang2), np.sin(ang2)], axis=1)
    return a.astype(np.float32), b.astype(np.float32)


def _fourier(wr, wi, wf_bd, batch, t, t1, t2):
    a_np, b_np = _dft_tables(t1, t2)
    a = jnp.asarray(a_np).astype(BF16)
    b = jnp.asarray(b_np).astype(BF16)
    cw = LANES
    blk = pl.BlockSpec((None, t, cw), lambda bi, ci: (bi, 0, ci))
    return pl.pallas_call(
        functools.partial(_fourier_kernel, t1=t1, t2=t2, scale=float((t * F_HD) ** -0.5)),
        grid=(batch, F_DIM // cw),
        in_specs=[blk, blk,
                  pl.BlockSpec(a.shape, lambda bi, ci: (0, 0, 0)),
                  pl.BlockSpec(b.shape, lambda bi, ci: (0, 0)),
                  pl.BlockSpec((None, cw, cw), lambda bi, ci: (ci, 0, 0))],
        out_specs=blk,
        out_shape=jax.ShapeDtypeStruct((batch, t, F_DIM), BF16),
        scratch_shapes=[pltpu.VMEM((t, cw), F32), pltpu.VMEM((t, cw), F32), pltpu.VMEM((t, cw), F32)],
        compiler_params=_cparams(("parallel", "parallel")),
        name="fourier_%d" % t,
    )(wr.reshape(batch, t, F_DIM), wi.reshape(batch, t, F_DIM), a, b, wf_bd)


def _attn_kernel(*refs, windowed, n_blocks):
    if windowed:
        q_ref, kp_ref, kc_ref, kn_ref, vp_ref, vc_ref, vn_ref, kx_ref, vx_ref, sink_ref, o_ref = refs
    else:
        q_ref, kx_ref, vx_ref, sink_ref, o_ref = refs
    nt = (((1,), (1,)), ((), ()))
    parts = [(kx_ref, vx_ref, None)]
    if windowed:
        i = pl.program_id(1)
        a = lax.broadcasted_iota(jnp.int32, (Q_BLOCK, Q_BLOCK), 0)
        j = lax.broadcasted_iota(jnp.int32, (Q_BLOCK, Q_BLOCK), 1)
        prev_ok = (j >= a) & (i > 0)
        next_ok = (j <= a) & (i < n_blocks - 1)
        parts = [(kp_ref, vp_ref, prev_ok), (kc_ref, vc_ref, None), (kn_ref, vn_ref, next_ok)] + parts
    stack = GQA_GROUP if windowed else 1
    outs = []
    for h0 in range(0, N_HEADS, stack):
        heads = range(h0, h0 + stack)
        kv = h0 // GQA_GROUP
        hs = slice(kv * HEAD_DIM, (kv + 1) * HEAD_DIM)
        qs = jnp.concatenate([q_ref[:, h * HEAD_DIM:(h + 1) * HEAD_DIM] for h in heads], axis=0)
        scores = []
        for k_ref, _, ok in parts:
            s = lax.dot_general(qs, k_ref[:, hs], nt, preferred_element_type=F32)
            if ok is not None:
                s = jnp.where(jnp.concatenate([ok] * stack, axis=0), s, -jnp.inf)
            scores.append(s)
        sk = jnp.concatenate([jnp.full((Q_BLOCK, 1), sink_ref[h], F32) for h in heads], axis=0)
        slabs = [s[:, c:c + LANES] for s in scores for c in range(0, s.shape[1], LANES)]
        m = jnp.maximum(jnp.max(functools.reduce(jnp.maximum, slabs), axis=1, keepdims=True), sk)
        probs = [jnp.exp(s - m) for s in scores]
        pslabs = [p[:, c:c + LANES] for p in probs for c in range(0, p.shape[1], LANES)]
        inv = 1.0 / (jnp.sum(functools.reduce(jnp.add, pslabs), axis=1, keepdims=True) + jnp.exp(sk - m))
        o = functools.reduce(jnp.add, [jnp.dot(p.astype(BF16), v_ref[:, hs], preferred_element_type=F32)
                                       for p, (_, v_ref, _) in zip(probs, parts)]) * inv
        outs += [o[g * Q_BLOCK:(g + 1) * Q_BLOCK] for g in range(stack)]
    o_ref[...] = jnp.concatenate(outs, axis=1).astype(BF16)


def _attention(q, k, v, kx, vx, sink_col, batch, t, windowed):
    nb = t // Q_BLOCK
    n_ctx = kx.shape[1]
    qspec = pl.BlockSpec((None, Q_BLOCK, ATTN_DIM), lambda b, i: (b, i, 0))
    xspec = pl.BlockSpec((None, n_ctx, KV_DIM), lambda b, i: (b, 0, 0))
    sspec = pl.BlockSpec(memory_space=pltpu.SMEM)
    if windowed:
        prev = pl.BlockSpec((None, Q_BLOCK, KV_DIM), lambda b, i: (b, jnp.maximum(i - 1, 0), 0))
        cur = pl.BlockSpec((None, Q_BLOCK, KV_DIM), lambda b, i: (b, i, 0))
        nxt = pl.BlockSpec((None, Q_BLOCK, KV_DIM), lambda b, i: (b, jnp.minimum(i + 1, nb - 1), 0))
        in_specs = [qspec, prev, cur, nxt, prev, cur, nxt, xspec, xspec, sspec]
        args = [q, k, k, k, v, v, v, kx, vx, sink_col]
    else:
        in_specs = [qspec, xspec, xspec, sspec]
        args = [q, kx, vx, sink_col]
    return pl.pallas_call(
        functools.partial(_attn_kernel, windowed=windowed, n_blocks=nb),
        grid=(batch, nb),
        in_specs=in_specs, out_specs=qspec,
        out_shape=jax.ShapeDtypeStruct((batch, t, ATTN_DIM), BF16),
        compiler_params=_cparams(("parallel", "parallel")),
        name="attention_latent" if windowed else "attention_context",
    )(*args)


def _outproj_kernel(x_ref, mf_ref, ma_ref, wof_ref, woa_ref, g1_ref, sh_ref, sc_ref, n2_ref, wr_ref,
                    x1_ref, h_ref, lg_ref):
    o = jnp.dot(mf_ref[...], wof_ref[...], preferred_element_type=F32)
    o = o + jnp.dot(ma_ref[...], woa_ref[...], preferred_element_type=F32)
    x1 = x_ref[...] + g1_ref[...] * o
    x1_ref[...] = x1
    h = (_rms(x1, n2_ref[...]) * (1.0 + sc_ref[...]) + sh_ref[...]).astype(BF16)
    h_ref[...] = h
    lg_ref[...] = lax.dot_general(wr_ref[...], h, (((1,), (1,)), ((), ())), preferred_element_type=F32)


def _outproj(x2d, mixf, mixa, wo_f, wo_a, mod, norm2, w_router_t, tm, row_fn):
    n = x2d.shape[0]
    tok = lambda w: pl.BlockSpec((tm, w), lambda i: (i, 0))
    full = lambda a: pl.BlockSpec(a.shape, lambda i: (0,) * a.ndim)
    return pl.pallas_call(
        _outproj_kernel,
        grid=(n // tm,),
        in_specs=[tok(D_MODEL), tok(F_DIM), tok(ATTN_DIM), full(wo_f), full(wo_a),
                  _mod_spec(2, row_fn), _mod_spec(3, row_fn), _mod_spec(4, row_fn), full(norm2), full(w_router_t)],
        out_specs=[tok(D_MODEL), tok(D_MODEL), pl.BlockSpec((N_EXPERTS, tm), lambda i: (0, i))],
        out_shape=[jax.ShapeDtypeStruct((n, D_MODEL), F32), jax.ShapeDtypeStruct((n, D_MODEL), BF16),
                   jax.ShapeDtypeStruct((N_EXPERTS, n), F32)],
        compiler_params=_cparams(("parallel",)),
        name="outproj",
    )(x2d, mixf, mixa, wo_f, wo_a, mod, mod, mod, norm2, w_router_t)


def _rank_before(vals, n):
    idx = lax.broadcasted_iota(jnp.int32, vals.shape, 0)
    cnt = jnp.zeros(vals.shape, jnp.int32)
    for r in range(n):
        row = vals[r:r + 1, :]
        ge = jnp.where(row >= vals, 1, 0)
        gt = jnp.where(row > vals, 1, 0)
        cnt = cnt + jnp.where(idx > r, ge, gt)
    return cnt


def _route_kernel(lg_ref, bias_ref, lo_ref, up_ref, pos_ref, gs_ref, cnt_ref):
    s = jax.nn.sigmoid(lg_ref[...])
    sc = s + bias_ref[...]
    tn = s.shape[1]
    per = N_EXPERTS // N_EXPERT_GROUPS
    g3 = sc.reshape(N_EXPERT_GROUPS, per, tn)
    member = lax.broadcasted_iota(jnp.int32, g3.shape, 1)
    m1 = jnp.max(g3, axis=1, keepdims=True)
    first = jnp.min(jnp.where(g3 == m1, member, per), axis=1, keepdims=True)
    m2 = jnp.max(jnp.where(member == first, -jnp.inf, g3), axis=1, keepdims=True)
    gscore = (m1 + m2).reshape(N_EXPERT_GROUPS, tn)
    gsel = _rank_before(gscore, N_EXPERT_GROUPS) < TOPK_GROUPS
    emask = jnp.broadcast_to(gsel.reshape(N_EXPERT_GROUPS, 1, tn), g3.shape).reshape(N_EXPERTS, tn)
    masked = jnp.where(emask, sc, -jnp.inf)
    sel = _rank_before(masked, N_EXPERTS) < TOP_K
    w = jnp.where(sel, s, 0.0)
    gate = w / jnp.sum(w, axis=0, keepdims=True) * ROUTED_SCALE

    self = jnp.where(sel, 1.0, 0.0)
    selb = self.astype(BF16)
    cnt = jnp.sum(self, axis=1, keepdims=True)
    pad = jnp.maximum(jnp.floor((cnt + (RUN_ALIGN - 1)) * (1.0 / RUN_ALIGN)), 1.0) * RUN_ALIGN
    lo = lo_ref[...]
    soff = jnp.dot(lo, jnp.broadcast_to(pad, (N_EXPERTS, LANES)).astype(BF16), preferred_element_type=F32)[:, :1]
    rank = jnp.dot(selb, up_ref[...], preferred_element_type=F32)
    kidx = jnp.dot(lo, selb, preferred_element_type=F32)
    pos_e = jnp.where(sel, soff + rank, 0.0)
    rows_p, rows_g = [], []
    for k in range(TOP_K):
        m = kidx == k
        rows_p.append(jnp.sum(jnp.where(m, pos_e, 0.0), axis=0, keepdims=True))
        rows_g.append(jnp.sum(jnp.where(m, gate, 0.0), axis=0, keepdims=True))
    pos_ref[...] = jnp.concatenate(rows_p, axis=0).astype(jnp.int32)
    gs_ref[...] = jnp.concatenate(rows_g, axis=0)
    cnt_ref[...] = pad.astype(jnp.int32)


def _route(logits_t, bias_col, tc):
    n = logits_t.shape[1]
    lo = jnp.asarray(np.tril(np.ones((N_EXPERTS, N_EXPERTS), np.float32), -1)).astype(BF16)
    up = jnp.asarray(np.triu(np.ones((tc, tc), np.float32), 1)).astype(BF16)
    slot = pl.BlockSpec((TOP_K, tc), lambda i: (0, i))
    return pl.pallas_call(
        _route_kernel,
        grid=(n // tc,),
        in_specs=[pl.BlockSpec((N_EXPERTS, tc), lambda i: (0, i)),
                  pl.BlockSpec((N_EXPERTS, 1), lambda i: (0, 0)),
                  pl.BlockSpec(lo.shape, lambda i: (0, 0)),
                  pl.BlockSpec(up.shape, lambda i: (0, 0))],
        out_specs=[slot, slot, pl.BlockSpec((None, N_EXPERTS, 1), lambda i: (i, 0, 0))],
        out_shape=[jax.ShapeDtypeStruct((TOP_K, n), jnp.int32), jax.ShapeDtypeStruct((TOP_K, n), F32),
                   jax.ShapeDtypeStruct((n // tc, N_EXPERTS, 1), jnp.int32)],
        compiler_params=_cparams(("parallel",)),
        name="route",
    )(logits_t, bias_col, lo, up)


def _aligned(x):
    return pl.multiple_of(x, RUN_ALIGN)


def _gather_kernel(src_ref, cnt_ref, dst_ref, tot_ref, tail0_ref, tailn_ref,
                   pos_ref, x_ref, xs_ref, z_ref, zero_ref, sem, *, tc, n_chunks):
    c = pl.program_id(0)
    slot = c % 2

    def wait_chunk(ci, s):
        n = _aligned(tot_ref[ci])
        pltpu.make_async_copy(z_ref.at[s, pl.ds(0, n)], xs_ref.at[pl.ds(0, n)], sem.at[s]).wait()

    def tail_copy(e):
        n = _aligned(tailn_ref[e])
        return n, pltpu.make_async_copy(zero_ref.at[pl.ds(0, n)], xs_ref.at[pl.ds(_aligned(tail0_ref[e]), n)],
                                        sem.at[2])

    @pl.when(c == 0)
    def _():
        zero_ref[...] = jnp.zeros(zero_ref.shape, zero_ref.dtype)

        def body(e, carry):
            n, cp = tail_copy(e)

            @pl.when(n > 0)
            def _():
                cp.start()
            return carry
        lax.fori_loop(0, N_EXPERTS, body, 0)

    @pl.when(c >= 2)
    def _():
        wait_chunk(c - 2, slot)

    def rb_body(rb2, carry):
        for half in range(PERM_UNROLL):
            r0 = pl.multiple_of((rb2 * PERM_UNROLL + half) * PERM_BLOCK, PERM_BLOCK)
            rows = r0 + lax.broadcasted_iota(jnp.int32, (PERM_BLOCK, tc), 0)
            onehot = jnp.zeros((PERM_BLOCK, tc), F32)
            for k in range(TOP_K):
                onehot = jnp.where(pos_ref[k:k + 1, :] == rows, 1.0, onehot)
            z_ref[slot, pl.ds(r0, PERM_BLOCK), :] = jnp.dot(
                onehot.astype(BF16), x_ref[...], preferred_element_type=F32).astype(BF16)
        return carry
    lax.fori_loop(0, _perm_trips(tot_ref[c]), rb_body, 0)

    def e_body(e8, carry):
        for j in range(DMA_UNROLL):
            i = c * N_EXPERTS + e8 * DMA_UNROLL + j
            n = _aligned(cnt_ref[i])
            pltpu.make_async_copy(z_ref.at[slot, pl.ds(_aligned(src_ref[i]), n)],
                                  xs_ref.at[pl.ds(_aligned(dst_ref[i]), n)], sem.at[slot]).start()
        return carry
    lax.fori_loop(0, N_EXPERTS // DMA_UNROLL, e_body, 0)

    @pl.when(c == n_chunks - 1)
    def _():
        wait_chunk(c, slot)
        if n_chunks > 1:
            wait_chunk(c - 1, 1 - slot)

        def body(e, carry):
            n, cp = tail_copy(e)

            @pl.when(n > 0)
            def _():
                cp.wait()
            return carry
        lax.fori_loop(0, N_EXPERTS, body, 0)


def _perm_trips(rows):
    step = PERM_BLOCK * PERM_UNROLL
    return (rows + step - 1) // step


def _chunk_rows_max(tc):
    return TOP_K * tc + N_EXPERTS * RUN_ALIGN


def _local_cap(tc):
    step = PERM_BLOCK * PERM_UNROLL
    return -(-_chunk_rows_max(tc) // step) * step


def _gather(tabs, pos, h2, r_max, tc):
    n = h2.shape[0]
    n_chunks = n // tc
    return pl.pallas_call(
        functools.partial(_gather_kernel, tc=tc, n_chunks=n_chunks),
        grid_spec=pltpu.PrefetchScalarGridSpec(
            num_scalar_prefetch=6, grid=(n_chunks,),
            in_specs=[pl.BlockSpec((TOP_K, tc), lambda c, *_: (0, c)),
                      pl.BlockSpec((tc, D_MODEL), lambda c, *_: (c, 0))],
            out_specs=pl.BlockSpec(memory_space=pl.ANY),
            scratch_shapes=[pltpu.VMEM((2, _local_cap(tc), D_MODEL), BF16),
                            pltpu.VMEM((FFN_BLOCK, D_MODEL), BF16),
                            pltpu.SemaphoreType.DMA((3,))]),
        out_shape=jax.ShapeDtypeStruct((r_max, D_MODEL), BF16),
        compiler_params=_cparams(("arbitrary",)),
        name="moe_gather",
    )(tabs["src"], tabs["cnt"], tabs["dst"], tabs["tot"], tabs["tail0"], tabs["tailn"], pos, h2)


def _ffn_kernel(be_ref, nb_ref, x_ref, wg_ref, wu_ref, wd_ref, y_ref):
    @pl.when(pl.program_id(0) < nb_ref[0])
    def _():
        x = x_ref[...]
        a = jnp.dot(x, wg_ref[...], preferred_element_type=F32)
        u = jnp.dot(x, wu_ref[...], preferred_element_type=F32)
        h = (a * jax.nn.sigmoid(a) * u).astype(BF16)
        y_ref[...] = jnp.dot(h, wd_ref[...], preferred_element_type=F32).astype(BF16)


def _ffn(tabs, xs, wg, wu, wd):
    r_max = xs.shape[0]
    rows = pl.BlockSpec((FFN_BLOCK, D_MODEL), lambda b, be, nb: (jnp.minimum(b, nb[0] - 1), 0))
    ew = lambda a: pl.BlockSpec((None,) + a.shape[1:], lambda b, be, nb: (be[b], 0, 0))
    return pl.pallas_call(
        _ffn_kernel,
        grid_spec=pltpu.PrefetchScalarGridSpec(
            num_scalar_prefetch=2, grid=(r_max // FFN_BLOCK,),
            in_specs=[rows, ew(wg), ew(wu), ew(wd)],
            out_specs=rows),
        out_shape=jax.ShapeDtypeStruct((r_max, D_MODEL), BF16),
        compiler_params=_cparams(("arbitrary",)),
        name="moe_ffn",
    )(tabs["blk_e"], tabs["nblk"], xs, wg, wu, wd)


def _combine_kernel(src_ref, cnt_ref, dst_ref, tot_ref,
                    pos_ref, gs_ref, x1_ref, h_ref, sg_ref, su_ref, sd_ref, g2_ref, nf_ref, ys_ref,
                    y_ref, ybuf, acc_ref, sem, *, tc, n_chunks):
    c = pl.program_id(0)
    slot = c % 2

    def issue(ci, s):
        def e_body(e8, carry):
            for j in range(DMA_UNROLL):
                i = ci * N_EXPERTS + e8 * DMA_UNROLL + j
                n = _aligned(cnt_ref[i])
                pltpu.make_async_copy(ys_ref.at[pl.ds(_aligned(dst_ref[i]), n)],
                                      ybuf.at[s, pl.ds(_aligned(src_ref[i]), n)], sem.at[s]).start()
            return carry
        lax.fori_loop(0, N_EXPERTS // DMA_UNROLL, e_body, 0)

    @pl.when(c == 0)
    def _():
        ybuf[...] = jnp.zeros(ybuf.shape, ybuf.dtype)
        issue(0, 0)

    @pl.when(c + 1 < n_chunks)
    def _():
        issue(c + 1, 1 - slot)

    h = h_ref[...]
    a = jnp.dot(h, sg_ref[...], preferred_element_type=F32)
    u = jnp.dot(h, su_ref[...], preferred_element_type=F32)
    acc_ref[...] = jnp.dot((a * jax.nn.sigmoid(a) * u).astype(BF16), sd_ref[...], preferred_element_type=F32)

    n = _aligned(tot_ref[c])
    pltpu.make_async_copy(ys_ref.at[pl.ds(0, n)], ybuf.at[slot, pl.ds(0, n)], sem.at[slot]).wait()

    tn = (((0,), (0,)), ((), ()))

    def rb_body(rb2, carry):
        part = None
        for half in range(PERM_UNROLL):
            r0 = pl.multiple_of((rb2 * PERM_UNROLL + half) * PERM_BLOCK, PERM_BLOCK)
            rows = r0 + lax.broadcasted_iota(jnp.int32, (PERM_BLOCK, tc), 0)
            wt = jnp.zeros((PERM_BLOCK, tc), F32)
            for k in range(TOP_K):
                wt = jnp.where(pos_ref[k:k + 1, :] == rows, gs_ref[k:k + 1, :], wt)
            d = lax.dot_general(wt.astype(BF16), ybuf[slot, pl.ds(r0, PERM_BLOCK), :], tn,
                                preferred_element_type=F32)
            part = d if part is None else part + d
        acc_ref[...] += part
        return carry
    lax.fori_loop(0, _perm_trips(tot_ref[c]), rb_body, 0)

    x2 = x1_ref[...] + g2_ref[...] * acc_ref[...]
    y_ref[...] = _rms(x2, nf_ref[...])


def _combine(tabs, pos, gslot, x1, h2, sg, su, sd, mod, norm_f, ys, tc, row_fn):
    n = x1.shape[0]
    n_chunks = n // tc
    tok = lambda w: pl.BlockSpec((tc, w), lambda c, *_: (c, 0))
    slot = pl.BlockSpec((TOP_K, tc), lambda c, *_: (0, c))
    full = lambda a: pl.BlockSpec(a.shape, lambda c, *_: (0,) * a.ndim)
    return pl.pallas_call(
        functools.partial(_combine_kernel, tc=tc, n_chunks=n_chunks),
        grid_spec=pltpu.PrefetchScalarGridSpec(
            num_scalar_prefetch=4, grid=(n_chunks,),
            in_specs=[slot, slot, tok(D_MODEL), tok(D_MODEL), full(sg), full(su), full(sd),
                      _mod_spec(5, lambda c, *_: row_fn(c)), full(norm_f), pl.BlockSpec(memory_space=pl.ANY)],
            out_specs=tok(D_MODEL),
            scratch_shapes=[pltpu.VMEM((2, _local_cap(tc), D_MODEL), BF16),
                            pltpu.VMEM((tc, D_MODEL), F32),
                            pltpu.SemaphoreType.DMA((2,))]),
        out_shape=jax.ShapeDtypeStruct((n, D_MODEL), F32),
        compiler_params=_cparams(("arbitrary",)),
        name="moe_combine",
    )(tabs["src"], tabs["cnt"], tabs["dst"], tabs["tot"], pos, gslot, x1, h2, sg, su, sd, mod, norm_f, ys)


def _dispatch_tables(pad_cnt, n_blocks_max):
    i32 = jnp.int32
    src = jnp.cumsum(pad_cnt, axis=1) - pad_cnt
    tot = jnp.sum(pad_cnt, axis=1)
    used = jnp.sum(pad_cnt, axis=0)
    region = (used + FFN_BLOCK - 1) // FFN_BLOCK * FFN_BLOCK
    region_end = jnp.cumsum(region)
    e_off = region_end - region
    dst = e_off[None, :] + jnp.cumsum(pad_cnt, axis=0) - pad_cnt
    nblk = region_end[-1] // FFN_BLOCK
    ends = region_end // FFN_BLOCK
    blk_e = jnp.minimum(jnp.sum(ends[None, :] <= jnp.arange(n_blocks_max)[:, None], axis=1), N_EXPERTS - 1)
    return {"src": src.reshape(-1).astype(i32), "cnt": pad_cnt.reshape(-1).astype(i32),
            "dst": dst.reshape(-1).astype(i32), "tot": tot.astype(i32),
            "tail0": (e_off + used).astype(i32), "tailn": (region - used).astype(i32),
            "blk_e": blk_e.astype(i32), "nblk": nblk.reshape(1).astype(i32)}


def _moe(x1, h2, logits_t, bias_col, wg, wu, wd, sg, su, sd, mod, norm_f, row_fn):
    n = x1.shape[0]
    tc = MOE_CHUNK
    pos, gslot, pad_cnt = _route(logits_t, bias_col, tc)
    worst_rows = (n // tc) * _chunk_rows_max(tc) + N_EXPERTS * (FFN_BLOCK - RUN_ALIGN)
    n_blocks_max = -(-worst_rows // FFN_BLOCK)
    tabs = _dispatch_tables(pad_cnt[:, :, 0], n_blocks_max)
    xs = _gather(tabs, pos, h2, n_blocks_max * FFN_BLOCK, tc)
    ys = _ffn(tabs, xs, wg, wu, wd)
    return _combine(tabs, pos, gslot, x1, h2, sg, su, sd, mod, norm_f, ys, tc, row_fn)


def _channel_dft():
    ang = 2.0 * np.pi * ((np.arange(F_HD)[:, None] * np.arange(F_HD)[None, :]) % F_HD) / F_HD
    eye = np.eye(F_GROUPS)
    return np.concatenate([np.kron(eye, np.cos(ang)), np.kron(eye, -np.sin(ang))], axis=1).astype(np.float32)


def _rope_tables(t):
    pos = np.arange(t)
    row, col = pos // GRID_W, pos % GRID_W
    n_freq = HEAD_DIM // 4
    inv = ROPE_THETA ** (-np.arange(n_freq, dtype=np.float64) / n_freq)
    lane = np.arange(LANES)
    hd = lane % HEAD_DIM
    within = hd % (HEAD_DIM // 2)
    freq = within % n_freq
    first = within < n_freq
    p = np.where((hd < HEAD_DIM // 2)[None, :], row[:, None], col[:, None]).astype(np.float64)
    ang = p * inv[freq][None, :]
    cos, sin = np.cos(ang), np.sin(ang)
    sin_up = np.where(first[None, :], -sin, 0.0)
    sin_dn = np.where(first[None, :], 0.0, sin)
    return [jnp.asarray(a.astype(np.float32)) for a in (cos, sin_up, sin_dn)]


def _block_diag_pairs(wf):
    per = LANES // F_HD
    z = jnp.zeros((F_HD, F_HD), wf.dtype)
    blocks = []
    for c in range(F_GROUPS // per):
        rows = [jnp.concatenate([wf[c * per + r] if r == cc else z for cc in range(per)], axis=1) for r in range(per)]
        blocks.append(jnp.concatenate(rows, axis=0))
    return jnp.stack(blocks)


def kernel(x_prompt, x_sample, cache_k, cache_v, c, c_ctx, w_mod, b_mod, norm1, w_in, w_fourier, sink,
           w_out, norm2, w_router, router_bias, w_gate, w_up, w_down, ws_gate, ws_up, ws_down, norm_f):
    nb_ctx, t_ctx, _ = x_prompt.shape
    nb_lat, t_lat, _ = x_sample.shape
    l = 0
    cvec = jnp.concatenate([c_ctx[None, :], c, jnp.zeros((MOD_ROWS - 1 - nb_lat, D_MODEL), F32)], axis=0)
    mod = _modulation(cvec, w_mod[l], b_mod[l]).reshape(MOD_ROWS, 6, 1, D_MODEL)

    w_in_b = w_in[l].astype(BF16)
    dft_c = jnp.asarray(_channel_dft()).astype(BF16)
    wf_bd = _block_diag_pairs(w_fourier[l]).astype(BF16)
    wo_f = w_out[l][:F_DIM].astype(BF16)
    wo_a = w_out[l][F_DIM:].astype(BF16)
    w_router_t = w_router[l].T.astype(BF16)
    bias_col = router_bias[l].reshape(N_EXPERTS, 1)
    n1 = norm1[l].reshape(1, D_MODEL)
    n2 = norm2[l].reshape(1, D_MODEL)
    nf = norm_f.reshape(1, D_MODEL)
    sink_col = sink[l]
    wg, wu, wd = w_gate[l].astype(BF16), w_up[l].astype(BF16), w_down[l].astype(BF16)
    sg, su, sd = ws_gate[l].astype(BF16), ws_up[l].astype(BF16), ws_down[l].astype(BF16)

    def layer(x, batch, t, latent, kx, vx, t1, t2, tm):
        n = batch * t
        x2d = x.reshape(n, D_MODEL)
        tiles = t // tm
        row_fn = (lambda i: 1 + i // tiles) if latent else (lambda i: 0)
        rope_tabs = _rope_tables(t) if latent else None
        outs = _project(x2d, mod, n1, w_in_b, dft_c, rope_tabs, tm, tiles, row_fn)
        wr, wi, q, k, v = outs[:5]
        mixf = _fourier(wr, wi, wf_bd, batch, t, t1, t2)
        q3, k3, v3 = (a.reshape(batch, t, -1) for a in (q, k, v))
        if latent:
            mixa = _attention(q3, k3, v3, kx, vx, sink_col, batch, t, True)
        else:
            mixa = _attention(q3, None, None, k3, v3, sink_col, batch, t, False)
        x1, h2, logits_t = _outproj(x2d, mixf.reshape(n, F_DIM), mixa.reshape(n, ATTN_DIM), wo_f, wo_a,
                                    mod, n2, w_router_t, tm, row_fn)
        chunks = t // MOE_CHUNK if latent else 1
        moe_row = (lambda ci: 1 + ci // chunks) if latent else (lambda ci: 0)
        y = _moe(x1, h2, logits_t, bias_col, wg, wu, wd, sg, su, sd, mod, nf, moe_row)
        return y.reshape(batch, t, D_MODEL), outs[5:]

    y_prompt, (kf, vf) = layer(x_prompt, nb_ctx, t_ctx, False, None, None, 16, 16, 256)
    kx = cache_k[:, l].reshape(nb_lat, -1, KV_DIM).astype(BF16)
    vx = cache_v[:, l].reshape(nb_lat, -1, KV_DIM).astype(BF16)
    y_sample, _ = layer(x_sample, nb_lat, t_lat, True, kx, vx, 64, 64, 512)
    new_k = kf.reshape(nb_ctx, 1, t_ctx, N_KV_HEADS, HEAD_DIM)
    new_v = vf.reshape(nb_ctx, 1, t_ctx, N_KV_HEADS, HEAD_DIM)
    return (y_prompt, y_sample, new_k, new_v)
```

```python
import functools

import numpy as np
import jax
import jax.numpy as jnp
from jax import lax
from jax.experimental import pallas as pl
from jax.experimental.pallas import tpu as pltpu

F32 = jnp.float32
BF16 = jnp.bfloat16

D_MODEL = 1024
GRID_W = 64
HEAD_DIM = 64
N_HEADS = 12
N_KV_HEADS = 4
GQA_GROUP = N_HEADS // N_KV_HEADS
ATTN_DIM = N_HEADS * HEAD_DIM
KV_DIM = N_KV_HEADS * HEAD_DIM
F_GROUPS = 4
F_HD = 64
F_DIM = F_GROUPS * F_HD
IN_DIM = F_DIM + ATTN_DIM + 2 * KV_DIM
WINDOW = 128
Q_BLOCK = 128
ROPE_THETA = 10000.0
N_EXPERTS = 64
TOP_K = 8
N_EXPERT_GROUPS = 8
TOPK_GROUPS = 4
EXPERT_DIM = 256
SHARED_DIM = 256
ROUTED_SCALE = 2.5
EPS = 1e-6

LANES = 128
MOD_ROWS = 8
RUN_ALIGN = 16
PERM_BLOCK = 256
FFN_BLOCK = 512
MOE_CHUNK = 256
PERM_UNROLL = 2
DMA_UNROLL = 8
VMEM_LIMIT = 56 * 1024 * 1024


def _cparams(sem):
    return pltpu.CompilerParams(dimension_semantics=sem, vmem_limit_bytes=VMEM_LIMIT)


def _bdot(a, b):
    return jnp.dot(a.astype(BF16), b.astype(BF16), preferred_element_type=F32)


def _rms(x, g):
    return x * lax.rsqrt(jnp.mean(x * x, axis=-1, keepdims=True) + EPS) * g


def _mod_kernel(c_ref, w_ref, b_ref, o_ref):
    c = c_ref[...]
    a = c * jax.nn.sigmoid(c)
    o_ref[...] = _bdot(a, w_ref[...]) + b_ref[...]


def _modulation(cvec, w_mod, b_mod):
    n = w_mod.shape[1]
    tn = 1024
    return pl.pallas_call(
        _mod_kernel,
        grid=(n // tn,),
        in_specs=[pl.BlockSpec((MOD_ROWS, D_MODEL), lambda j: (0, 0)),
                  pl.BlockSpec((D_MODEL, tn), lambda j: (0, j)),
                  pl.BlockSpec((1, tn), lambda j: (0, j))],
        out_specs=pl.BlockSpec((MOD_ROWS, tn), lambda j: (0, j)),
        out_shape=jax.ShapeDtypeStruct((MOD_ROWS, n), F32),
        compiler_params=_cparams(("parallel",)),
        name="modulation",
    )(cvec, w_mod, b_mod.reshape(1, n))


def _mod_spec(piece, row_fn):
    return pl.BlockSpec((None, None, 1, D_MODEL), lambda *idx: (row_fn(*idx), piece, 0, 0))


def _rope(x, cos, sin_up, sin_dn):
    outs = []
    for j in range(x.shape[1] // LANES):
        xj = x[:, j * LANES:(j + 1) * LANES]
        up = pltpu.roll(xj, LANES - 16, axis=1)
        dn = pltpu.roll(xj, 16, axis=1)
        outs.append(xj * cos + up * sin_up + dn * sin_dn)
    return jnp.concatenate(outs, axis=1)


def _proj_kernel(*refs, latent):
    if latent:
        (x_ref, sh_ref, sc_ref, g_ref, w_ref, dft_ref, cos_ref, sup_ref, sdn_ref,
         wr_ref, wi_ref, q_ref, k_ref, v_ref) = refs
    else:
        (x_ref, sh_ref, sc_ref, g_ref, w_ref, dft_ref,
         wr_ref, wi_ref, q_ref, k_ref, v_ref, kf_ref, vf_ref) = refs
    x = x_ref[...]
    h = _rms(x, g_ref[...]) * (1.0 + sc_ref[...]) + sh_ref[...]
    p = _bdot(h, w_ref[...])
    u = p[:, :F_DIM]
    q = p[:, F_DIM:F_DIM + ATTN_DIM]
    k = p[:, F_DIM + ATTN_DIM:F_DIM + ATTN_DIM + KV_DIM]
    v = p[:, F_DIM + ATTN_DIM + KV_DIM:]
    w = _bdot(u, dft_ref[...])
    wr_ref[...] = w[:, :F_DIM]
    wi_ref[...] = w[:, F_DIM:]
    if latent:
        cos, sup, sdn = cos_ref[...], sup_ref[...], sdn_ref[...]
        q = _rope(q, cos, sup, sdn)
        k = _rope(k, cos, sup, sdn)
    else:
        kf_ref[...] = k
        vf_ref[...] = v
    q_ref[...] = (q * (HEAD_DIM ** -0.5)).astype(BF16)
    k_ref[...] = k.astype(BF16)
    v_ref[...] = v.astype(BF16)


def _project(x2d, mod, norm1, w_in, dft_c, rope_tabs, tm, tiles_per_batch, row_fn):
    n = x2d.shape[0]
    latent = rope_tabs is not None
    tok = lambda w: pl.BlockSpec((tm, w), lambda i: (i, 0))
    full = lambda a: pl.BlockSpec(a.shape, lambda i: (0,) * a.ndim)
    in_specs = [tok(D_MODEL), _mod_spec(0, row_fn), _mod_spec(1, row_fn), full(norm1), full(w_in), full(dft_c)]
    args = [x2d, mod, mod, norm1, w_in, dft_c]
    out_specs = [tok(F_DIM), tok(F_DIM), tok(ATTN_DIM), tok(KV_DIM), tok(KV_DIM)]
    out_shape = [jax.ShapeDtypeStruct((n, F_DIM), F32), jax.ShapeDtypeStruct((n, F_DIM), F32),
                 jax.ShapeDtypeStruct((n, ATTN_DIM), BF16), jax.ShapeDtypeStruct((n, KV_DIM), BF16),
                 jax.ShapeDtypeStruct((n, KV_DIM), BF16)]
    if latent:
        pos = pl.BlockSpec((tm, LANES), lambda i: (i % tiles_per_batch, 0))
        in_specs += [pos, pos, pos]
        args += list(rope_tabs)
    else:
        out_specs += [tok(KV_DIM), tok(KV_DIM)]
        out_shape += [jax.ShapeDtypeStruct((n, KV_DIM), F32), jax.ShapeDtypeStruct((n, KV_DIM), F32)]
    return pl.pallas_call(
        functools.partial(_proj_kernel, latent=latent),
        grid=(n // tm,),
        in_specs=in_specs, out_specs=out_specs, out_shape=out_shape,
        compiler_params=_cparams(("parallel",)),
        name="project_latent" if latent else "project_context",
    )(*args)


def _fourier_kernel(wr_ref, wi_ref, a_ref, b_ref, wf_ref, o_ref, yr_ref, yi_ref, z_ref, *, t1, t2, scale):
    for j in range(t2):
        rows = pl.ds(j, t1, stride=t2)
        xin = jnp.concatenate([wr_ref[rows, :], wi_ref[rows, :]], axis=0)
        y = _bdot(a_ref[j], xin)
        yr_ref[rows, :] = y[:t1]
        yi_ref[rows, :] = y[t1:]
    bm = b_ref[...]
    for k1 in range(t1):
        rows = pl.ds(k1 * t2, t2)
        yin = jnp.concatenate([yr_ref[rows, :], yi_ref[rows, :]], axis=0)
        z_ref[pl.ds(k1, t2, stride=t1), :] = _bdot(bm, yin)
    o_ref[...] = (_bdot(z_ref[...], wf_ref[...]) * scale).astype(BF16)


def _dft_tables(t1, t2):
    t = t1 * t2
    k1 = np.arange(t1)[None, :, None]
    pos = (t2 * np.arange(t1)[None, None, :] + np.arange(t2)[:, None, None])
    ang = 2.0 * np.pi * ((k1 * pos) % t) / t
    c, s = np.cos(ang), np.sin(ang)
    a = np.concatenate([np.concatenate([c, s], axis=2), np.concatenate([-s, c], axis=2)], axis=1)
    ang2 = 2.0 * np.pi * ((np.arange(t2)[:, None] * np.arange(t2)[None, :]) % t2) / t2
    b = np.concatenate([np.cos(ang2), np.sin(ang2)], axis=1)
    return a.astype(np.float32), b.astype(np.float32)


def _fourier(wr, wi, wf_bd, batch, t, t1, t2):
    a_np, b_np = _dft_tables(t1, t2)
    a = jnp.asarray(a_np).astype(BF16)
    b = jnp.asarray(b_np).astype(BF16)
    cw = LANES
    blk = pl.BlockSpec((None, t, cw), lambda bi, ci: (bi, 0, ci))
    return pl.pallas_call(
        functools.partial(_fourier_kernel, t1=t1, t2=t2, scale=float((t * F_HD) ** -0.5)),
        grid=(batch, F_DIM // cw),
        in_specs=[blk, blk,
                  pl.BlockSpec(a.shape, lambda bi, ci: (0, 0, 0)),
                  pl.BlockSpec(b.shape, lambda bi, ci: (0, 0)),
                  pl.BlockSpec((None, cw, cw), lambda bi, ci: (ci, 0, 0))],
        out_specs=blk,
        out_shape=jax.ShapeDtypeStruct((batch, t, F_DIM), BF16),
        scratch_shapes=[pltpu.VMEM((t, cw), F32), pltpu.VMEM((t, cw), F32), pltpu.VMEM((t, cw), F32)],
        compiler_params=_cparams(("parallel", "parallel")),
        name="fourier_%d" % t,
    )(wr.reshape(batch, t, F_DIM), wi.reshape(batch, t, F_DIM), a, b, wf_bd)


def _attn_kernel(*refs, windowed, n_blocks):
    if windowed:
        q_ref, kp_ref, kc_ref, kn_ref, vp_ref, vc_ref, vn_ref, kx_ref, vx_ref, sink_ref, o_ref = refs
    else:
        q_ref, kx_ref, vx_ref, sink_ref, o_ref = refs
    nt = (((1,), (1,)), ((), ()))
    parts = [(kx_ref, vx_ref, None)]
    if windowed:
        i = pl.program_id(1)
        a = lax.broadcasted_iota(jnp.int32, (Q_BLOCK, Q_BLOCK), 0)
        j = lax.broadcasted_iota(jnp.int32, (Q_BLOCK, Q_BLOCK), 1)
        prev_ok = (j >= a) & (i > 0)
        next_ok = (j <= a) & (i < n_blocks - 1)
        parts = [(kp_ref, vp_ref, prev_ok), (kc_ref, vc_ref, None), (kn_ref, vn_ref, next_ok)] + parts
    stack = GQA_GROUP if windowed else 1
    outs = []
    for h0 in range(0, N_HEADS, stack):
        heads = range(h0, h0 + stack)
        kv = h0 // GQA_GROUP
        hs = slice(kv * HEAD_DIM, (kv + 1) * HEAD_DIM)
        qs = jnp.concatenate([q_ref[:, h * HEAD_DIM:(h + 1) * HEAD_DIM] for h in heads], axis=0)
        scores = []
        for k_ref, _, ok in parts:
            s = lax.dot_general(qs, k_ref[:, hs], nt, preferred_element_type=F32)
            if ok is not None:
                s = jnp.where(jnp.concatenate([ok] * stack, axis=0), s, -jnp.inf)
            scores.append(s)
        sk = jnp.concatenate([jnp.full((Q_BLOCK, 1), sink_ref[h], F32) for h in heads], axis=0)
        slabs = [s[:, c:c + LANES] for s in scores for c in range(0, s.shape[1], LANES)]
        m = jnp.maximum(jnp.max(functools.reduce(jnp.maximum, slabs), axis=1, keepdims=True), sk)
        probs = [jnp.exp(s - m) for s in scores]
        pslabs = [p[:, c:c + LANES] for p in probs for c in range(0, p.shape[1], LANES)]
        inv = 1.0 / (jnp.sum(functools.reduce(jnp.add, pslabs), axis=1, keepdims=True) + jnp.exp(sk - m))
        o = functools.reduce(jnp.add, [jnp.dot(p.astype(BF16), v_ref[:, hs], preferred_element_type=F32)
                                       for p, (_, v_ref, _) in zip(probs, parts)]) * inv
        outs += [o[g * Q_BLOCK:(g + 1) * Q_BLOCK] for g in range(stack)]
    o_ref[...] = jnp.concatenate(outs, axis=1).astype(BF16)


def _attention(q, k, v, kx, vx, sink_col, batch, t, windowed):
    nb = t // Q_BLOCK
    n_ctx = kx.shape[1]
    qspec = pl.BlockSpec((None, Q_BLOCK, ATTN_DIM), lambda b, i: (b, i, 0))
    xspec = pl.BlockSpec((None, n_ctx, KV_DIM), lambda b, i: (b, 0, 0))
    sspec = pl.BlockSpec(memory_space=pltpu.SMEM)
    if windowed:
        prev = pl.BlockSpec((None, Q_BLOCK, KV_DIM), lambda b, i: (b, jnp.maximum(i - 1, 0), 0))
        cur = pl.BlockSpec((None, Q_BLOCK, KV_DIM), lambda b, i: (b, i, 0))
        nxt = pl.BlockSpec((None, Q_BLOCK, KV_DIM), lambda b, i: (b, jnp.minimum(i + 1, nb - 1), 0))
        in_specs = [qspec, prev, cur, nxt, prev, cur, nxt, xspec, xspec, sspec]
        args = [q, k, k, k, v, v, v, kx, vx, sink_col]
    else:
        in_specs = [qspec, xspec, xspec, sspec]
        args = [q, kx, vx, sink_col]
    return pl.pallas_call(
        functools.partial(_attn_kernel, windowed=windowed, n_blocks=nb),
        grid=(batch, nb),
        in_specs=in_specs, out_specs=qspec,
        out_shape=jax.ShapeDtypeStruct((batch, t, ATTN_DIM), BF16),
        compiler_params=_cparams(("parallel", "parallel")),
        name="attention_latent" if windowed else "attention_context",
    )(*args)


def _outproj_kernel(xc_ref, mfc_ref, mac_ref, xl_ref, mfl_ref, mal_ref,
                    wof_ref, woa_ref, g1_ref, sh_ref, sc_ref, n2_ref, wr_ref,
                    x1_ref, h_ref, lg_ref, *, ctx_tiles):
    def body(x_ref, mf_ref, ma_ref):
        o = jnp.dot(mf_ref[...], wof_ref[...], preferred_element_type=F32)
        o = o + jnp.dot(ma_ref[...], woa_ref[...], preferred_element_type=F32)
        x1 = x_ref[...] + g1_ref[...] * o
        x1_ref[...] = x1
        h = (_rms(x1, n2_ref[...]) * (1.0 + sc_ref[...]) + sh_ref[...]).astype(BF16)
        h_ref[...] = h
        lg_ref[...] = lax.dot_general(wr_ref[...], h, (((1,), (1,)), ((), ())), preferred_element_type=F32)

    is_ctx = pl.program_id(0) < ctx_tiles
    pl.when(is_ctx)(lambda: body(xc_ref, mfc_ref, mac_ref))
    pl.when(jnp.logical_not(is_ctx))(lambda: body(xl_ref, mfl_ref, mal_ref))


def _outproj(ctx, lat, wo_f, wo_a, mod, norm2, w_router_t, tm, row_fn):
    n_c, n_l = ctx[0].shape[0], lat[0].shape[0]
    ctx_tiles = n_c // tm
    n = n_c + n_l
    ctok = lambda w: pl.BlockSpec((tm, w), lambda i: (jnp.minimum(i, ctx_tiles - 1), 0))
    ltok = lambda w: pl.BlockSpec((tm, w), lambda i: (jnp.maximum(i - ctx_tiles, 0), 0))
    tok = lambda w: pl.BlockSpec((tm, w), lambda i: (i, 0))
    full = lambda a: pl.BlockSpec(a.shape, lambda i: (0,) * a.ndim)
    widths = (D_MODEL, F_DIM, ATTN_DIM)
    return pl.pallas_call(
        functools.partial(_outproj_kernel, ctx_tiles=ctx_tiles),
        grid=(n // tm,),
        in_specs=[ctok(w) for w in widths] + [ltok(w) for w in widths] + [
            full(wo_f), full(wo_a), _mod_spec(2, row_fn), _mod_spec(3, row_fn), _mod_spec(4, row_fn),
            full(norm2), full(w_router_t)],
        out_specs=[tok(D_MODEL), tok(D_MODEL), pl.BlockSpec((N_EXPERTS, tm), lambda i: (0, i))],
        out_shape=[jax.ShapeDtypeStruct((n, D_MODEL), F32), jax.ShapeDtypeStruct((n, D_MODEL), BF16),
                   jax.ShapeDtypeStruct((N_EXPERTS, n), F32)],
        compiler_params=_cparams(("parallel",)),
        name="outproj",
    )(*ctx, *lat, wo_f, wo_a, mod, mod, mod, norm2, w_router_t)


def _rank_before(vals, n):
    idx = lax.broadcasted_iota(jnp.int32, vals.shape, 0)
    cnt = jnp.zeros(vals.shape, jnp.int32)
    for r in range(n):
        row = vals[r:r + 1, :]
        ge = jnp.where(row >= vals, 1, 0)
        gt = jnp.where(row > vals, 1, 0)
        cnt = cnt + jnp.where(idx > r, ge, gt)
    return cnt


def _route_kernel(lg_ref, bias_ref, lo_ref, up_ref, pos_ref, gs_ref, cnt_ref):
    s = jax.nn.sigmoid(lg_ref[...])
    sc = s + bias_ref[...]
    tn = s.shape[1]
    per = N_EXPERTS // N_EXPERT_GROUPS
    g3 = sc.reshape(N_EXPERT_GROUPS, per, tn)
    member = lax.broadcasted_iota(jnp.int32, g3.shape, 1)
    m1 = jnp.max(g3, axis=1, keepdims=True)
    first = jnp.min(jnp.where(g3 == m1, member, per), axis=1, keepdims=True)
    m2 = jnp.max(jnp.where(member == first, -jnp.inf, g3), axis=1, keepdims=True)
    gscore = (m1 + m2).reshape(N_EXPERT_GROUPS, tn)
    gsel = _rank_before(gscore, N_EXPERT_GROUPS) < TOPK_GROUPS
    emask = jnp.broadcast_to(gsel.reshape(N_EXPERT_GROUPS, 1, tn), g3.shape).reshape(N_EXPERTS, tn)
    masked = jnp.where(emask, sc, -jnp.inf)
    sel = _rank_before(masked, N_EXPERTS) < TOP_K
    w = jnp.where(sel, s, 0.0)
    gate = w / jnp.sum(w, axis=0, keepdims=True) * ROUTED_SCALE

    self = jnp.where(sel, 1.0, 0.0)
    selb = self.astype(BF16)
    cnt = jnp.sum(self, axis=1, keepdims=True)
    pad = jnp.maximum(jnp.floor((cnt + (RUN_ALIGN - 1)) * (1.0 / RUN_ALIGN)), 1.0) * RUN_ALIGN
    lo = lo_ref[...]
    soff = jnp.dot(lo, jnp.broadcast_to(pad, (N_EXPERTS, LANES)).astype(BF16), preferred_element_type=F32)[:, :1]
    rank = jnp.dot(selb, up_ref[...], preferred_element_type=F32)
    kidx = jnp.dot(lo, selb, preferred_element_type=F32)
    pos_e = jnp.where(sel, soff + rank, 0.0)
    rows_p, rows_g = [], []
    for k in range(TOP_K):
        m = kidx == k
        rows_p.append(jnp.sum(jnp.where(m, pos_e, 0.0), axis=0, keepdims=True))
        rows_g.append(jnp.sum(jnp.where(m, gate, 0.0), axis=0, keepdims=True))
    pos_ref[...] = jnp.concatenate(rows_p, axis=0).astype(jnp.int32)
    gs_ref[...] = jnp.concatenate(rows_g, axis=0)
    cnt_ref[...] = pad.astype(jnp.int32)


def _route(logits_t, bias_col, tc):
    n = logits_t.shape[1]
    lo = jnp.asarray(np.tril(np.ones((N_EXPERTS, N_EXPERTS), np.float32), -1)).astype(BF16)
    up = jnp.asarray(np.triu(np.ones((tc, tc), np.float32), 1)).astype(BF16)
    slot = pl.BlockSpec((TOP_K, tc), lambda i: (0, i))
    return pl.pallas_call(
        _route_kernel,
        grid=(n // tc,),
        in_specs=[pl.BlockSpec((N_EXPERTS, tc), lambda i: (0, i)),
                  pl.BlockSpec((N_EXPERTS, 1), lambda i: (0, 0)),
                  pl.BlockSpec(lo.shape, lambda i: (0, 0)),
                  pl.BlockSpec(up.shape, lambda i: (0, 0))],
        out_specs=[slot, slot, pl.BlockSpec((None, N_EXPERTS, 1), lambda i: (i, 0, 0))],
        out_shape=[jax.ShapeDtypeStruct((TOP_K, n), jnp.int32), jax.ShapeDtypeStruct((TOP_K, n), F32),
                   jax.ShapeDtypeStruct((n // tc, N_EXPERTS, 1), jnp.int32)],
        compiler_params=_cparams(("parallel",)),
        name="route",
    )(logits_t, bias_col, lo, up)


def _aligned(x):
    return pl.multiple_of(x, RUN_ALIGN)


def _block_rows(tc):
    return lax.broadcasted_iota(jnp.int32, (PERM_BLOCK, tc), 0).astype(F32).astype(BF16)


def _block_relative(pos, r0):
    return (pos - r0).astype(F32).astype(BF16)


def _gather_kernel(src_ref, cnt_ref, dst_ref, tot_ref, tail0_ref, tailn_ref,
                   pos_ref, x_ref, xs_ref, z_ref, zero_ref, sem, *, tc, n_chunks):
    c = pl.program_id(0)
    slot = c % 2

    def wait_chunk(ci, s):
        n = _aligned(tot_ref[ci])
        pltpu.make_async_copy(z_ref.at[s, pl.ds(0, n)], xs_ref.at[pl.ds(0, n)], sem.at[s]).wait()

    def tail_copy(e):
        n = _aligned(tailn_ref[e])
        return n, pltpu.make_async_copy(zero_ref.at[pl.ds(0, n)], xs_ref.at[pl.ds(_aligned(tail0_ref[e]), n)],
                                        sem.at[2])

    @pl.when(c == 0)
    def _():
        zero_ref[...] = jnp.zeros(zero_ref.shape, zero_ref.dtype)

        def body(e, carry):
            n, cp = tail_copy(e)

            @pl.when(n > 0)
            def _():
                cp.start()
            return carry
        lax.fori_loop(0, N_EXPERTS, body, 0)

    @pl.when(c >= 2)
    def _():
        wait_chunk(c - 2, slot)

    rows = _block_rows(tc)

    def rb_body(rb2, carry):
        for half in range(PERM_UNROLL):
            r0 = pl.multiple_of((rb2 * PERM_UNROLL + half) * PERM_BLOCK, PERM_BLOCK)
            rel = _block_relative(pos_ref[...], r0)
            onehot = jnp.zeros((PERM_BLOCK, tc), BF16)
            for k in range(TOP_K):
                onehot = jnp.where(rel[k:k + 1, :] == rows, jnp.ones((), BF16), onehot)
            z_ref[slot, pl.ds(r0, PERM_BLOCK), :] = jnp.dot(
                onehot, x_ref[...], preferred_element_type=F32).astype(BF16)
        return carry
    lax.fori_loop(0, _perm_trips(tot_ref[c]), rb_body, 0)

    def e_body(e8, carry):
        for j in range(DMA_UNROLL):
            i = c * N_EXPERTS + e8 * DMA_UNROLL + j
            n = _aligned(cnt_ref[i])
            pltpu.make_async_copy(z_ref.at[slot, pl.ds(_aligned(src_ref[i]), n)],
                                  xs_ref.at[pl.ds(_aligned(dst_ref[i]), n)], sem.at[slot]).start()
        return carry
    lax.fori_loop(0, N_EXPERTS // DMA_UNROLL, e_body, 0)

    @pl.when(c == n_chunks - 1)
    def _():
        wait_chunk(c, slot)
        if n_chunks > 1:
            wait_chunk(c - 1, 1 - slot)

        def body(e, carry):
            n, cp = tail_copy(e)

            @pl.when(n > 0)
            def _():
                cp.wait()
            return carry
        lax.fori_loop(0, N_EXPERTS, body, 0)


def _perm_trips(rows):
    step = PERM_BLOCK * PERM_UNROLL
    return (rows + step - 1) // step


def _chunk_rows_max(tc):
    return TOP_K * tc + N_EXPERTS * RUN_ALIGN


def _local_cap(tc):
    step = PERM_BLOCK * PERM_UNROLL
    return -(-_chunk_rows_max(tc) // step) * step


def _gather(tabs, pos, h2, r_max, tc):
    n = h2.shape[0]
    n_chunks = n // tc
    return pl.pallas_call(
        functools.partial(_gather_kernel, tc=tc, n_chunks=n_chunks),
        grid_spec=pltpu.PrefetchScalarGridSpec(
            num_scalar_prefetch=6, grid=(n_chunks,),
            in_specs=[pl.BlockSpec((TOP_K, tc), lambda c, *_: (0, c)),
                      pl.BlockSpec((tc, D_MODEL), lambda c, *_: (c, 0))],
            out_specs=pl.BlockSpec(memory_space=pl.ANY),
            scratch_shapes=[pltpu.VMEM((2, _local_cap(tc), D_MODEL), BF16),
                            pltpu.VMEM((FFN_BLOCK, D_MODEL), BF16),
                            pltpu.SemaphoreType.DMA((3,))]),
        out_shape=jax.ShapeDtypeStruct((r_max, D_MODEL), BF16),
        compiler_params=_cparams(("arbitrary",)),
        name="moe_gather",
    )(tabs["src"], tabs["cnt"], tabs["dst"], tabs["tot"], tabs["tail0"], tabs["tailn"], pos, h2)


def _ffn_kernel(be_ref, nb_ref, x_ref, wg_ref, wu_ref, wd_ref, y_ref):
    @pl.when(pl.program_id(0) < nb_ref[0])
    def _():
        x = x_ref[...]
        a = jnp.dot(x, wg_ref[...], preferred_element_type=F32)
        u = jnp.dot(x, wu_ref[...], preferred_element_type=F32)
        h = (a * jax.nn.sigmoid(a) * u).astype(BF16)
        y_ref[...] = jnp.dot(h, wd_ref[...], preferred_element_type=F32).astype(BF16)


def _ffn(tabs, xs, wg, wu, wd):
    r_max = xs.shape[0]
    rows = pl.BlockSpec((FFN_BLOCK, D_MODEL), lambda b, be, nb: (jnp.minimum(b, nb[0] - 1), 0))
    ew = lambda a: pl.BlockSpec((None,) + a.shape[1:], lambda b, be, nb: (be[b], 0, 0))
    return pl.pallas_call(
        _ffn_kernel,
        grid_spec=pltpu.PrefetchScalarGridSpec(
            num_scalar_prefetch=2, grid=(r_max // FFN_BLOCK,),
            in_specs=[rows, ew(wg), ew(wu), ew(wd)],
            out_specs=rows),
        out_shape=jax.ShapeDtypeStruct((r_max, D_MODEL), BF16),
        compiler_params=_cparams(("arbitrary",)),
        name="moe_ffn",
    )(tabs["blk_e"], tabs["nblk"], xs, wg, wu, wd)


def _combine_kernel(src_ref, cnt_ref, dst_ref, tot_ref,
                    pos_ref, gs_ref, x1_ref, h_ref, sg_ref, su_ref, sd_ref, g2_ref, nf_ref, ys_ref,
                    yc_ref, yl_ref, ybuf, acc_ref, sem, *, tc, n_chunks, ctx_chunks):
    c = pl.program_id(0)
    slot = c % 2

    def issue(ci, s):
        def e_body(e8, carry):
            for j in range(DMA_UNROLL):
                i = ci * N_EXPERTS + e8 * DMA_UNROLL + j
                n = _aligned(cnt_ref[i])
                pltpu.make_async_copy(ys_ref.at[pl.ds(_aligned(dst_ref[i]), n)],
                                      ybuf.at[s, pl.ds(_aligned(src_ref[i]), n)], sem.at[s]).start()
            return carry
        lax.fori_loop(0, N_EXPERTS // DMA_UNROLL, e_body, 0)

    @pl.when(c == 0)
    def _():
        ybuf[...] = jnp.zeros(ybuf.shape, ybuf.dtype)
        issue(0, 0)

    @pl.when(c + 1 < n_chunks)
    def _():
        issue(c + 1, 1 - slot)

    h = h_ref[...]
    a = jnp.dot(h, sg_ref[...], preferred_element_type=F32)
    u = jnp.dot(h, su_ref[...], preferred_element_type=F32)
    acc_ref[...] = jnp.dot((a * jax.nn.sigmoid(a) * u).astype(BF16), sd_ref[...], preferred_element_type=F32)

    n = _aligned(tot_ref[c])
    pltpu.make_async_copy(ys_ref.at[pl.ds(0, n)], ybuf.at[slot, pl.ds(0, n)], sem.at[slot]).wait()

    tn = (((0,), (0,)), ((), ()))
    rows = _block_rows(tc)
    gates = gs_ref[...].astype(BF16)

    def rb_body(rb2, carry):
        part = None
        for half in range(PERM_UNROLL):
            r0 = pl.multiple_of((rb2 * PERM_UNROLL + half) * PERM_BLOCK, PERM_BLOCK)
            rel = _block_relative(pos_ref[...], r0)
            wt = jnp.zeros((PERM_BLOCK, tc), BF16)
            for k in range(TOP_K):
                wt = jnp.where(rel[k:k + 1, :] == rows, gates[k:k + 1, :], wt)
            d = lax.dot_general(wt, ybuf[slot, pl.ds(r0, PERM_BLOCK), :], tn, preferred_element_type=F32)
            part = d if part is None else part + d
        acc_ref[...] += part
        return carry
    lax.fori_loop(0, _perm_trips(tot_ref[c]), rb_body, 0)

    x2 = x1_ref[...] + g2_ref[...] * acc_ref[...]
    y = _rms(x2, nf_ref[...])

    @pl.when(c < ctx_chunks)
    def _():
        yc_ref[...] = y

    @pl.when(c >= ctx_chunks)
    def _():
        yl_ref[...] = y


def _combine(tabs, pos, gslot, x1, h2, sg, su, sd, mod, norm_f, ys, tc, row_fn, n_ctx):
    n = x1.shape[0]
    n_chunks = n // tc
    ctx_chunks = n_ctx // tc
    tok = lambda w: pl.BlockSpec((tc, w), lambda c, *_: (c, 0))
    slot = pl.BlockSpec((TOP_K, tc), lambda c, *_: (0, c))
    full = lambda a: pl.BlockSpec(a.shape, lambda c, *_: (0,) * a.ndim)
    return pl.pallas_call(
        functools.partial(_combine_kernel, tc=tc, n_chunks=n_chunks, ctx_chunks=ctx_chunks),
        grid_spec=pltpu.PrefetchScalarGridSpec(
            num_scalar_prefetch=4, grid=(n_chunks,),
            in_specs=[slot, slot, tok(D_MODEL), tok(D_MODEL), full(sg), full(su), full(sd),
                      _mod_spec(5, lambda c, *_: row_fn(c)), full(norm_f), pl.BlockSpec(memory_space=pl.ANY)],
            out_specs=[pl.BlockSpec((tc, D_MODEL), lambda c, *_: (jnp.minimum(c, ctx_chunks - 1), 0)),
                       pl.BlockSpec((tc, D_MODEL), lambda c, *_: (jnp.maximum(c - ctx_chunks, 0), 0))],
            scratch_shapes=[pltpu.VMEM((2, _local_cap(tc), D_MODEL), BF16),
                            pltpu.VMEM((tc, D_MODEL), F32),
                            pltpu.SemaphoreType.DMA((2,))]),
        out_shape=[jax.ShapeDtypeStruct((n_ctx, D_MODEL), F32), jax.ShapeDtypeStruct((n - n_ctx, D_MODEL), F32)],
        compiler_params=_cparams(("arbitrary",)),
        name="moe_combine",
    )(tabs["src"], tabs["cnt"], tabs["dst"], tabs["tot"], pos, gslot, x1, h2, sg, su, sd, mod, norm_f, ys)


def _dispatch_tables(pad_cnt, n_blocks_max):
    i32 = jnp.int32
    src = jnp.cumsum(pad_cnt, axis=1) - pad_cnt
    tot = jnp.sum(pad_cnt, axis=1)
    used = jnp.sum(pad_cnt, axis=0)
    region = (used + FFN_BLOCK - 1) // FFN_BLOCK * FFN_BLOCK
    region_end = jnp.cumsum(region)
    e_off = region_end - region
    dst = e_off[None, :] + jnp.cumsum(pad_cnt, axis=0) - pad_cnt
    nblk = region_end[-1] // FFN_BLOCK
    ends = region_end // FFN_BLOCK
    blk_e = jnp.minimum(jnp.sum(ends[None, :] <= jnp.arange(n_blocks_max)[:, None], axis=1), N_EXPERTS - 1)
    return {"src": src.reshape(-1).astype(i32), "cnt": pad_cnt.reshape(-1).astype(i32),
            "dst": dst.reshape(-1).astype(i32), "tot": tot.astype(i32),
            "tail0": (e_off + used).astype(i32), "tailn": (region - used).astype(i32),
            "blk_e": blk_e.astype(i32), "nblk": nblk.reshape(1).astype(i32)}


def _moe(x1, h2, logits_t, bias_col, wg, wu, wd, sg, su, sd, mod, norm_f, row_fn, n_ctx):
    n = x1.shape[0]
    tc = MOE_CHUNK
    pos, gslot, pad_cnt = _route(logits_t, bias_col, tc)
    worst_rows = (n // tc) * _chunk_rows_max(tc) + N_EXPERTS * (FFN_BLOCK - RUN_ALIGN)
    n_blocks_max = -(-worst_rows // FFN_BLOCK)
    tabs = _dispatch_tables(pad_cnt[:, :, 0], n_blocks_max)
    xs = _gather(tabs, pos, h2, n_blocks_max * FFN_BLOCK, tc)
    ys = _ffn(tabs, xs, wg, wu, wd)
    return _combine(tabs, pos, gslot, x1, h2, sg, su, sd, mod, norm_f, ys, tc, row_fn, n_ctx)


def _channel_dft():
    ang = 2.0 * np.pi * ((np.arange(F_HD)[:, None] * np.arange(F_HD)[None, :]) % F_HD) / F_HD
    eye = np.eye(F_GROUPS)
    return np.concatenate([np.kron(eye, np.cos(ang)), np.kron(eye, -np.sin(ang))], axis=1).astype(np.float32)


def _rope_tables(t):
    pos = np.arange(t)
    row, col = pos // GRID_W, pos % GRID_W
    n_freq = HEAD_DIM // 4
    inv = ROPE_THETA ** (-np.arange(n_freq, dtype=np.float64) / n_freq)
    lane = np.arange(LANES)
    hd = lane % HEAD_DIM
    within = hd % (HEAD_DIM // 2)
    freq = within % n_freq
    first = within < n_freq
    p = np.where((hd < HEAD_DIM // 2)[None, :], row[:, None], col[:, None]).astype(np.float64)
    ang = p * inv[freq][None, :]
    cos, sin = np.cos(ang), np.sin(ang)
    sin_up = np.where(first[None, :], -sin, 0.0)
    sin_dn = np.where(first[None, :], 0.0, sin)
    return [jnp.asarray(a.astype(np.float32)) for a in (cos, sin_up, sin_dn)]


def _block_diag_pairs(wf):
    per = LANES // F_HD
    z = jnp.zeros((F_HD, F_HD), wf.dtype)
    blocks = []
    for c in range(F_GROUPS // per):
        rows = [jnp.concatenate([wf[c * per + r] if r == cc else z for cc in range(per)], axis=1) for r in range(per)]
        blocks.append(jnp.concatenate(rows, axis=0))
    return jnp.stack(blocks)


def kernel(x_prompt, x_sample, cache_k, cache_v, c, c_ctx, w_mod, b_mod, norm1, w_in, w_fourier, sink,
           w_out, norm2, w_router, router_bias, w_gate, w_up, w_down, ws_gate, ws_up, ws_down, norm_f):
    nb_ctx, t_ctx, _ = x_prompt.shape
    nb_lat, t_lat, _ = x_sample.shape
    l = 0
    cvec = jnp.concatenate([c_ctx[None, :], c, jnp.zeros((MOD_ROWS - 1 - nb_lat, D_MODEL), F32)], axis=0)
    mod = _modulation(cvec, w_mod[l], b_mod[l]).reshape(MOD_ROWS, 6, 1, D_MODEL)

    w_in_b = w_in[l].astype(BF16)
    dft_c = jnp.asarray(_channel_dft()).astype(BF16)
    wf_bd = _block_diag_pairs(w_fourier[l]).astype(BF16)
    wo_f = w_out[l][:F_DIM].astype(BF16)
    wo_a = w_out[l][F_DIM:].astype(BF16)
    w_router_t = w_router[l].T.astype(BF16)
    bias_col = router_bias[l].reshape(N_EXPERTS, 1)
    n1 = norm1[l].reshape(1, D_MODEL)
    n2 = norm2[l].reshape(1, D_MODEL)
    nf = norm_f.reshape(1, D_MODEL)
    sink_col = sink[l]
    wg, wu, wd = w_gate[l].astype(BF16), w_up[l].astype(BF16), w_down[l].astype(BF16)
    sg, su, sd = ws_gate[l].astype(BF16), ws_up[l].astype(BF16), ws_down[l].astype(BF16)

    def mixers(x, batch, t, latent, kx, vx, t1, t2, tm):
        n = batch * t
        x2d = x.reshape(n, D_MODEL)
        tiles = t // tm
        row_fn = (lambda i: 1 + i // tiles) if latent else (lambda i: 0)
        rope_tabs = _rope_tables(t) if latent else None
        outs = _project(x2d, mod, n1, w_in_b, dft_c, rope_tabs, tm, tiles, row_fn)
        wr, wi, q, k, v = outs[:5]
        mixf = _fourier(wr, wi, wf_bd, batch, t, t1, t2)
        q3, k3, v3 = (a.reshape(batch, t, -1) for a in (q, k, v))
        if latent:
            mixa = _attention(q3, k3, v3, kx, vx, sink_col, batch, t, True)
        else:
            mixa = _attention(q3, None, None, k3, v3, sink_col, batch, t, False)
        return (x2d, mixf.reshape(n, F_DIM), mixa.reshape(n, ATTN_DIM)), outs[5:]

    ctx, (kf, vf) = mixers(x_prompt, nb_ctx, t_ctx, False, None, None, 16, 16, 256)
    kx = cache_k[:, l].reshape(nb_lat, -1, KV_DIM).astype(BF16)
    vx = cache_v[:, l].reshape(nb_lat, -1, KV_DIM).astype(BF16)
    lat, _ = mixers(x_sample, nb_lat, t_lat, True, kx, vx, 64, 64, 512)

    n_ctx = nb_ctx * t_ctx
    ctx_tiles = n_ctx // MOE_CHUNK
    lat_tiles = t_lat // MOE_CHUNK
    tile_row = lambda i: jnp.where(i < ctx_tiles, 0, 1 + (i - ctx_tiles) // lat_tiles)
    x1, h2, logits_t = _outproj(ctx, lat, wo_f, wo_a, mod, n2, w_router_t, MOE_CHUNK, tile_row)
    y_prompt, y_sample = _moe(x1, h2, logits_t, bias_col, wg, wu, wd, sg, su, sd, mod, nf, tile_row, n_ctx)
    new_k = kf.reshape(nb_ctx, 1, t_ctx, N_KV_HEADS, HEAD_DIM)
    new_v = vf.reshape(nb_ctx, 1, t_ctx, N_KV_HEADS, HEAD_DIM)
    return (y_prompt.reshape(x_prompt.shape), y_sample.reshape(x_sample.shape), new_k, new_v)
```

```python
import functools

import numpy as np
import jax
import jax.numpy as jnp
from jax import lax
from jax.experimental import pallas as pl
from jax.experimental.pallas import tpu as pltpu

F32 = jnp.float32
BF16 = jnp.bfloat16

D_MODEL = 1024
GRID_W = 64
HEAD_DIM = 64
N_HEADS = 12
N_KV_HEADS = 4
GQA_GROUP = N_HEADS // N_KV_HEADS
ATTN_DIM = N_HEADS * HEAD_DIM
KV_DIM = N_KV_HEADS * HEAD_DIM
F_GROUPS = 4
F_HD = 64
F_DIM = F_GROUPS * F_HD
IN_DIM = F_DIM + ATTN_DIM + 2 * KV_DIM
WINDOW = 128
Q_BLOCK = 128
ROPE_THETA = 10000.0
N_EXPERTS = 64
TOP_K = 8
N_EXPERT_GROUPS = 8
TOPK_GROUPS = 4
EXPERT_DIM = 256
SHARED_DIM = 256
ROUTED_SCALE = 2.5
EPS = 1e-6

LANES = 128
MOD_ROWS = 8
RUN_ALIGN = 16
PERM_BLOCK = 256
FFN_BLOCK = 512
MOE_CHUNK = 256
PERM_UNROLL = 2
DMA_UNROLL = 8
ISSUE_TRIPS = 4
RUNS_PER_TRIP = N_EXPERTS // ISSUE_TRIPS
VMEM_LIMIT = 56 * 1024 * 1024


def _cparams(sem):
    return pltpu.CompilerParams(dimension_semantics=sem, vmem_limit_bytes=VMEM_LIMIT)


def _bdot(a, b):
    return jnp.dot(a.astype(BF16), b.astype(BF16), preferred_element_type=F32)


def _rms(x, g):
    return x * lax.rsqrt(jnp.mean(x * x, axis=-1, keepdims=True) + EPS) * g


def _mod_kernel(c_ref, w_ref, b_ref, o_ref):
    c = c_ref[...]
    a = c * jax.nn.sigmoid(c)
    o_ref[...] = _bdot(a, w_ref[...]) + b_ref[...]


def _modulation(cvec, w_mod, b_mod):
    n = w_mod.shape[1]
    tn = 1024
    return pl.pallas_call(
        _mod_kernel,
        grid=(n // tn,),
        in_specs=[pl.BlockSpec((MOD_ROWS, D_MODEL), lambda j: (0, 0)),
                  pl.BlockSpec((D_MODEL, tn), lambda j: (0, j)),
                  pl.BlockSpec((1, tn), lambda j: (0, j))],
        out_specs=pl.BlockSpec((MOD_ROWS, tn), lambda j: (0, j)),
        out_shape=jax.ShapeDtypeStruct((MOD_ROWS, n), F32),
        compiler_params=_cparams(("parallel",)),
        name="modulation",
    )(cvec, w_mod, b_mod.reshape(1, n))


def _mod_spec(piece, row_fn):
    return pl.BlockSpec((None, None, 1, D_MODEL), lambda *idx: (row_fn(*idx), piece, 0, 0))


def _rope(x, cos, sin_up, sin_dn):
    outs = []
    for j in range(x.shape[1] // LANES):
        xj = x[:, j * LANES:(j + 1) * LANES]
        up = pltpu.roll(xj, LANES - 16, axis=1)
        dn = pltpu.roll(xj, 16, axis=1)
        outs.append(xj * cos + up * sin_up + dn * sin_dn)
    return jnp.concatenate(outs, axis=1)


def _proj_kernel(*refs, latent):
    if latent:
        (x_ref, sh_ref, sc_ref, g_ref, w_ref, dft_ref, cos_ref, sup_ref, sdn_ref,
         wr_ref, wi_ref, q_ref, k_ref, v_ref) = refs
    else:
        (x_ref, sh_ref, sc_ref, g_ref, w_ref, dft_ref,
         wr_ref, wi_ref, q_ref, k_ref, v_ref, kf_ref, vf_ref) = refs
    x = x_ref[...]
    h = _rms(x, g_ref[...]) * (1.0 + sc_ref[...]) + sh_ref[...]
    p = _bdot(h, w_ref[...])
    u = p[:, :F_DIM]
    q = p[:, F_DIM:F_DIM + ATTN_DIM]
    k = p[:, F_DIM + ATTN_DIM:F_DIM + ATTN_DIM + KV_DIM]
    v = p[:, F_DIM + ATTN_DIM + KV_DIM:]
    w = _bdot(u, dft_ref[...])
    wr_ref[...] = w[:, :F_DIM]
    wi_ref[...] = w[:, F_DIM:]
    if latent:
        cos, sup, sdn = cos_ref[...], sup_ref[...], sdn_ref[...]
        q = _rope(q, cos, sup, sdn)
        k = _rope(k, cos, sup, sdn)
    else:
        kf_ref[...] = k
        vf_ref[...] = v
    q_ref[...] = (q * (HEAD_DIM ** -0.5)).astype(BF16)
    k_ref[...] = k.astype(BF16)
    v_ref[...] = v.astype(BF16)


def _project(x2d, mod, norm1, w_in, dft_c, rope_tabs, tm, tiles_per_batch, row_fn):
    n = x2d.shape[0]
    latent = rope_tabs is not None
    tok = lambda w: pl.BlockSpec((tm, w), lambda i: (i, 0))
    full = lambda a: pl.BlockSpec(a.shape, lambda i: (0,) * a.ndim)
    in_specs = [tok(D_MODEL), _mod_spec(0, row_fn), _mod_spec(1, row_fn), full(norm1), full(w_in), full(dft_c)]
    args = [x2d, mod, mod, norm1, w_in, dft_c]
    out_specs = [tok(F_DIM), tok(F_DIM), tok(ATTN_DIM), tok(KV_DIM), tok(KV_DIM)]
    out_shape = [jax.ShapeDtypeStruct((n, F_DIM), F32), jax.ShapeDtypeStruct((n, F_DIM), F32),
                 jax.ShapeDtypeStruct((n, ATTN_DIM), BF16), jax.ShapeDtypeStruct((n, KV_DIM), BF16),
                 jax.ShapeDtypeStruct((n, KV_DIM), BF16)]
    if latent:
        pos = pl.BlockSpec((tm, LANES), lambda i: (i % tiles_per_batch, 0))
        in_specs += [pos, pos, pos]
        args += list(rope_tabs)
    else:
        out_specs += [tok(KV_DIM), tok(KV_DIM)]
        out_shape += [jax.ShapeDtypeStruct((n, KV_DIM), F32), jax.ShapeDtypeStruct((n, KV_DIM), F32)]
    return pl.pallas_call(
        functools.partial(_proj_kernel, latent=latent),
        grid=(n // tm,),
        in_specs=in_specs, out_specs=out_specs, out_shape=out_shape,
        compiler_params=_cparams(("parallel",)),
        name="project_latent" if latent else "project_context",
    )(*args)


def _fourier_kernel(wr_ref, wi_ref, a_ref, b_ref, wf_ref, o_ref, yr_ref, yi_ref, z_ref, *, t1, t2, scale):
    for j in range(t2):
        rows = pl.ds(j, t1, stride=t2)
        xin = jnp.concatenate([wr_ref[rows, :], wi_ref[rows, :]], axis=0)
        y = _bdot(a_ref[j], xin)
        yr_ref[rows, :] = y[:t1]
        yi_ref[rows, :] = y[t1:]
    bm = b_ref[...]
    for k1 in range(t1):
        rows = pl.ds(k1 * t2, t2)
        yin = jnp.concatenate([yr_ref[rows, :], yi_ref[rows, :]], axis=0)
        z_ref[pl.ds(k1, t2, stride=t1), :] = _bdot(bm, yin)
    o_ref[...] = (_bdot(z_ref[...], wf_ref[...]) * scale).astype(BF16)


def _dft_tables(t1, t2):
    t = t1 * t2
    k1 = np.arange(t1)[None, :, None]
    pos = (t2 * np.arange(t1)[None, None, :] + np.arange(t2)[:, None, None])
    ang = 2.0 * np.pi * ((k1 * pos) % t) / t
    c, s = np.cos(ang), np.sin(ang)
    a = np.concatenate([np.concatenate([c, s], axis=2), np.concatenate([-s, c], axis=2)], axis=1)
    ang2 = 2.0 * np.pi * ((np.arange(t2)[:, None] * np.arange(t2)[None, :]) % t2) / t2
    b = np.concatenate([np.cos(ang2), np.sin(ang2)], axis=1)
    return a.astype(np.float32), b.astype(np.float32)


def _fourier(wr, wi, wf_bd, batch, t, t1, t2):
    a_np, b_np = _dft_tables(t1, t2)
    a = jnp.asarray(a_np).astype(BF16)
    b = jnp.asarray(b_np).astype(BF16)
    cw = LANES
    blk = pl.BlockSpec((None, t, cw), lambda bi, ci: (bi, 0, ci))
    return pl.pallas_call(
        functools.partial(_fourier_kernel, t1=t1, t2=t2, scale=float((t * F_HD) ** -0.5)),
        grid=(batch, F_DIM // cw),
        in_specs=[blk, blk,
                  pl.BlockSpec(a.shape, lambda bi, ci: (0, 0, 0)),
                  pl.BlockSpec(b.shape, lambda bi, ci: (0, 0)),
                  pl.BlockSpec((None, cw, cw), lambda bi, ci: (ci, 0, 0))],
        out_specs=blk,
        out_shape=jax.ShapeDtypeStruct((batch, t, F_DIM), BF16),
        scratch_shapes=[pltpu.VMEM((t, cw), F32), pltpu.VMEM((t, cw), F32), pltpu.VMEM((t, cw), F32)],
        compiler_params=_cparams(("parallel", "parallel")),
        name="fourier_%d" % t,
    )(wr.reshape(batch, t, F_DIM), wi.reshape(batch, t, F_DIM), a, b, wf_bd)


def _attn_kernel(*refs, windowed, n_blocks):
    if windowed:
        q_ref, kp_ref, kc_ref, kn_ref, vp_ref, vc_ref, vn_ref, kx_ref, vx_ref, sink_ref, o_ref = refs
    else:
        q_ref, kx_ref, vx_ref, sink_ref, o_ref = refs
    nt = (((1,), (1,)), ((), ()))
    parts = [(kx_ref, vx_ref, None)]
    if windowed:
        i = pl.program_id(1)
        a = lax.broadcasted_iota(jnp.int32, (Q_BLOCK, Q_BLOCK), 0)
        j = lax.broadcasted_iota(jnp.int32, (Q_BLOCK, Q_BLOCK), 1)
        prev_ok = (j >= a) & (i > 0)
        next_ok = (j <= a) & (i < n_blocks - 1)
        parts = [(kp_ref, vp_ref, prev_ok), (kc_ref, vc_ref, None), (kn_ref, vn_ref, next_ok)] + parts
    stack = GQA_GROUP if windowed else 1
    outs = []
    for h0 in range(0, N_HEADS, stack):
        heads = range(h0, h0 + stack)
        kv = h0 // GQA_GROUP
        hs = slice(kv * HEAD_DIM, (kv + 1) * HEAD_DIM)
        qs = jnp.concatenate([q_ref[:, h * HEAD_DIM:(h + 1) * HEAD_DIM] for h in heads], axis=0)
        scores = []
        for k_ref, _, ok in parts:
            s = lax.dot_general(qs, k_ref[:, hs], nt, preferred_element_type=F32)
            if ok is not None:
                s = jnp.where(jnp.concatenate([ok] * stack, axis=0), s, -jnp.inf)
            scores.append(s)
        sk = jnp.concatenate([jnp.full((Q_BLOCK, 1), sink_ref[h], F32) for h in heads], axis=0)
        slabs = [s[:, c:c + LANES] for s in scores for c in range(0, s.shape[1], LANES)]
        m = jnp.maximum(jnp.max(functools.reduce(jnp.maximum, slabs), axis=1, keepdims=True), sk)
        probs = [jnp.exp(s - m) for s in scores]
        pslabs = [p[:, c:c + LANES] for p in probs for c in range(0, p.shape[1], LANES)]
        inv = 1.0 / (jnp.sum(functools.reduce(jnp.add, pslabs), axis=1, keepdims=True) + jnp.exp(sk - m))
        o = functools.reduce(jnp.add, [jnp.dot(p.astype(BF16), v_ref[:, hs], preferred_element_type=F32)
                                       for p, (_, v_ref, _) in zip(probs, parts)]) * inv
        outs += [o[g * Q_BLOCK:(g + 1) * Q_BLOCK] for g in range(stack)]
    o_ref[...] = jnp.concatenate(outs, axis=1).astype(BF16)


def _attention(q, k, v, kx, vx, sink_col, batch, t, windowed):
    nb = t // Q_BLOCK
    n_ctx = kx.shape[1]
    qspec = pl.BlockSpec((None, Q_BLOCK, ATTN_DIM), lambda b, i: (b, i, 0))
    xspec = pl.BlockSpec((None, n_ctx, KV_DIM), lambda b, i: (b, 0, 0))
    sspec = pl.BlockSpec(memory_space=pltpu.SMEM)
    if windowed:
        prev = pl.BlockSpec((None, Q_BLOCK, KV_DIM), lambda b, i: (b, jnp.maximum(i - 1, 0), 0))
        cur = pl.BlockSpec((None, Q_BLOCK, KV_DIM), lambda b, i: (b, i, 0))
        nxt = pl.BlockSpec((None, Q_BLOCK, KV_DIM), lambda b, i: (b, jnp.minimum(i + 1, nb - 1), 0))
        in_specs = [qspec, prev, cur, nxt, prev, cur, nxt, xspec, xspec, sspec]
        args = [q, k, k, k, v, v, v, kx, vx, sink_col]
    else:
        in_specs = [qspec, xspec, xspec, sspec]
        args = [q, kx, vx, sink_col]
    return pl.pallas_call(
        functools.partial(_attn_kernel, windowed=windowed, n_blocks=nb),
        grid=(batch, nb),
        in_specs=in_specs, out_specs=qspec,
        out_shape=jax.ShapeDtypeStruct((batch, t, ATTN_DIM), BF16),
        compiler_params=_cparams(("parallel", "parallel")),
        name="attention_latent" if windowed else "attention_context",
    )(*args)


def _outproj_kernel(xc_ref, mfc_ref, mac_ref, xl_ref, mfl_ref, mal_ref,
                    wof_ref, woa_ref, g1_ref, sh_ref, sc_ref, n2_ref, wr_ref,
                    x1_ref, h_ref, lg_ref, *, ctx_tiles):
    def body(x_ref, mf_ref, ma_ref):
        o = jnp.dot(mf_ref[...], wof_ref[...], preferred_element_type=F32)
        o = o + jnp.dot(ma_ref[...], woa_ref[...], preferred_element_type=F32)
        x1 = x_ref[...] + g1_ref[...] * o
        x1_ref[...] = x1
        h = (_rms(x1, n2_ref[...]) * (1.0 + sc_ref[...]) + sh_ref[...]).astype(BF16)
        h_ref[...] = h
        lg_ref[...] = lax.dot_general(wr_ref[...], h, (((1,), (1,)), ((), ())), preferred_element_type=F32)

    is_ctx = pl.program_id(0) < ctx_tiles
    pl.when(is_ctx)(lambda: body(xc_ref, mfc_ref, mac_ref))
    pl.when(jnp.logical_not(is_ctx))(lambda: body(xl_ref, mfl_ref, mal_ref))


def _outproj(ctx, lat, wo_f, wo_a, mod, norm2, w_router_t, tm, row_fn):
    n_c, n_l = ctx[0].shape[0], lat[0].shape[0]
    ctx_tiles = n_c // tm
    n = n_c + n_l
    ctok = lambda w: pl.BlockSpec((tm, w), lambda i: (jnp.minimum(i, ctx_tiles - 1), 0))
    ltok = lambda w: pl.BlockSpec((tm, w), lambda i: (jnp.maximum(i - ctx_tiles, 0), 0))
    tok = lambda w: pl.BlockSpec((tm, w), lambda i: (i, 0))
    full = lambda a: pl.BlockSpec(a.shape, lambda i: (0,) * a.ndim)
    widths = (D_MODEL, F_DIM, ATTN_DIM)
    return pl.pallas_call(
        functools.partial(_outproj_kernel, ctx_tiles=ctx_tiles),
        grid=(n // tm,),
        in_specs=[ctok(w) for w in widths] + [ltok(w) for w in widths] + [
            full(wo_f), full(wo_a), _mod_spec(2, row_fn), _mod_spec(3, row_fn), _mod_spec(4, row_fn),
            full(norm2), full(w_router_t)],
        out_specs=[tok(D_MODEL), tok(D_MODEL), pl.BlockSpec((N_EXPERTS, tm), lambda i: (0, i))],
        out_shape=[jax.ShapeDtypeStruct((n, D_MODEL), F32), jax.ShapeDtypeStruct((n, D_MODEL), BF16),
                   jax.ShapeDtypeStruct((N_EXPERTS, n), F32)],
        compiler_params=_cparams(("parallel",)),
        name="outproj",
    )(*ctx, *lat, wo_f, wo_a, mod, mod, mod, norm2, w_router_t)


def _rank_before(vals, n):
    idx = lax.broadcasted_iota(jnp.int32, vals.shape, 0)
    cnt = jnp.zeros(vals.shape, jnp.int32)
    for r in range(n):
        row = vals[r:r + 1, :]
        ge = jnp.where(row >= vals, 1, 0)
        gt = jnp.where(row > vals, 1, 0)
        cnt = cnt + jnp.where(idx > r, ge, gt)
    return cnt


def _route_kernel(lg_ref, bias_ref, lo_ref, up_ref, pos_ref, gs_ref, cnt_ref):
    s = jax.nn.sigmoid(lg_ref[...])
    sc = s + bias_ref[...]
    tn = s.shape[1]
    per = N_EXPERTS // N_EXPERT_GROUPS
    g3 = sc.reshape(N_EXPERT_GROUPS, per, tn)
    member = lax.broadcasted_iota(jnp.int32, g3.shape, 1)
    m1 = jnp.max(g3, axis=1, keepdims=True)
    first = jnp.min(jnp.where(g3 == m1, member, per), axis=1, keepdims=True)
    m2 = jnp.max(jnp.where(member == first, -jnp.inf, g3), axis=1, keepdims=True)
    gscore = (m1 + m2).reshape(N_EXPERT_GROUPS, tn)
    gsel = _rank_before(gscore, N_EXPERT_GROUPS) < TOPK_GROUPS
    emask = jnp.broadcast_to(gsel.reshape(N_EXPERT_GROUPS, 1, tn), g3.shape).reshape(N_EXPERTS, tn)
    masked = jnp.where(emask, sc, -jnp.inf)
    sel = _rank_before(masked, N_EXPERTS) < TOP_K
    w = jnp.where(sel, s, 0.0)
    gate = w / jnp.sum(w, axis=0, keepdims=True) * ROUTED_SCALE

    self = jnp.where(sel, 1.0, 0.0)
    selb = self.astype(BF16)
    cnt = jnp.sum(self, axis=1, keepdims=True)
    pad = jnp.maximum(jnp.floor((cnt + (RUN_ALIGN - 1)) * (1.0 / RUN_ALIGN)), 1.0) * RUN_ALIGN
    lo = lo_ref[...]
    soff = jnp.dot(lo, jnp.broadcast_to(pad, (N_EXPERTS, LANES)).astype(BF16), preferred_element_type=F32)[:, :1]
    rank = jnp.dot(selb, up_ref[...], preferred_element_type=F32)
    kidx = jnp.dot(lo, selb, preferred_element_type=F32)
    pos_e = jnp.where(sel, soff + rank, 0.0)
    rows_p, rows_g = [], []
    for k in range(TOP_K):
        m = kidx == k
        rows_p.append(jnp.sum(jnp.where(m, pos_e, 0.0), axis=0, keepdims=True))
        rows_g.append(jnp.sum(jnp.where(m, gate, 0.0), axis=0, keepdims=True))
    pos_ref[...] = jnp.concatenate(rows_p, axis=0).astype(jnp.int32)
    gs_ref[...] = jnp.concatenate(rows_g, axis=0)
    cnt_ref[...] = pad.astype(jnp.int32)


def _route(logits_t, bias_col, tc):
    n = logits_t.shape[1]
    lo = jnp.asarray(np.tril(np.ones((N_EXPERTS, N_EXPERTS), np.float32), -1)).astype(BF16)
    up = jnp.asarray(np.triu(np.ones((tc, tc), np.float32), 1)).astype(BF16)
    slot = pl.BlockSpec((TOP_K, tc), lambda i: (0, i))
    return pl.pallas_call(
        _route_kernel,
        grid=(n // tc,),
        in_specs=[pl.BlockSpec((N_EXPERTS, tc), lambda i: (0, i)),
                  pl.BlockSpec((N_EXPERTS, 1), lambda i: (0, 0)),
                  pl.BlockSpec(lo.shape, lambda i: (0, 0)),
                  pl.BlockSpec(up.shape, lambda i: (0, 0))],
        out_specs=[slot, slot, pl.BlockSpec((None, N_EXPERTS, 1), lambda i: (i, 0, 0))],
        out_shape=[jax.ShapeDtypeStruct((TOP_K, n), jnp.int32), jax.ShapeDtypeStruct((TOP_K, n), F32),
                   jax.ShapeDtypeStruct((n // tc, N_EXPERTS, 1), jnp.int32)],
        compiler_params=_cparams(("parallel",)),
        name="route",
    )(logits_t, bias_col, lo, up)


def _aligned(x):
    return pl.multiple_of(x, RUN_ALIGN)


def _block_rows(tc):
    return lax.broadcasted_iota(jnp.int32, (PERM_BLOCK, tc), 0).astype(F32).astype(BF16)


def _block_relative(pos, r0):
    return (pos - r0).astype(F32).astype(BF16)


def _gather_kernel(src_ref, cnt_ref, dst_ref, tot_ref,
                   pos_ref, x_ref, xs_ref, z_ref, sem, *, tc, n_chunks):
    c = pl.program_id(0)
    slot = c % 2

    def wait_chunk(ci, s):
        n = _aligned(tot_ref[ci])
        pltpu.make_async_copy(z_ref.at[s, pl.ds(0, n)], xs_ref.at[pl.ds(0, n)], sem.at[s]).wait()

    def start_runs(ci, s, first, count):
        for j in range(count):
            i = ci * N_EXPERTS + first + j
            n = _aligned(cnt_ref[i])
            pltpu.make_async_copy(z_ref.at[s, pl.ds(_aligned(src_ref[i]), n)],
                                  xs_ref.at[pl.ds(_aligned(dst_ref[i]), n)], sem.at[s]).start()

    @pl.when(c >= 2)
    def _():
        wait_chunk(c - 2, slot)

    rows = _block_rows(tc)

    def permute_blocks(trip):
        for half in range(PERM_UNROLL):
            r0 = pl.multiple_of((trip * PERM_UNROLL + half) * PERM_BLOCK, PERM_BLOCK)
            rel = _block_relative(pos_ref[...], r0)
            onehot = jnp.zeros((PERM_BLOCK, tc), BF16)
            for k in range(TOP_K):
                onehot = jnp.where(rel[k:k + 1, :] == rows, jnp.ones((), BF16), onehot)
            z_ref[slot, pl.ds(r0, PERM_BLOCK), :] = jnp.dot(
                onehot, x_ref[...], preferred_element_type=F32).astype(BF16)

    def plain_trip(trip, carry):
        permute_blocks(trip)
        return carry

    def issuing_trip(trip, carry):
        permute_blocks(trip)
        start_runs(c - 1, 1 - slot, trip * RUNS_PER_TRIP, RUNS_PER_TRIP)
        return carry

    trips = _perm_trips(tot_ref[c])

    @pl.when(c == 0)
    def _():
        lax.fori_loop(0, ISSUE_TRIPS, plain_trip, 0)

    @pl.when(c > 0)
    def _():
        lax.fori_loop(0, ISSUE_TRIPS, issuing_trip, 0)

    lax.fori_loop(ISSUE_TRIPS, trips, plain_trip, 0)

    @pl.when(c == n_chunks - 1)
    def _():
        def e_body(e8, carry):
            start_runs(c, slot, e8 * DMA_UNROLL, DMA_UNROLL)
            return carry
        lax.fori_loop(0, N_EXPERTS // DMA_UNROLL, e_body, 0)
        wait_chunk(c, slot)
        if n_chunks > 1:
            wait_chunk(c - 1, 1 - slot)


def _perm_trips(rows):
    step = PERM_BLOCK * PERM_UNROLL
    return (rows + step - 1) // step


def _chunk_rows_max(tc):
    return TOP_K * tc + N_EXPERTS * RUN_ALIGN


def _local_cap(tc):
    step = PERM_BLOCK * PERM_UNROLL
    return -(-_chunk_rows_max(tc) // step) * step


def _gather(tabs, pos, h2, r_max, tc):
    n = h2.shape[0]
    n_chunks = n // tc
    assert ISSUE_TRIPS * RUNS_PER_TRIP == N_EXPERTS and ISSUE_TRIPS * PERM_BLOCK * PERM_UNROLL <= TOP_K * tc
    return pl.pallas_call(
        functools.partial(_gather_kernel, tc=tc, n_chunks=n_chunks),
        grid_spec=pltpu.PrefetchScalarGridSpec(
            num_scalar_prefetch=4, grid=(n_chunks,),
            in_specs=[pl.BlockSpec((TOP_K, tc), lambda c, *_: (0, c)),
                      pl.BlockSpec((tc, D_MODEL), lambda c, *_: (c, 0))],
            out_specs=pl.BlockSpec(memory_space=pl.ANY),
            scratch_shapes=[pltpu.VMEM((2, _local_cap(tc), D_MODEL), BF16),
                            pltpu.SemaphoreType.DMA((2,))]),
        out_shape=jax.ShapeDtypeStruct((r_max, D_MODEL), BF16),
        compiler_params=_cparams(("arbitrary",)),
        name="moe_gather",
    )(tabs["src"], tabs["cnt"], tabs["dst"], tabs["tot"], pos, h2)


def _ffn_kernel(start_ref, rows_ref, xs_ref, wg_ref, wu_ref, wd_ref, ys_ref,
                xbuf, ybuf, wgb, wub, wdb, sem_in, sem_out):
    e = pl.program_id(0)
    start = start_ref[e]
    rows = rows_ref[e]
    nb = (rows + FFN_BLOCK - 1) // FFN_BLOCK

    def tile_rows(r, j):
        return _aligned(jnp.minimum(FFN_BLOCK, r - j * FFN_BLOCK))

    def in_copy(st, r, j, s):
        n = tile_rows(r, j)
        return pltpu.make_async_copy(xs_ref.at[pl.ds(_aligned(st + j * FFN_BLOCK), n)],
                                     xbuf.at[s, pl.ds(0, n)], sem_in.at[s])

    def out_copy(j, s):
        n = tile_rows(rows, j)
        return pltpu.make_async_copy(ybuf.at[s, pl.ds(0, n)],
                                     ys_ref.at[pl.ds(_aligned(start + j * FFN_BLOCK), n)], sem_out.at[s])

    @pl.when(e == 0)
    def _():
        xbuf[...] = jnp.zeros(xbuf.shape, xbuf.dtype)
        in_copy(start, rows, 0, 0).start()

    wgb[...] = wg_ref[...].astype(BF16)
    wub[...] = wu_ref[...].astype(BF16)
    wdb[...] = wd_ref[...].astype(BF16)

    def body(j, carry):
        s = j % 2

        @pl.when(j + 1 < nb)
        def _():
            in_copy(start, rows, j + 1, 1 - s).start()
        in_copy(start, rows, j, s).wait()

        @pl.when(j >= 2)
        def _():
            out_copy(j - 2, s).wait()
        x = xbuf[s]
        a = jnp.dot(x, wgb[...], preferred_element_type=F32)
        u = jnp.dot(x, wub[...], preferred_element_type=F32)
        h = (a * jax.nn.sigmoid(a) * u).astype(BF16)
        ybuf[s] = jnp.dot(h, wdb[...], preferred_element_type=F32).astype(BF16)
        out_copy(j, s).start()
        return carry
    lax.fori_loop(0, nb, body, 0)

    @pl.when(e + 1 < pl.num_programs(0))
    def _():
        in_copy(start_ref[e + 1], rows_ref[e + 1], 0, 0).start()

    @pl.when(nb >= 2)
    def _():
        out_copy(nb - 2, nb % 2).wait()
    out_copy(nb - 1, (nb - 1) % 2).wait()


def _ffn(tabs, xs, wg, wu, wd):
    ew = lambda a: pl.BlockSpec((None,) + a.shape[1:], lambda e, *_: (e, 0, 0))
    tile = pltpu.VMEM((2, FFN_BLOCK, D_MODEL), BF16)
    return pl.pallas_call(
        _ffn_kernel,
        grid_spec=pltpu.PrefetchScalarGridSpec(
            num_scalar_prefetch=2, grid=(N_EXPERTS,),
            in_specs=[pl.BlockSpec(memory_space=pl.ANY), ew(wg), ew(wu), ew(wd)],
            out_specs=pl.BlockSpec(memory_space=pl.ANY),
            scratch_shapes=[tile, tile, pltpu.VMEM(wg.shape[1:], BF16), pltpu.VMEM(wu.shape[1:], BF16),
                            pltpu.VMEM(wd.shape[1:], BF16),
                            pltpu.SemaphoreType.DMA((2,)), pltpu.SemaphoreType.DMA((2,))]),
        out_shape=jax.ShapeDtypeStruct(xs.shape, BF16),
        compiler_params=_cparams(("arbitrary",)),
        name="moe_ffn",
    )(tabs["e_start"], tabs["e_rows"], xs, wg, wu, wd)


def _combine_kernel(src_ref, cnt_ref, dst_ref, tot_ref,
                    pos_ref, gs_ref, x1_ref, h_ref, sg_ref, su_ref, sd_ref, g2_ref, nf_ref, ys_ref,
                    yc_ref, yl_ref, ybuf, acc_ref, sem, *, tc, n_chunks, ctx_chunks):
    c = pl.program_id(0)
    slot = c % 2

    def issue(ci, s):
        def e_body(e8, carry):
            for j in range(DMA_UNROLL):
                i = ci * N_EXPERTS + e8 * DMA_UNROLL + j
                n = _aligned(cnt_ref[i])
                pltpu.make_async_copy(ys_ref.at[pl.ds(_aligned(dst_ref[i]), n)],
                                      ybuf.at[s, pl.ds(_aligned(src_ref[i]), n)], sem.at[s]).start()
            return carry
        lax.fori_loop(0, N_EXPERTS // DMA_UNROLL, e_body, 0)

    @pl.when(c == 0)
    def _():
        ybuf[...] = jnp.zeros(ybuf.shape, ybuf.dtype)
        issue(0, 0)

    @pl.when(c + 1 < n_chunks)
    def _():
        issue(c + 1, 1 - slot)

    h = h_ref[...]
    a = jnp.dot(h, sg_ref[...], preferred_element_type=F32)
    u = jnp.dot(h, su_ref[...], preferred_element_type=F32)
    acc_ref[...] = jnp.dot((a * jax.nn.sigmoid(a) * u).astype(BF16), sd_ref[...], preferred_element_type=F32)

    n = _aligned(tot_ref[c])
    pltpu.make_async_copy(ys_ref.at[pl.ds(0, n)], ybuf.at[slot, pl.ds(0, n)], sem.at[slot]).wait()

    tn = (((0,), (0,)), ((), ()))
    rows = _block_rows(tc)
    gates = gs_ref[...].astype(BF16)

    def rb_body(rb2, carry):
        part = None
        for half in range(PERM_UNROLL):
            r0 = pl.multiple_of((rb2 * PERM_UNROLL + half) * PERM_BLOCK, PERM_BLOCK)
            rel = _block_relative(pos_ref[...], r0)
            wt = jnp.zeros((PERM_BLOCK, tc), BF16)
            for k in range(TOP_K):
                wt = jnp.where(rel[k:k + 1, :] == rows, gates[k:k + 1, :], wt)
            d = lax.dot_general(wt, ybuf[slot, pl.ds(r0, PERM_BLOCK), :], tn, preferred_element_type=F32)
            part = d if part is None else part + d
        acc_ref[...] += part
        return carry
    lax.fori_loop(0, _perm_trips(tot_ref[c]), rb_body, 0)

    x2 = x1_ref[...] + g2_ref[...] * acc_ref[...]
    y = _rms(x2, nf_ref[...])

    @pl.when(c < ctx_chunks)
    def _():
        yc_ref[...] = y

    @pl.when(c >= ctx_chunks)
    def _():
        yl_ref[...] = y


def _combine(tabs, pos, gslot, x1, h2, sg, su, sd, mod, norm_f, ys, tc, row_fn, n_ctx):
    n = x1.shape[0]
    n_chunks = n // tc
    ctx_chunks = n_ctx // tc
    tok = lambda w: pl.BlockSpec((tc, w), lambda c, *_: (c, 0))
    slot = pl.BlockSpec((TOP_K, tc), lambda c, *_: (0, c))
    full = lambda a: pl.BlockSpec(a.shape, lambda c, *_: (0,) * a.ndim)
    return pl.pallas_call(
        functools.partial(_combine_kernel, tc=tc, n_chunks=n_chunks, ctx_chunks=ctx_chunks),
        grid_spec=pltpu.PrefetchScalarGridSpec(
            num_scalar_prefetch=4, grid=(n_chunks,),
            in_specs=[slot, slot, tok(D_MODEL), tok(D_MODEL), full(sg), full(su), full(sd),
                      _mod_spec(5, lambda c, *_: row_fn(c)), full(norm_f), pl.BlockSpec(memory_space=pl.ANY)],
            out_specs=[pl.BlockSpec((tc, D_MODEL), lambda c, *_: (jnp.minimum(c, ctx_chunks - 1), 0)),
                       pl.BlockSpec((tc, D_MODEL), lambda c, *_: (jnp.maximum(c - ctx_chunks, 0), 0))],
            scratch_shapes=[pltpu.VMEM((2, _local_cap(tc), D_MODEL), BF16),
                            pltpu.VMEM((tc, D_MODEL), F32),
                            pltpu.SemaphoreType.DMA((2,))]),
        out_shape=[jax.ShapeDtypeStruct((n_ctx, D_MODEL), F32), jax.ShapeDtypeStruct((n - n_ctx, D_MODEL), F32)],
        compiler_params=_cparams(("arbitrary",)),
        name="moe_combine",
    )(tabs["src"], tabs["cnt"], tabs["dst"], tabs["tot"], pos, gslot, x1, h2, sg, su, sd, mod, norm_f, ys)


def _dispatch_tables(pad_cnt):
    i32 = jnp.int32
    src = jnp.cumsum(pad_cnt, axis=1) - pad_cnt
    tot = jnp.sum(pad_cnt, axis=1)
    e_rows = jnp.sum(pad_cnt, axis=0)
    e_start = jnp.cumsum(e_rows) - e_rows
    dst = e_start[None, :] + jnp.cumsum(pad_cnt, axis=0) - pad_cnt
    return {"src": src.reshape(-1).astype(i32), "cnt": pad_cnt.reshape(-1).astype(i32),
            "dst": dst.reshape(-1).astype(i32), "tot": tot.astype(i32),
            "e_start": e_start.astype(i32), "e_rows": e_rows.astype(i32)}


def _moe(x1, h2, logits_t, bias_col, wg, wu, wd, sg, su, sd, mod, norm_f, row_fn, n_ctx):
    n = x1.shape[0]
    tc = MOE_CHUNK
    pos, gslot, pad_cnt = _route(logits_t, bias_col, tc)
    tabs = _dispatch_tables(pad_cnt[:, :, 0])
    xs = _gather(tabs, pos, h2, (n // tc) * _chunk_rows_max(tc), tc)
    ys = _ffn(tabs, xs, wg, wu, wd)
    return _combine(tabs, pos, gslot, x1, h2, sg, su, sd, mod, norm_f, ys, tc, row_fn, n_ctx)


def _channel_dft():
    ang = 2.0 * np.pi * ((np.arange(F_HD)[:, None] * np.arange(F_HD)[None, :]) % F_HD) / F_HD
    eye = np.eye(F_GROUPS)
    return np.concatenate([np.kron(eye, np.cos(ang)), np.kron(eye, -np.sin(ang))], axis=1).astype(np.float32)


def _rope_tables(t):
    pos = np.arange(t)
    row, col = pos // GRID_W, pos % GRID_W
    n_freq = HEAD_DIM // 4
    inv = ROPE_THETA ** (-np.arange(n_freq, dtype=np.float64) / n_freq)
    lane = np.arange(LANES)
    hd = lane % HEAD_DIM
    within = hd % (HEAD_DIM // 2)
    freq = within % n_freq
    first = within < n_freq
    p = np.where((hd < HEAD_DIM // 2)[None, :], row[:, None], col[:, None]).astype(np.float64)
    ang = p * inv[freq][None, :]
    cos, sin = np.cos(ang), np.sin(ang)
    sin_up = np.where(first[None, :], -sin, 0.0)
    sin_dn = np.where(first[None, :], 0.0, sin)
    return [jnp.asarray(a.astype(np.float32)) for a in (cos, sin_up, sin_dn)]


def _block_diag_pairs(wf):
    per = LANES // F_HD
    z = jnp.zeros((F_HD, F_HD), wf.dtype)
    blocks = []
    for c in range(F_GROUPS // per):
        rows = [jnp.concatenate([wf[c * per + r] if r == cc else z for cc in range(per)], axis=1) for r in range(per)]
        blocks.append(jnp.concatenate(rows, axis=0))
    return jnp.stack(blocks)


def kernel(x_prompt, x_sample, cache_k, cache_v, c, c_ctx, w_mod, b_mod, norm1, w_in, w_fourier, sink,
           w_out, norm2, w_router, router_bias, w_gate, w_up, w_down, ws_gate, ws_up, ws_down, norm_f):
    nb_ctx, t_ctx, _ = x_prompt.shape
    nb_lat, t_lat, _ = x_sample.shape
    l = 0
    cvec = jnp.concatenate([c_ctx[None, :], c, jnp.zeros((MOD_ROWS - 1 - nb_lat, D_MODEL), F32)], axis=0)
    mod = _modulation(cvec, w_mod[l], b_mod[l]).reshape(MOD_ROWS, 6, 1, D_MODEL)

    w_in_b = w_in[l].astype(BF16)
    dft_c = jnp.asarray(_channel_dft()).astype(BF16)
    wf_bd = _block_diag_pairs(w_fourier[l]).astype(BF16)
    wo_f = w_out[l][:F_DIM].astype(BF16)
    wo_a = w_out[l][F_DIM:].astype(BF16)
    w_router_t = w_router[l].T.astype(BF16)
    bias_col = router_bias[l].reshape(N_EXPERTS, 1)
    n1 = norm1[l].reshape(1, D_MODEL)
    n2 = norm2[l].reshape(1, D_MODEL)
    nf = norm_f.reshape(1, D_MODEL)
    sink_col = sink[l]
    wg, wu, wd = w_gate[l], w_up[l], w_down[l]
    sg, su, sd = ws_gate[l].astype(BF16), ws_up[l].astype(BF16), ws_down[l].astype(BF16)

    def mixers(x, batch, t, latent, kx, vx, t1, t2, tm):
        n = batch * t
        x2d = x.reshape(n, D_MODEL)
        tiles = t // tm
        row_fn = (lambda i: 1 + i // tiles) if latent else (lambda i: 0)
        rope_tabs = _rope_tables(t) if latent else None
        outs = _project(x2d, mod, n1, w_in_b, dft_c, rope_tabs, tm, tiles, row_fn)
        wr, wi, q, k, v = outs[:5]
        mixf = _fourier(wr, wi, wf_bd, batch, t, t1, t2)
        q3, k3, v3 = (a.reshape(batch, t, -1) for a in (q, k, v))
        if latent:
            mixa = _attention(q3, k3, v3, kx, vx, sink_col, batch, t, True)
        else:
            mixa = _attention(q3, None, None, k3, v3, sink_col, batch, t, False)
        return (x2d, mixf.reshape(n, F_DIM), mixa.reshape(n, ATTN_DIM)), outs[5:]

    ctx, (kf, vf) = mixers(x_prompt, nb_ctx, t_ctx, False, None, None, 16, 16, 256)
    kx = cache_k[:, l].reshape(nb_lat, -1, KV_DIM).astype(BF16)
    vx = cache_v[:, l].reshape(nb_lat, -1, KV_DIM).astype(BF16)
    lat, _ = mixers(x_sample, nb_lat, t_lat, True, kx, vx, 64, 64, 512)

    n_ctx = nb_ctx * t_ctx
    ctx_tiles = n_ctx // MOE_CHUNK
    lat_tiles = t_lat // MOE_CHUNK
    tile_row = lambda i: jnp.where(i < ctx_tiles, 0, 1 + (i - ctx_tiles) // lat_tiles)
    x1, h2, logits_t = _outproj(ctx, lat, wo_f, wo_a, mod, n2, w_router_t, MOE_CHUNK, tile_row)
    y_prompt, y_sample = _moe(x1, h2, logits_t, bias_col, wg, wu, wd, sg, su, sd, mod, nf, tile_row, n_ctx)
    new_k = kf.reshape(nb_ctx, 1, t_ctx, N_KV_HEADS, HEAD_DIM)
    new_v = vf.reshape(nb_ctx, 1, t_ctx, N_KV_HEADS, HEAD_DIM)
    return (y_prompt.reshape(x_prompt.shape), y_sample.reshape(x_sample.shape), new_k, new_v)
```

```python
import functools

import numpy as np
import jax
import jax.numpy as jnp
from jax import lax
from jax.experimental import pallas as pl
from jax.experimental.pallas import tpu as pltpu

F32 = jnp.float32
BF16 = jnp.bfloat16

D_MODEL = 1024
GRID_W = 64
HEAD_DIM = 64
N_HEADS = 12
N_KV_HEADS = 4
GQA_GROUP = N_HEADS // N_KV_HEADS
ATTN_DIM = N_HEADS * HEAD_DIM
KV_DIM = N_KV_HEADS * HEAD_DIM
F_GROUPS = 4
F_HD = 64
F_DIM = F_GROUPS * F_HD
IN_DIM = F_DIM + ATTN_DIM + 2 * KV_DIM
WINDOW = 128
Q_BLOCK = 128
ROPE_THETA = 10000.0
N_EXPERTS = 64
TOP_K = 8
N_EXPERT_GROUPS = 8
TOPK_GROUPS = 4
EXPERT_DIM = 256
SHARED_DIM = 256
ROUTED_SCALE = 2.5
EPS = 1e-6

LANES = 128
MOD_ROWS = 8
RUN_ALIGN = 16
PERM_BLOCK = 256
FFN_BLOCK = 512
FFN_LOOKAHEAD = 3
FFN_IN_SLOTS = FFN_LOOKAHEAD + 1
MOE_CHUNK = 256
OUTPROJ_TILE = 512
PERM_UNROLL = 2
DMA_UNROLL = 8
ISSUE_TRIPS = 4
RUNS_PER_TRIP = N_EXPERTS // ISSUE_TRIPS
VMEM_LIMIT = 56 * 1024 * 1024


def _cparams(sem):
    return pltpu.CompilerParams(dimension_semantics=sem, vmem_limit_bytes=VMEM_LIMIT)


def _bdot(a, b):
    return jnp.dot(a.astype(BF16), b.astype(BF16), preferred_element_type=F32)


def _rms(x, g):
    return x * lax.rsqrt(jnp.mean(x * x, axis=-1, keepdims=True) + EPS) * g


def _mod_kernel(c_ref, w_ref, b_ref, o_ref):
    c = c_ref[...]
    a = c * jax.nn.sigmoid(c)
    o_ref[...] = _bdot(a, w_ref[...]) + b_ref[...]


def _modulation(cvec, w_mod, b_mod):
    n = w_mod.shape[1]
    tn = 1024
    return pl.pallas_call(
        _mod_kernel,
        grid=(n // tn,),
        in_specs=[pl.BlockSpec((MOD_ROWS, D_MODEL), lambda j: (0, 0)),
                  pl.BlockSpec((D_MODEL, tn), lambda j: (0, j)),
                  pl.BlockSpec((1, tn), lambda j: (0, j))],
        out_specs=pl.BlockSpec((MOD_ROWS, tn), lambda j: (0, j)),
        out_shape=jax.ShapeDtypeStruct((MOD_ROWS, n), F32),
        compiler_params=_cparams(("parallel",)),
        name="modulation",
    )(cvec, w_mod, b_mod.reshape(1, n))


def _mod_spec(piece, row_fn):
    return pl.BlockSpec((None, None, 1, D_MODEL), lambda *idx: (row_fn(*idx), piece, 0, 0))


def _rope(x, cos, sin_up, sin_dn):
    outs = []
    for j in range(x.shape[1] // LANES):
        xj = x[:, j * LANES:(j + 1) * LANES]
        up = pltpu.roll(xj, LANES - 16, axis=1)
        dn = pltpu.roll(xj, 16, axis=1)
        outs.append(xj * cos + up * sin_up + dn * sin_dn)
    return jnp.concatenate(outs, axis=1)


def _proj_kernel(*refs, latent):
    if latent:
        (x_ref, sh_ref, sc_ref, g_ref, w_ref, dft_ref, cos_ref, sup_ref, sdn_ref,
         wr_ref, wi_ref, q_ref, k_ref, v_ref) = refs
    else:
        (x_ref, sh_ref, sc_ref, g_ref, w_ref, dft_ref,
         wr_ref, wi_ref, q_ref, k_ref, v_ref, kf_ref, vf_ref) = refs
    x = x_ref[...]
    h = _rms(x, g_ref[...]) * (1.0 + sc_ref[...]) + sh_ref[...]
    p = _bdot(h, w_ref[...])
    u = p[:, :F_DIM]
    q = p[:, F_DIM:F_DIM + ATTN_DIM]
    k = p[:, F_DIM + ATTN_DIM:F_DIM + ATTN_DIM + KV_DIM]
    v = p[:, F_DIM + ATTN_DIM + KV_DIM:]
    w = _bdot(u, dft_ref[...])
    wr_ref[...] = w[:, :F_DIM]
    wi_ref[...] = w[:, F_DIM:]
    if latent:
        cos, sup, sdn = cos_ref[...], sup_ref[...], sdn_ref[...]
        q = _rope(q, cos, sup, sdn)
        k = _rope(k, cos, sup, sdn)
    else:
        kf_ref[...] = k
        vf_ref[...] = v
    q_ref[...] = (q * (HEAD_DIM ** -0.5)).astype(BF16)
    k_ref[...] = k.astype(BF16)
    v_ref[...] = v.astype(BF16)


def _project(x2d, mod, norm1, w_in, dft_c, rope_tabs, tm, tiles_per_batch, row_fn):
    n = x2d.shape[0]
    latent = rope_tabs is not None
    tok = lambda w: pl.BlockSpec((tm, w), lambda i: (i, 0))
    full = lambda a: pl.BlockSpec(a.shape, lambda i: (0,) * a.ndim)
    in_specs = [tok(D_MODEL), _mod_spec(0, row_fn), _mod_spec(1, row_fn), full(norm1), full(w_in), full(dft_c)]
    args = [x2d, mod, mod, norm1, w_in, dft_c]
    out_specs = [tok(F_DIM), tok(F_DIM), tok(ATTN_DIM), tok(KV_DIM), tok(KV_DIM)]
    out_shape = [jax.ShapeDtypeStruct((n, F_DIM), F32), jax.ShapeDtypeStruct((n, F_DIM), F32),
                 jax.ShapeDtypeStruct((n, ATTN_DIM), BF16), jax.ShapeDtypeStruct((n, KV_DIM), BF16),
                 jax.ShapeDtypeStruct((n, KV_DIM), BF16)]
    if latent:
        pos = pl.BlockSpec((tm, LANES), lambda i: (i % tiles_per_batch, 0))
        in_specs += [pos, pos, pos]
        args += list(rope_tabs)
    else:
        out_specs += [tok(KV_DIM), tok(KV_DIM)]
        out_shape += [jax.ShapeDtypeStruct((n, KV_DIM), F32), jax.ShapeDtypeStruct((n, KV_DIM), F32)]
    return pl.pallas_call(
        functools.partial(_proj_kernel, latent=latent),
        grid=(n // tm,),
        in_specs=in_specs, out_specs=out_specs, out_shape=out_shape,
        compiler_params=_cparams(("parallel",)),
        name="project_latent" if latent else "project_context",
    )(*args)


def _fourier_kernel(wr_ref, wi_ref, a_ref, b_ref, wf_ref, o_ref, yr_ref, yi_ref, z_ref, *, t1, t2, scale):
    for j in range(t2):
        rows = pl.ds(j, t1, stride=t2)
        xin = jnp.concatenate([wr_ref[rows, :], wi_ref[rows, :]], axis=0)
        y = _bdot(a_ref[j], xin)
        yr_ref[rows, :] = y[:t1]
        yi_ref[rows, :] = y[t1:]
    bm = b_ref[...]
    for k1 in range(t1):
        rows = pl.ds(k1 * t2, t2)
        yin = jnp.concatenate([yr_ref[rows, :], yi_ref[rows, :]], axis=0)
        z_ref[pl.ds(k1, t2, stride=t1), :] = _bdot(bm, yin)
    o_ref[...] = (_bdot(z_ref[...], wf_ref[...]) * scale).astype(BF16)


def _dft_tables(t1, t2):
    t = t1 * t2
    k1 = np.arange(t1)[None, :, None]
    pos = (t2 * np.arange(t1)[None, None, :] + np.arange(t2)[:, None, None])
    ang = 2.0 * np.pi * ((k1 * pos) % t) / t
    c, s = np.cos(ang), np.sin(ang)
    a = np.concatenate([np.concatenate([c, s], axis=2), np.concatenate([-s, c], axis=2)], axis=1)
    ang2 = 2.0 * np.pi * ((np.arange(t2)[:, None] * np.arange(t2)[None, :]) % t2) / t2
    b = np.concatenate([np.cos(ang2), np.sin(ang2)], axis=1)
    return a.astype(np.float32), b.astype(np.float32)


def _fourier(wr, wi, wf_bd, batch, t, t1, t2):
    a_np, b_np = _dft_tables(t1, t2)
    a = jnp.asarray(a_np).astype(BF16)
    b = jnp.asarray(b_np).astype(BF16)
    cw = LANES
    blk = pl.BlockSpec((None, t, cw), lambda bi, ci: (bi, 0, ci))
    return pl.pallas_call(
        functools.partial(_fourier_kernel, t1=t1, t2=t2, scale=float((t * F_HD) ** -0.5)),
        grid=(batch, F_DIM // cw),
        in_specs=[blk, blk,
                  pl.BlockSpec(a.shape, lambda bi, ci: (0, 0, 0)),
                  pl.BlockSpec(b.shape, lambda bi, ci: (0, 0)),
                  pl.BlockSpec((None, cw, cw), lambda bi, ci: (ci, 0, 0))],
        out_specs=blk,
        out_shape=jax.ShapeDtypeStruct((batch, t, F_DIM), BF16),
        scratch_shapes=[pltpu.VMEM((t, cw), F32), pltpu.VMEM((t, cw), F32), pltpu.VMEM((t, cw), F32)],
        compiler_params=_cparams(("parallel", "parallel")),
        name="fourier_%d" % t,
    )(wr.reshape(batch, t, F_DIM), wi.reshape(batch, t, F_DIM), a, b, wf_bd)


def _attn_kernel(*refs, windowed, n_blocks):
    if windowed:
        q_ref, kp_ref, kc_ref, kn_ref, vp_ref, vc_ref, vn_ref, kx_ref, vx_ref, sink_ref, o_ref = refs
    else:
        q_ref, kx_ref, vx_ref, sink_ref, o_ref = refs
    nt = (((1,), (1,)), ((), ()))
    parts = [(kx_ref, vx_ref, None)]
    if windowed:
        i = pl.program_id(1)
        a = lax.broadcasted_iota(jnp.int32, (Q_BLOCK, Q_BLOCK), 0)
        j = lax.broadcasted_iota(jnp.int32, (Q_BLOCK, Q_BLOCK), 1)
        prev_ok = (j >= a) & (i > 0)
        next_ok = (j <= a) & (i < n_blocks - 1)
        parts = [(kp_ref, vp_ref, prev_ok), (kc_ref, vc_ref, None), (kn_ref, vn_ref, next_ok)] + parts
    stack = GQA_GROUP if windowed else 1
    outs = []
    for h0 in range(0, N_HEADS, stack):
        heads = range(h0, h0 + stack)
        kv = h0 // GQA_GROUP
        hs = slice(kv * HEAD_DIM, (kv + 1) * HEAD_DIM)
        qs = jnp.concatenate([q_ref[:, h * HEAD_DIM:(h + 1) * HEAD_DIM] for h in heads], axis=0)
        scores = []
        for k_ref, _, ok in parts:
            s = lax.dot_general(qs, k_ref[:, hs], nt, preferred_element_type=F32)
            if ok is not None:
                s = jnp.where(jnp.concatenate([ok] * stack, axis=0), s, -jnp.inf)
            scores.append(s)
        sk = jnp.concatenate([jnp.full((Q_BLOCK, 1), sink_ref[h], F32) for h in heads], axis=0)
        slabs = [s[:, c:c + LANES] for s in scores for c in range(0, s.shape[1], LANES)]
        m = jnp.maximum(jnp.max(functools.reduce(jnp.maximum, slabs), axis=1, keepdims=True), sk)
        probs = [jnp.exp(s - m) for s in scores]
        pslabs = [p[:, c:c + LANES] for p in probs for c in range(0, p.shape[1], LANES)]
        inv = 1.0 / (jnp.sum(functools.reduce(jnp.add, pslabs), axis=1, keepdims=True) + jnp.exp(sk - m))
        o = functools.reduce(jnp.add, [jnp.dot(p.astype(BF16), v_ref[:, hs], preferred_element_type=F32)
                                       for p, (_, v_ref, _) in zip(probs, parts)]) * inv
        outs += [o[g * Q_BLOCK:(g + 1) * Q_BLOCK] for g in range(stack)]
    o_ref[...] = jnp.concatenate(outs, axis=1).astype(BF16)


def _attention(q, k, v, kx, vx, sink_col, batch, t, windowed):
    nb = t // Q_BLOCK
    n_ctx = kx.shape[1]
    qspec = pl.BlockSpec((None, Q_BLOCK, ATTN_DIM), lambda b, i: (b, i, 0))
    xspec = pl.BlockSpec((None, n_ctx, KV_DIM), lambda b, i: (b, 0, 0))
    sspec = pl.BlockSpec(memory_space=pltpu.SMEM)
    if windowed:
        prev = pl.BlockSpec((None, Q_BLOCK, KV_DIM), lambda b, i: (b, jnp.maximum(i - 1, 0), 0))
        cur = pl.BlockSpec((None, Q_BLOCK, KV_DIM), lambda b, i: (b, i, 0))
        nxt = pl.BlockSpec((None, Q_BLOCK, KV_DIM), lambda b, i: (b, jnp.minimum(i + 1, nb - 1), 0))
        in_specs = [qspec, prev, cur, nxt, prev, cur, nxt, xspec, xspec, sspec]
        args = [q, k, k, k, v, v, v, kx, vx, sink_col]
    else:
        in_specs = [qspec, xspec, xspec, sspec]
        args = [q, kx, vx, sink_col]
    return pl.pallas_call(
        functools.partial(_attn_kernel, windowed=windowed, n_blocks=nb),
        grid=(batch, nb),
        in_specs=in_specs, out_specs=qspec,
        out_shape=jax.ShapeDtypeStruct((batch, t, ATTN_DIM), BF16),
        compiler_params=_cparams(("parallel", "parallel")),
        name="attention_latent" if windowed else "attention_context",
    )(*args)


def _outproj_kernel(xc_ref, mfc_ref, mac_ref, xl_ref, mfl_ref, mal_ref,
                    wof_ref, woa_ref, g1_ref, sh_ref, sc_ref, n2_ref, wr_ref,
                    x1_ref, h_ref, lg_ref, *, ctx_tiles):
    def body(x_ref, mf_ref, ma_ref):
        o = jnp.dot(mf_ref[...], wof_ref[...], preferred_element_type=F32)
        o = o + jnp.dot(ma_ref[...], woa_ref[...], preferred_element_type=F32)
        x1 = x_ref[...] + g1_ref[...] * o
        x1_ref[...] = x1
        h = (_rms(x1, n2_ref[...]) * (1.0 + sc_ref[...]) + sh_ref[...]).astype(BF16)
        h_ref[...] = h
        lg_ref[...] = lax.dot_general(wr_ref[...], h, (((1,), (1,)), ((), ())), preferred_element_type=F32)

    is_ctx = pl.program_id(0) < ctx_tiles
    pl.when(is_ctx)(lambda: body(xc_ref, mfc_ref, mac_ref))
    pl.when(jnp.logical_not(is_ctx))(lambda: body(xl_ref, mfl_ref, mal_ref))


def _outproj(ctx, lat, wo_f, wo_a, mod, norm2, w_router_t, tm, row_fn):
    n_c, n_l = ctx[0].shape[0], lat[0].shape[0]
    ctx_tiles = n_c // tm
    n = n_c + n_l
    ctok = lambda w: pl.BlockSpec((tm, w), lambda i: (jnp.minimum(i, ctx_tiles - 1), 0))
    ltok = lambda w: pl.BlockSpec((tm, w), lambda i: (jnp.maximum(i - ctx_tiles, 0), 0))
    tok = lambda w: pl.BlockSpec((tm, w), lambda i: (i, 0))
    full = lambda a: pl.BlockSpec(a.shape, lambda i: (0,) * a.ndim)
    widths = (D_MODEL, F_DIM, ATTN_DIM)
    return pl.pallas_call(
        functools.partial(_outproj_kernel, ctx_tiles=ctx_tiles),
        grid=(n // tm,),
        in_specs=[ctok(w) for w in widths] + [ltok(w) for w in widths] + [
            full(wo_f), full(wo_a), _mod_spec(2, row_fn), _mod_spec(3, row_fn), _mod_spec(4, row_fn),
            full(norm2), full(w_router_t)],
        out_specs=[tok(D_MODEL), tok(D_MODEL), pl.BlockSpec((N_EXPERTS, tm), lambda i: (0, i))],
        out_shape=[jax.ShapeDtypeStruct((n, D_MODEL), F32), jax.ShapeDtypeStruct((n, D_MODEL), BF16),
                   jax.ShapeDtypeStruct((N_EXPERTS, n), F32)],
        compiler_params=_cparams(("parallel",)),
        name="outproj",
    )(*ctx, *lat, wo_f, wo_a, mod, mod, mod, norm2, w_router_t)


def _rank_before(vals, n):
    idx = lax.broadcasted_iota(jnp.int32, vals.shape, 0)
    cnt = jnp.zeros(vals.shape, jnp.int32)
    for r in range(n):
        row = vals[r:r + 1, :]
        ge = jnp.where(row >= vals, 1, 0)
        gt = jnp.where(row > vals, 1, 0)
        cnt = cnt + jnp.where(idx > r, ge, gt)
    return cnt


def _route_kernel(lg_ref, bias_ref, lo_ref, up_ref, pos_ref, gs_ref, cnt_ref):
    s = jax.nn.sigmoid(lg_ref[...])
    sc = s + bias_ref[...]
    tn = s.shape[1]
    per = N_EXPERTS // N_EXPERT_GROUPS
    g3 = sc.reshape(N_EXPERT_GROUPS, per, tn)
    member = lax.broadcasted_iota(jnp.int32, g3.shape, 1)
    m1 = jnp.max(g3, axis=1, keepdims=True)
    first = jnp.min(jnp.where(g3 == m1, member, per), axis=1, keepdims=True)
    m2 = jnp.max(jnp.where(member == first, -jnp.inf, g3), axis=1, keepdims=True)
    gscore = (m1 + m2).reshape(N_EXPERT_GROUPS, tn)
    gsel = _rank_before(gscore, N_EXPERT_GROUPS) < TOPK_GROUPS
    emask = jnp.broadcast_to(gsel.reshape(N_EXPERT_GROUPS, 1, tn), g3.shape).reshape(N_EXPERTS, tn)
    masked = jnp.where(emask, sc, -jnp.inf)
    sel = _rank_before(masked, N_EXPERTS) < TOP_K
    w = jnp.where(sel, s, 0.0)
    gate = w / jnp.sum(w, axis=0, keepdims=True) * ROUTED_SCALE

    self = jnp.where(sel, 1.0, 0.0)
    selb = self.astype(BF16)
    cnt = jnp.sum(self, axis=1, keepdims=True)
    pad = jnp.maximum(jnp.floor((cnt + (RUN_ALIGN - 1)) * (1.0 / RUN_ALIGN)), 1.0) * RUN_ALIGN
    lo = lo_ref[...]
    soff = jnp.dot(lo, jnp.broadcast_to(pad, (N_EXPERTS, LANES)).astype(BF16), preferred_element_type=F32)[:, :1]
    rank = jnp.dot(selb, up_ref[...], preferred_element_type=F32)
    kidx = jnp.dot(lo, selb, preferred_element_type=F32)
    pos_e = jnp.where(sel, soff + rank, 0.0)
    rows_p, rows_g = [], []
    for k in range(TOP_K):
        m = kidx == k
        rows_p.append(jnp.sum(jnp.where(m, pos_e, 0.0), axis=0, keepdims=True))
        rows_g.append(jnp.sum(jnp.where(m, gate, 0.0), axis=0, keepdims=True))
    pos_ref[...] = jnp.concatenate(rows_p, axis=0).astype(jnp.int32)
    gs_ref[...] = jnp.concatenate(rows_g, axis=0)
    cnt_ref[...] = pad.astype(jnp.int32)


def _route(logits_t, bias_col, tc):
    n = logits_t.shape[1]
    lo = jnp.asarray(np.tril(np.ones((N_EXPERTS, N_EXPERTS), np.float32), -1)).astype(BF16)
    up = jnp.asarray(np.triu(np.ones((tc, tc), np.float32), 1)).astype(BF16)
    slot = pl.BlockSpec((TOP_K, tc), lambda i: (0, i))
    return pl.pallas_call(
        _route_kernel,
        grid=(n // tc,),
        in_specs=[pl.BlockSpec((N_EXPERTS, tc), lambda i: (0, i)),
                  pl.BlockSpec((N_EXPERTS, 1), lambda i: (0, 0)),
                  pl.BlockSpec(lo.shape, lambda i: (0, 0)),
                  pl.BlockSpec(up.shape, lambda i: (0, 0))],
        out_specs=[slot, slot, pl.BlockSpec((None, N_EXPERTS, 1), lambda i: (i, 0, 0))],
        out_shape=[jax.ShapeDtypeStruct((TOP_K, n), jnp.int32), jax.ShapeDtypeStruct((TOP_K, n), F32),
                   jax.ShapeDtypeStruct((n // tc, N_EXPERTS, 1), jnp.int32)],
        compiler_params=_cparams(("parallel",)),
        name="route",
    )(logits_t, bias_col, lo, up)


def _aligned(x):
    return pl.multiple_of(x, RUN_ALIGN)


def _block_rows(tc):
    return lax.broadcasted_iota(jnp.int32, (PERM_BLOCK, tc), 0).astype(F32).astype(BF16)


def _block_relative(pos, r0):
    return (pos - r0).astype(F32).astype(BF16)


def _gather_kernel(src_ref, cnt_ref, dst_ref, tot_ref,
                   pos_ref, x_ref, xs_ref, z_ref, sem, *, tc, n_chunks):
    c = pl.program_id(0)
    slot = c % 2

    def wait_chunk(ci, s):
        n = _aligned(tot_ref[ci])
        pltpu.make_async_copy(z_ref.at[s, pl.ds(0, n)], xs_ref.at[pl.ds(0, n)], sem.at[s]).wait()

    def start_runs(ci, s, first, count):
        for j in range(count):
            i = ci * N_EXPERTS + first + j
            n = _aligned(cnt_ref[i])
            pltpu.make_async_copy(z_ref.at[s, pl.ds(_aligned(src_ref[i]), n)],
                                  xs_ref.at[pl.ds(_aligned(dst_ref[i]), n)], sem.at[s]).start()

    @pl.when(c >= 2)
    def _():
        wait_chunk(c - 2, slot)

    rows = _block_rows(tc)

    def permute_blocks(trip):
        for half in range(PERM_UNROLL):
            r0 = pl.multiple_of((trip * PERM_UNROLL + half) * PERM_BLOCK, PERM_BLOCK)
            rel = _block_relative(pos_ref[...], r0)
            onehot = jnp.zeros((PERM_BLOCK, tc), BF16)
            for k in range(TOP_K):
                onehot = jnp.where(rel[k:k + 1, :] == rows, jnp.ones((), BF16), onehot)
            z_ref[slot, pl.ds(r0, PERM_BLOCK), :] = jnp.dot(
                onehot, x_ref[...], preferred_element_type=F32).astype(BF16)

    def plain_trip(trip, carry):
        permute_blocks(trip)
        return carry

    def issuing_trip(trip, carry):
        permute_blocks(trip)
        start_runs(c - 1, 1 - slot, trip * RUNS_PER_TRIP, RUNS_PER_TRIP)
        return carry

    trips = _perm_trips(tot_ref[c])

    @pl.when(c == 0)
    def _():
        lax.fori_loop(0, ISSUE_TRIPS, plain_trip, 0)

    @pl.when(c > 0)
    def _():
        lax.fori_loop(0, ISSUE_TRIPS, issuing_trip, 0)

    lax.fori_loop(ISSUE_TRIPS, trips, plain_trip, 0)

    @pl.when(c == n_chunks - 1)
    def _():
        def e_body(e8, carry):
            start_runs(c, slot, e8 * DMA_UNROLL, DMA_UNROLL)
            return carry
        lax.fori_loop(0, N_EXPERTS // DMA_UNROLL, e_body, 0)
        wait_chunk(c, slot)
        if n_chunks > 1:
            wait_chunk(c - 1, 1 - slot)


def _perm_trips(rows):
    step = PERM_BLOCK * PERM_UNROLL
    return (rows + step - 1) // step


def _chunk_rows_max(tc):
    return TOP_K * tc + N_EXPERTS * RUN_ALIGN


def _local_cap(tc):
    step = PERM_BLOCK * PERM_UNROLL
    return -(-_chunk_rows_max(tc) // step) * step


def _gather(tabs, pos, h2, r_max, tc):
    n = h2.shape[0]
    n_chunks = n // tc
    assert ISSUE_TRIPS * RUNS_PER_TRIP == N_EXPERTS and ISSUE_TRIPS * PERM_BLOCK * PERM_UNROLL <= TOP_K * tc
    return pl.pallas_call(
        functools.partial(_gather_kernel, tc=tc, n_chunks=n_chunks),
        grid_spec=pltpu.PrefetchScalarGridSpec(
            num_scalar_prefetch=4, grid=(n_chunks,),
            in_specs=[pl.BlockSpec((TOP_K, tc), lambda c, *_: (0, c)),
                      pl.BlockSpec((tc, D_MODEL), lambda c, *_: (c, 0))],
            out_specs=pl.BlockSpec(memory_space=pl.ANY),
            scratch_shapes=[pltpu.VMEM((2, _local_cap(tc), D_MODEL), BF16),
                            pltpu.SemaphoreType.DMA((2,))]),
        out_shape=jax.ShapeDtypeStruct((r_max, D_MODEL), BF16),
        compiler_params=_cparams(("arbitrary",)),
        name="moe_gather",
    )(tabs["src"], tabs["cnt"], tabs["dst"], tabs["tot"], pos, h2)


def _ffn_kernel(first_ref, ntile_ref, tstart_ref, trows_ref, total_ref, xs_ref, wg_ref, wu_ref, wd_ref, ys_ref,
                xbuf, ybuf, wgb, wub, wdb, sem_in, sem_out):
    e = pl.program_id(0)
    total = total_ref[0]

    def in_copy(g):
        s = g % FFN_IN_SLOTS
        n = _aligned(trows_ref[g])
        return pltpu.make_async_copy(xs_ref.at[pl.ds(_aligned(tstart_ref[g]), n)],
                                     xbuf.at[s, pl.ds(0, n)], sem_in.at[s])

    def out_copy(g):
        s = g % 2
        n = _aligned(trows_ref[g])
        return pltpu.make_async_copy(ybuf.at[s, pl.ds(0, n)],
                                     ys_ref.at[pl.ds(_aligned(tstart_ref[g]), n)], sem_out.at[s])

    @pl.when(e == 0)
    def _():
        xbuf[...] = jnp.zeros(xbuf.shape, xbuf.dtype)
        for g in range(FFN_LOOKAHEAD):
            @pl.when(g < total)
            def _():
                in_copy(g).start()

    wgb[...] = wg_ref[...].astype(BF16)
    wub[...] = wu_ref[...].astype(BF16)
    wdb[...] = wd_ref[...].astype(BF16)

    def body(g, carry):
        @pl.when(g + FFN_LOOKAHEAD < total)
        def _():
            in_copy(g + FFN_LOOKAHEAD).start()
        in_copy(g).wait()

        @pl.when(g >= 2)
        def _():
            out_copy(g - 2).wait()
        x = xbuf[g % FFN_IN_SLOTS]
        a = jnp.dot(x, wgb[...], preferred_element_type=F32)
        u = jnp.dot(x, wub[...], preferred_element_type=F32)
        h = (a * jax.nn.sigmoid(a) * u).astype(BF16)
        ybuf[g % 2] = jnp.dot(h, wdb[...], preferred_element_type=F32).astype(BF16)
        out_copy(g).start()
        return carry
    lax.fori_loop(first_ref[e], first_ref[e] + ntile_ref[e], body, 0)

    @pl.when(e == pl.num_programs(0) - 1)
    def _():
        @pl.when(total >= 2)
        def _():
            out_copy(total - 2).wait()
        out_copy(total - 1).wait()


def _ffn(tabs, xs, wg, wu, wd):
    ew = lambda a: pl.BlockSpec((None,) + a.shape[1:], lambda e, *_: (e, 0, 0))
    return pl.pallas_call(
        _ffn_kernel,
        grid_spec=pltpu.PrefetchScalarGridSpec(
            num_scalar_prefetch=5, grid=(N_EXPERTS,),
            in_specs=[pl.BlockSpec(memory_space=pl.ANY), ew(wg), ew(wu), ew(wd)],
            out_specs=pl.BlockSpec(memory_space=pl.ANY),
            scratch_shapes=[pltpu.VMEM((FFN_IN_SLOTS, FFN_BLOCK, D_MODEL), BF16),
                            pltpu.VMEM((2, FFN_BLOCK, D_MODEL), BF16),
                            pltpu.VMEM(wg.shape[1:], BF16), pltpu.VMEM(wu.shape[1:], BF16),
                            pltpu.VMEM(wd.shape[1:], BF16),
                            pltpu.SemaphoreType.DMA((FFN_IN_SLOTS,)), pltpu.SemaphoreType.DMA((2,))]),
        out_shape=jax.ShapeDtypeStruct(xs.shape, BF16),
        compiler_params=_cparams(("arbitrary",)),
        name="moe_ffn",
    )(tabs["first_tile"], tabs["n_tiles"], tabs["tile_start"], tabs["tile_rows"], tabs["total_tiles"],
      xs, wg, wu, wd)


def _combine_kernel(src_ref, cnt_ref, dst_ref, tot_ref,
                    pos_ref, gs_ref, x1_ref, h_ref, sg_ref, su_ref, sd_ref, g2_ref, nf_ref, ys_ref,
                    yc_ref, yl_ref, ybuf, acc_ref, sem, *, tc, n_chunks, ctx_chunks):
    c = pl.program_id(0)
    slot = c % 2

    def issue(ci, s):
        def e_body(e8, carry):
            for j in range(DMA_UNROLL):
                i = ci * N_EXPERTS + e8 * DMA_UNROLL + j
                n = _aligned(cnt_ref[i])
                pltpu.make_async_copy(ys_ref.at[pl.ds(_aligned(dst_ref[i]), n)],
                                      ybuf.at[s, pl.ds(_aligned(src_ref[i]), n)], sem.at[s]).start()
            return carry
        lax.fori_loop(0, N_EXPERTS // DMA_UNROLL, e_body, 0)

    @pl.when(c == 0)
    def _():
        ybuf[...] = jnp.zeros(ybuf.shape, ybuf.dtype)
        issue(0, 0)

    @pl.when(c + 1 < n_chunks)
    def _():
        issue(c + 1, 1 - slot)

    h = h_ref[...]
    a = jnp.dot(h, sg_ref[...], preferred_element_type=F32)
    u = jnp.dot(h, su_ref[...], preferred_element_type=F32)
    acc_ref[...] = jnp.dot((a * jax.nn.sigmoid(a) * u).astype(BF16), sd_ref[...], preferred_element_type=F32)

    n = _aligned(tot_ref[c])
    pltpu.make_async_copy(ys_ref.at[pl.ds(0, n)], ybuf.at[slot, pl.ds(0, n)], sem.at[slot]).wait()

    tn = (((0,), (0,)), ((), ()))
    rows = _block_rows(tc)
    gates = gs_ref[...].astype(BF16)

    def rb_body(rb2, carry):
        part = None
        for half in range(PERM_UNROLL):
            r0 = pl.multiple_of((rb2 * PERM_UNROLL + half) * PERM_BLOCK, PERM_BLOCK)
            rel = _block_relative(pos_ref[...], r0)
            wt = jnp.zeros((PERM_BLOCK, tc), BF16)
            for k in range(TOP_K):
                wt = jnp.where(rel[k:k + 1, :] == rows, gates[k:k + 1, :], wt)
            d = lax.dot_general(wt, ybuf[slot, pl.ds(r0, PERM_BLOCK), :], tn, preferred_element_type=F32)
            part = d if part is None else part + d
        acc_ref[...] += part
        return carry
    lax.fori_loop(0, _perm_trips(tot_ref[c]), rb_body, 0)

    x2 = x1_ref[...] + g2_ref[...] * acc_ref[...]
    y = _rms(x2, nf_ref[...])

    @pl.when(c < ctx_chunks)
    def _():
        yc_ref[...] = y

    @pl.when(c >= ctx_chunks)
    def _():
        yl_ref[...] = y


def _combine(tabs, pos, gslot, x1, h2, sg, su, sd, mod, norm_f, ys, tc, row_fn, n_ctx):
    n = x1.shape[0]
    n_chunks = n // tc
    ctx_chunks = n_ctx // tc
    tok = lambda w: pl.BlockSpec((tc, w), lambda c, *_: (c, 0))
    slot = pl.BlockSpec((TOP_K, tc), lambda c, *_: (0, c))
    full = lambda a: pl.BlockSpec(a.shape, lambda c, *_: (0,) * a.ndim)
    return pl.pallas_call(
        functools.partial(_combine_kernel, tc=tc, n_chunks=n_chunks, ctx_chunks=ctx_chunks),
        grid_spec=pltpu.PrefetchScalarGridSpec(
            num_scalar_prefetch=4, grid=(n_chunks,),
            in_specs=[slot, slot, tok(D_MODEL), tok(D_MODEL), full(sg), full(su), full(sd),
                      _mod_spec(5, lambda c, *_: row_fn(c)), full(norm_f), pl.BlockSpec(memory_space=pl.ANY)],
            out_specs=[pl.BlockSpec((tc, D_MODEL), lambda c, *_: (jnp.minimum(c, ctx_chunks - 1), 0)),
                       pl.BlockSpec((tc, D_MODEL), lambda c, *_: (jnp.maximum(c - ctx_chunks, 0), 0))],
            scratch_shapes=[pltpu.VMEM((2, _local_cap(tc), D_MODEL), BF16),
                            pltpu.VMEM((tc, D_MODEL), F32),
                            pltpu.SemaphoreType.DMA((2,))]),
        out_shape=[jax.ShapeDtypeStruct((n_ctx, D_MODEL), F32), jax.ShapeDtypeStruct((n - n_ctx, D_MODEL), F32)],
        compiler_params=_cparams(("arbitrary",)),
        name="moe_combine",
    )(tabs["src"], tabs["cnt"], tabs["dst"], tabs["tot"], pos, gslot, x1, h2, sg, su, sd, mod, norm_f, ys)


def _dispatch_tables(pad_cnt, r_max):
    i32 = jnp.int32
    src = jnp.cumsum(pad_cnt, axis=1) - pad_cnt
    tot = jnp.sum(pad_cnt, axis=1)
    e_rows = jnp.sum(pad_cnt, axis=0)
    e_start = jnp.cumsum(e_rows) - e_rows
    dst = e_start[None, :] + jnp.cumsum(pad_cnt, axis=0) - pad_cnt
    n_tiles = (e_rows + FFN_BLOCK - 1) // FFN_BLOCK
    tile_end = jnp.cumsum(n_tiles)
    first_tile = tile_end - n_tiles
    g = jnp.arange(r_max // FFN_BLOCK + N_EXPERTS)
    tile_e = jnp.minimum(jnp.sum(tile_end[None, :] <= g[:, None], axis=1), N_EXPERTS - 1)
    offset = (g - first_tile[tile_e]) * FFN_BLOCK
    tile_start = jnp.clip(e_start[tile_e] + offset, 0, r_max - FFN_BLOCK)
    tile_rows = jnp.clip(e_rows[tile_e] - offset, RUN_ALIGN, FFN_BLOCK)
    return {"src": src.reshape(-1).astype(i32), "cnt": pad_cnt.reshape(-1).astype(i32),
            "dst": dst.reshape(-1).astype(i32), "tot": tot.astype(i32),
            "first_tile": first_tile.astype(i32), "n_tiles": n_tiles.astype(i32),
            "tile_start": tile_start.astype(i32), "tile_rows": tile_rows.astype(i32),
            "total_tiles": tile_end[-1:].astype(i32)}


def _moe(x1, h2, logits_t, bias_col, wg, wu, wd, sg, su, sd, mod, norm_f, row_fn, n_ctx):
    n = x1.shape[0]
    tc = MOE_CHUNK
    pos, gslot, pad_cnt = _route(logits_t, bias_col, tc)
    r_max = (n // tc) * _chunk_rows_max(tc)
    tabs = _dispatch_tables(pad_cnt[:, :, 0], r_max)
    xs = _gather(tabs, pos, h2, r_max, tc)
    ys = _ffn(tabs, xs, wg, wu, wd)
    return _combine(tabs, pos, gslot, x1, h2, sg, su, sd, mod, norm_f, ys, tc, row_fn, n_ctx)


def _channel_dft():
    ang = 2.0 * np.pi * ((np.arange(F_HD)[:, None] * np.arange(F_HD)[None, :]) % F_HD) / F_HD
    eye = np.eye(F_GROUPS)
    return np.concatenate([np.kron(eye, np.cos(ang)), np.kron(eye, -np.sin(ang))], axis=1).astype(np.float32)


def _rope_tables(t):
    pos = np.arange(t)
    row, col = pos // GRID_W, pos % GRID_W
    n_freq = HEAD_DIM // 4
    inv = ROPE_THETA ** (-np.arange(n_freq, dtype=np.float64) / n_freq)
    lane = np.arange(LANES)
    hd = lane % HEAD_DIM
    within = hd % (HEAD_DIM // 2)
    freq = within % n_freq
    first = within < n_freq
    p = np.where((hd < HEAD_DIM // 2)[None, :], row[:, None], col[:, None]).astype(np.float64)
    ang = p * inv[freq][None, :]
    cos, sin = np.cos(ang), np.sin(ang)
    sin_up = np.where(first[None, :], -sin, 0.0)
    sin_dn = np.where(first[None, :], 0.0, sin)
    return [jnp.asarray(a.astype(np.float32)) for a in (cos, sin_up, sin_dn)]


def _block_diag_pairs(wf):
    per = LANES // F_HD
    z = jnp.zeros((F_HD, F_HD), wf.dtype)
    blocks = []
    for c in range(F_GROUPS // per):
        rows = [jnp.concatenate([wf[c * per + r] if r == cc else z for cc in range(per)], axis=1) for r in range(per)]
        blocks.append(jnp.concatenate(rows, axis=0))
    return jnp.stack(blocks)


def kernel(x_prompt, x_sample, cache_k, cache_v, c, c_ctx, w_mod, b_mod, norm1, w_in, w_fourier, sink,
           w_out, norm2, w_router, router_bias, w_gate, w_up, w_down, ws_gate, ws_up, ws_down, norm_f):
    nb_ctx, t_ctx, _ = x_prompt.shape
    nb_lat, t_lat, _ = x_sample.shape
    l = 0
    cvec = jnp.concatenate([c_ctx[None, :], c, jnp.zeros((MOD_ROWS - 1 - nb_lat, D_MODEL), F32)], axis=0)
    mod = _modulation(cvec, w_mod[l], b_mod[l]).reshape(MOD_ROWS, 6, 1, D_MODEL)

    w_in_b = w_in[l].astype(BF16)
    dft_c = jnp.asarray(_channel_dft()).astype(BF16)
    wf_bd = _block_diag_pairs(w_fourier[l]).astype(BF16)
    wo_f = w_out[l][:F_DIM].astype(BF16)
    wo_a = w_out[l][F_DIM:].astype(BF16)
    w_router_t = w_router[l].T.astype(BF16)
    bias_col = router_bias[l].reshape(N_EXPERTS, 1)
    n1 = norm1[l].reshape(1, D_MODEL)
    n2 = norm2[l].reshape(1, D_MODEL)
    nf = norm_f.reshape(1, D_MODEL)
    sink_col = sink[l]
    wg, wu, wd = w_gate[l], w_up[l], w_down[l]
    sg, su, sd = ws_gate[l].astype(BF16), ws_up[l].astype(BF16), ws_down[l].astype(BF16)

    def mixers(x, batch, t, latent, kx, vx, t1, t2, tm):
        n = batch * t
        x2d = x.reshape(n, D_MODEL)
        tiles = t // tm
        row_fn = (lambda i: 1 + i // tiles) if latent else (lambda i: 0)
        rope_tabs = _rope_tables(t) if latent else None
        outs = _project(x2d, mod, n1, w_in_b, dft_c, rope_tabs, tm, tiles, row_fn)
        wr, wi, q, k, v = outs[:5]
        mixf = _fourier(wr, wi, wf_bd, batch, t, t1, t2)
        q3, k3, v3 = (a.reshape(batch, t, -1) for a in (q, k, v))
        if latent:
            mixa = _attention(q3, k3, v3, kx, vx, sink_col, batch, t, True)
        else:
            mixa = _attention(q3, None, None, k3, v3, sink_col, batch, t, False)
        return (x2d, mixf.reshape(n, F_DIM), mixa.reshape(n, ATTN_DIM)), outs[5:]

    ctx, (kf, vf) = mixers(x_prompt, nb_ctx, t_ctx, False, None, None, 16, 16, 256)
    kx = cache_k[:, l].reshape(nb_lat, -1, KV_DIM).astype(BF16)
    vx = cache_v[:, l].reshape(nb_lat, -1, KV_DIM).astype(BF16)
    lat, _ = mixers(x_sample, nb_lat, t_lat, True, kx, vx, 64, 64, 512)

    n_ctx = nb_ctx * t_ctx

    def tile_row(tile):
        ctx_tiles, lat_tiles = n_ctx // tile, t_lat // tile
        return lambda i: jnp.where(i < ctx_tiles, 0, 1 + (i - ctx_tiles) // lat_tiles)

    x1, h2, logits_t = _outproj(ctx, lat, wo_f, wo_a, mod, n2, w_router_t, OUTPROJ_TILE, tile_row(OUTPROJ_TILE))
    y_prompt, y_sample = _moe(x1, h2, logits_t, bias_col, wg, wu, wd, sg, su, sd, mod, nf, tile_row(MOE_CHUNK),
                              n_ctx)
    new_k = kf.reshape(nb_ctx, 1, t_ctx, N_KV_HEADS, HEAD_DIM)
    new_v = vf.reshape(nb_ctx, 1, t_ctx, N_KV_HEADS, HEAD_DIM)
    return (y_prompt.reshape(x_prompt.shape), y_sample.reshape(x_sample.shape), new_k, new_v)
```

```python
import functools

import numpy as np
import jax
import jax.numpy as jnp
from jax import lax
from jax.experimental import pallas as pl
from jax.experimental.pallas import tpu as pltpu

F32 = jnp.float32
BF16 = jnp.bfloat16

D_MODEL = 1024
GRID_W = 64
HEAD_DIM = 64
N_HEADS = 12
N_KV_HEADS = 4
GQA_GROUP = N_HEADS // N_KV_HEADS
ATTN_DIM = N_HEADS * HEAD_DIM
KV_DIM = N_KV_HEADS * HEAD_DIM
F_GROUPS = 4
F_HD = 64
F_DIM = F_GROUPS * F_HD
IN_DIM = F_DIM + ATTN_DIM + 2 * KV_DIM
WINDOW = 128
Q_BLOCK = 128
ROPE_THETA = 10000.0
N_EXPERTS = 64
TOP_K = 8
N_EXPERT_GROUPS = 8
TOPK_GROUPS = 4
EXPERT_DIM = 256
SHARED_DIM = 256
ROUTED_SCALE = 2.5
EPS = 1e-6

LANES = 128
MOD_ROWS = 8
RUN_ALIGN = 16
PERM_BLOCK = 256
FFN_BLOCK = 512
FFN_LOOKAHEAD = 3
FFN_IN_SLOTS = FFN_LOOKAHEAD + 1
MOE_CHUNK = 256
OUTPROJ_TILE = 512
PERM_UNROLL = 2
DMA_UNROLL = 8
ISSUE_TRIPS = 4
RUNS_PER_TRIP = N_EXPERTS // ISSUE_TRIPS
VMEM_LIMIT = 56 * 1024 * 1024


def _cparams(sem, manual_dma=False):
    return pltpu.CompilerParams(dimension_semantics=sem, vmem_limit_bytes=VMEM_LIMIT,
                                disable_bounds_checks=manual_dma)


def _bdot(a, b):
    return jnp.dot(a.astype(BF16), b.astype(BF16), preferred_element_type=F32)


def _rms(x, g):
    return x * lax.rsqrt(jnp.mean(x * x, axis=-1, keepdims=True) + EPS) * g


def _mod_kernel(c_ref, w_ref, b_ref, o_ref):
    c = c_ref[...]
    a = c * jax.nn.sigmoid(c)
    o_ref[...] = _bdot(a, w_ref[...]) + b_ref[...]


def _modulation(cvec, w_mod, b_mod):
    n = w_mod.shape[1]
    tn = 1024
    return pl.pallas_call(
        _mod_kernel,
        grid=(n // tn,),
        in_specs=[pl.BlockSpec((MOD_ROWS, D_MODEL), lambda j: (0, 0)),
                  pl.BlockSpec((D_MODEL, tn), lambda j: (0, j)),
                  pl.BlockSpec((1, tn), lambda j: (0, j))],
        out_specs=pl.BlockSpec((MOD_ROWS, tn), lambda j: (0, j)),
        out_shape=jax.ShapeDtypeStruct((MOD_ROWS, n), F32),
        compiler_params=_cparams(("parallel",)),
        name="modulation",
    )(cvec, w_mod, b_mod.reshape(1, n))


def _mod_spec(piece, row_fn):
    return pl.BlockSpec((None, None, 1, D_MODEL), lambda *idx: (row_fn(*idx), piece, 0, 0))


def _rope(x, cos, sin_up, sin_dn):
    outs = []
    for j in range(x.shape[1] // LANES):
        xj = x[:, j * LANES:(j + 1) * LANES]
        up = pltpu.roll(xj, LANES - 16, axis=1)
        dn = pltpu.roll(xj, 16, axis=1)
        outs.append(xj * cos + up * sin_up + dn * sin_dn)
    return jnp.concatenate(outs, axis=1)


def _proj_kernel(*refs, latent):
    if latent:
        (x_ref, sh_ref, sc_ref, g_ref, w_ref, dft_ref, cos_ref, sup_ref, sdn_ref,
         wr_ref, wi_ref, q_ref, k_ref, v_ref) = refs
    else:
        (x_ref, sh_ref, sc_ref, g_ref, w_ref, dft_ref,
         wr_ref, wi_ref, q_ref, k_ref, v_ref, kf_ref, vf_ref) = refs
    x = x_ref[...]
    h = _rms(x, g_ref[...]) * (1.0 + sc_ref[...]) + sh_ref[...]
    p = _bdot(h, w_ref[...])
    u = p[:, :F_DIM]
    q = p[:, F_DIM:F_DIM + ATTN_DIM]
    k = p[:, F_DIM + ATTN_DIM:F_DIM + ATTN_DIM + KV_DIM]
    v = p[:, F_DIM + ATTN_DIM + KV_DIM:]
    w = _bdot(u, dft_ref[...])
    wr_ref[...] = w[:, :F_DIM]
    wi_ref[...] = w[:, F_DIM:]
    if latent:
        cos, sup, sdn = cos_ref[...], sup_ref[...], sdn_ref[...]
        q = _rope(q, cos, sup, sdn)
        k = _rope(k, cos, sup, sdn)
    else:
        kf_ref[...] = k
        vf_ref[...] = v
    q_ref[...] = (q * (HEAD_DIM ** -0.5)).astype(BF16)
    k_ref[...] = k.astype(BF16)
    v_ref[...] = v.astype(BF16)


def _project(x2d, mod, norm1, w_in, dft_c, rope_tabs, tm, tiles_per_batch, row_fn):
    n = x2d.shape[0]
    latent = rope_tabs is not None
    tok = lambda w: pl.BlockSpec((tm, w), lambda i: (i, 0))
    full = lambda a: pl.BlockSpec(a.shape, lambda i: (0,) * a.ndim)
    in_specs = [tok(D_MODEL), _mod_spec(0, row_fn), _mod_spec(1, row_fn), full(norm1), full(w_in), full(dft_c)]
    args = [x2d, mod, mod, norm1, w_in, dft_c]
    out_specs = [tok(F_DIM), tok(F_DIM), tok(ATTN_DIM), tok(KV_DIM), tok(KV_DIM)]
    out_shape = [jax.ShapeDtypeStruct((n, F_DIM), F32), jax.ShapeDtypeStruct((n, F_DIM), F32),
                 jax.ShapeDtypeStruct((n, ATTN_DIM), BF16), jax.ShapeDtypeStruct((n, KV_DIM), BF16),
                 jax.ShapeDtypeStruct((n, KV_DIM), BF16)]
    if latent:
        pos = pl.BlockSpec((tm, LANES), lambda i: (i % tiles_per_batch, 0))
        in_specs += [pos, pos, pos]
        args += list(rope_tabs)
    else:
        out_specs += [tok(KV_DIM), tok(KV_DIM)]
        out_shape += [jax.ShapeDtypeStruct((n, KV_DIM), F32), jax.ShapeDtypeStruct((n, KV_DIM), F32)]
    return pl.pallas_call(
        functools.partial(_proj_kernel, latent=latent),
        grid=(n // tm,),
        in_specs=in_specs, out_specs=out_specs, out_shape=out_shape,
        compiler_params=_cparams(("parallel",)),
        name="project_latent" if latent else "project_context",
    )(*args)


def _fourier_kernel(wr_ref, wi_ref, a_ref, b_ref, wf_ref, o_ref, yr_ref, yi_ref, z_ref, *, t1, t2, scale):
    for j in range(t2):
        rows = pl.ds(j, t1, stride=t2)
        xin = jnp.concatenate([wr_ref[rows, :], wi_ref[rows, :]], axis=0)
        y = _bdot(a_ref[j], xin)
        yr_ref[rows, :] = y[:t1]
        yi_ref[rows, :] = y[t1:]
    bm = b_ref[...]
    for k1 in range(t1):
        rows = pl.ds(k1 * t2, t2)
        yin = jnp.concatenate([yr_ref[rows, :], yi_ref[rows, :]], axis=0)
        z_ref[pl.ds(k1, t2, stride=t1), :] = _bdot(bm, yin)
    o_ref[...] = (_bdot(z_ref[...], wf_ref[...]) * scale).astype(BF16)


def _dft_tables(t1, t2):
    t = t1 * t2
    k1 = np.arange(t1)[None, :, None]
    pos = (t2 * np.arange(t1)[None, None, :] + np.arange(t2)[:, None, None])
    ang = 2.0 * np.pi * ((k1 * pos) % t) / t
    c, s = np.cos(ang), np.sin(ang)
    a = np.concatenate([np.concatenate([c, s], axis=2), np.concatenate([-s, c], axis=2)], axis=1)
    ang2 = 2.0 * np.pi * ((np.arange(t2)[:, None] * np.arange(t2)[None, :]) % t2) / t2
    b = np.concatenate([np.cos(ang2), np.sin(ang2)], axis=1)
    return a.astype(np.float32), b.astype(np.float32)


def _fourier(wr, wi, wf_bd, batch, t, t1, t2):
    a_np, b_np = _dft_tables(t1, t2)
    a = jnp.asarray(a_np).astype(BF16)
    b = jnp.asarray(b_np).astype(BF16)
    cw = LANES
    blk = pl.BlockSpec((None, t, cw), lambda bi, ci: (bi, 0, ci))
    return pl.pallas_call(
        functools.partial(_fourier_kernel, t1=t1, t2=t2, scale=float((t * F_HD) ** -0.5)),
        grid=(batch, F_DIM // cw),
        in_specs=[blk, blk,
                  pl.BlockSpec(a.shape, lambda bi, ci: (0, 0, 0)),
                  pl.BlockSpec(b.shape, lambda bi, ci: (0, 0)),
                  pl.BlockSpec((None, cw, cw), lambda bi, ci: (ci, 0, 0))],
        out_specs=blk,
        out_shape=jax.ShapeDtypeStruct((batch, t, F_DIM), BF16),
        scratch_shapes=[pltpu.VMEM((t, cw), F32), pltpu.VMEM((t, cw), F32), pltpu.VMEM((t, cw), F32)],
        compiler_params=_cparams(("parallel", "parallel")),
        name="fourier_%d" % t,
    )(wr.reshape(batch, t, F_DIM), wi.reshape(batch, t, F_DIM), a, b, wf_bd)


def _attn_kernel(*refs, windowed, n_blocks):
    if windowed:
        q_ref, kp_ref, kc_ref, kn_ref, vp_ref, vc_ref, vn_ref, kx_ref, vx_ref, sink_ref, o_ref = refs
    else:
        q_ref, kx_ref, vx_ref, sink_ref, o_ref = refs
    nt = (((1,), (1,)), ((), ()))
    parts = [(kx_ref, vx_ref, None)]
    if windowed:
        i = pl.program_id(1)
        a = lax.broadcasted_iota(jnp.int32, (Q_BLOCK, Q_BLOCK), 0)
        j = lax.broadcasted_iota(jnp.int32, (Q_BLOCK, Q_BLOCK), 1)
        prev_ok = (j >= a) & (i > 0)
        next_ok = (j <= a) & (i < n_blocks - 1)
        parts = [(kp_ref, vp_ref, prev_ok), (kc_ref, vc_ref, None), (kn_ref, vn_ref, next_ok)] + parts
    stack = GQA_GROUP if windowed else 1
    outs = []
    for h0 in range(0, N_HEADS, stack):
        heads = range(h0, h0 + stack)
        kv = h0 // GQA_GROUP
        hs = slice(kv * HEAD_DIM, (kv + 1) * HEAD_DIM)
        qs = jnp.concatenate([q_ref[:, h * HEAD_DIM:(h + 1) * HEAD_DIM] for h in heads], axis=0)
        scores = []
        for k_ref, _, ok in parts:
            s = lax.dot_general(qs, k_ref[:, hs], nt, preferred_element_type=F32)
            if ok is not None:
                s = jnp.where(jnp.concatenate([ok] * stack, axis=0), s, -jnp.inf)
            scores.append(s)
        sk = jnp.concatenate([jnp.full((Q_BLOCK, 1), sink_ref[h], F32) for h in heads], axis=0)
        slabs = [s[:, c:c + LANES] for s in scores for c in range(0, s.shape[1], LANES)]
        m = jnp.maximum(jnp.max(functools.reduce(jnp.maximum, slabs), axis=1, keepdims=True), sk)
        probs = [jnp.exp(s - m) for s in scores]
        pslabs = [p[:, c:c + LANES] for p in probs for c in range(0, p.shape[1], LANES)]
        inv = 1.0 / (jnp.sum(functools.reduce(jnp.add, pslabs), axis=1, keepdims=True) + jnp.exp(sk - m))
        o = functools.reduce(jnp.add, [jnp.dot(p.astype(BF16), v_ref[:, hs], preferred_element_type=F32)
                                       for p, (_, v_ref, _) in zip(probs, parts)]) * inv
        outs += [o[g * Q_BLOCK:(g + 1) * Q_BLOCK] for g in range(stack)]
    o_ref[...] = jnp.concatenate(outs, axis=1).astype(BF16)


def _attention(q, k, v, kx, vx, sink_col, batch, t, windowed):
    nb = t // Q_BLOCK
    n_ctx = kx.shape[1]
    qspec = pl.BlockSpec((None, Q_BLOCK, ATTN_DIM), lambda b, i: (b, i, 0))
    xspec = pl.BlockSpec((None, n_ctx, KV_DIM), lambda b, i: (b, 0, 0))
    sspec = pl.BlockSpec(memory_space=pltpu.SMEM)
    if windowed:
        prev = pl.BlockSpec((None, Q_BLOCK, KV_DIM), lambda b, i: (b, jnp.maximum(i - 1, 0), 0))
        cur = pl.BlockSpec((None, Q_BLOCK, KV_DIM), lambda b, i: (b, i, 0))
        nxt = pl.BlockSpec((None, Q_BLOCK, KV_DIM), lambda b, i: (b, jnp.minimum(i + 1, nb - 1), 0))
        in_specs = [qspec, prev, cur, nxt, prev, cur, nxt, xspec, xspec, sspec]
        args = [q, k, k, k, v, v, v, kx, vx, sink_col]
    else:
        in_specs = [qspec, xspec, xspec, sspec]
        args = [q, kx, vx, sink_col]
    return pl.pallas_call(
        functools.partial(_attn_kernel, windowed=windowed, n_blocks=nb),
        grid=(batch, nb),
        in_specs=in_specs, out_specs=qspec,
        out_shape=jax.ShapeDtypeStruct((batch, t, ATTN_DIM), BF16),
        compiler_params=_cparams(("parallel", "parallel")),
        name="attention_latent" if windowed else "attention_context",
    )(*args)


def _outproj_kernel(xc_ref, mfc_ref, mac_ref, xl_ref, mfl_ref, mal_ref,
                    wof_ref, woa_ref, g1_ref, sh_ref, sc_ref, n2_ref, wr_ref,
                    x1_ref, h_ref, lg_ref, *, ctx_tiles):
    def body(x_ref, mf_ref, ma_ref):
        o = jnp.dot(mf_ref[...], wof_ref[...], preferred_element_type=F32)
        o = o + jnp.dot(ma_ref[...], woa_ref[...], preferred_element_type=F32)
        x1 = x_ref[...] + g1_ref[...] * o
        x1_ref[...] = x1
        h = (_rms(x1, n2_ref[...]) * (1.0 + sc_ref[...]) + sh_ref[...]).astype(BF16)
        h_ref[...] = h
        lg_ref[...] = lax.dot_general(wr_ref[...], h, (((1,), (1,)), ((), ())), preferred_element_type=F32)

    is_ctx = pl.program_id(0) < ctx_tiles
    pl.when(is_ctx)(lambda: body(xc_ref, mfc_ref, mac_ref))
    pl.when(jnp.logical_not(is_ctx))(lambda: body(xl_ref, mfl_ref, mal_ref))


def _outproj(ctx, lat, wo_f, wo_a, mod, norm2, w_router_t, tm, row_fn):
    n_c, n_l = ctx[0].shape[0], lat[0].shape[0]
    ctx_tiles = n_c // tm
    n = n_c + n_l
    ctok = lambda w: pl.BlockSpec((tm, w), lambda i: (jnp.minimum(i, ctx_tiles - 1), 0))
    ltok = lambda w: pl.BlockSpec((tm, w), lambda i: (jnp.maximum(i - ctx_tiles, 0), 0))
    tok = lambda w: pl.BlockSpec((tm, w), lambda i: (i, 0))
    full = lambda a: pl.BlockSpec(a.shape, lambda i: (0,) * a.ndim)
    widths = (D_MODEL, F_DIM, ATTN_DIM)
    return pl.pallas_call(
        functools.partial(_outproj_kernel, ctx_tiles=ctx_tiles),
        grid=(n // tm,),
        in_specs=[ctok(w) for w in widths] + [ltok(w) for w in widths] + [
            full(wo_f), full(wo_a), _mod_spec(2, row_fn), _mod_spec(3, row_fn), _mod_spec(4, row_fn),
            full(norm2), full(w_router_t)],
        out_specs=[tok(D_MODEL), tok(D_MODEL), pl.BlockSpec((N_EXPERTS, tm), lambda i: (0, i))],
        out_shape=[jax.ShapeDtypeStruct((n, D_MODEL), F32), jax.ShapeDtypeStruct((n, D_MODEL), BF16),
                   jax.ShapeDtypeStruct((N_EXPERTS, n), F32)],
        compiler_params=_cparams(("parallel",)),
        name="outproj",
    )(*ctx, *lat, wo_f, wo_a, mod, mod, mod, norm2, w_router_t)


def _top_rows(vals, k):
    n = vals.shape[0]
    idx = lax.broadcasted_iota(jnp.int32, vals.shape, 0)
    picked = jnp.zeros(vals.shape, F32)
    for _ in range(k):
        best = jnp.max(vals, axis=0, keepdims=True)
        first = jnp.min(jnp.where(vals == best, idx, n), axis=0, keepdims=True)
        hit = idx == first
        picked = jnp.where(hit, 1.0, picked)
        vals = jnp.where(hit, -jnp.inf, vals)
    return picked


def _route_kernel(lg_ref, bias_ref, lo_ref, up_ref, pos_ref, gs_ref, cnt_ref):
    s = jax.nn.sigmoid(lg_ref[...])
    sc = s + bias_ref[...]
    tn = s.shape[1]
    per = N_EXPERTS // N_EXPERT_GROUPS
    g3 = sc.reshape(N_EXPERT_GROUPS, per, tn)
    member = lax.broadcasted_iota(jnp.int32, g3.shape, 1)
    m1 = jnp.max(g3, axis=1, keepdims=True)
    first = jnp.min(jnp.where(g3 == m1, member, per), axis=1, keepdims=True)
    m2 = jnp.max(jnp.where(member == first, -jnp.inf, g3), axis=1, keepdims=True)
    gscore = (m1 + m2).reshape(N_EXPERT_GROUPS, tn)
    gsel = _top_rows(gscore, TOPK_GROUPS)
    emask = jnp.broadcast_to(gsel.reshape(N_EXPERT_GROUPS, 1, tn), g3.shape).reshape(N_EXPERTS, tn)
    masked = jnp.where(emask > 0.5, sc, -jnp.inf)
    self = _top_rows(masked, TOP_K)
    sel = self > 0.5
    w = jnp.where(sel, s, 0.0)
    gate = w / jnp.sum(w, axis=0, keepdims=True) * ROUTED_SCALE

    selb = self.astype(BF16)
    cnt = jnp.sum(self, axis=1, keepdims=True)
    pad = jnp.maximum(jnp.floor((cnt + (RUN_ALIGN - 1)) * (1.0 / RUN_ALIGN)), 1.0) * RUN_ALIGN
    lo = lo_ref[...]
    soff = jnp.dot(lo, jnp.broadcast_to(pad, (N_EXPERTS, LANES)).astype(BF16), preferred_element_type=F32)[:, :1]
    rank = jnp.dot(selb, up_ref[...], preferred_element_type=F32)
    kidx = jnp.dot(lo, selb, preferred_element_type=F32)
    pos_e = jnp.where(sel, soff + rank, 0.0)
    rows_p, rows_g = [], []
    for k in range(TOP_K):
        m = kidx == k
        rows_p.append(jnp.sum(jnp.where(m, pos_e, 0.0), axis=0, keepdims=True))
        rows_g.append(jnp.sum(jnp.where(m, gate, 0.0), axis=0, keepdims=True))
    pos_ref[...] = jnp.concatenate(rows_p, axis=0).astype(jnp.int32)
    gs_ref[...] = jnp.concatenate(rows_g, axis=0)
    cnt_ref[...] = pad.astype(jnp.int32)


def _route(logits_t, bias_col, tc):
    n = logits_t.shape[1]
    lo = jnp.asarray(np.tril(np.ones((N_EXPERTS, N_EXPERTS), np.float32), -1)).astype(BF16)
    up = jnp.asarray(np.triu(np.ones((tc, tc), np.float32), 1)).astype(BF16)
    slot = pl.BlockSpec((TOP_K, tc), lambda i: (0, i))
    return pl.pallas_call(
        _route_kernel,
        grid=(n // tc,),
        in_specs=[pl.BlockSpec((N_EXPERTS, tc), lambda i: (0, i)),
                  pl.BlockSpec((N_EXPERTS, 1), lambda i: (0, 0)),
                  pl.BlockSpec(lo.shape, lambda i: (0, 0)),
                  pl.BlockSpec(up.shape, lambda i: (0, 0))],
        out_specs=[slot, slot, pl.BlockSpec((None, N_EXPERTS, 1), lambda i: (i, 0, 0))],
        out_shape=[jax.ShapeDtypeStruct((TOP_K, n), jnp.int32), jax.ShapeDtypeStruct((TOP_K, n), F32),
                   jax.ShapeDtypeStruct((n // tc, N_EXPERTS, 1), jnp.int32)],
        compiler_params=_cparams(("parallel",)),
        name="route",
    )(logits_t, bias_col, lo, up)


def _aligned(x):
    return pl.multiple_of(x, RUN_ALIGN)


def _block_rows(tc):
    return lax.broadcasted_iota(jnp.int32, (PERM_BLOCK, tc), 0).astype(F32).astype(BF16)


def _block_relative(pos, r0):
    return (pos - r0).astype(F32).astype(BF16)


def _gather_kernel(src_ref, cnt_ref, dst_ref, tot_ref,
                   pos_ref, x_ref, xs_ref, z_ref, sem, *, tc, n_chunks):
    c = pl.program_id(0)
    slot = c % 2

    def wait_chunk(ci, s):
        n = _aligned(tot_ref[ci])
        pltpu.make_async_copy(z_ref.at[s, pl.ds(0, n)], xs_ref.at[pl.ds(0, n)], sem.at[s]).wait()

    def start_runs(ci, s, first, count):
        for j in range(count):
            i = ci * N_EXPERTS + first + j
            n = _aligned(cnt_ref[i])
            pltpu.make_async_copy(z_ref.at[s, pl.ds(_aligned(src_ref[i]), n)],
                                  xs_ref.at[pl.ds(_aligned(dst_ref[i]), n)], sem.at[s]).start()

    @pl.when(c >= 2)
    def _():
        wait_chunk(c - 2, slot)

    rows = _block_rows(tc)

    def permute_blocks(trip):
        for half in range(PERM_UNROLL):
            r0 = pl.multiple_of((trip * PERM_UNROLL + half) * PERM_BLOCK, PERM_BLOCK)
            rel = _block_relative(pos_ref[...], r0)
            onehot = jnp.zeros((PERM_BLOCK, tc), BF16)
            for k in range(TOP_K):
                onehot = jnp.where(rel[k:k + 1, :] == rows, jnp.ones((), BF16), onehot)
            z_ref[slot, pl.ds(r0, PERM_BLOCK), :] = jnp.dot(
                onehot, x_ref[...], preferred_element_type=F32).astype(BF16)

    def plain_trip(trip, carry):
        permute_blocks(trip)
        return carry

    def issuing_trip(trip, carry):
        permute_blocks(trip)
        start_runs(c - 1, 1 - slot, trip * RUNS_PER_TRIP, RUNS_PER_TRIP)
        return carry

    trips = _perm_trips(tot_ref[c])

    @pl.when(c == 0)
    def _():
        lax.fori_loop(0, ISSUE_TRIPS, plain_trip, 0)

    @pl.when(c > 0)
    def _():
        lax.fori_loop(0, ISSUE_TRIPS, issuing_trip, 0)

    lax.fori_loop(ISSUE_TRIPS, trips, plain_trip, 0)

    @pl.when(c == n_chunks - 1)
    def _():
        def e_body(e8, carry):
            start_runs(c, slot, e8 * DMA_UNROLL, DMA_UNROLL)
            return carry
        lax.fori_loop(0, N_EXPERTS // DMA_UNROLL, e_body, 0)
        wait_chunk(c, slot)
        if n_chunks > 1:
            wait_chunk(c - 1, 1 - slot)


def _perm_trips(rows):
    step = PERM_BLOCK * PERM_UNROLL
    return (rows + step - 1) // step


def _chunk_rows_max(tc):
    return TOP_K * tc + N_EXPERTS * RUN_ALIGN


def _local_cap(tc):
    step = PERM_BLOCK * PERM_UNROLL
    return -(-_chunk_rows_max(tc) // step) * step


def _gather(tabs, pos, h2, r_max, tc):
    n = h2.shape[0]
    n_chunks = n // tc
    assert ISSUE_TRIPS * RUNS_PER_TRIP == N_EXPERTS and ISSUE_TRIPS * PERM_BLOCK * PERM_UNROLL <= TOP_K * tc
    return pl.pallas_call(
        functools.partial(_gather_kernel, tc=tc, n_chunks=n_chunks),
        grid_spec=pltpu.PrefetchScalarGridSpec(
            num_scalar_prefetch=4, grid=(n_chunks,),
            in_specs=[pl.BlockSpec((TOP_K, tc), lambda c, *_: (0, c)),
                      pl.BlockSpec((tc, D_MODEL), lambda c, *_: (c, 0))],
            out_specs=pl.BlockSpec(memory_space=pl.ANY),
            scratch_shapes=[pltpu.VMEM((2, _local_cap(tc), D_MODEL), BF16),
                            pltpu.SemaphoreType.DMA((2,))]),
        out_shape=jax.ShapeDtypeStruct((r_max, D_MODEL), BF16),
        compiler_params=_cparams(("arbitrary",), manual_dma=True),
        name="moe_gather",
    )(tabs["src"], tabs["cnt"], tabs["dst"], tabs["tot"], pos, h2)


def _ffn_kernel(first_ref, ntile_ref, tstart_ref, trows_ref, total_ref, xs_ref, wg_ref, wu_ref, wd_ref, ys_ref,
                xbuf, ybuf, wgb, wub, wdb, sem_in, sem_out):
    e = pl.program_id(0)
    total = total_ref[0]

    def in_copy(g):
        s = g % FFN_IN_SLOTS
        n = _aligned(trows_ref[g])
        return pltpu.make_async_copy(xs_ref.at[pl.ds(_aligned(tstart_ref[g]), n)],
                                     xbuf.at[s, pl.ds(0, n)], sem_in.at[s])

    def out_copy(g):
        s = g % 2
        n = _aligned(trows_ref[g])
        return pltpu.make_async_copy(ybuf.at[s, pl.ds(0, n)],
                                     ys_ref.at[pl.ds(_aligned(tstart_ref[g]), n)], sem_out.at[s])

    @pl.when(e == 0)
    def _():
        xbuf[...] = jnp.zeros(xbuf.shape, xbuf.dtype)
        for g in range(FFN_LOOKAHEAD):
            @pl.when(g < total)
            def _():
                in_copy(g).start()

    wgb[...] = wg_ref[...].astype(BF16)
    wub[...] = wu_ref[...].astype(BF16)
    wdb[...] = wd_ref[...].astype(BF16)

    def body(g, carry):
        @pl.when(g + FFN_LOOKAHEAD < total)
        def _():
            in_copy(g + FFN_LOOKAHEAD).start()
        in_copy(g).wait()

        @pl.when(g >= 2)
        def _():
            out_copy(g - 2).wait()
        x = xbuf[g % FFN_IN_SLOTS]
        a = jnp.dot(x, wgb[...], preferred_element_type=F32)
        u = jnp.dot(x, wub[...], preferred_element_type=F32)
        h = (a * jax.nn.sigmoid(a) * u).astype(BF16)
        ybuf[g % 2] = jnp.dot(h, wdb[...], preferred_element_type=F32).astype(BF16)
        out_copy(g).start()
        return carry
    lax.fori_loop(first_ref[e], first_ref[e] + ntile_ref[e], body, 0)

    @pl.when(e == pl.num_programs(0) - 1)
    def _():
        @pl.when(total >= 2)
        def _():
            out_copy(total - 2).wait()
        out_copy(total - 1).wait()


def _ffn(tabs, xs, wg, wu, wd):
    ew = lambda a: pl.BlockSpec((None,) + a.shape[1:], lambda e, *_: (e, 0, 0))
    return pl.pallas_call(
        _ffn_kernel,
        grid_spec=pltpu.PrefetchScalarGridSpec(
            num_scalar_prefetch=5, grid=(N_EXPERTS,),
            in_specs=[pl.BlockSpec(memory_space=pl.ANY), ew(wg), ew(wu), ew(wd)],
            out_specs=pl.BlockSpec(memory_space=pl.ANY),
            scratch_shapes=[pltpu.VMEM((FFN_IN_SLOTS, FFN_BLOCK, D_MODEL), BF16),
                            pltpu.VMEM((2, FFN_BLOCK, D_MODEL), BF16),
                            pltpu.VMEM(wg.shape[1:], BF16), pltpu.VMEM(wu.shape[1:], BF16),
                            pltpu.VMEM(wd.shape[1:], BF16),
                            pltpu.SemaphoreType.DMA((FFN_IN_SLOTS,)), pltpu.SemaphoreType.DMA((2,))]),
        out_shape=jax.ShapeDtypeStruct(xs.shape, BF16),
        compiler_params=_cparams(("arbitrary",), manual_dma=True),
        name="moe_ffn",
    )(tabs["first_tile"], tabs["n_tiles"], tabs["tile_start"], tabs["tile_rows"], tabs["total_tiles"],
      xs, wg, wu, wd)


def _combine_kernel(src_ref, cnt_ref, dst_ref, tot_ref,
                    pos_ref, gs_ref, x1_ref, h_ref, sg_ref, su_ref, sd_ref, g2_ref, nf_ref, ys_ref,
                    yc_ref, yl_ref, ybuf, acc_ref, sem, *, tc, n_chunks, ctx_chunks):
    c = pl.program_id(0)
    slot = c % 2

    def issue(ci, s):
        def e_body(e8, carry):
            for j in range(DMA_UNROLL):
                i = ci * N_EXPERTS + e8 * DMA_UNROLL + j
                n = _aligned(cnt_ref[i])
                pltpu.make_async_copy(ys_ref.at[pl.ds(_aligned(dst_ref[i]), n)],
                                      ybuf.at[s, pl.ds(_aligned(src_ref[i]), n)], sem.at[s]).start()
            return carry
        lax.fori_loop(0, N_EXPERTS // DMA_UNROLL, e_body, 0)

    @pl.when(c == 0)
    def _():
        ybuf[...] = jnp.zeros(ybuf.shape, ybuf.dtype)
        issue(0, 0)

    @pl.when(c + 1 < n_chunks)
    def _():
        issue(c + 1, 1 - slot)

    h = h_ref[...]
    a = jnp.dot(h, sg_ref[...], preferred_element_type=F32)
    u = jnp.dot(h, su_ref[...], preferred_element_type=F32)
    acc_ref[...] = jnp.dot((a * jax.nn.sigmoid(a) * u).astype(BF16), sd_ref[...], preferred_element_type=F32)

    n = _aligned(tot_ref[c])
    pltpu.make_async_copy(ys_ref.at[pl.ds(0, n)], ybuf.at[slot, pl.ds(0, n)], sem.at[slot]).wait()

    tn = (((0,), (0,)), ((), ()))
    rows = _block_rows(tc)
    gates = gs_ref[...].astype(BF16)

    def rb_body(rb2, carry):
        part = None
        for half in range(PERM_UNROLL):
            r0 = pl.multiple_of((rb2 * PERM_UNROLL + half) * PERM_BLOCK, PERM_BLOCK)
            rel = _block_relative(pos_ref[...], r0)
            wt = jnp.zeros((PERM_BLOCK, tc), BF16)
            for k in range(TOP_K):
                wt = jnp.where(rel[k:k + 1, :] == rows, gates[k:k + 1, :], wt)
            d = lax.dot_general(wt, ybuf[slot, pl.ds(r0, PERM_BLOCK), :], tn, preferred_element_type=F32)
            part = d if part is None else part + d
        acc_ref[...] += part
        return carry
    lax.fori_loop(0, _perm_trips(tot_ref[c]), rb_body, 0)

    x2 = x1_ref[...] + g2_ref[...] * acc_ref[...]
    y = _rms(x2, nf_ref[...])

    @pl.when(c < ctx_chunks)
    def _():
        yc_ref[...] = y

    @pl.when(c >= ctx_chunks)
    def _():
        yl_ref[...] = y


def _combine(tabs, pos, gslot, x1, h2, sg, su, sd, mod, norm_f, ys, tc, row_fn, n_ctx):
    n = x1.shape[0]
    n_chunks = n // tc
    ctx_chunks = n_ctx // tc
    tok = lambda w: pl.BlockSpec((tc, w), lambda c, *_: (c, 0))
    slot = pl.BlockSpec((TOP_K, tc), lambda c, *_: (0, c))
    full = lambda a: pl.BlockSpec(a.shape, lambda c, *_: (0,) * a.ndim)
    return pl.pallas_call(
        functools.partial(_combine_kernel, tc=tc, n_chunks=n_chunks, ctx_chunks=ctx_chunks),
        grid_spec=pltpu.PrefetchScalarGridSpec(
            num_scalar_prefetch=4, grid=(n_chunks,),
            in_specs=[slot, slot, tok(D_MODEL), tok(D_MODEL), full(sg), full(su), full(sd),
                      _mod_spec(5, lambda c, *_: row_fn(c)), full(norm_f), pl.BlockSpec(memory_space=pl.ANY)],
            out_specs=[pl.BlockSpec((tc, D_MODEL), lambda c, *_: (jnp.minimum(c, ctx_chunks - 1), 0)),
                       pl.BlockSpec((tc, D_MODEL), lambda c, *_: (jnp.maximum(c - ctx_chunks, 0), 0))],
            scratch_shapes=[pltpu.VMEM((2, _local_cap(tc), D_MODEL), BF16),
                            pltpu.VMEM((tc, D_MODEL), F32),
                            pltpu.SemaphoreType.DMA((2,))]),
        out_shape=[jax.ShapeDtypeStruct((n_ctx, D_MODEL), F32), jax.ShapeDtypeStruct((n - n_ctx, D_MODEL), F32)],
        compiler_params=_cparams(("arbitrary",), manual_dma=True),
        name="moe_combine",
    )(tabs["src"], tabs["cnt"], tabs["dst"], tabs["tot"], pos, gslot, x1, h2, sg, su, sd, mod, norm_f, ys)


def _dispatch_tables(pad_cnt, r_max):
    i32 = jnp.int32
    src = jnp.cumsum(pad_cnt, axis=1) - pad_cnt
    tot = jnp.sum(pad_cnt, axis=1)
    e_rows = jnp.sum(pad_cnt, axis=0)
    e_start = jnp.cumsum(e_rows) - e_rows
    dst = e_start[None, :] + jnp.cumsum(pad_cnt, axis=0) - pad_cnt
    n_tiles = (e_rows + FFN_BLOCK - 1) // FFN_BLOCK
    tile_end = jnp.cumsum(n_tiles)
    first_tile = tile_end - n_tiles
    g = jnp.arange(r_max // FFN_BLOCK + N_EXPERTS)
    before = tile_end[None, :] <= g[:, None]
    owner = jnp.concatenate([jnp.ones_like(before[:, :1]), before[:, :-1]], axis=1) & ~before
    offset = (g - jnp.sum(jnp.where(before, n_tiles, 0), axis=1)) * FFN_BLOCK
    tile_start = jnp.clip(jnp.sum(jnp.where(before, e_rows, 0), axis=1) + offset, 0, r_max - FFN_BLOCK)
    tile_rows = jnp.clip(jnp.sum(jnp.where(owner, e_rows, 0), axis=1) - offset, RUN_ALIGN, FFN_BLOCK)
    return {"src": src.reshape(-1).astype(i32), "cnt": pad_cnt.reshape(-1).astype(i32),
            "dst": dst.reshape(-1).astype(i32), "tot": tot.astype(i32),
            "first_tile": first_tile.astype(i32), "n_tiles": n_tiles.astype(i32),
            "tile_start": tile_start.astype(i32), "tile_rows": tile_rows.astype(i32),
            "total_tiles": tile_end[-1:].astype(i32)}


def _moe(x1, h2, logits_t, bias_col, wg, wu, wd, sg, su, sd, mod, norm_f, row_fn, n_ctx):
    n = x1.shape[0]
    tc = MOE_CHUNK
    pos, gslot, pad_cnt = _route(logits_t, bias_col, tc)
    r_max = (n // tc) * _chunk_rows_max(tc)
    tabs = _dispatch_tables(pad_cnt[:, :, 0], r_max)
    xs = _gather(tabs, pos, h2, r_max, tc)
    ys = _ffn(tabs, xs, wg, wu, wd)
    return _combine(tabs, pos, gslot, x1, h2, sg, su, sd, mod, norm_f, ys, tc, row_fn, n_ctx)


def _channel_dft():
    ang = 2.0 * np.pi * ((np.arange(F_HD)[:, None] * np.arange(F_HD)[None, :]) % F_HD) / F_HD
    eye = np.eye(F_GROUPS)
    return np.concatenate([np.kron(eye, np.cos(ang)), np.kron(eye, -np.sin(ang))], axis=1).astype(np.float32)


def _rope_tables(t):
    pos = np.arange(t)
    row, col = pos // GRID_W, pos % GRID_W
    n_freq = HEAD_DIM // 4
    inv = ROPE_THETA ** (-np.arange(n_freq, dtype=np.float64) / n_freq)
    lane = np.arange(LANES)
    hd = lane % HEAD_DIM
    within = hd % (HEAD_DIM // 2)
    freq = within % n_freq
    first = within < n_freq
    p = np.where((hd < HEAD_DIM // 2)[None, :], row[:, None], col[:, None]).astype(np.float64)
    ang = p * inv[freq][None, :]
    cos, sin = np.cos(ang), np.sin(ang)
    sin_up = np.where(first[None, :], -sin, 0.0)
    sin_dn = np.where(first[None, :], 0.0, sin)
    return [jnp.asarray(a.astype(np.float32)) for a in (cos, sin_up, sin_dn)]


def _block_diag_pairs(wf):
    per = LANES // F_HD
    z = jnp.zeros((F_HD, F_HD), wf.dtype)
    blocks = []
    for c in range(F_GROUPS // per):
        rows = [jnp.concatenate([wf[c * per + r] if r == cc else z for cc in range(per)], axis=1) for r in range(per)]
        blocks.append(jnp.concatenate(rows, axis=0))
    return jnp.stack(blocks)


def kernel(x_prompt, x_sample, cache_k, cache_v, c, c_ctx, w_mod, b_mod, norm1, w_in, w_fourier, sink,
           w_out, norm2, w_router, router_bias, w_gate, w_up, w_down, ws_gate, ws_up, ws_down, norm_f):
    nb_ctx, t_ctx, _ = x_prompt.shape
    nb_lat, t_lat, _ = x_sample.shape
    l = 0
    cvec = jnp.concatenate([c_ctx[None, :], c, jnp.zeros((MOD_ROWS - 1 - nb_lat, D_MODEL), F32)], axis=0)
    mod = _modulation(cvec, w_mod[l], b_mod[l]).reshape(MOD_ROWS, 6, 1, D_MODEL)

    w_in_b = w_in[l].astype(BF16)
    dft_c = jnp.asarray(_channel_dft()).astype(BF16)
    wf_bd = _block_diag_pairs(w_fourier[l]).astype(BF16)
    wo_f = w_out[l][:F_DIM].astype(BF16)
    wo_a = w_out[l][F_DIM:].astype(BF16)
    w_router_t = w_router[l].T.astype(BF16)
    bias_col = router_bias[l].reshape(N_EXPERTS, 1)
    n1 = norm1[l].reshape(1, D_MODEL)
    n2 = norm2[l].reshape(1, D_MODEL)
    nf = norm_f.reshape(1, D_MODEL)
    sink_col = sink[l]
    wg, wu, wd = w_gate[l], w_up[l], w_down[l]
    sg, su, sd = ws_gate[l].astype(BF16), ws_up[l].astype(BF16), ws_down[l].astype(BF16)

    def mixers(x, batch, t, latent, kx, vx, t1, t2, tm):
        n = batch * t
        x2d = x.reshape(n, D_MODEL)
        tiles = t // tm
        row_fn = (lambda i: 1 + i // tiles) if latent else (lambda i: 0)
        rope_tabs = _rope_tables(t) if latent else None
        outs = _project(x2d, mod, n1, w_in_b, dft_c, rope_tabs, tm, tiles, row_fn)
        wr, wi, q, k, v = outs[:5]
        mixf = _fourier(wr, wi, wf_bd, batch, t, t1, t2)
        q3, k3, v3 = (a.reshape(batch, t, -1) for a in (q, k, v))
        if latent:
            mixa = _attention(q3, k3, v3, kx, vx, sink_col, batch, t, True)
        else:
            mixa = _attention(q3, None, None, k3, v3, sink_col, batch, t, False)
        return (x2d, mixf.reshape(n, F_DIM), mixa.reshape(n, ATTN_DIM)), outs[5:]

    ctx, (kf, vf) = mixers(x_prompt, nb_ctx, t_ctx, False, None, None, 16, 16, 256)
    kx = cache_k[:, l].reshape(nb_lat, -1, KV_DIM).astype(BF16)
    vx = cache_v[:, l].reshape(nb_lat, -1, KV_DIM).astype(BF16)
    lat, _ = mixers(x_sample, nb_lat, t_lat, True, kx, vx, 64, 64, 512)

    n_ctx = nb_ctx * t_ctx

    def tile_row(tile):
        ctx_tiles, lat_tiles = n_ctx // tile, t_lat // tile
        return lambda i: jnp.where(i < ctx_tiles, 0, 1 + (i - ctx_tiles) // lat_tiles)

    x1, h2, logits_t = _outproj(ctx, lat, wo_f, wo_a, mod, n2, w_router_t, OUTPROJ_TILE, tile_row(OUTPROJ_TILE))
    y_prompt, y_sample = _moe(x1, h2, logits_t, bias_col, wg, wu, wd, sg, su, sd, mod, nf, tile_row(MOE_CHUNK),
                              n_ctx)
    new_k = kf.reshape(nb_ctx, 1, t_ctx, N_KV_HEADS, HEAD_DIM)
    new_v = vf.reshape(nb_ctx, 1, t_ctx, N_KV_HEADS, HEAD_DIM)
    return (y_prompt.reshape(x_prompt.shape), y_sample.reshape(x_sample.shape), new_k, new_v)
```

```python
import functools

import numpy as np
import jax
import jax.numpy as jnp
from jax import lax
from jax.experimental import pallas as pl
from jax.experimental.pallas import tpu as pltpu

F32 = jnp.float32
BF16 = jnp.bfloat16

D_MODEL = 1024
GRID_W = 64
HEAD_DIM = 64
N_HEADS = 12
N_KV_HEADS = 4
GQA_GROUP = N_HEADS // N_KV_HEADS
ATTN_DIM = N_HEADS * HEAD_DIM
KV_DIM = N_KV_HEADS * HEAD_DIM
F_GROUPS = 4
F_HD = 64
F_DIM = F_GROUPS * F_HD
IN_DIM = F_DIM + ATTN_DIM + 2 * KV_DIM
WINDOW = 128
Q_BLOCK = 128
ROPE_THETA = 10000.0
N_EXPERTS = 64
TOP_K = 8
N_EXPERT_GROUPS = 8
TOPK_GROUPS = 4
EXPERT_DIM = 256
SHARED_DIM = 256
ROUTED_SCALE = 2.5
EPS = 1e-6

LANES = 128
MOD_ROWS = 8
RUN_ALIGN = 16
PERM_BLOCK = 256
FFN_BLOCK = 512
FFN_LOOKAHEAD = 3
FFN_IN_SLOTS = FFN_LOOKAHEAD + 1
MOE_CHUNK = 256
OUTPROJ_TILE = 512
SOFTMAX_ROWS = 32
PERM_UNROLL = 2
DMA_UNROLL = 8
ISSUE_TRIPS = 4
RUNS_PER_TRIP = N_EXPERTS // ISSUE_TRIPS
VMEM_LIMIT = 56 * 1024 * 1024


def _cparams(sem, manual_dma=False):
    return pltpu.CompilerParams(dimension_semantics=sem, vmem_limit_bytes=VMEM_LIMIT,
                                disable_bounds_checks=manual_dma)


def _bdot(a, b):
    return jnp.dot(a.astype(BF16), b.astype(BF16), preferred_element_type=F32)


def _rms(x, g):
    return x * lax.rsqrt(jnp.mean(x * x, axis=-1, keepdims=True) + EPS) * g


def _mod_kernel(c_ref, w_ref, b_ref, o_ref):
    c = c_ref[...]
    a = c * jax.nn.sigmoid(c)
    o_ref[...] = _bdot(a, w_ref[...]) + b_ref[...]


def _modulation(cvec, w_mod, b_mod):
    n = w_mod.shape[1]
    tn = 1024
    return pl.pallas_call(
        _mod_kernel,
        grid=(n // tn,),
        in_specs=[pl.BlockSpec((MOD_ROWS, D_MODEL), lambda j: (0, 0)),
                  pl.BlockSpec((D_MODEL, tn), lambda j: (0, j)),
                  pl.BlockSpec((1, tn), lambda j: (0, j))],
        out_specs=pl.BlockSpec((MOD_ROWS, tn), lambda j: (0, j)),
        out_shape=jax.ShapeDtypeStruct((MOD_ROWS, n), F32),
        compiler_params=_cparams(("parallel",)),
        name="modulation",
    )(cvec, w_mod, b_mod.reshape(1, n))


def _mod_spec(piece, row_fn):
    return pl.BlockSpec((None, None, 1, D_MODEL), lambda *idx: (row_fn(*idx), piece, 0, 0))


def _rope(x, cos, sin_up, sin_dn):
    outs = []
    for j in range(x.shape[1] // LANES):
        xj = x[:, j * LANES:(j + 1) * LANES]
        up = pltpu.roll(xj, LANES - 16, axis=1)
        dn = pltpu.roll(xj, 16, axis=1)
        outs.append(xj * cos + up * sin_up + dn * sin_dn)
    return jnp.concatenate(outs, axis=1)


def _proj_kernel(*refs, latent):
    if latent:
        (x_ref, sh_ref, sc_ref, g_ref, w_ref, dft_ref, cos_ref, sup_ref, sdn_ref,
         wr_ref, wi_ref, q_ref, k_ref, v_ref) = refs
    else:
        (x_ref, sh_ref, sc_ref, g_ref, w_ref, dft_ref,
         wr_ref, wi_ref, q_ref, k_ref, v_ref, kf_ref, vf_ref) = refs
    x = x_ref[...]
    h = _rms(x, g_ref[...]) * (1.0 + sc_ref[...]) + sh_ref[...]
    p = _bdot(h, w_ref[...])
    u = p[:, :F_DIM]
    q = p[:, F_DIM:F_DIM + ATTN_DIM]
    k = p[:, F_DIM + ATTN_DIM:F_DIM + ATTN_DIM + KV_DIM]
    v = p[:, F_DIM + ATTN_DIM + KV_DIM:]
    w = _bdot(u, dft_ref[...])
    wr_ref[...] = w[:, :F_DIM]
    wi_ref[...] = w[:, F_DIM:]
    if latent:
        cos, sup, sdn = cos_ref[...], sup_ref[...], sdn_ref[...]
        q = _rope(q, cos, sup, sdn)
        k = _rope(k, cos, sup, sdn)
    else:
        kf_ref[...] = k
        vf_ref[...] = v
    q_ref[...] = (q * (HEAD_DIM ** -0.5)).astype(BF16)
    k_ref[...] = k.astype(BF16)
    v_ref[...] = v.astype(BF16)


def _project(x2d, mod, norm1, w_in, dft_c, rope_tabs, tm, tiles_per_batch, row_fn):
    n = x2d.shape[0]
    latent = rope_tabs is not None
    tok = lambda w: pl.BlockSpec((tm, w), lambda i: (i, 0))
    full = lambda a: pl.BlockSpec(a.shape, lambda i: (0,) * a.ndim)
    in_specs = [tok(D_MODEL), _mod_spec(0, row_fn), _mod_spec(1, row_fn), full(norm1), full(w_in), full(dft_c)]
    args = [x2d, mod, mod, norm1, w_in, dft_c]
    out_specs = [tok(F_DIM), tok(F_DIM), tok(ATTN_DIM), tok(KV_DIM), tok(KV_DIM)]
    out_shape = [jax.ShapeDtypeStruct((n, F_DIM), F32), jax.ShapeDtypeStruct((n, F_DIM), F32),
                 jax.ShapeDtypeStruct((n, ATTN_DIM), BF16), jax.ShapeDtypeStruct((n, KV_DIM), BF16),
                 jax.ShapeDtypeStruct((n, KV_DIM), BF16)]
    if latent:
        pos = pl.BlockSpec((tm, LANES), lambda i: (i % tiles_per_batch, 0))
        in_specs += [pos, pos, pos]
        args += list(rope_tabs)
    else:
        out_specs += [tok(KV_DIM), tok(KV_DIM)]
        out_shape += [jax.ShapeDtypeStruct((n, KV_DIM), F32), jax.ShapeDtypeStruct((n, KV_DIM), F32)]
    return pl.pallas_call(
        functools.partial(_proj_kernel, latent=latent),
        grid=(n // tm,),
        in_specs=in_specs, out_specs=out_specs, out_shape=out_shape,
        compiler_params=_cparams(("parallel",)),
        name="project_latent" if latent else "project_context",
    )(*args)


def _fourier_kernel(wr_ref, wi_ref, a_ref, b_ref, wf_ref, o_ref, yr_ref, yi_ref, z_ref, *, t1, t2, scale):
    for j in range(t2):
        rows = pl.ds(j, t1, stride=t2)
        xin = jnp.concatenate([wr_ref[rows, :], wi_ref[rows, :]], axis=0)
        y = _bdot(a_ref[j], xin)
        yr_ref[rows, :] = y[:t1]
        yi_ref[rows, :] = y[t1:]
    bm = b_ref[...]
    for k1 in range(t1):
        rows = pl.ds(k1 * t2, t2)
        yin = jnp.concatenate([yr_ref[rows, :], yi_ref[rows, :]], axis=0)
        z_ref[pl.ds(k1, t2, stride=t1), :] = _bdot(bm, yin)
    o_ref[...] = (_bdot(z_ref[...], wf_ref[...]) * scale).astype(BF16)


def _dft_tables(t1, t2):
    t = t1 * t2
    k1 = np.arange(t1)[None, :, None]
    pos = (t2 * np.arange(t1)[None, None, :] + np.arange(t2)[:, None, None])
    ang = 2.0 * np.pi * ((k1 * pos) % t) / t
    c, s = np.cos(ang), np.sin(ang)
    a = np.concatenate([np.concatenate([c, s], axis=2), np.concatenate([-s, c], axis=2)], axis=1)
    ang2 = 2.0 * np.pi * ((np.arange(t2)[:, None] * np.arange(t2)[None, :]) % t2) / t2
    b = np.concatenate([np.cos(ang2), np.sin(ang2)], axis=1)
    return a.astype(np.float32), b.astype(np.float32)


def _fourier(wr, wi, wf_bd, batch, t, t1, t2):
    a_np, b_np = _dft_tables(t1, t2)
    a = jnp.asarray(a_np).astype(BF16)
    b = jnp.asarray(b_np).astype(BF16)
    cw = LANES
    blk = pl.BlockSpec((None, t, cw), lambda bi, ci: (bi, 0, ci))
    return pl.pallas_call(
        functools.partial(_fourier_kernel, t1=t1, t2=t2, scale=float((t * F_HD) ** -0.5)),
        grid=(batch, F_DIM // cw),
        in_specs=[blk, blk,
                  pl.BlockSpec(a.shape, lambda bi, ci: (0, 0, 0)),
                  pl.BlockSpec(b.shape, lambda bi, ci: (0, 0)),
                  pl.BlockSpec((None, cw, cw), lambda bi, ci: (ci, 0, 0))],
        out_specs=blk,
        out_shape=jax.ShapeDtypeStruct((batch, t, F_DIM), BF16),
        scratch_shapes=[pltpu.VMEM((t, cw), F32), pltpu.VMEM((t, cw), F32), pltpu.VMEM((t, cw), F32)],
        compiler_params=_cparams(("parallel", "parallel")),
        name="fourier_%d" % t,
    )(wr.reshape(batch, t, F_DIM), wi.reshape(batch, t, F_DIM), a, b, wf_bd)


def _attn_kernel(*refs, windowed, n_blocks):
    if windowed:
        q_ref, kp_ref, kc_ref, kn_ref, vp_ref, vc_ref, vn_ref, kx_ref, vx_ref, sink_ref, o_ref, s_ref, p_ref = refs
    else:
        q_ref, kx_ref, vx_ref, sink_ref, o_ref, s_ref, p_ref = refs
    nt = (((1,), (1,)), ((), ()))
    parts = [(kx_ref, vx_ref, None)]
    if windowed:
        i = pl.program_id(1)
        a = lax.broadcasted_iota(jnp.int32, (Q_BLOCK, Q_BLOCK), 0)
        j = lax.broadcasted_iota(jnp.int32, (Q_BLOCK, Q_BLOCK), 1)
        prev_ok = (j >= a) & (i > 0)
        next_ok = (j <= a) & (i < n_blocks - 1)
        parts = [(kp_ref, vp_ref, prev_ok), (kc_ref, vc_ref, None), (kn_ref, vn_ref, next_ok)] + parts
    n_keys = sum(k_ref.shape[0] for k_ref, _, _ in parts)
    outs = []
    for kv in range(N_KV_HEADS):
        heads = range(kv * GQA_GROUP, (kv + 1) * GQA_GROUP)
        hs = slice(kv * HEAD_DIM, (kv + 1) * HEAD_DIM)
        qs = jnp.concatenate([q_ref[:, h * HEAD_DIM:(h + 1) * HEAD_DIM] for h in heads], axis=0)
        c0 = 0
        for k_ref, _, ok in parts:
            s = lax.dot_general(qs, k_ref[:, hs], nt, preferred_element_type=F32)
            if ok is not None:
                s = jnp.where(jnp.concatenate([ok] * GQA_GROUP, axis=0), s, -jnp.inf)
            s_ref[:, c0:c0 + s.shape[1]] = s
            c0 += s.shape[1]
        invs = []
        for r0 in range(0, GQA_GROUP * Q_BLOCK, SOFTMAX_ROWS):
            s = s_ref[r0:r0 + SOFTMAX_ROWS, :]
            sk = sink_ref[kv * GQA_GROUP + r0 // Q_BLOCK]
            slabs = [s[:, c:c + LANES] for c in range(0, n_keys, LANES)]
            m = jnp.maximum(jnp.max(functools.reduce(jnp.maximum, slabs), axis=1, keepdims=True), sk)
            p = jnp.exp(s - m)
            pslabs = [p[:, c:c + LANES] for c in range(0, n_keys, LANES)]
            den = jnp.sum(functools.reduce(jnp.add, pslabs), axis=1, keepdims=True) + jnp.exp(sk - m)
            p_ref[r0:r0 + SOFTMAX_ROWS, :] = p.astype(BF16)
            invs.append(1.0 / den)
        o = None
        c0 = 0
        for k_ref, v_ref, _ in parts:
            nk = k_ref.shape[0]
            d = jnp.dot(p_ref[:, c0:c0 + nk], v_ref[:, hs], preferred_element_type=F32)
            o = d if o is None else o + d
            c0 += nk
        o = o * jnp.concatenate(invs, axis=0)
        outs += [o[g * Q_BLOCK:(g + 1) * Q_BLOCK] for g in range(GQA_GROUP)]
    o_ref[...] = jnp.concatenate(outs, axis=1).astype(BF16)


def _attention(q, k, v, kx, vx, sink_col, batch, t, windowed):
    nb = t // Q_BLOCK
    n_ctx = kx.shape[1]
    n_keys = n_ctx + (3 * Q_BLOCK if windowed else 0)
    qspec = pl.BlockSpec((None, Q_BLOCK, ATTN_DIM), lambda b, i: (b, i, 0))
    xspec = pl.BlockSpec((None, n_ctx, KV_DIM), lambda b, i: (b, 0, 0))
    sspec = pl.BlockSpec(memory_space=pltpu.SMEM)
    if windowed:
        prev = pl.BlockSpec((None, Q_BLOCK, KV_DIM), lambda b, i: (b, jnp.maximum(i - 1, 0), 0))
        cur = pl.BlockSpec((None, Q_BLOCK, KV_DIM), lambda b, i: (b, i, 0))
        nxt = pl.BlockSpec((None, Q_BLOCK, KV_DIM), lambda b, i: (b, jnp.minimum(i + 1, nb - 1), 0))
        in_specs = [qspec, prev, cur, nxt, prev, cur, nxt, xspec, xspec, sspec]
        args = [q, k, k, k, v, v, v, kx, vx, sink_col]
    else:
        in_specs = [qspec, xspec, xspec, sspec]
        args = [q, kx, vx, sink_col]
    return pl.pallas_call(
        functools.partial(_attn_kernel, windowed=windowed, n_blocks=nb),
        grid=(batch, nb),
        in_specs=in_specs, out_specs=qspec,
        out_shape=jax.ShapeDtypeStruct((batch, t, ATTN_DIM), BF16),
        scratch_shapes=[pltpu.VMEM((GQA_GROUP * Q_BLOCK, n_keys), F32),
                        pltpu.VMEM((GQA_GROUP * Q_BLOCK, n_keys), BF16)],
        compiler_params=_cparams(("parallel", "parallel")),
        name="attention_latent" if windowed else "attention_context",
    )(*args)


def _outproj_kernel(xc_ref, mfc_ref, mac_ref, xl_ref, mfl_ref, mal_ref,
                    wof_ref, woa_ref, g1_ref, sh_ref, sc_ref, n2_ref, wr_ref,
                    x1_ref, h_ref, lg_ref, *, ctx_tiles):
    def body(x_ref, mf_ref, ma_ref):
        o = jnp.dot(mf_ref[...], wof_ref[...], preferred_element_type=F32)
        o = o + jnp.dot(ma_ref[...], woa_ref[...], preferred_element_type=F32)
        x1 = x_ref[...] + g1_ref[...] * o
        x1_ref[...] = x1
        h = (_rms(x1, n2_ref[...]) * (1.0 + sc_ref[...]) + sh_ref[...]).astype(BF16)
        h_ref[...] = h
        lg_ref[...] = lax.dot_general(wr_ref[...], h, (((1,), (1,)), ((), ())), preferred_element_type=F32)

    is_ctx = pl.program_id(0) < ctx_tiles
    pl.when(is_ctx)(lambda: body(xc_ref, mfc_ref, mac_ref))
    pl.when(jnp.logical_not(is_ctx))(lambda: body(xl_ref, mfl_ref, mal_ref))


def _outproj(ctx, lat, wo_f, wo_a, mod, norm2, w_router_t, tm, row_fn):
    n_c, n_l = ctx[0].shape[0], lat[0].shape[0]
    ctx_tiles = n_c // tm
    n = n_c + n_l
    ctok = lambda w: pl.BlockSpec((tm, w), lambda i: (jnp.minimum(i, ctx_tiles - 1), 0))
    ltok = lambda w: pl.BlockSpec((tm, w), lambda i: (jnp.maximum(i - ctx_tiles, 0), 0))
    tok = lambda w: pl.BlockSpec((tm, w), lambda i: (i, 0))
    full = lambda a: pl.BlockSpec(a.shape, lambda i: (0,) * a.ndim)
    widths = (D_MODEL, F_DIM, ATTN_DIM)
    return pl.pallas_call(
        functools.partial(_outproj_kernel, ctx_tiles=ctx_tiles),
        grid=(n // tm,),
        in_specs=[ctok(w) for w in widths] + [ltok(w) for w in widths] + [
            full(wo_f), full(wo_a), _mod_spec(2, row_fn), _mod_spec(3, row_fn), _mod_spec(4, row_fn),
            full(norm2), full(w_router_t)],
        out_specs=[tok(D_MODEL), tok(D_MODEL), pl.BlockSpec((N_EXPERTS, tm), lambda i: (0, i))],
        out_shape=[jax.ShapeDtypeStruct((n, D_MODEL), F32), jax.ShapeDtypeStruct((n, D_MODEL), BF16),
                   jax.ShapeDtypeStruct((N_EXPERTS, n), F32)],
        compiler_params=_cparams(("parallel",)),
        name="outproj",
    )(*ctx, *lat, wo_f, wo_a, mod, mod, mod, norm2, w_router_t)


def _top_rows(vals, k):
    n = vals.shape[0]
    idx = lax.broadcasted_iota(jnp.int32, vals.shape, 0)
    picked = jnp.zeros(vals.shape, F32)
    for _ in range(k):
        best = jnp.max(vals, axis=0, keepdims=True)
        first = jnp.min(jnp.where(vals == best, idx, n), axis=0, keepdims=True)
        hit = idx == first
        picked = jnp.where(hit, 1.0, picked)
        vals = jnp.where(hit, -jnp.inf, vals)
    return picked


def _route_kernel(lg_ref, bias_ref, lo_ref, up_ref, pos_ref, gs_ref, cnt_ref):
    s = jax.nn.sigmoid(lg_ref[...])
    sc = s + bias_ref[...]
    tn = s.shape[1]
    per = N_EXPERTS // N_EXPERT_GROUPS
    g3 = sc.reshape(N_EXPERT_GROUPS, per, tn)
    member = lax.broadcasted_iota(jnp.int32, g3.shape, 1)
    m1 = jnp.max(g3, axis=1, keepdims=True)
    first = jnp.min(jnp.where(g3 == m1, member, per), axis=1, keepdims=True)
    m2 = jnp.max(jnp.where(member == first, -jnp.inf, g3), axis=1, keepdims=True)
    gscore = (m1 + m2).reshape(N_EXPERT_GROUPS, tn)
    gsel = _top_rows(gscore, TOPK_GROUPS)
    emask = jnp.broadcast_to(gsel.reshape(N_EXPERT_GROUPS, 1, tn), g3.shape).reshape(N_EXPERTS, tn)
    masked = jnp.where(emask > 0.5, sc, -jnp.inf)
    self = _top_rows(masked, TOP_K)
    sel = self > 0.5
    w = jnp.where(sel, s, 0.0)
    gate = w / jnp.sum(w, axis=0, keepdims=True) * ROUTED_SCALE

    selb = self.astype(BF16)
    cnt = jnp.sum(self, axis=1, keepdims=True)
    pad = jnp.maximum(jnp.floor((cnt + (RUN_ALIGN - 1)) * (1.0 / RUN_ALIGN)), 1.0) * RUN_ALIGN
    lo = lo_ref[...]
    soff = jnp.dot(lo, jnp.broadcast_to(pad, (N_EXPERTS, LANES)).astype(BF16), preferred_element_type=F32)[:, :1]
    rank = jnp.dot(selb, up_ref[...], preferred_element_type=F32)
    kidx = jnp.dot(lo, selb, preferred_element_type=F32)
    pos_e = jnp.where(sel, soff + rank, 0.0)
    rows_p, rows_g = [], []
    for k in range(TOP_K):
        m = kidx == k
        rows_p.append(jnp.sum(jnp.where(m, pos_e, 0.0), axis=0, keepdims=True))
        rows_g.append(jnp.sum(jnp.where(m, gate, 0.0), axis=0, keepdims=True))
    pos_ref[...] = jnp.concatenate(rows_p, axis=0).astype(jnp.int32)
    gs_ref[...] = jnp.concatenate(rows_g, axis=0)
    cnt_ref[...] = pad.astype(jnp.int32)


def _route(logits_t, bias_col, tc):
    n = logits_t.shape[1]
    lo = jnp.asarray(np.tril(np.ones((N_EXPERTS, N_EXPERTS), np.float32), -1)).astype(BF16)
    up = jnp.asarray(np.triu(np.ones((tc, tc), np.float32), 1)).astype(BF16)
    slot = pl.BlockSpec((TOP_K, tc), lambda i: (0, i))
    return pl.pallas_call(
        _route_kernel,
        grid=(n // tc,),
        in_specs=[pl.BlockSpec((N_EXPERTS, tc), lambda i: (0, i)),
                  pl.BlockSpec((N_EXPERTS, 1), lambda i: (0, 0)),
                  pl.BlockSpec(lo.shape, lambda i: (0, 0)),
                  pl.BlockSpec(up.shape, lambda i: (0, 0))],
        out_specs=[slot, slot, pl.BlockSpec((None, N_EXPERTS, 1), lambda i: (i, 0, 0))],
        out_shape=[jax.ShapeDtypeStruct((TOP_K, n), jnp.int32), jax.ShapeDtypeStruct((TOP_K, n), F32),
                   jax.ShapeDtypeStruct((n // tc, N_EXPERTS, 1), jnp.int32)],
        compiler_params=_cparams(("parallel",)),
        name="route",
    )(logits_t, bias_col, lo, up)


def _aligned(x):
    return pl.multiple_of(x, RUN_ALIGN)


def _block_rows(tc):
    return lax.broadcasted_iota(jnp.int32, (PERM_BLOCK, tc), 0).astype(F32).astype(BF16)


def _block_relative(pos, r0):
    return (pos - r0).astype(F32).astype(BF16)


def _gather_kernel(src_ref, cnt_ref, dst_ref, tot_ref,
                   pos_ref, x_ref, xs_ref, z_ref, sem, *, tc, n_chunks):
    c = pl.program_id(0)
    slot = c % 2

    def wait_chunk(ci, s):
        n = _aligned(tot_ref[ci])
        pltpu.make_async_copy(z_ref.at[s, pl.ds(0, n)], xs_ref.at[pl.ds(0, n)], sem.at[s]).wait()

    def start_runs(ci, s, first, count):
        for j in range(count):
            i = ci * N_EXPERTS + first + j
            n = _aligned(cnt_ref[i])
            pltpu.make_async_copy(z_ref.at[s, pl.ds(_aligned(src_ref[i]), n)],
                                  xs_ref.at[pl.ds(_aligned(dst_ref[i]), n)], sem.at[s]).start()

    @pl.when(c >= 2)
    def _():
        wait_chunk(c - 2, slot)

    rows = _block_rows(tc)

    def permute_blocks(trip):
        for half in range(PERM_UNROLL):
            r0 = pl.multiple_of((trip * PERM_UNROLL + half) * PERM_BLOCK, PERM_BLOCK)
            rel = _block_relative(pos_ref[...], r0)
            onehot = jnp.zeros((PERM_BLOCK, tc), BF16)
            for k in range(TOP_K):
                onehot = jnp.where(rel[k:k + 1, :] == rows, jnp.ones((), BF16), onehot)
            z_ref[slot, pl.ds(r0, PERM_BLOCK), :] = jnp.dot(
                onehot, x_ref[...], preferred_element_type=F32).astype(BF16)

    def plain_trip(trip, carry):
        permute_blocks(trip)
        return carry

    def issuing_trip(trip, carry):
        permute_blocks(trip)
        start_runs(c - 1, 1 - slot, trip * RUNS_PER_TRIP, RUNS_PER_TRIP)
        return carry

    trips = _perm_trips(tot_ref[c])

    @pl.when(c == 0)
    def _():
        lax.fori_loop(0, ISSUE_TRIPS, plain_trip, 0)

    @pl.when(c > 0)
    def _():
        lax.fori_loop(0, ISSUE_TRIPS, issuing_trip, 0)

    lax.fori_loop(ISSUE_TRIPS, trips, plain_trip, 0)

    @pl.when(c == n_chunks - 1)
    def _():
        def e_body(e8, carry):
            start_runs(c, slot, e8 * DMA_UNROLL, DMA_UNROLL)
            return carry
        lax.fori_loop(0, N_EXPERTS // DMA_UNROLL, e_body, 0)
        wait_chunk(c, slot)
        if n_chunks > 1:
            wait_chunk(c - 1, 1 - slot)


def _perm_trips(rows):
    step = PERM_BLOCK * PERM_UNROLL
    return (rows + step - 1) // step


def _chunk_rows_max(tc):
    return TOP_K * tc + N_EXPERTS * RUN_ALIGN


def _local_cap(tc):
    step = PERM_BLOCK * PERM_UNROLL
    return -(-_chunk_rows_max(tc) // step) * step


def _gather(tabs, pos, h2, r_max, tc):
    n = h2.shape[0]
    n_chunks = n // tc
    assert ISSUE_TRIPS * RUNS_PER_TRIP == N_EXPERTS and ISSUE_TRIPS * PERM_BLOCK * PERM_UNROLL <= TOP_K * tc
    return pl.pallas_call(
        functools.partial(_gather_kernel, tc=tc, n_chunks=n_chunks),
        grid_spec=pltpu.PrefetchScalarGridSpec(
            num_scalar_prefetch=4, grid=(n_chunks,),
            in_specs=[pl.BlockSpec((TOP_K, tc), lambda c, *_: (0, c)),
                      pl.BlockSpec((tc, D_MODEL), lambda c, *_: (c, 0))],
            out_specs=pl.BlockSpec(memory_space=pl.ANY),
            scratch_shapes=[pltpu.VMEM((2, _local_cap(tc), D_MODEL), BF16),
                            pltpu.SemaphoreType.DMA((2,))]),
        out_shape=jax.ShapeDtypeStruct((r_max, D_MODEL), BF16),
        compiler_params=_cparams(("arbitrary",), manual_dma=True),
        name="moe_gather",
    )(tabs["src"], tabs["cnt"], tabs["dst"], tabs["tot"], pos, h2)


def _ffn_kernel(first_ref, ntile_ref, tstart_ref, trows_ref, total_ref, xs_ref, wg_ref, wu_ref, wd_ref, ys_ref,
                xbuf, ybuf, wgb, wub, wdb, sem_in, sem_out):
    e = pl.program_id(0)
    total = total_ref[0]

    def in_copy(g):
        s = g % FFN_IN_SLOTS
        n = _aligned(trows_ref[g])
        return pltpu.make_async_copy(xs_ref.at[pl.ds(_aligned(tstart_ref[g]), n)],
                                     xbuf.at[s, pl.ds(0, n)], sem_in.at[s])

    def out_copy(g):
        s = g % 2
        n = _aligned(trows_ref[g])
        return pltpu.make_async_copy(ybuf.at[s, pl.ds(0, n)],
                                     ys_ref.at[pl.ds(_aligned(tstart_ref[g]), n)], sem_out.at[s])

    @pl.when(e == 0)
    def _():
        xbuf[...] = jnp.zeros(xbuf.shape, xbuf.dtype)
        for g in range(FFN_LOOKAHEAD):
            @pl.when(g < total)
            def _():
                in_copy(g).start()

    wgb[...] = wg_ref[...].astype(BF16)
    wub[...] = wu_ref[...].astype(BF16)
    wdb[...] = wd_ref[...].astype(BF16)

    def body(g, carry):
        @pl.when(g + FFN_LOOKAHEAD < total)
        def _():
            in_copy(g + FFN_LOOKAHEAD).start()
        in_copy(g).wait()

        @pl.when(g >= 2)
        def _():
            out_copy(g - 2).wait()
        x = xbuf[g % FFN_IN_SLOTS]
        a = jnp.dot(x, wgb[...], preferred_element_type=F32)
        u = jnp.dot(x, wub[...], preferred_element_type=F32)
        h = (a * jax.nn.sigmoid(a) * u).astype(BF16)
        ybuf[g % 2] = jnp.dot(h, wdb[...], preferred_element_type=F32).astype(BF16)
        out_copy(g).start()
        return carry
    lax.fori_loop(first_ref[e], first_ref[e] + ntile_ref[e], body, 0)

    @pl.when(e == pl.num_programs(0) - 1)
    def _():
        @pl.when(total >= 2)
        def _():
            out_copy(total - 2).wait()
        out_copy(total - 1).wait()


def _ffn(tabs, xs, wg, wu, wd):
    ew = lambda a: pl.BlockSpec((None,) + a.shape[1:], lambda e, *_: (e, 0, 0))
    return pl.pallas_call(
        _ffn_kernel,
        grid_spec=pltpu.PrefetchScalarGridSpec(
            num_scalar_prefetch=5, grid=(N_EXPERTS,),
            in_specs=[pl.BlockSpec(memory_space=pl.ANY), ew(wg), ew(wu), ew(wd)],
            out_specs=pl.BlockSpec(memory_space=pl.ANY),
            scratch_shapes=[pltpu.VMEM((FFN_IN_SLOTS, FFN_BLOCK, D_MODEL), BF16),
                            pltpu.VMEM((2, FFN_BLOCK, D_MODEL), BF16),
                            pltpu.VMEM(wg.shape[1:], BF16), pltpu.VMEM(wu.shape[1:], BF16),
                            pltpu.VMEM(wd.shape[1:], BF16),
                            pltpu.SemaphoreType.DMA((FFN_IN_SLOTS,)), pltpu.SemaphoreType.DMA((2,))]),
        out_shape=jax.ShapeDtypeStruct(xs.shape, BF16),
        compiler_params=_cparams(("arbitrary",), manual_dma=True),
        name="moe_ffn",
    )(tabs["first_tile"], tabs["n_tiles"], tabs["tile_start"], tabs["tile_rows"], tabs["total_tiles"],
      xs, wg, wu, wd)


def _combine_kernel(src_ref, cnt_ref, dst_ref, tot_ref,
                    pos_ref, gs_ref, x1_ref, h_ref, sg_ref, su_ref, sd_ref, g2_ref, nf_ref, ys_ref,
                    yc_ref, yl_ref, ybuf, acc_ref, sem, *, tc, n_chunks, ctx_chunks):
    c = pl.program_id(0)
    slot = c % 2

    def issue(ci, s):
        def e_body(e8, carry):
            for j in range(DMA_UNROLL):
                i = ci * N_EXPERTS + e8 * DMA_UNROLL + j
                n = _aligned(cnt_ref[i])
                pltpu.make_async_copy(ys_ref.at[pl.ds(_aligned(dst_ref[i]), n)],
                                      ybuf.at[s, pl.ds(_aligned(src_ref[i]), n)], sem.at[s]).start()
            return carry
        lax.fori_loop(0, N_EXPERTS // DMA_UNROLL, e_body, 0)

    @pl.when(c == 0)
    def _():
        ybuf[...] = jnp.zeros(ybuf.shape, ybuf.dtype)
        issue(0, 0)

    @pl.when(c + 1 < n_chunks)
    def _():
        issue(c + 1, 1 - slot)

    h = h_ref[...]
    a = jnp.dot(h, sg_ref[...], preferred_element_type=F32)
    u = jnp.dot(h, su_ref[...], preferred_element_type=F32)
    acc_ref[...] = jnp.dot((a * jax.nn.sigmoid(a) * u).astype(BF16), sd_ref[...], preferred_element_type=F32)

    n = _aligned(tot_ref[c])
    pltpu.make_async_copy(ys_ref.at[pl.ds(0, n)], ybuf.at[slot, pl.ds(0, n)], sem.at[slot]).wait()

    tn = (((0,), (0,)), ((), ()))
    rows = _block_rows(tc)
    gates = gs_ref[...].astype(BF16)

    def rb_body(rb2, carry):
        part = None
        for half in range(PERM_UNROLL):
            r0 = pl.multiple_of((rb2 * PERM_UNROLL + half) * PERM_BLOCK, PERM_BLOCK)
            rel = _block_relative(pos_ref[...], r0)
            wt = jnp.zeros((PERM_BLOCK, tc), BF16)
            for k in range(TOP_K):
                wt = jnp.where(rel[k:k + 1, :] == rows, gates[k:k + 1, :], wt)
            d = lax.dot_general(wt, ybuf[slot, pl.ds(r0, PERM_BLOCK), :], tn, preferred_element_type=F32)
            part = d if part is None else part + d
        acc_ref[...] += part
        return carry
    lax.fori_loop(0, _perm_trips(tot_ref[c]), rb_body, 0)

    x2 = x1_ref[...] + g2_ref[...] * acc_ref[...]
    y = _rms(x2, nf_ref[...])

    @pl.when(c < ctx_chunks)
    def _():
        yc_ref[...] = y

    @pl.when(c >= ctx_chunks)
    def _():
        yl_ref[...] = y


def _combine(tabs, pos, gslot, x1, h2, sg, su, sd, mod, norm_f, ys, tc, row_fn, n_ctx):
    n = x1.shape[0]
    n_chunks = n // tc
    ctx_chunks = n_ctx // tc
    tok = lambda w: pl.BlockSpec((tc, w), lambda c, *_: (c, 0))
    slot = pl.BlockSpec((TOP_K, tc), lambda c, *_: (0, c))
    full = lambda a: pl.BlockSpec(a.shape, lambda c, *_: (0,) * a.ndim)
    return pl.pallas_call(
        functools.partial(_combine_kernel, tc=tc, n_chunks=n_chunks, ctx_chunks=ctx_chunks),
        grid_spec=pltpu.PrefetchScalarGridSpec(
            num_scalar_prefetch=4, grid=(n_chunks,),
            in_specs=[slot, slot, tok(D_MODEL), tok(D_MODEL), full(sg), full(su), full(sd),
                      _mod_spec(5, lambda c, *_: row_fn(c)), full(norm_f), pl.BlockSpec(memory_space=pl.ANY)],
            out_specs=[pl.BlockSpec((tc, D_MODEL), lambda c, *_: (jnp.minimum(c, ctx_chunks - 1), 0)),
                       pl.BlockSpec((tc, D_MODEL), lambda c, *_: (jnp.maximum(c - ctx_chunks, 0), 0))],
            scratch_shapes=[pltpu.VMEM((2, _local_cap(tc), D_MODEL), BF16),
                            pltpu.VMEM((tc, D_MODEL), F32),
                            pltpu.SemaphoreType.DMA((2,))]),
        out_shape=[jax.ShapeDtypeStruct((n_ctx, D_MODEL), F32), jax.ShapeDtypeStruct((n - n_ctx, D_MODEL), F32)],
        compiler_params=_cparams(("arbitrary",), manual_dma=True),
        name="moe_combine",
    )(tabs["src"], tabs["cnt"], tabs["dst"], tabs["tot"], pos, gslot, x1, h2, sg, su, sd, mod, norm_f, ys)


def _dispatch_tables(pad_cnt, r_max):
    i32 = jnp.int32
    src = jnp.cumsum(pad_cnt, axis=1) - pad_cnt
    tot = jnp.sum(pad_cnt, axis=1)
    e_rows = jnp.sum(pad_cnt, axis=0)
    e_start = jnp.cumsum(e_rows) - e_rows
    dst = e_start[None, :] + jnp.cumsum(pad_cnt, axis=0) - pad_cnt
    n_tiles = (e_rows + FFN_BLOCK - 1) // FFN_BLOCK
    tile_end = jnp.cumsum(n_tiles)
    first_tile = tile_end - n_tiles
    g = jnp.arange(r_max // FFN_BLOCK + N_EXPERTS)
    before = tile_end[None, :] <= g[:, None]
    owner = jnp.concatenate([jnp.ones_like(before[:, :1]), before[:, :-1]], axis=1) & ~before
    offset = (g - jnp.sum(jnp.where(before, n_tiles, 0), axis=1)) * FFN_BLOCK
    tile_start = jnp.clip(jnp.sum(jnp.where(before, e_rows, 0), axis=1) + offset, 0, r_max - FFN_BLOCK)
    tile_rows = jnp.clip(jnp.sum(jnp.where(owner, e_rows, 0), axis=1) - offset, RUN_ALIGN, FFN_BLOCK)
    return {"src": src.reshape(-1).astype(i32), "cnt": pad_cnt.reshape(-1).astype(i32),
            "dst": dst.reshape(-1).astype(i32), "tot": tot.astype(i32),
            "first_tile": first_tile.astype(i32), "n_tiles": n_tiles.astype(i32),
            "tile_start": tile_start.astype(i32), "tile_rows": tile_rows.astype(i32),
            "total_tiles": tile_end[-1:].astype(i32)}


def _moe(x1, h2, logits_t, bias_col, wg, wu, wd, sg, su, sd, mod, norm_f, row_fn, n_ctx):
    n = x1.shape[0]
    tc = MOE_CHUNK
    pos, gslot, pad_cnt = _route(logits_t, bias_col, tc)
    r_max = (n // tc) * _chunk_rows_max(tc)
    tabs = _dispatch_tables(pad_cnt[:, :, 0], r_max)
    xs = _gather(tabs, pos, h2, r_max, tc)
    ys = _ffn(tabs, xs, wg, wu, wd)
    return _combine(tabs, pos, gslot, x1, h2, sg, su, sd, mod, norm_f, ys, tc, row_fn, n_ctx)


def _channel_dft():
    ang = 2.0 * np.pi * ((np.arange(F_HD)[:, None] * np.arange(F_HD)[None, :]) % F_HD) / F_HD
    eye = np.eye(F_GROUPS)
    return np.concatenate([np.kron(eye, np.cos(ang)), np.kron(eye, -np.sin(ang))], axis=1).astype(np.float32)


def _rope_tables(t):
    pos = np.arange(t)
    row, col = pos // GRID_W, pos % GRID_W
    n_freq = HEAD_DIM // 4
    inv = ROPE_THETA ** (-np.arange(n_freq, dtype=np.float64) / n_freq)
    lane = np.arange(LANES)
    hd = lane % HEAD_DIM
    within = hd % (HEAD_DIM // 2)
    freq = within % n_freq
    first = within < n_freq
    p = np.where((hd < HEAD_DIM // 2)[None, :], row[:, None], col[:, None]).astype(np.float64)
    ang = p * inv[freq][None, :]
    cos, sin = np.cos(ang), np.sin(ang)
    sin_up = np.where(first[None, :], -sin, 0.0)
    sin_dn = np.where(first[None, :], 0.0, sin)
    return [jnp.asarray(a.astype(np.float32)) for a in (cos, sin_up, sin_dn)]


def _block_diag_pairs(wf):
    per = LANES // F_HD
    z = jnp.zeros((F_HD, F_HD), wf.dtype)
    blocks = []
    for c in range(F_GROUPS // per):
        rows = [jnp.concatenate([wf[c * per + r] if r == cc else z for cc in range(per)], axis=1) for r in range(per)]
        blocks.append(jnp.concatenate(rows, axis=0))
    return jnp.stack(blocks)


def kernel(x_prompt, x_sample, cache_k, cache_v, c, c_ctx, w_mod, b_mod, norm1, w_in, w_fourier, sink,
           w_out, norm2, w_router, router_bias, w_gate, w_up, w_down, ws_gate, ws_up, ws_down, norm_f):
    nb_ctx, t_ctx, _ = x_prompt.shape
    nb_lat, t_lat, _ = x_sample.shape
    l = 0
    cvec = jnp.concatenate([c_ctx[None, :], c, jnp.zeros((MOD_ROWS - 1 - nb_lat, D_MODEL), F32)], axis=0)
    mod = _modulation(cvec, w_mod[l], b_mod[l]).reshape(MOD_ROWS, 6, 1, D_MODEL)

    w_in_b = w_in[l].astype(BF16)
    dft_c = jnp.asarray(_channel_dft()).astype(BF16)
    wf_bd = _block_diag_pairs(w_fourier[l]).astype(BF16)
    wo_f = w_out[l][:F_DIM].astype(BF16)
    wo_a = w_out[l][F_DIM:].astype(BF16)
    w_router_t = w_router[l].T.astype(BF16)
    bias_col = router_bias[l].reshape(N_EXPERTS, 1)
    n1 = norm1[l].reshape(1, D_MODEL)
    n2 = norm2[l].reshape(1, D_MODEL)
    nf = norm_f.reshape(1, D_MODEL)
    sink_col = sink[l]
    wg, wu, wd = w_gate[l], w_up[l], w_down[l]
    sg, su, sd = ws_gate[l].astype(BF16), ws_up[l].astype(BF16), ws_down[l].astype(BF16)

    def mixers(x, batch, t, latent, kx, vx, t1, t2, tm):
        n = batch * t
        x2d = x.reshape(n, D_MODEL)
        tiles = t // tm
        row_fn = (lambda i: 1 + i // tiles) if latent else (lambda i: 0)
        rope_tabs = _rope_tables(t) if latent else None
        outs = _project(x2d, mod, n1, w_in_b, dft_c, rope_tabs, tm, tiles, row_fn)
        wr, wi, q, k, v = outs[:5]
        mixf = _fourier(wr, wi, wf_bd, batch, t, t1, t2)
        q3, k3, v3 = (a.reshape(batch, t, -1) for a in (q, k, v))
        if latent:
            mixa = _attention(q3, k3, v3, kx, vx, sink_col, batch, t, True)
        else:
            mixa = _attention(q3, None, None, k3, v3, sink_col, batch, t, False)
        return (x2d, mixf.reshape(n, F_DIM), mixa.reshape(n, ATTN_DIM)), outs[5:]

    ctx, (kf, vf) = mixers(x_prompt, nb_ctx, t_ctx, False, None, None, 16, 16, 256)
    kx = cache_k[:, l].reshape(nb_lat, -1, KV_DIM).astype(BF16)
    vx = cache_v[:, l].reshape(nb_lat, -1, KV_DIM).astype(BF16)
    lat, _ = mixers(x_sample, nb_lat, t_lat, True, kx, vx, 64, 64, 512)

    n_ctx = nb_ctx * t_ctx

    def tile_row(tile):
        ctx_tiles, lat_tiles = n_ctx // tile, t_lat // tile
        return lambda i: jnp.where(i < ctx_tiles, 0, 1 + (i - ctx_tiles) // lat_tiles)

    x1, h2, logits_t = _outproj(ctx, lat, wo_f, wo_a, mod, n2, w_router_t, OUTPROJ_TILE, tile_row(OUTPROJ_TILE))
    y_prompt, y_sample = _moe(x1, h2, logits_t, bias_col, wg, wu, wd, sg, su, sd, mod, nf, tile_row(MOE_CHUNK),
                              n_ctx)
    new_k = kf.reshape(nb_ctx, 1, t_ctx, N_KV_HEADS, HEAD_DIM)
    new_v = vf.reshape(nb_ctx, 1, t_ctx, N_KV_HEADS, HEAD_DIM)
    return (y_prompt.reshape(x_prompt.shape), y_sample.reshape(x_sample.shape), new_k, new_v)
```

```python
import functools

import numpy as np
import jax
import jax.numpy as jnp
from jax import lax
from jax.experimental import pallas as pl
from jax.experimental.pallas import tpu as pltpu

F32 = jnp.float32
BF16 = jnp.bfloat16

D_MODEL = 1024
GRID_W = 64
HEAD_DIM = 64
N_HEADS = 12
N_KV_HEADS = 4
GQA_GROUP = N_HEADS // N_KV_HEADS
ATTN_DIM = N_HEADS * HEAD_DIM
KV_DIM = N_KV_HEADS * HEAD_DIM
F_GROUPS = 4
F_HD = 64
F_DIM = F_GROUPS * F_HD
IN_DIM = F_DIM + ATTN_DIM + 2 * KV_DIM
WINDOW = 128
Q_BLOCK = 128
ROPE_THETA = 10000.0
N_EXPERTS = 64
TOP_K = 8
N_EXPERT_GROUPS = 8
TOPK_GROUPS = 4
EXPERT_DIM = 256
SHARED_DIM = 256
ROUTED_SCALE = 2.5
EPS = 1e-6

LANES = 128
MOD_ROWS = 8
RUN_ALIGN = 16
PERM_BLOCK = 256
FFN_BLOCK = 512
FFN_LOOKAHEAD = 3
FFN_IN_SLOTS = FFN_LOOKAHEAD + 1
MOE_CHUNK = 256
OUTPROJ_TILE = 512
SOFTMAX_ROWS = 32
DMA_UNROLL = 8
VMEM_LIMIT = 56 * 1024 * 1024


def _cparams(sem, manual_dma=False):
    return pltpu.CompilerParams(dimension_semantics=sem, vmem_limit_bytes=VMEM_LIMIT,
                                disable_bounds_checks=manual_dma)


def _bdot(a, b):
    return jnp.dot(a.astype(BF16), b.astype(BF16), preferred_element_type=F32)


def _rms(x, g):
    return x * lax.rsqrt(jnp.mean(x * x, axis=-1, keepdims=True) + EPS) * g


def _mod_kernel(c_ref, w_ref, b_ref, o_ref):
    c = c_ref[...]
    a = c * jax.nn.sigmoid(c)
    o_ref[...] = _bdot(a, w_ref[...]) + b_ref[...]


def _modulation(cvec, w_mod, b_mod):
    n = w_mod.shape[1]
    tn = 1024
    return pl.pallas_call(
        _mod_kernel,
        grid=(n // tn,),
        in_specs=[pl.BlockSpec((MOD_ROWS, D_MODEL), lambda j: (0, 0)),
                  pl.BlockSpec((D_MODEL, tn), lambda j: (0, j)),
                  pl.BlockSpec((1, tn), lambda j: (0, j))],
        out_specs=pl.BlockSpec((MOD_ROWS, tn), lambda j: (0, j)),
        out_shape=jax.ShapeDtypeStruct((MOD_ROWS, n), F32),
        compiler_params=_cparams(("parallel",)),
        name="modulation",
    )(cvec, w_mod, b_mod.reshape(1, n))


def _mod_spec(piece, row_fn):
    return pl.BlockSpec((None, None, 1, D_MODEL), lambda *idx: (row_fn(*idx), piece, 0, 0))


def _rope(x, cos, sin_up, sin_dn):
    outs = []
    for j in range(x.shape[1] // LANES):
        xj = x[:, j * LANES:(j + 1) * LANES]
        up = pltpu.roll(xj, LANES - 16, axis=1)
        dn = pltpu.roll(xj, 16, axis=1)
        outs.append(xj * cos + up * sin_up + dn * sin_dn)
    return jnp.concatenate(outs, axis=1)


def _proj_kernel(*refs, latent):
    if latent:
        (x_ref, sh_ref, sc_ref, g_ref, w_ref, dft_ref, cos_ref, sup_ref, sdn_ref,
         wr_ref, wi_ref, q_ref, k_ref, v_ref) = refs
    else:
        (x_ref, sh_ref, sc_ref, g_ref, w_ref, dft_ref,
         wr_ref, wi_ref, q_ref, k_ref, v_ref, kf_ref, vf_ref) = refs
    x = x_ref[...]
    h = _rms(x, g_ref[...]) * (1.0 + sc_ref[...]) + sh_ref[...]
    p = _bdot(h, w_ref[...])
    u = p[:, :F_DIM]
    q = p[:, F_DIM:F_DIM + ATTN_DIM]
    k = p[:, F_DIM + ATTN_DIM:F_DIM + ATTN_DIM + KV_DIM]
    v = p[:, F_DIM + ATTN_DIM + KV_DIM:]
    w = _bdot(u, dft_ref[...])
    wr_ref[...] = w[:, :F_DIM]
    wi_ref[...] = w[:, F_DIM:]
    if latent:
        cos, sup, sdn = cos_ref[...], sup_ref[...], sdn_ref[...]
        q = _rope(q, cos, sup, sdn)
        k = _rope(k, cos, sup, sdn)
    else:
        kf_ref[...] = k
        vf_ref[...] = v
    q_ref[...] = (q * (HEAD_DIM ** -0.5)).astype(BF16)
    k_ref[...] = k.astype(BF16)
    v_ref[...] = v.astype(BF16)


def _project(x2d, mod, norm1, w_in, dft_c, rope_tabs, tm, tiles_per_batch, row_fn):
    n = x2d.shape[0]
    latent = rope_tabs is not None
    tok = lambda w: pl.BlockSpec((tm, w), lambda i: (i, 0))
    full = lambda a: pl.BlockSpec(a.shape, lambda i: (0,) * a.ndim)
    in_specs = [tok(D_MODEL), _mod_spec(0, row_fn), _mod_spec(1, row_fn), full(norm1), full(w_in), full(dft_c)]
    args = [x2d, mod, mod, norm1, w_in, dft_c]
    out_specs = [tok(F_DIM), tok(F_DIM), tok(ATTN_DIM), tok(KV_DIM), tok(KV_DIM)]
    out_shape = [jax.ShapeDtypeStruct((n, F_DIM), F32), jax.ShapeDtypeStruct((n, F_DIM), F32),
                 jax.ShapeDtypeStruct((n, ATTN_DIM), BF16), jax.ShapeDtypeStruct((n, KV_DIM), BF16),
                 jax.ShapeDtypeStruct((n, KV_DIM), BF16)]
    if latent:
        pos = pl.BlockSpec((tm, LANES), lambda i: (i % tiles_per_batch, 0))
        in_specs += [pos, pos, pos]
        args += list(rope_tabs)
    else:
        out_specs += [tok(KV_DIM), tok(KV_DIM)]
        out_shape += [jax.ShapeDtypeStruct((n, KV_DIM), F32), jax.ShapeDtypeStruct((n, KV_DIM), F32)]
    return pl.pallas_call(
        functools.partial(_proj_kernel, latent=latent),
        grid=(n // tm,),
        in_specs=in_specs, out_specs=out_specs, out_shape=out_shape,
        compiler_params=_cparams(("parallel",)),
        name="project_latent" if latent else "project_context",
    )(*args)


def _fourier_kernel(wr_ref, wi_ref, a_ref, b_ref, wf_ref, o_ref, yr_ref, yi_ref, z_ref, *, t1, t2, scale):
    for j in range(t2):
        rows = pl.ds(j, t1, stride=t2)
        xin = jnp.concatenate([wr_ref[rows, :], wi_ref[rows, :]], axis=0)
        y = _bdot(a_ref[j], xin)
        yr_ref[rows, :] = y[:t1]
        yi_ref[rows, :] = y[t1:]
    bm = b_ref[...]
    for k1 in range(t1):
        rows = pl.ds(k1 * t2, t2)
        yin = jnp.concatenate([yr_ref[rows, :], yi_ref[rows, :]], axis=0)
        z_ref[pl.ds(k1, t2, stride=t1), :] = _bdot(bm, yin)
    o_ref[...] = (_bdot(z_ref[...], wf_ref[...]) * scale).astype(BF16)


def _dft_tables(t1, t2):
    t = t1 * t2
    k1 = np.arange(t1)[None, :, None]
    pos = (t2 * np.arange(t1)[None, None, :] + np.arange(t2)[:, None, None])
    ang = 2.0 * np.pi * ((k1 * pos) % t) / t
    c, s = np.cos(ang), np.sin(ang)
    a = np.concatenate([np.concatenate([c, s], axis=2), np.concatenate([-s, c], axis=2)], axis=1)
    ang2 = 2.0 * np.pi * ((np.arange(t2)[:, None] * np.arange(t2)[None, :]) % t2) / t2
    b = np.concatenate([np.cos(ang2), np.sin(ang2)], axis=1)
    return a.astype(np.float32), b.astype(np.float32)


def _fourier(wr, wi, wf_bd, batch, t, t1, t2):
    a_np, b_np = _dft_tables(t1, t2)
    a = jnp.asarray(a_np).astype(BF16)
    b = jnp.asarray(b_np).astype(BF16)
    cw = LANES
    blk = pl.BlockSpec((None, t, cw), lambda bi, ci: (bi, 0, ci))
    return pl.pallas_call(
        functools.partial(_fourier_kernel, t1=t1, t2=t2, scale=float((t * F_HD) ** -0.5)),
        grid=(batch, F_DIM // cw),
        in_specs=[blk, blk,
                  pl.BlockSpec(a.shape, lambda bi, ci: (0, 0, 0)),
                  pl.BlockSpec(b.shape, lambda bi, ci: (0, 0)),
                  pl.BlockSpec((None, cw, cw), lambda bi, ci: (ci, 0, 0))],
        out_specs=blk,
        out_shape=jax.ShapeDtypeStruct((batch, t, F_DIM), BF16),
        scratch_shapes=[pltpu.VMEM((t, cw), F32), pltpu.VMEM((t, cw), F32), pltpu.VMEM((t, cw), F32)],
        compiler_params=_cparams(("parallel", "parallel")),
        name="fourier_%d" % t,
    )(wr.reshape(batch, t, F_DIM), wi.reshape(batch, t, F_DIM), a, b, wf_bd)


def _attn_kernel(*refs, windowed, n_blocks):
    if windowed:
        q_ref, kp_ref, kc_ref, kn_ref, vp_ref, vc_ref, vn_ref, kx_ref, vx_ref, sink_ref, o_ref, s_ref, p_ref = refs
    else:
        q_ref, kx_ref, vx_ref, sink_ref, o_ref, s_ref, p_ref = refs
    nt = (((1,), (1,)), ((), ()))
    parts = [(kx_ref, vx_ref, None)]
    if windowed:
        i = pl.program_id(1)
        a = lax.broadcasted_iota(jnp.int32, (Q_BLOCK, Q_BLOCK), 0)
        j = lax.broadcasted_iota(jnp.int32, (Q_BLOCK, Q_BLOCK), 1)
        prev_ok = (j >= a) & (i > 0)
        next_ok = (j <= a) & (i < n_blocks - 1)
        parts = [(kp_ref, vp_ref, prev_ok), (kc_ref, vc_ref, None), (kn_ref, vn_ref, next_ok)] + parts
    n_keys = sum(k_ref.shape[0] for k_ref, _, _ in parts)
    outs = []
    for kv in range(N_KV_HEADS):
        heads = range(kv * GQA_GROUP, (kv + 1) * GQA_GROUP)
        hs = slice(kv * HEAD_DIM, (kv + 1) * HEAD_DIM)
        qs = jnp.concatenate([q_ref[:, h * HEAD_DIM:(h + 1) * HEAD_DIM] for h in heads], axis=0)
        c0 = 0
        for k_ref, _, ok in parts:
            s = lax.dot_general(qs, k_ref[:, hs], nt, preferred_element_type=F32)
            if ok is not None:
                s = jnp.where(jnp.concatenate([ok] * GQA_GROUP, axis=0), s, -jnp.inf)
            s_ref[:, c0:c0 + s.shape[1]] = s
            c0 += s.shape[1]
        invs = []
        for r0 in range(0, GQA_GROUP * Q_BLOCK, SOFTMAX_ROWS):
            s = s_ref[r0:r0 + SOFTMAX_ROWS, :]
            sk = sink_ref[kv * GQA_GROUP + r0 // Q_BLOCK]
            slabs = [s[:, c:c + LANES] for c in range(0, n_keys, LANES)]
            m = jnp.maximum(jnp.max(functools.reduce(jnp.maximum, slabs), axis=1, keepdims=True), sk)
            p = jnp.exp(s - m)
            pslabs = [p[:, c:c + LANES] for c in range(0, n_keys, LANES)]
            den = jnp.sum(functools.reduce(jnp.add, pslabs), axis=1, keepdims=True) + jnp.exp(sk - m)
            p_ref[r0:r0 + SOFTMAX_ROWS, :] = p.astype(BF16)
            invs.append(1.0 / den)
        o = None
        c0 = 0
        for k_ref, v_ref, _ in parts:
            nk = k_ref.shape[0]
            d = jnp.dot(p_ref[:, c0:c0 + nk], v_ref[:, hs], preferred_element_type=F32)
            o = d if o is None else o + d
            c0 += nk
        o = o * jnp.concatenate(invs, axis=0)
        outs += [o[g * Q_BLOCK:(g + 1) * Q_BLOCK] for g in range(GQA_GROUP)]
    o_ref[...] = jnp.concatenate(outs, axis=1).astype(BF16)


def _attention(q, k, v, kx, vx, sink_col, batch, t, windowed):
    nb = t // Q_BLOCK
    n_ctx = kx.shape[1]
    n_keys = n_ctx + (3 * Q_BLOCK if windowed else 0)
    qspec = pl.BlockSpec((None, Q_BLOCK, ATTN_DIM), lambda b, i: (b, i, 0))
    xspec = pl.BlockSpec((None, n_ctx, KV_DIM), lambda b, i: (b, 0, 0))
    sspec = pl.BlockSpec(memory_space=pltpu.SMEM)
    if windowed:
        prev = pl.BlockSpec((None, Q_BLOCK, KV_DIM), lambda b, i: (b, jnp.maximum(i - 1, 0), 0))
        cur = pl.BlockSpec((None, Q_BLOCK, KV_DIM), lambda b, i: (b, i, 0))
        nxt = pl.BlockSpec((None, Q_BLOCK, KV_DIM), lambda b, i: (b, jnp.minimum(i + 1, nb - 1), 0))
        in_specs = [qspec, prev, cur, nxt, prev, cur, nxt, xspec, xspec, sspec]
        args = [q, k, k, k, v, v, v, kx, vx, sink_col]
    else:
        in_specs = [qspec, xspec, xspec, sspec]
        args = [q, kx, vx, sink_col]
    return pl.pallas_call(
        functools.partial(_attn_kernel, windowed=windowed, n_blocks=nb),
        grid=(batch, nb),
        in_specs=in_specs, out_specs=qspec,
        out_shape=jax.ShapeDtypeStruct((batch, t, ATTN_DIM), BF16),
        scratch_shapes=[pltpu.VMEM((GQA_GROUP * Q_BLOCK, n_keys), F32),
                        pltpu.VMEM((GQA_GROUP * Q_BLOCK, n_keys), BF16)],
        compiler_params=_cparams(("parallel", "parallel")),
        name="attention_latent" if windowed else "attention_context",
    )(*args)


def _outproj_kernel(xc_ref, mfc_ref, mac_ref, xl_ref, mfl_ref, mal_ref,
                    wof_ref, woa_ref, g1_ref, sh_ref, sc_ref, n2_ref, wr_ref,
                    x1_ref, h_ref, lg_ref, *, ctx_tiles):
    def body(x_ref, mf_ref, ma_ref):
        o = jnp.dot(mf_ref[...], wof_ref[...], preferred_element_type=F32)
        o = o + jnp.dot(ma_ref[...], woa_ref[...], preferred_element_type=F32)
        x1 = x_ref[...] + g1_ref[...] * o
        x1_ref[...] = x1
        h = (_rms(x1, n2_ref[...]) * (1.0 + sc_ref[...]) + sh_ref[...]).astype(BF16)
        h_ref[...] = h
        lg_ref[...] = lax.dot_general(wr_ref[...], h, (((1,), (1,)), ((), ())), preferred_element_type=F32)

    is_ctx = pl.program_id(0) < ctx_tiles
    pl.when(is_ctx)(lambda: body(xc_ref, mfc_ref, mac_ref))
    pl.when(jnp.logical_not(is_ctx))(lambda: body(xl_ref, mfl_ref, mal_ref))


def _outproj(ctx, lat, wo_f, wo_a, mod, norm2, w_router_t, tm, row_fn):
    n_c, n_l = ctx[0].shape[0], lat[0].shape[0]
    ctx_tiles = n_c // tm
    n = n_c + n_l
    ctok = lambda w: pl.BlockSpec((tm, w), lambda i: (jnp.minimum(i, ctx_tiles - 1), 0))
    ltok = lambda w: pl.BlockSpec((tm, w), lambda i: (jnp.maximum(i - ctx_tiles, 0), 0))
    tok = lambda w: pl.BlockSpec((tm, w), lambda i: (i, 0))
    full = lambda a: pl.BlockSpec(a.shape, lambda i: (0,) * a.ndim)
    widths = (D_MODEL, F_DIM, ATTN_DIM)
    return pl.pallas_call(
        functools.partial(_outproj_kernel, ctx_tiles=ctx_tiles),
        grid=(n // tm,),
        in_specs=[ctok(w) for w in widths] + [ltok(w) for w in widths] + [
            full(wo_f), full(wo_a), _mod_spec(2, row_fn), _mod_spec(3, row_fn), _mod_spec(4, row_fn),
            full(norm2), full(w_router_t)],
        out_specs=[tok(D_MODEL), tok(D_MODEL), pl.BlockSpec((N_EXPERTS, tm), lambda i: (0, i))],
        out_shape=[jax.ShapeDtypeStruct((n, D_MODEL), F32), jax.ShapeDtypeStruct((n, D_MODEL), BF16),
                   jax.ShapeDtypeStruct((N_EXPERTS, n), F32)],
        compiler_params=_cparams(("parallel",)),
        name="outproj",
    )(*ctx, *lat, wo_f, wo_a, mod, mod, mod, norm2, w_router_t)


def _top_rows(vals, k):
    n = vals.shape[0]
    idx = lax.broadcasted_iota(jnp.int32, vals.shape, 0)
    picked = jnp.zeros(vals.shape, F32)
    for _ in range(k):
        best = jnp.max(vals, axis=0, keepdims=True)
        first = jnp.min(jnp.where(vals == best, idx, n), axis=0, keepdims=True)
        hit = idx == first
        picked = jnp.where(hit, 1.0, picked)
        vals = jnp.where(hit, -jnp.inf, vals)
    return picked


def _route_kernel(lg_ref, bias_ref, lo_ref, up_ref, pos_ref, gs_ref, cnt_ref):
    s = jax.nn.sigmoid(lg_ref[...])
    sc = s + bias_ref[...]
    tn = s.shape[1]
    per = N_EXPERTS // N_EXPERT_GROUPS
    g3 = sc.reshape(N_EXPERT_GROUPS, per, tn)
    member = lax.broadcasted_iota(jnp.int32, g3.shape, 1)
    m1 = jnp.max(g3, axis=1, keepdims=True)
    first = jnp.min(jnp.where(g3 == m1, member, per), axis=1, keepdims=True)
    m2 = jnp.max(jnp.where(member == first, -jnp.inf, g3), axis=1, keepdims=True)
    gscore = (m1 + m2).reshape(N_EXPERT_GROUPS, tn)
    gsel = _top_rows(gscore, TOPK_GROUPS)
    emask = jnp.broadcast_to(gsel.reshape(N_EXPERT_GROUPS, 1, tn), g3.shape).reshape(N_EXPERTS, tn)
    masked = jnp.where(emask > 0.5, sc, -jnp.inf)
    self = _top_rows(masked, TOP_K)
    sel = self > 0.5
    w = jnp.where(sel, s, 0.0)
    gate = w / jnp.sum(w, axis=0, keepdims=True) * ROUTED_SCALE

    selb = self.astype(BF16)
    cnt = jnp.sum(self, axis=1, keepdims=True)
    pad = jnp.maximum(jnp.floor((cnt + (RUN_ALIGN - 1)) * (1.0 / RUN_ALIGN)), 1.0) * RUN_ALIGN
    lo = lo_ref[...]
    soff = jnp.dot(lo, jnp.broadcast_to(pad, (N_EXPERTS, LANES)).astype(BF16), preferred_element_type=F32)[:, :1]
    rank = jnp.dot(selb, up_ref[...], preferred_element_type=F32)
    kidx = jnp.dot(lo, selb, preferred_element_type=F32)
    pos_e = jnp.where(sel, soff + rank, 0.0)
    rows_p, rows_g = [], []
    for k in range(TOP_K):
        m = kidx == k
        rows_p.append(jnp.sum(jnp.where(m, pos_e, 0.0), axis=0, keepdims=True))
        rows_g.append(jnp.sum(jnp.where(m, gate, 0.0), axis=0, keepdims=True))
    pos_ref[...] = jnp.concatenate(rows_p, axis=0).astype(jnp.int32)
    gs_ref[...] = jnp.concatenate(rows_g, axis=0)
    cnt_ref[...] = pad.astype(jnp.int32)


def _route(logits_t, bias_col, tc):
    n = logits_t.shape[1]
    lo = jnp.asarray(np.tril(np.ones((N_EXPERTS, N_EXPERTS), np.float32), -1)).astype(BF16)
    up = jnp.asarray(np.triu(np.ones((tc, tc), np.float32), 1)).astype(BF16)
    slot = pl.BlockSpec((TOP_K, tc), lambda i: (0, i))
    return pl.pallas_call(
        _route_kernel,
        grid=(n // tc,),
        in_specs=[pl.BlockSpec((N_EXPERTS, tc), lambda i: (0, i)),
                  pl.BlockSpec((N_EXPERTS, 1), lambda i: (0, 0)),
                  pl.BlockSpec(lo.shape, lambda i: (0, 0)),
                  pl.BlockSpec(up.shape, lambda i: (0, 0))],
        out_specs=[slot, slot, pl.BlockSpec((None, N_EXPERTS, 1), lambda i: (i, 0, 0))],
        out_shape=[jax.ShapeDtypeStruct((TOP_K, n), jnp.int32), jax.ShapeDtypeStruct((TOP_K, n), F32),
                   jax.ShapeDtypeStruct((n // tc, N_EXPERTS, 1), jnp.int32)],
        compiler_params=_cparams(("parallel",)),
        name="route",
    )(logits_t, bias_col, lo, up)


def _aligned(x):
    return pl.multiple_of(x, RUN_ALIGN)


def _block_rows(tc):
    return lax.broadcasted_iota(jnp.int32, (PERM_BLOCK, tc), 0).astype(F32).astype(BF16)


def _block_relative(pos, r0):
    return (pos - r0).astype(F32).astype(BF16)


def _gather_kernel(src_ref, cnt_ref, dst_ref, tot_ref,
                   pos_ref, x_ref, xs_ref, z_ref, sem, *, tc, n_chunks):
    c = pl.program_id(0)
    slot = c % 2

    def wait_chunk(ci, s):
        n = _aligned(tot_ref[ci])
        pltpu.make_async_copy(z_ref.at[s, pl.ds(0, n)], xs_ref.at[pl.ds(0, n)], sem.at[s]).wait()

    def start_runs(ci, s, first, count):
        for j in range(count):
            i = ci * N_EXPERTS + first + j
            n = _aligned(cnt_ref[i])
            pltpu.make_async_copy(z_ref.at[s, pl.ds(_aligned(src_ref[i]), n)],
                                  xs_ref.at[pl.ds(_aligned(dst_ref[i]), n)], sem.at[s]).start()

    @pl.when(c >= 2)
    def _():
        wait_chunk(c - 2, slot)

    rows = _block_rows(tc)

    def permute_block(b):
        r0 = _block_start(b)
        rel = _block_relative(pos_ref[...], r0)
        onehot = jnp.zeros((PERM_BLOCK, tc), BF16)
        for k in range(TOP_K):
            onehot = jnp.where(rel[k:k + 1, :] == rows, jnp.ones((), BF16), onehot)
        z_ref[slot, pl.ds(r0, PERM_BLOCK), :] = jnp.dot(
            onehot, x_ref[...], preferred_element_type=F32).astype(BF16)

    for b in range(_sure_blocks(tc)):
        permute_block(b)

    def tail_block(b, carry):
        permute_block(b)
        return carry
    lax.fori_loop(_sure_blocks(tc), _perm_blocks(tot_ref[c]), tail_block, 0)

    def e_body(e8, carry):
        start_runs(c, slot, e8 * DMA_UNROLL, DMA_UNROLL)
        return carry
    lax.fori_loop(0, N_EXPERTS // DMA_UNROLL, e_body, 0)

    @pl.when(c == n_chunks - 1)
    def _():
        wait_chunk(c, slot)
        if n_chunks > 1:
            wait_chunk(c - 1, 1 - slot)


def _perm_blocks(rows):
    return (rows + PERM_BLOCK - 1) // PERM_BLOCK


def _sure_blocks(tc):
    return TOP_K * tc // PERM_BLOCK


def _block_start(b):
    return b * PERM_BLOCK if isinstance(b, int) else pl.multiple_of(b * PERM_BLOCK, PERM_BLOCK)


def _chunk_rows_max(tc):
    return TOP_K * tc + N_EXPERTS * RUN_ALIGN


def _local_cap(tc):
    return -(-_chunk_rows_max(tc) // PERM_BLOCK) * PERM_BLOCK


def _gather(tabs, pos, h2, r_max, tc):
    n = h2.shape[0]
    n_chunks = n // tc
    return pl.pallas_call(
        functools.partial(_gather_kernel, tc=tc, n_chunks=n_chunks),
        grid_spec=pltpu.PrefetchScalarGridSpec(
            num_scalar_prefetch=4, grid=(n_chunks,),
            in_specs=[pl.BlockSpec((TOP_K, tc), lambda c, *_: (0, c)),
                      pl.BlockSpec((tc, D_MODEL), lambda c, *_: (c, 0))],
            out_specs=pl.BlockSpec(memory_space=pl.ANY),
            scratch_shapes=[pltpu.VMEM((2, _local_cap(tc), D_MODEL), BF16),
                            pltpu.SemaphoreType.DMA((2,))]),
        out_shape=jax.ShapeDtypeStruct((r_max, D_MODEL), BF16),
        compiler_params=_cparams(("arbitrary",), manual_dma=True),
        name="moe_gather",
    )(tabs["src"], tabs["cnt"], tabs["dst"], tabs["tot"], pos, h2)


def _ffn_kernel(first_ref, ntile_ref, tstart_ref, trows_ref, total_ref, xs_ref, wg_ref, wu_ref, wd_ref, ys_ref,
                xbuf, ybuf, wgb, wub, wdb, sem_in, sem_out):
    e = pl.program_id(0)
    total = total_ref[0]

    def in_copy(g):
        s = g % FFN_IN_SLOTS
        n = _aligned(trows_ref[g])
        return pltpu.make_async_copy(xs_ref.at[pl.ds(_aligned(tstart_ref[g]), n)],
                                     xbuf.at[s, pl.ds(0, n)], sem_in.at[s])

    def out_copy(g):
        s = g % 2
        n = _aligned(trows_ref[g])
        return pltpu.make_async_copy(ybuf.at[s, pl.ds(0, n)],
                                     ys_ref.at[pl.ds(_aligned(tstart_ref[g]), n)], sem_out.at[s])

    @pl.when(e == 0)
    def _():
        xbuf[...] = jnp.zeros(xbuf.shape, xbuf.dtype)
        for g in range(FFN_LOOKAHEAD):
            @pl.when(g < total)
            def _():
                in_copy(g).start()

    wgb[...] = wg_ref[...].astype(BF16)
    wub[...] = wu_ref[...].astype(BF16)
    wdb[...] = wd_ref[...].astype(BF16)

    def body(g, carry):
        @pl.when(g + FFN_LOOKAHEAD < total)
        def _():
            in_copy(g + FFN_LOOKAHEAD).start()
        in_copy(g).wait()

        @pl.when(g >= 2)
        def _():
            out_copy(g - 2).wait()
        x = xbuf[g % FFN_IN_SLOTS]
        a = jnp.dot(x, wgb[...], preferred_element_type=F32)
        u = jnp.dot(x, wub[...], preferred_element_type=F32)
        h = (a * jax.nn.sigmoid(a) * u).astype(BF16)
        ybuf[g % 2] = jnp.dot(h, wdb[...], preferred_element_type=F32).astype(BF16)
        out_copy(g).start()
        return carry
    lax.fori_loop(first_ref[e], first_ref[e] + ntile_ref[e], body, 0)

    @pl.when(e == pl.num_programs(0) - 1)
    def _():
        @pl.when(total >= 2)
        def _():
            out_copy(total - 2).wait()
        out_copy(total - 1).wait()


def _ffn(tabs, xs, wg, wu, wd):
    ew = lambda a: pl.BlockSpec((None,) + a.shape[1:], lambda e, *_: (e, 0, 0))
    return pl.pallas_call(
        _ffn_kernel,
        grid_spec=pltpu.PrefetchScalarGridSpec(
            num_scalar_prefetch=5, grid=(N_EXPERTS,),
            in_specs=[pl.BlockSpec(memory_space=pl.ANY), ew(wg), ew(wu), ew(wd)],
            out_specs=pl.BlockSpec(memory_space=pl.ANY),
            scratch_shapes=[pltpu.VMEM((FFN_IN_SLOTS, FFN_BLOCK, D_MODEL), BF16),
                            pltpu.VMEM((2, FFN_BLOCK, D_MODEL), BF16),
                            pltpu.VMEM(wg.shape[1:], BF16), pltpu.VMEM(wu.shape[1:], BF16),
                            pltpu.VMEM(wd.shape[1:], BF16),
                            pltpu.SemaphoreType.DMA((FFN_IN_SLOTS,)), pltpu.SemaphoreType.DMA((2,))]),
        out_shape=jax.ShapeDtypeStruct(xs.shape, BF16),
        compiler_params=_cparams(("arbitrary",), manual_dma=True),
        name="moe_ffn",
    )(tabs["first_tile"], tabs["n_tiles"], tabs["tile_start"], tabs["tile_rows"], tabs["total_tiles"],
      xs, wg, wu, wd)


def _combine_kernel(src_ref, cnt_ref, dst_ref, tot_ref,
                    pos_ref, gs_ref, x1_ref, h_ref, sg_ref, su_ref, sd_ref, g2_ref, nf_ref, ys_ref,
                    yc_ref, yl_ref, ybuf, acc_ref, sem, *, tc, n_chunks, ctx_chunks):
    c = pl.program_id(0)
    slot = c % 2

    def issue(ci, s):
        def e_body(e8, carry):
            for j in range(DMA_UNROLL):
                i = ci * N_EXPERTS + e8 * DMA_UNROLL + j
                n = _aligned(cnt_ref[i])
                pltpu.make_async_copy(ys_ref.at[pl.ds(_aligned(dst_ref[i]), n)],
                                      ybuf.at[s, pl.ds(_aligned(src_ref[i]), n)], sem.at[s]).start()
            return carry
        lax.fori_loop(0, N_EXPERTS // DMA_UNROLL, e_body, 0)

    @pl.when(c == 0)
    def _():
        ybuf[...] = jnp.zeros(ybuf.shape, ybuf.dtype)
        issue(0, 0)

    @pl.when(c + 1 < n_chunks)
    def _():
        issue(c + 1, 1 - slot)

    h = h_ref[...]
    a = jnp.dot(h, sg_ref[...], preferred_element_type=F32)
    u = jnp.dot(h, su_ref[...], preferred_element_type=F32)
    acc_ref[...] = jnp.dot((a * jax.nn.sigmoid(a) * u).astype(BF16), sd_ref[...], preferred_element_type=F32)

    n = _aligned(tot_ref[c])
    pltpu.make_async_copy(ys_ref.at[pl.ds(0, n)], ybuf.at[slot, pl.ds(0, n)], sem.at[slot]).wait()

    tn = (((0,), (0,)), ((), ()))
    rows = _block_rows(tc)
    gates = gs_ref[...].astype(BF16)

    def gate_block(b):
        rel = _block_relative(pos_ref[...], _block_start(b))
        wt = jnp.zeros((PERM_BLOCK, tc), BF16)
        for k in range(TOP_K):
            wt = jnp.where(rel[k:k + 1, :] == rows, gates[k:k + 1, :], wt)
        return wt

    sure_rows = _sure_blocks(tc) * PERM_BLOCK
    wt_sure = jnp.concatenate([gate_block(b) for b in range(_sure_blocks(tc))], axis=0)
    acc_ref[...] += lax.dot_general(wt_sure, ybuf[slot, pl.ds(0, sure_rows), :], tn, preferred_element_type=F32)

    def tail_block(b, carry):
        acc_ref[...] += lax.dot_general(gate_block(b), ybuf[slot, pl.ds(_block_start(b), PERM_BLOCK), :], tn,
                                        preferred_element_type=F32)
        return carry
    lax.fori_loop(_sure_blocks(tc), _perm_blocks(tot_ref[c]), tail_block, 0)

    x2 = x1_ref[...] + g2_ref[...] * acc_ref[...]
    y = _rms(x2, nf_ref[...])

    @pl.when(c < ctx_chunks)
    def _():
        yc_ref[...] = y

    @pl.when(c >= ctx_chunks)
    def _():
        yl_ref[...] = y


def _combine(tabs, pos, gslot, x1, h2, sg, su, sd, mod, norm_f, ys, tc, row_fn, n_ctx):
    n = x1.shape[0]
    n_chunks = n // tc
    ctx_chunks = n_ctx // tc
    tok = lambda w: pl.BlockSpec((tc, w), lambda c, *_: (c, 0))
    slot = pl.BlockSpec((TOP_K, tc), lambda c, *_: (0, c))
    full = lambda a: pl.BlockSpec(a.shape, lambda c, *_: (0,) * a.ndim)
    return pl.pallas_call(
        functools.partial(_combine_kernel, tc=tc, n_chunks=n_chunks, ctx_chunks=ctx_chunks),
        grid_spec=pltpu.PrefetchScalarGridSpec(
            num_scalar_prefetch=4, grid=(n_chunks,),
            in_specs=[slot, slot, tok(D_MODEL), tok(D_MODEL), full(sg), full(su), full(sd),
                      _mod_spec(5, lambda c, *_: row_fn(c)), full(norm_f), pl.BlockSpec(memory_space=pl.ANY)],
            out_specs=[pl.BlockSpec((tc, D_MODEL), lambda c, *_: (jnp.minimum(c, ctx_chunks - 1), 0)),
                       pl.BlockSpec((tc, D_MODEL), lambda c, *_: (jnp.maximum(c - ctx_chunks, 0), 0))],
            scratch_shapes=[pltpu.VMEM((2, _local_cap(tc), D_MODEL), BF16),
                            pltpu.VMEM((tc, D_MODEL), F32),
                            pltpu.SemaphoreType.DMA((2,))]),
        out_shape=[jax.ShapeDtypeStruct((n_ctx, D_MODEL), F32), jax.ShapeDtypeStruct((n - n_ctx, D_MODEL), F32)],
        compiler_params=_cparams(("arbitrary",), manual_dma=True),
        name="moe_combine",
    )(tabs["src"], tabs["cnt"], tabs["dst"], tabs["tot"], pos, gslot, x1, h2, sg, su, sd, mod, norm_f, ys)


def _dispatch_tables(pad_cnt, r_max):
    i32 = jnp.int32
    src = jnp.cumsum(pad_cnt, axis=1) - pad_cnt
    tot = jnp.sum(pad_cnt, axis=1)
    e_rows = jnp.sum(pad_cnt, axis=0)
    e_start = jnp.cumsum(e_rows) - e_rows
    dst = e_start[None, :] + jnp.cumsum(pad_cnt, axis=0) - pad_cnt
    n_tiles = (e_rows + FFN_BLOCK - 1) // FFN_BLOCK
    tile_end = jnp.cumsum(n_tiles)
    first_tile = tile_end - n_tiles
    g = jnp.arange(r_max // FFN_BLOCK + N_EXPERTS)
    before = tile_end[None, :] <= g[:, None]
    owner = jnp.concatenate([jnp.ones_like(before[:, :1]), before[:, :-1]], axis=1) & ~before
    offset = (g - jnp.sum(jnp.where(before, n_tiles, 0), axis=1)) * FFN_BLOCK
    tile_start = jnp.clip(jnp.sum(jnp.where(before, e_rows, 0), axis=1) + offset, 0, r_max - FFN_BLOCK)
    tile_rows = jnp.clip(jnp.sum(jnp.where(owner, e_rows, 0), axis=1) - offset, RUN_ALIGN, FFN_BLOCK)
    return {"src": src.reshape(-1).astype(i32), "cnt": pad_cnt.reshape(-1).astype(i32),
            "dst": dst.reshape(-1).astype(i32), "tot": tot.astype(i32),
            "first_tile": first_tile.astype(i32), "n_tiles": n_tiles.astype(i32),
            "tile_start": tile_start.astype(i32), "tile_rows": tile_rows.astype(i32),
            "total_tiles": tile_end[-1:].astype(i32)}


def _moe(x1, h2, logits_t, bias_col, wg, wu, wd, sg, su, sd, mod, norm_f, row_fn, n_ctx):
    n = x1.shape[0]
    tc = MOE_CHUNK
    pos, gslot, pad_cnt = _route(logits_t, bias_col, tc)
    r_max = (n // tc) * _chunk_rows_max(tc)
    tabs = _dispatch_tables(pad_cnt[:, :, 0], r_max)
    xs = _gather(tabs, pos, h2, r_max, tc)
    ys = _ffn(tabs, xs, wg, wu, wd)
    return _combine(tabs, pos, gslot, x1, h2, sg, su, sd, mod, norm_f, ys, tc, row_fn, n_ctx)


def _channel_dft():
    ang = 2.0 * np.pi * ((np.arange(F_HD)[:, None] * np.arange(F_HD)[None, :]) % F_HD) / F_HD
    eye = np.eye(F_GROUPS)
    return np.concatenate([np.kron(eye, np.cos(ang)), np.kron(eye, -np.sin(ang))], axis=1).astype(np.float32)


def _rope_tables(t):
    pos = np.arange(t)
    row, col = pos // GRID_W, pos % GRID_W
    n_freq = HEAD_DIM // 4
    inv = ROPE_THETA ** (-np.arange(n_freq, dtype=np.float64) / n_freq)
    lane = np.arange(LANES)
    hd = lane % HEAD_DIM
    within = hd % (HEAD_DIM // 2)
    freq = within % n_freq
    first = within < n_freq
    p = np.where((hd < HEAD_DIM // 2)[None, :], row[:, None], col[:, None]).astype(np.float64)
    ang = p * inv[freq][None, :]
    cos, sin = np.cos(ang), np.sin(ang)
    sin_up = np.where(first[None, :], -sin, 0.0)
    sin_dn = np.where(first[None, :], 0.0, sin)
    return [jnp.asarray(a.astype(np.float32)) for a in (cos, sin_up, sin_dn)]


def _block_diag_pairs(wf):
    per = LANES // F_HD
    z = jnp.zeros((F_HD, F_HD), wf.dtype)
    blocks = []
    for c in range(F_GROUPS // per):
        rows = [jnp.concatenate([wf[c * per + r] if r == cc else z for cc in range(per)], axis=1) for r in range(per)]
        blocks.append(jnp.concatenate(rows, axis=0))
    return jnp.stack(blocks)


def kernel(x_prompt, x_sample, cache_k, cache_v, c, c_ctx, w_mod, b_mod, norm1, w_in, w_fourier, sink,
           w_out, norm2, w_router, router_bias, w_gate, w_up, w_down, ws_gate, ws_up, ws_down, norm_f):
    nb_ctx, t_ctx, _ = x_prompt.shape
    nb_lat, t_lat, _ = x_sample.shape
    l = 0
    cvec = jnp.concatenate([c_ctx[None, :], c, jnp.zeros((MOD_ROWS - 1 - nb_lat, D_MODEL), F32)], axis=0)
    mod = _modulation(cvec, w_mod[l], b_mod[l]).reshape(MOD_ROWS, 6, 1, D_MODEL)

    w_in_b = w_in[l].astype(BF16)
    dft_c = jnp.asarray(_channel_dft()).astype(BF16)
    wf_bd = _block_diag_pairs(w_fourier[l]).astype(BF16)
    wo_f = w_out[l][:F_DIM].astype(BF16)
    wo_a = w_out[l][F_DIM:].astype(BF16)
    w_router_t = w_router[l].T.astype(BF16)
    bias_col = router_bias[l].reshape(N_EXPERTS, 1)
    n1 = norm1[l].reshape(1, D_MODEL)
    n2 = norm2[l].reshape(1, D_MODEL)
    nf = norm_f.reshape(1, D_MODEL)
    sink_col = sink[l]
    wg, wu, wd = w_gate[l], w_up[l], w_down[l]
    sg, su, sd = ws_gate[l].astype(BF16), ws_up[l].astype(BF16), ws_down[l].astype(BF16)

    def mixers(x, batch, t, latent, kx, vx, t1, t2, tm):
        n = batch * t
        x2d = x.reshape(n, D_MODEL)
        tiles = t // tm
        row_fn = (lambda i: 1 + i // tiles) if latent else (lambda i: 0)
        rope_tabs = _rope_tables(t) if latent else None
        outs = _project(x2d, mod, n1, w_in_b, dft_c, rope_tabs, tm, tiles, row_fn)
        wr, wi, q, k, v = outs[:5]
        mixf = _fourier(wr, wi, wf_bd, batch, t, t1, t2)
        q3, k3, v3 = (a.reshape(batch, t, -1) for a in (q, k, v))
        if latent:
            mixa = _attention(q3, k3, v3, kx, vx, sink_col, batch, t, True)
        else:
            mixa = _attention(q3, None, None, k3, v3, sink_col, batch, t, False)
        return (x2d, mixf.reshape(n, F_DIM), mixa.reshape(n, ATTN_DIM)), outs[5:]

    ctx, (kf, vf) = mixers(x_prompt, nb_ctx, t_ctx, False, None, None, 16, 16, 256)
    kx = cache_k[:, l].reshape(nb_lat, -1, KV_DIM).astype(BF16)
    vx = cache_v[:, l].reshape(nb_lat, -1, KV_DIM).astype(BF16)
    lat, _ = mixers(x_sample, nb_lat, t_lat, True, kx, vx, 64, 64, 512)

    n_ctx = nb_ctx * t_ctx

    def tile_row(tile):
        ctx_tiles, lat_tiles = n_ctx // tile, t_lat // tile
        return lambda i: jnp.where(i < ctx_tiles, 0, 1 + (i - ctx_tiles) // lat_tiles)

    x1, h2, logits_t = _outproj(ctx, lat, wo_f, wo_a, mod, n2, w_router_t, OUTPROJ_TILE, tile_row(OUTPROJ_TILE))
    y_prompt, y_sample = _moe(x1, h2, logits_t, bias_col, wg, wu, wd, sg, su, sd, mod, nf, tile_row(MOE_CHUNK),
                              n_ctx)
    new_k = kf.reshape(nb_ctx, 1, t_ctx, N_KV_HEADS, HEAD_DIM)
    new_v = vf.reshape(nb_ctx, 1, t_ctx, N_KV_HEADS, HEAD_DIM)
    return (y_prompt.reshape(x_prompt.shape), y_sample.reshape(x_sample.shape), new_k, new_v)
```

```python
import functools

import numpy as np
import jax
import jax.numpy as jnp
from jax import lax
from jax.experimental import pallas as pl
from jax.experimental.pallas import tpu as pltpu

F32 = jnp.float32
BF16 = jnp.bfloat16

D_MODEL = 1024
GRID_W = 64
HEAD_DIM = 64
N_HEADS = 12
N_KV_HEADS = 4
GQA_GROUP = N_HEADS // N_KV_HEADS
ATTN_DIM = N_HEADS * HEAD_DIM
KV_DIM = N_KV_HEADS * HEAD_DIM
F_GROUPS = 4
F_HD = 64
F_DIM = F_GROUPS * F_HD
IN_DIM = F_DIM + ATTN_DIM + 2 * KV_DIM
WINDOW = 128
Q_BLOCK = 128
ROPE_THETA = 10000.0
N_EXPERTS = 64
TOP_K = 8
N_EXPERT_GROUPS = 8
TOPK_GROUPS = 4
EXPERT_DIM = 256
SHARED_DIM = 256
ROUTED_SCALE = 2.5
EPS = 1e-6

LANES = 128
MOD_ROWS = 8
RUN_ALIGN = 16
PERM_BLOCK = 256
FFN_BLOCK = 512
FFN_LOOKAHEAD = 3
FFN_IN_SLOTS = FFN_LOOKAHEAD + 1
MOE_CHUNK = 256
OUTPROJ_TILE = 512
SOFTMAX_ROWS = 32
DMA_UNROLL = 8
STRAIGHT_EXTRA_BLOCKS = 2
VMEM_LIMIT = 56 * 1024 * 1024


def _cparams(sem, manual_dma=False):
    return pltpu.CompilerParams(dimension_semantics=sem, vmem_limit_bytes=VMEM_LIMIT,
                                disable_bounds_checks=manual_dma)


def _bdot(a, b):
    return jnp.dot(a.astype(BF16), b.astype(BF16), preferred_element_type=F32)


def _rms(x, g):
    return x * lax.rsqrt(jnp.mean(x * x, axis=-1, keepdims=True) + EPS) * g


def _mod_kernel(c_ref, w_ref, b_ref, o_ref):
    c = c_ref[...]
    a = c * jax.nn.sigmoid(c)
    o_ref[...] = _bdot(a, w_ref[...]) + b_ref[...]


def _modulation(cvec, w_mod, b_mod):
    n = w_mod.shape[1]
    tn = 1024
    return pl.pallas_call(
        _mod_kernel,
        grid=(n // tn,),
        in_specs=[pl.BlockSpec((MOD_ROWS, D_MODEL), lambda j: (0, 0)),
                  pl.BlockSpec((D_MODEL, tn), lambda j: (0, j)),
                  pl.BlockSpec((1, tn), lambda j: (0, j))],
        out_specs=pl.BlockSpec((MOD_ROWS, tn), lambda j: (0, j)),
        out_shape=jax.ShapeDtypeStruct((MOD_ROWS, n), F32),
        compiler_params=_cparams(("parallel",)),
        name="modulation",
    )(cvec, w_mod, b_mod.reshape(1, n))


def _mod_spec(piece, row_fn):
    return pl.BlockSpec((None, None, 1, D_MODEL), lambda *idx: (row_fn(*idx), piece, 0, 0))


def _rope(x, cos, sin_up, sin_dn):
    outs = []
    for j in range(x.shape[1] // LANES):
        xj = x[:, j * LANES:(j + 1) * LANES]
        up = pltpu.roll(xj, LANES - 16, axis=1)
        dn = pltpu.roll(xj, 16, axis=1)
        outs.append(xj * cos + up * sin_up + dn * sin_dn)
    return jnp.concatenate(outs, axis=1)


def _proj_kernel(*refs, latent):
    if latent:
        (x_ref, sh_ref, sc_ref, g_ref, w_ref, dft_ref, cos_ref, sup_ref, sdn_ref,
         wr_ref, wi_ref, q_ref, k_ref, v_ref) = refs
    else:
        (x_ref, sh_ref, sc_ref, g_ref, w_ref, dft_ref,
         wr_ref, wi_ref, q_ref, k_ref, v_ref, kf_ref, vf_ref) = refs
    x = x_ref[...]
    h = _rms(x, g_ref[...]) * (1.0 + sc_ref[...]) + sh_ref[...]
    p = _bdot(h, w_ref[...])
    u = p[:, :F_DIM]
    q = p[:, F_DIM:F_DIM + ATTN_DIM]
    k = p[:, F_DIM + ATTN_DIM:F_DIM + ATTN_DIM + KV_DIM]
    v = p[:, F_DIM + ATTN_DIM + KV_DIM:]
    w = _bdot(u, dft_ref[...])
    wr_ref[...] = w[:, :F_DIM]
    wi_ref[...] = w[:, F_DIM:]
    if latent:
        cos, sup, sdn = cos_ref[...], sup_ref[...], sdn_ref[...]
        q = _rope(q, cos, sup, sdn)
        k = _rope(k, cos, sup, sdn)
    else:
        kf_ref[...] = k
        vf_ref[...] = v
    q_ref[...] = (q * (HEAD_DIM ** -0.5)).astype(BF16)
    k_ref[...] = k.astype(BF16)
    v_ref[...] = v.astype(BF16)


def _project(x2d, mod, norm1, w_in, dft_c, rope_tabs, tm, tiles_per_batch, row_fn):
    n = x2d.shape[0]
    latent = rope_tabs is not None
    tok = lambda w: pl.BlockSpec((tm, w), lambda i: (i, 0))
    full = lambda a: pl.BlockSpec(a.shape, lambda i: (0,) * a.ndim)
    in_specs = [tok(D_MODEL), _mod_spec(0, row_fn), _mod_spec(1, row_fn), full(norm1), full(w_in), full(dft_c)]
    args = [x2d, mod, mod, norm1, w_in, dft_c]
    out_specs = [tok(F_DIM), tok(F_DIM), tok(ATTN_DIM), tok(KV_DIM), tok(KV_DIM)]
    out_shape = [jax.ShapeDtypeStruct((n, F_DIM), F32), jax.ShapeDtypeStruct((n, F_DIM), F32),
                 jax.ShapeDtypeStruct((n, ATTN_DIM), BF16), jax.ShapeDtypeStruct((n, KV_DIM), BF16),
                 jax.ShapeDtypeStruct((n, KV_DIM), BF16)]
    if latent:
        pos = pl.BlockSpec((tm, LANES), lambda i: (i % tiles_per_batch, 0))
        in_specs += [pos, pos, pos]
        args += list(rope_tabs)
    else:
        out_specs += [tok(KV_DIM), tok(KV_DIM)]
        out_shape += [jax.ShapeDtypeStruct((n, KV_DIM), F32), jax.ShapeDtypeStruct((n, KV_DIM), F32)]
    return pl.pallas_call(
        functools.partial(_proj_kernel, latent=latent),
        grid=(n // tm,),
        in_specs=in_specs, out_specs=out_specs, out_shape=out_shape,
        compiler_params=_cparams(("parallel",)),
        name="project_latent" if latent else "project_context",
    )(*args)


def _fourier_kernel(wr_ref, wi_ref, a_ref, b_ref, wf_ref, o_ref, yr_ref, yi_ref, z_ref, *, t1, t2, scale):
    for j in range(t2):
        rows = pl.ds(j, t1, stride=t2)
        xin = jnp.concatenate([wr_ref[rows, :], wi_ref[rows, :]], axis=0)
        y = _bdot(a_ref[j], xin)
        yr_ref[rows, :] = y[:t1]
        yi_ref[rows, :] = y[t1:]
    bm = b_ref[...]
    for k1 in range(t1):
        rows = pl.ds(k1 * t2, t2)
        yin = jnp.concatenate([yr_ref[rows, :], yi_ref[rows, :]], axis=0)
        z_ref[pl.ds(k1, t2, stride=t1), :] = _bdot(bm, yin)
    o_ref[...] = (_bdot(z_ref[...], wf_ref[...]) * scale).astype(BF16)


def _dft_tables(t1, t2):
    t = t1 * t2
    k1 = np.arange(t1)[None, :, None]
    pos = (t2 * np.arange(t1)[None, None, :] + np.arange(t2)[:, None, None])
    ang = 2.0 * np.pi * ((k1 * pos) % t) / t
    c, s = np.cos(ang), np.sin(ang)
    a = np.concatenate([np.concatenate([c, s], axis=2), np.concatenate([-s, c], axis=2)], axis=1)
    ang2 = 2.0 * np.pi * ((np.arange(t2)[:, None] * np.arange(t2)[None, :]) % t2) / t2
    b = np.concatenate([np.cos(ang2), np.sin(ang2)], axis=1)
    return a.astype(np.float32), b.astype(np.float32)


def _fourier(wr, wi, wf_bd, batch, t, t1, t2):
    a_np, b_np = _dft_tables(t1, t2)
    a = jnp.asarray(a_np).astype(BF16)
    b = jnp.asarray(b_np).astype(BF16)
    cw = LANES
    blk = pl.BlockSpec((None, t, cw), lambda bi, ci: (bi, 0, ci))
    return pl.pallas_call(
        functools.partial(_fourier_kernel, t1=t1, t2=t2, scale=float((t * F_HD) ** -0.5)),
        grid=(batch, F_DIM // cw),
        in_specs=[blk, blk,
                  pl.BlockSpec(a.shape, lambda bi, ci: (0, 0, 0)),
                  pl.BlockSpec(b.shape, lambda bi, ci: (0, 0)),
                  pl.BlockSpec((None, cw, cw), lambda bi, ci: (ci, 0, 0))],
        out_specs=blk,
        out_shape=jax.ShapeDtypeStruct((batch, t, F_DIM), BF16),
        scratch_shapes=[pltpu.VMEM((t, cw), F32), pltpu.VMEM((t, cw), F32), pltpu.VMEM((t, cw), F32)],
        compiler_params=_cparams(("parallel", "parallel")),
        name="fourier_%d" % t,
    )(wr.reshape(batch, t, F_DIM), wi.reshape(batch, t, F_DIM), a, b, wf_bd)


def _attn_kernel(*refs, windowed, n_blocks):
    if windowed:
        q_ref, kp_ref, kc_ref, kn_ref, vp_ref, vc_ref, vn_ref, kx_ref, vx_ref, sink_ref, o_ref, s_ref, p_ref = refs
    else:
        q_ref, kx_ref, vx_ref, sink_ref, o_ref, s_ref, p_ref = refs
    nt = (((1,), (1,)), ((), ()))
    parts = [(kx_ref, vx_ref, None)]
    if windowed:
        i = pl.program_id(1)
        a = lax.broadcasted_iota(jnp.int32, (Q_BLOCK, Q_BLOCK), 0)
        j = lax.broadcasted_iota(jnp.int32, (Q_BLOCK, Q_BLOCK), 1)
        prev_ok = (j >= a) & (i > 0)
        next_ok = (j <= a) & (i < n_blocks - 1)
        parts = [(kp_ref, vp_ref, prev_ok), (kc_ref, vc_ref, None), (kn_ref, vn_ref, next_ok)] + parts
    n_keys = sum(k_ref.shape[0] for k_ref, _, _ in parts)
    outs = []
    for kv in range(N_KV_HEADS):
        heads = range(kv * GQA_GROUP, (kv + 1) * GQA_GROUP)
        hs = slice(kv * HEAD_DIM, (kv + 1) * HEAD_DIM)
        qs = jnp.concatenate([q_ref[:, h * HEAD_DIM:(h + 1) * HEAD_DIM] for h in heads], axis=0)
        c0 = 0
        for k_ref, _, ok in parts:
            s = lax.dot_general(qs, k_ref[:, hs], nt, preferred_element_type=F32)
            if ok is not None:
                s = jnp.where(jnp.concatenate([ok] * GQA_GROUP, axis=0), s, -jnp.inf)
            s_ref[:, c0:c0 + s.shape[1]] = s
            c0 += s.shape[1]
        invs = []
        for r0 in range(0, GQA_GROUP * Q_BLOCK, SOFTMAX_ROWS):
            s = s_ref[r0:r0 + SOFTMAX_ROWS, :]
            sk = sink_ref[kv * GQA_GROUP + r0 // Q_BLOCK]
            slabs = [s[:, c:c + LANES] for c in range(0, n_keys, LANES)]
            m = jnp.maximum(jnp.max(functools.reduce(jnp.maximum, slabs), axis=1, keepdims=True), sk)
            p = jnp.exp(s - m)
            pslabs = [p[:, c:c + LANES] for c in range(0, n_keys, LANES)]
            den = jnp.sum(functools.reduce(jnp.add, pslabs), axis=1, keepdims=True) + jnp.exp(sk - m)
            p_ref[r0:r0 + SOFTMAX_ROWS, :] = p.astype(BF16)
            invs.append(1.0 / den)
        o = None
        c0 = 0
        for k_ref, v_ref, _ in parts:
            nk = k_ref.shape[0]
            d = jnp.dot(p_ref[:, c0:c0 + nk], v_ref[:, hs], preferred_element_type=F32)
            o = d if o is None else o + d
            c0 += nk
        o = o * jnp.concatenate(invs, axis=0)
        outs += [o[g * Q_BLOCK:(g + 1) * Q_BLOCK] for g in range(GQA_GROUP)]
    o_ref[...] = jnp.concatenate(outs, axis=1).astype(BF16)


def _attention(q, k, v, kx, vx, sink_col, batch, t, windowed):
    nb = t // Q_BLOCK
    n_ctx = kx.shape[1]
    n_keys = n_ctx + (3 * Q_BLOCK if windowed else 0)
    qspec = pl.BlockSpec((None, Q_BLOCK, ATTN_DIM), lambda b, i: (b, i, 0))
    xspec = pl.BlockSpec((None, n_ctx, KV_DIM), lambda b, i: (b, 0, 0))
    sspec = pl.BlockSpec(memory_space=pltpu.SMEM)
    if windowed:
        prev = pl.BlockSpec((None, Q_BLOCK, KV_DIM), lambda b, i: (b, jnp.maximum(i - 1, 0), 0))
        cur = pl.BlockSpec((None, Q_BLOCK, KV_DIM), lambda b, i: (b, i, 0))
        nxt = pl.BlockSpec((None, Q_BLOCK, KV_DIM), lambda b, i: (b, jnp.minimum(i + 1, nb - 1), 0))
        in_specs = [qspec, prev, cur, nxt, prev, cur, nxt, xspec, xspec, sspec]
        args = [q, k, k, k, v, v, v, kx, vx, sink_col]
    else:
        in_specs = [qspec, xspec, xspec, sspec]
        args = [q, kx, vx, sink_col]
    return pl.pallas_call(
        functools.partial(_attn_kernel, windowed=windowed, n_blocks=nb),
        grid=(batch, nb),
        in_specs=in_specs, out_specs=qspec,
        out_shape=jax.ShapeDtypeStruct((batch, t, ATTN_DIM), BF16),
        scratch_shapes=[pltpu.VMEM((GQA_GROUP * Q_BLOCK, n_keys), F32),
                        pltpu.VMEM((GQA_GROUP * Q_BLOCK, n_keys), BF16)],
        compiler_params=_cparams(("parallel", "parallel")),
        name="attention_latent" if windowed else "attention_context",
    )(*args)


def _outproj_kernel(xc_ref, mfc_ref, mac_ref, xl_ref, mfl_ref, mal_ref,
                    wof_ref, woa_ref, g1_ref, sh_ref, sc_ref, n2_ref, wr_ref,
                    x1_ref, h_ref, lg_ref, *, ctx_tiles):
    def body(x_ref, mf_ref, ma_ref):
        o = jnp.dot(mf_ref[...], wof_ref[...], preferred_element_type=F32)
        o = o + jnp.dot(ma_ref[...], woa_ref[...], preferred_element_type=F32)
        x1 = x_ref[...] + g1_ref[...] * o
        x1_ref[...] = x1
        h = (_rms(x1, n2_ref[...]) * (1.0 + sc_ref[...]) + sh_ref[...]).astype(BF16)
        h_ref[...] = h
        lg_ref[...] = lax.dot_general(wr_ref[...], h, (((1,), (1,)), ((), ())), preferred_element_type=F32)

    is_ctx = pl.program_id(0) < ctx_tiles
    pl.when(is_ctx)(lambda: body(xc_ref, mfc_ref, mac_ref))
    pl.when(jnp.logical_not(is_ctx))(lambda: body(xl_ref, mfl_ref, mal_ref))


def _outproj(ctx, lat, wo_f, wo_a, mod, norm2, w_router_t, tm, row_fn):
    n_c, n_l = ctx[0].shape[0], lat[0].shape[0]
    ctx_tiles = n_c // tm
    n = n_c + n_l
    ctok = lambda w: pl.BlockSpec((tm, w), lambda i: (jnp.minimum(i, ctx_tiles - 1), 0))
    ltok = lambda w: pl.BlockSpec((tm, w), lambda i: (jnp.maximum(i - ctx_tiles, 0), 0))
    tok = lambda w: pl.BlockSpec((tm, w), lambda i: (i, 0))
    full = lambda a: pl.BlockSpec(a.shape, lambda i: (0,) * a.ndim)
    widths = (D_MODEL, F_DIM, ATTN_DIM)
    return pl.pallas_call(
        functools.partial(_outproj_kernel, ctx_tiles=ctx_tiles),
        grid=(n // tm,),
        in_specs=[ctok(w) for w in widths] + [ltok(w) for w in widths] + [
            full(wo_f), full(wo_a), _mod_spec(2, row_fn), _mod_spec(3, row_fn), _mod_spec(4, row_fn),
            full(norm2), full(w_router_t)],
        out_specs=[tok(D_MODEL), tok(D_MODEL), pl.BlockSpec((N_EXPERTS, tm), lambda i: (0, i))],
        out_shape=[jax.ShapeDtypeStruct((n, D_MODEL), F32), jax.ShapeDtypeStruct((n, D_MODEL), BF16),
                   jax.ShapeDtypeStruct((N_EXPERTS, n), F32)],
        compiler_params=_cparams(("parallel",)),
        name="outproj",
    )(*ctx, *lat, wo_f, wo_a, mod, mod, mod, norm2, w_router_t)


def _top_rows(vals, k):
    n = vals.shape[0]
    idx = lax.broadcasted_iota(jnp.int32, vals.shape, 0)
    picked = jnp.zeros(vals.shape, F32)
    for _ in range(k):
        best = jnp.max(vals, axis=0, keepdims=True)
        first = jnp.min(jnp.where(vals == best, idx, n), axis=0, keepdims=True)
        hit = idx == first
        picked = jnp.where(hit, 1.0, picked)
        vals = jnp.where(hit, -jnp.inf, vals)
    return picked


def _route_kernel(lg_ref, bias_ref, lo_ref, up_ref, pos_ref, gs_ref, cnt_ref):
    s = jax.nn.sigmoid(lg_ref[...])
    sc = s + bias_ref[...]
    tn = s.shape[1]
    per = N_EXPERTS // N_EXPERT_GROUPS
    g3 = sc.reshape(N_EXPERT_GROUPS, per, tn)
    member = lax.broadcasted_iota(jnp.int32, g3.shape, 1)
    m1 = jnp.max(g3, axis=1, keepdims=True)
    first = jnp.min(jnp.where(g3 == m1, member, per), axis=1, keepdims=True)
    m2 = jnp.max(jnp.where(member == first, -jnp.inf, g3), axis=1, keepdims=True)
    gscore = (m1 + m2).reshape(N_EXPERT_GROUPS, tn)
    gsel = _top_rows(gscore, TOPK_GROUPS)
    emask = jnp.broadcast_to(gsel.reshape(N_EXPERT_GROUPS, 1, tn), g3.shape).reshape(N_EXPERTS, tn)
    masked = jnp.where(emask > 0.5, sc, -jnp.inf)
    self = _top_rows(masked, TOP_K)
    sel = self > 0.5
    w = jnp.where(sel, s, 0.0)
    gate = w / jnp.sum(w, axis=0, keepdims=True) * ROUTED_SCALE

    selb = self.astype(BF16)
    cnt = jnp.sum(self, axis=1, keepdims=True)
    pad = jnp.maximum(jnp.floor((cnt + (RUN_ALIGN - 1)) * (1.0 / RUN_ALIGN)), 1.0) * RUN_ALIGN
    lo = lo_ref[...]
    soff = jnp.dot(lo, jnp.broadcast_to(pad, (N_EXPERTS, LANES)).astype(BF16), preferred_element_type=F32)[:, :1]
    rank = jnp.dot(selb, up_ref[...], preferred_element_type=F32)
    kidx = jnp.dot(lo, selb, preferred_element_type=F32)
    pos_e = jnp.where(sel, soff + rank, 0.0)
    rows_p, rows_g = [], []
    for k in range(TOP_K):
        m = kidx == k
        rows_p.append(jnp.sum(jnp.where(m, pos_e, 0.0), axis=0, keepdims=True))
        rows_g.append(jnp.sum(jnp.where(m, gate, 0.0), axis=0, keepdims=True))
    pos_ref[...] = jnp.concatenate(rows_p, axis=0).astype(jnp.int32)
    gs_ref[...] = jnp.concatenate(rows_g, axis=0)
    cnt_ref[...] = pad.astype(jnp.int32)


def _route(logits_t, bias_col, tc):
    n = logits_t.shape[1]
    lo = jnp.asarray(np.tril(np.ones((N_EXPERTS, N_EXPERTS), np.float32), -1)).astype(BF16)
    up = jnp.asarray(np.triu(np.ones((tc, tc), np.float32), 1)).astype(BF16)
    slot = pl.BlockSpec((TOP_K, tc), lambda i: (0, i))
    return pl.pallas_call(
        _route_kernel,
        grid=(n // tc,),
        in_specs=[pl.BlockSpec((N_EXPERTS, tc), lambda i: (0, i)),
                  pl.BlockSpec((N_EXPERTS, 1), lambda i: (0, 0)),
                  pl.BlockSpec(lo.shape, lambda i: (0, 0)),
                  pl.BlockSpec(up.shape, lambda i: (0, 0))],
        out_specs=[slot, slot, pl.BlockSpec((None, N_EXPERTS, 1), lambda i: (i, 0, 0))],
        out_shape=[jax.ShapeDtypeStruct((TOP_K, n), jnp.int32), jax.ShapeDtypeStruct((TOP_K, n), F32),
                   jax.ShapeDtypeStruct((n // tc, N_EXPERTS, 1), jnp.int32)],
        compiler_params=_cparams(("parallel",)),
        name="route",
    )(logits_t, bias_col, lo, up)


def _aligned(x):
    return pl.multiple_of(x, RUN_ALIGN)


def _block_rows(tc):
    return lax.broadcasted_iota(jnp.int32, (PERM_BLOCK, tc), 0).astype(F32).astype(BF16)


def _block_relative(pos, r0):
    return (pos - r0).astype(F32).astype(BF16)


def _gather_kernel(src_ref, cnt_ref, dst_ref, tot_ref,
                   pos_ref, x_ref, xs_ref, z_ref, sem, *, tc, n_chunks):
    c = pl.program_id(0)
    slot = c % 2

    def wait_chunk(ci, s):
        n = _aligned(tot_ref[ci])
        pltpu.make_async_copy(z_ref.at[s, pl.ds(0, n)], xs_ref.at[pl.ds(0, n)], sem.at[s]).wait()

    def start_runs(ci, s, first, count):
        for j in range(count):
            i = ci * N_EXPERTS + first + j
            n = _aligned(cnt_ref[i])
            pltpu.make_async_copy(z_ref.at[s, pl.ds(_aligned(src_ref[i]), n)],
                                  xs_ref.at[pl.ds(_aligned(dst_ref[i]), n)], sem.at[s]).start()

    @pl.when(c >= 2)
    def _():
        wait_chunk(c - 2, slot)

    rows = _block_rows(tc)

    def permute_block(b):
        r0 = _block_start(b)
        rel = _block_relative(pos_ref[...], r0)
        onehot = jnp.zeros((PERM_BLOCK, tc), BF16)
        for k in range(TOP_K):
            onehot = jnp.where(rel[k:k + 1, :] == rows, jnp.ones((), BF16), onehot)
        z_ref[slot, pl.ds(r0, PERM_BLOCK), :] = jnp.dot(
            onehot, x_ref[...], preferred_element_type=F32).astype(BF16)

    for b in range(_sure_blocks(tc)):
        permute_block(b)

    def tail_block(b, carry):
        permute_block(b)
        return carry
    lax.fori_loop(_sure_blocks(tc), _perm_blocks(tot_ref[c]), tail_block, 0)

    def e_body(e8, carry):
        start_runs(c, slot, e8 * DMA_UNROLL, DMA_UNROLL)
        return carry
    lax.fori_loop(0, N_EXPERTS // DMA_UNROLL, e_body, 0)

    @pl.when(c == n_chunks - 1)
    def _():
        wait_chunk(c, slot)
        if n_chunks > 1:
            wait_chunk(c - 1, 1 - slot)


def _perm_blocks(rows):
    return (rows + PERM_BLOCK - 1) // PERM_BLOCK


def _sure_blocks(tc):
    return min(TOP_K * tc // PERM_BLOCK + STRAIGHT_EXTRA_BLOCKS, _local_cap(tc) // PERM_BLOCK)


def _block_start(b):
    return b * PERM_BLOCK if isinstance(b, int) else pl.multiple_of(b * PERM_BLOCK, PERM_BLOCK)


def _chunk_rows_max(tc):
    return TOP_K * tc + N_EXPERTS * RUN_ALIGN


def _local_cap(tc):
    return -(-_chunk_rows_max(tc) // PERM_BLOCK) * PERM_BLOCK


def _gather(tabs, pos, h2, r_max, tc):
    n = h2.shape[0]
    n_chunks = n // tc
    return pl.pallas_call(
        functools.partial(_gather_kernel, tc=tc, n_chunks=n_chunks),
        grid_spec=pltpu.PrefetchScalarGridSpec(
            num_scalar_prefetch=4, grid=(n_chunks,),
            in_specs=[pl.BlockSpec((TOP_K, tc), lambda c, *_: (0, c)),
                      pl.BlockSpec((tc, D_MODEL), lambda c, *_: (c, 0))],
            out_specs=pl.BlockSpec(memory_space=pl.ANY),
            scratch_shapes=[pltpu.VMEM((2, _local_cap(tc), D_MODEL), BF16),
                            pltpu.SemaphoreType.DMA((2,))]),
        out_shape=jax.ShapeDtypeStruct((r_max, D_MODEL), BF16),
        compiler_params=_cparams(("arbitrary",), manual_dma=True),
        name="moe_gather",
    )(tabs["src"], tabs["cnt"], tabs["dst"], tabs["tot"], pos, h2)


def _ffn_kernel(first_ref, ntile_ref, tstart_ref, trows_ref, total_ref, xs_ref, wg_ref, wu_ref, wd_ref, ys_ref,
                xbuf, ybuf, wgb, wub, wdb, sem_in, sem_out):
    e = pl.program_id(0)
    total = total_ref[0]

    def in_copy(g):
        s = g % FFN_IN_SLOTS
        n = _aligned(trows_ref[g])
        return pltpu.make_async_copy(xs_ref.at[pl.ds(_aligned(tstart_ref[g]), n)],
                                     xbuf.at[s, pl.ds(0, n)], sem_in.at[s])

    def out_copy(g):
        s = g % 2
        n = _aligned(trows_ref[g])
        return pltpu.make_async_copy(ybuf.at[s, pl.ds(0, n)],
                                     ys_ref.at[pl.ds(_aligned(tstart_ref[g]), n)], sem_out.at[s])

    @pl.when(e == 0)
    def _():
        xbuf[...] = jnp.zeros(xbuf.shape, xbuf.dtype)
        for g in range(FFN_LOOKAHEAD):
            @pl.when(g < total)
            def _():
                in_copy(g).start()

    wgb[...] = wg_ref[...].astype(BF16)
    wub[...] = wu_ref[...].astype(BF16)
    wdb[...] = wd_ref[...].astype(BF16)

    def body(g, carry):
        @pl.when(g + FFN_LOOKAHEAD < total)
        def _():
            in_copy(g + FFN_LOOKAHEAD).start()
        in_copy(g).wait()

        @pl.when(g >= 2)
        def _():
            out_copy(g - 2).wait()
        x = xbuf[g % FFN_IN_SLOTS]
        a = jnp.dot(x, wgb[...], preferred_element_type=F32)
        u = jnp.dot(x, wub[...], preferred_element_type=F32)
        h = (a * jax.nn.sigmoid(a) * u).astype(BF16)
        ybuf[g % 2] = jnp.dot(h, wdb[...], preferred_element_type=F32).astype(BF16)
        out_copy(g).start()
        return carry
    lax.fori_loop(first_ref[e], first_ref[e] + ntile_ref[e], body, 0)

    @pl.when(e == pl.num_programs(0) - 1)
    def _():
        @pl.when(total >= 2)
        def _():
            out_copy(total - 2).wait()
        out_copy(total - 1).wait()


def _ffn(tabs, xs, wg, wu, wd):
    ew = lambda a: pl.BlockSpec((None,) + a.shape[1:], lambda e, *_: (e, 0, 0))
    return pl.pallas_call(
        _ffn_kernel,
        grid_spec=pltpu.PrefetchScalarGridSpec(
            num_scalar_prefetch=5, grid=(N_EXPERTS,),
            in_specs=[pl.BlockSpec(memory_space=pl.ANY), ew(wg), ew(wu), ew(wd)],
            out_specs=pl.BlockSpec(memory_space=pl.ANY),
            scratch_shapes=[pltpu.VMEM((FFN_IN_SLOTS, FFN_BLOCK, D_MODEL), BF16),
                            pltpu.VMEM((2, FFN_BLOCK, D_MODEL), BF16),
                            pltpu.VMEM(wg.shape[1:], BF16), pltpu.VMEM(wu.shape[1:], BF16),
                            pltpu.VMEM(wd.shape[1:], BF16),
                            pltpu.SemaphoreType.DMA((FFN_IN_SLOTS,)), pltpu.SemaphoreType.DMA((2,))]),
        out_shape=jax.ShapeDtypeStruct(xs.shape, BF16),
        compiler_params=_cparams(("arbitrary",), manual_dma=True),
        name="moe_ffn",
    )(tabs["first_tile"], tabs["n_tiles"], tabs["tile_start"], tabs["tile_rows"], tabs["total_tiles"],
      xs, wg, wu, wd)


def _combine_kernel(src_ref, cnt_ref, dst_ref, tot_ref,
                    pos_ref, gs_ref, x1_ref, h_ref, sg_ref, su_ref, sd_ref, g2_ref, nf_ref, ys_ref,
                    yc_ref, yl_ref, ybuf, acc_ref, sem, *, tc, n_chunks, ctx_chunks):
    c = pl.program_id(0)
    slot = c % 2

    def issue(ci, s):
        def e_body(e8, carry):
            for j in range(DMA_UNROLL):
                i = ci * N_EXPERTS + e8 * DMA_UNROLL + j
                n = _aligned(cnt_ref[i])
                pltpu.make_async_copy(ys_ref.at[pl.ds(_aligned(dst_ref[i]), n)],
                                      ybuf.at[s, pl.ds(_aligned(src_ref[i]), n)], sem.at[s]).start()
            return carry
        lax.fori_loop(0, N_EXPERTS // DMA_UNROLL, e_body, 0)

    @pl.when(c == 0)
    def _():
        ybuf[...] = jnp.zeros(ybuf.shape, ybuf.dtype)
        issue(0, 0)

    @pl.when(c + 1 < n_chunks)
    def _():
        issue(c + 1, 1 - slot)

    h = h_ref[...]
    a = jnp.dot(h, sg_ref[...], preferred_element_type=F32)
    u = jnp.dot(h, su_ref[...], preferred_element_type=F32)
    acc_ref[...] = jnp.dot((a * jax.nn.sigmoid(a) * u).astype(BF16), sd_ref[...], preferred_element_type=F32)

    n = _aligned(tot_ref[c])
    pltpu.make_async_copy(ys_ref.at[pl.ds(0, n)], ybuf.at[slot, pl.ds(0, n)], sem.at[slot]).wait()

    tn = (((0,), (0,)), ((), ()))
    rows = _block_rows(tc)
    gates = gs_ref[...].astype(BF16)

    def gate_block(b):
        rel = _block_relative(pos_ref[...], _block_start(b))
        wt = jnp.zeros((PERM_BLOCK, tc), BF16)
        for k in range(TOP_K):
            wt = jnp.where(rel[k:k + 1, :] == rows, gates[k:k + 1, :], wt)
        return wt

    sure_rows = _sure_blocks(tc) * PERM_BLOCK
    wt_sure = jnp.concatenate([gate_block(b) for b in range(_sure_blocks(tc))], axis=0)
    acc_ref[...] += lax.dot_general(wt_sure, ybuf[slot, pl.ds(0, sure_rows), :], tn, preferred_element_type=F32)

    def tail_block(b, carry):
        acc_ref[...] += lax.dot_general(gate_block(b), ybuf[slot, pl.ds(_block_start(b), PERM_BLOCK), :], tn,
                                        preferred_element_type=F32)
        return carry
    lax.fori_loop(_sure_blocks(tc), _perm_blocks(tot_ref[c]), tail_block, 0)

    x2 = x1_ref[...] + g2_ref[...] * acc_ref[...]
    y = _rms(x2, nf_ref[...])

    @pl.when(c < ctx_chunks)
    def _():
        yc_ref[...] = y

    @pl.when(c >= ctx_chunks)
    def _():
        yl_ref[...] = y


def _combine(tabs, pos, gslot, x1, h2, sg, su, sd, mod, norm_f, ys, tc, row_fn, n_ctx):
    n = x1.shape[0]
    n_chunks = n // tc
    ctx_chunks = n_ctx // tc
    tok = lambda w: pl.BlockSpec((tc, w), lambda c, *_: (c, 0))
    slot = pl.BlockSpec((TOP_K, tc), lambda c, *_: (0, c))
    full = lambda a: pl.BlockSpec(a.shape, lambda c, *_: (0,) * a.ndim)
    return pl.pallas_call(
        functools.partial(_combine_kernel, tc=tc, n_chunks=n_chunks, ctx_chunks=ctx_chunks),
        grid_spec=pltpu.PrefetchScalarGridSpec(
            num_scalar_prefetch=4, grid=(n_chunks,),
            in_specs=[slot, slot, tok(D_MODEL), tok(D_MODEL), full(sg), full(su), full(sd),
                      _mod_spec(5, lambda c, *_: row_fn(c)), full(norm_f), pl.BlockSpec(memory_space=pl.ANY)],
            out_specs=[pl.BlockSpec((tc, D_MODEL), lambda c, *_: (jnp.minimum(c, ctx_chunks - 1), 0)),
                       pl.BlockSpec((tc, D_MODEL), lambda c, *_: (jnp.maximum(c - ctx_chunks, 0), 0))],
            scratch_shapes=[pltpu.VMEM((2, _local_cap(tc), D_MODEL), BF16),
                            pltpu.VMEM((tc, D_MODEL), F32),
                            pltpu.SemaphoreType.DMA((2,))]),
        out_shape=[jax.ShapeDtypeStruct((n_ctx, D_MODEL), F32), jax.ShapeDtypeStruct((n - n_ctx, D_MODEL), F32)],
        compiler_params=_cparams(("arbitrary",), manual_dma=True),
        name="moe_combine",
    )(tabs["src"], tabs["cnt"], tabs["dst"], tabs["tot"], pos, gslot, x1, h2, sg, su, sd, mod, norm_f, ys)


def _dispatch_tables(pad_cnt, r_max):
    i32 = jnp.int32
    src = jnp.cumsum(pad_cnt, axis=1) - pad_cnt
    tot = jnp.sum(pad_cnt, axis=1)
    e_rows = jnp.sum(pad_cnt, axis=0)
    e_start = jnp.cumsum(e_rows) - e_rows
    dst = e_start[None, :] + jnp.cumsum(pad_cnt, axis=0) - pad_cnt
    n_tiles = (e_rows + FFN_BLOCK - 1) // FFN_BLOCK
    tile_end = jnp.cumsum(n_tiles)
    first_tile = tile_end - n_tiles
    g = jnp.arange(r_max // FFN_BLOCK + N_EXPERTS)
    before = tile_end[None, :] <= g[:, None]
    owner = jnp.concatenate([jnp.ones_like(before[:, :1]), before[:, :-1]], axis=1) & ~before
    offset = (g - jnp.sum(jnp.where(before, n_tiles, 0), axis=1)) * FFN_BLOCK
    tile_start = jnp.clip(jnp.sum(jnp.where(before, e_rows, 0), axis=1) + offset, 0, r_max - FFN_BLOCK)
    tile_rows = jnp.clip(jnp.sum(jnp.where(owner, e_rows, 0), axis=1) - offset, RUN_ALIGN, FFN_BLOCK)
    return {"src": src.reshape(-1).astype(i32), "cnt": pad_cnt.reshape(-1).astype(i32),
            "dst": dst.reshape(-1).astype(i32), "tot": tot.astype(i32),
            "first_tile": first_tile.astype(i32), "n_tiles": n_tiles.astype(i32),
            "tile_start": tile_start.astype(i32), "tile_rows": tile_rows.astype(i32),
            "total_tiles": tile_end[-1:].astype(i32)}


def _moe(x1, h2, logits_t, bias_col, wg, wu, wd, sg, su, sd, mod, norm_f, row_fn, n_ctx):
    n = x1.shape[0]
    tc = MOE_CHUNK
    pos, gslot, pad_cnt = _route(logits_t, bias_col, tc)
    r_max = (n // tc) * _chunk_rows_max(tc)
    tabs = _dispatch_tables(pad_cnt[:, :, 0], r_max)
    xs = _gather(tabs, pos, h2, r_max, tc)
    ys = _ffn(tabs, xs, wg, wu, wd)
    return _combine(tabs, pos, gslot, x1, h2, sg, su, sd, mod, norm_f, ys, tc, row_fn, n_ctx)


def _channel_dft():
    ang = 2.0 * np.pi * ((np.arange(F_HD)[:, None] * np.arange(F_HD)[None, :]) % F_HD) / F_HD
    eye = np.eye(F_GROUPS)
    return np.concatenate([np.kron(eye, np.cos(ang)), np.kron(eye, -np.sin(ang))], axis=1).astype(np.float32)


def _rope_tables(t):
    pos = np.arange(t)
    row, col = pos // GRID_W, pos % GRID_W
    n_freq = HEAD_DIM // 4
    inv = ROPE_THETA ** (-np.arange(n_freq, dtype=np.float64) / n_freq)
    lane = np.arange(LANES)
    hd = lane % HEAD_DIM
    within = hd % (HEAD_DIM // 2)
    freq = within % n_freq
    first = within < n_freq
    p = np.where((hd < HEAD_DIM // 2)[None, :], row[:, None], col[:, None]).astype(np.float64)
    ang = p * inv[freq][None, :]
    cos, sin = np.cos(ang), np.sin(ang)
    sin_up = np.where(first[None, :], -sin, 0.0)
    sin_dn = np.where(first[None, :], 0.0, sin)
    return [jnp.asarray(a.astype(np.float32)) for a in (cos, sin_up, sin_dn)]


def _block_diag_pairs(wf):
    per = LANES // F_HD
    z = jnp.zeros((F_HD, F_HD), wf.dtype)
    blocks = []
    for c in range(F_GROUPS // per):
        rows = [jnp.concatenate([wf[c * per + r] if r == cc else z for cc in range(per)], axis=1) for r in range(per)]
        blocks.append(jnp.concatenate(rows, axis=0))
    return jnp.stack(blocks)


def kernel(x_prompt, x_sample, cache_k, cache_v, c, c_ctx, w_mod, b_mod, norm1, w_in, w_fourier, sink,
           w_out, norm2, w_router, router_bias, w_gate, w_up, w_down, ws_gate, ws_up, ws_down, norm_f):
    nb_ctx, t_ctx, _ = x_prompt.shape
    nb_lat, t_lat, _ = x_sample.shape
    l = 0
    cvec = jnp.concatenate([c_ctx[None, :], c, jnp.zeros((MOD_ROWS - 1 - nb_lat, D_MODEL), F32)], axis=0)
    mod = _modulation(cvec, w_mod[l], b_mod[l]).reshape(MOD_ROWS, 6, 1, D_MODEL)

    w_in_b = w_in[l].astype(BF16)
    dft_c = jnp.asarray(_channel_dft()).astype(BF16)
    wf_bd = _block_diag_pairs(w_fourier[l]).astype(BF16)
    wo_f = w_out[l][:F_DIM].astype(BF16)
    wo_a = w_out[l][F_DIM:].astype(BF16)
    w_router_t = w_router[l].T.astype(BF16)
    bias_col = router_bias[l].reshape(N_EXPERTS, 1)
    n1 = norm1[l].reshape(1, D_MODEL)
    n2 = norm2[l].reshape(1, D_MODEL)
    nf = norm_f.reshape(1, D_MODEL)
    sink_col = sink[l]
    wg, wu, wd = w_gate[l], w_up[l], w_down[l]
    sg, su, sd = ws_gate[l].astype(BF16), ws_up[l].astype(BF16), ws_down[l].astype(BF16)

    def mixers(x, batch, t, latent, kx, vx, t1, t2, tm):
        n = batch * t
        x2d = x.reshape(n, D_MODEL)
        tiles = t // tm
        row_fn = (lambda i: 1 + i // tiles) if latent else (lambda i: 0)
        rope_tabs = _rope_tables(t) if latent else None
        outs = _project(x2d, mod, n1, w_in_b, dft_c, rope_tabs, tm, tiles, row_fn)
        wr, wi, q, k, v = outs[:5]
        mixf = _fourier(wr, wi, wf_bd, batch, t, t1, t2)
        q3, k3, v3 = (a.reshape(batch, t, -1) for a in (q, k, v))
        if latent:
            mixa = _attention(q3, k3, v3, kx, vx, sink_col, batch, t, True)
        else:
            mixa = _attention(q3, None, None, k3, v3, sink_col, batch, t, False)
        return (x2d, mixf.reshape(n, F_DIM), mixa.reshape(n, ATTN_DIM)), outs[5:]

    ctx, (kf, vf) = mixers(x_prompt, nb_ctx, t_ctx, False, None, None, 16, 16, 256)
    kx = cache_k[:, l].reshape(nb_lat, -1, KV_DIM).astype(BF16)
    vx = cache_v[:, l].reshape(nb_lat, -1, KV_DIM).astype(BF16)
    lat, _ = mixers(x_sample, nb_lat, t_lat, True, kx, vx, 64, 64, 512)

    n_ctx = nb_ctx * t_ctx

    def tile_row(tile):
        ctx_tiles, lat_tiles = n_ctx // tile, t_lat // tile
        return lambda i: jnp.where(i < ctx_tiles, 0, 1 + (i - ctx_tiles) // lat_tiles)

    x1, h2, logits_t = _outproj(ctx, lat, wo_f, wo_a, mod, n2, w_router_t, OUTPROJ_TILE, tile_row(OUTPROJ_TILE))
    y_prompt, y_sample = _moe(x1, h2, logits_t, bias_col, wg, wu, wd, sg, su, sd, mod, nf, tile_row(MOE_CHUNK),
                              n_ctx)
    new_k = kf.reshape(nb_ctx, 1, t_ctx, N_KV_HEADS, HEAD_DIM)
    new_v = vf.reshape(nb_ctx, 1, t_ctx, N_KV_HEADS, HEAD_DIM)
    return (y_prompt.reshape(x_prompt.shape), y_sample.reshape(x_sample.shape), new_k, new_v)
```

```python
import functools

import numpy as np
import jax
import jax.numpy as jnp
from jax import lax
from jax.experimental import pallas as pl
from jax.experimental.pallas import tpu as pltpu

F32 = jnp.float32
BF16 = jnp.bfloat16

D_MODEL = 1024
GRID_W = 64
HEAD_DIM = 64
N_HEADS = 12
N_KV_HEADS = 4
GQA_GROUP = N_HEADS // N_KV_HEADS
ATTN_DIM = N_HEADS * HEAD_DIM
KV_DIM = N_KV_HEADS * HEAD_DIM
F_GROUPS = 4
F_HD = 64
F_DIM = F_GROUPS * F_HD
IN_DIM = F_DIM + ATTN_DIM + 2 * KV_DIM
WINDOW = 128
Q_BLOCK = 128
ROPE_THETA = 10000.0
N_EXPERTS = 64
TOP_K = 8
N_EXPERT_GROUPS = 8
TOPK_GROUPS = 4
EXPERT_DIM = 256
SHARED_DIM = 256
ROUTED_SCALE = 2.5
EPS = 1e-6

LANES = 128
MOD_ROWS = 8
RUN_ALIGN = 16
PERM_BLOCK = 256
FFN_BLOCK = 512
FFN_LOOKAHEAD = 3
FFN_IN_SLOTS = FFN_LOOKAHEAD + 1
MOE_CHUNK = 256
OUTPROJ_TILE = 512
ROUTE_CHUNKS = 4
SOFTMAX_ROWS = 32
DMA_UNROLL = 8
STRAIGHT_EXTRA_BLOCKS = 2
VMEM_LIMIT = 56 * 1024 * 1024


def _cparams(sem, manual_dma=False):
    return pltpu.CompilerParams(dimension_semantics=sem, vmem_limit_bytes=VMEM_LIMIT,
                                disable_bounds_checks=manual_dma)


def _bdot(a, b):
    return jnp.dot(a.astype(BF16), b.astype(BF16), preferred_element_type=F32)


def _rms(x, g):
    return x * lax.rsqrt(jnp.mean(x * x, axis=-1, keepdims=True) + EPS) * g


def _mod_kernel(c_ref, w_ref, b_ref, o_ref):
    c = c_ref[...]
    a = c * jax.nn.sigmoid(c)
    o_ref[...] = _bdot(a, w_ref[...]) + b_ref[...]


def _modulation(cvec, w_mod, b_mod):
    n = w_mod.shape[1]
    tn = 1024
    return pl.pallas_call(
        _mod_kernel,
        grid=(n // tn,),
        in_specs=[pl.BlockSpec((MOD_ROWS, D_MODEL), lambda j: (0, 0)),
                  pl.BlockSpec((D_MODEL, tn), lambda j: (0, j)),
                  pl.BlockSpec((1, tn), lambda j: (0, j))],
        out_specs=pl.BlockSpec((MOD_ROWS, tn), lambda j: (0, j)),
        out_shape=jax.ShapeDtypeStruct((MOD_ROWS, n), F32),
        compiler_params=_cparams(("parallel",)),
        name="modulation",
    )(cvec, w_mod, b_mod.reshape(1, n))


def _mod_spec(piece, row_fn):
    return pl.BlockSpec((None, None, 1, D_MODEL), lambda *idx: (row_fn(*idx), piece, 0, 0))


def _rope(x, cos, sin_up, sin_dn):
    outs = []
    for j in range(x.shape[1] // LANES):
        xj = x[:, j * LANES:(j + 1) * LANES]
        up = pltpu.roll(xj, LANES - 16, axis=1)
        dn = pltpu.roll(xj, 16, axis=1)
        outs.append(xj * cos + up * sin_up + dn * sin_dn)
    return jnp.concatenate(outs, axis=1)


def _proj_kernel(*refs, latent):
    if latent:
        (x_ref, sh_ref, sc_ref, g_ref, w_ref, dft_ref, cos_ref, sup_ref, sdn_ref,
         wr_ref, wi_ref, q_ref, k_ref, v_ref) = refs
    else:
        (x_ref, sh_ref, sc_ref, g_ref, w_ref, dft_ref,
         wr_ref, wi_ref, q_ref, k_ref, v_ref, kf_ref, vf_ref) = refs
    x = x_ref[...]
    h = _rms(x, g_ref[...]) * (1.0 + sc_ref[...]) + sh_ref[...]
    p = _bdot(h, w_ref[...])
    u = p[:, :F_DIM]
    q = p[:, F_DIM:F_DIM + ATTN_DIM]
    k = p[:, F_DIM + ATTN_DIM:F_DIM + ATTN_DIM + KV_DIM]
    v = p[:, F_DIM + ATTN_DIM + KV_DIM:]
    w = _bdot(u, dft_ref[...])
    wr_ref[...] = w[:, :F_DIM]
    wi_ref[...] = w[:, F_DIM:]
    if latent:
        cos, sup, sdn = cos_ref[...], sup_ref[...], sdn_ref[...]
        q = _rope(q, cos, sup, sdn)
        k = _rope(k, cos, sup, sdn)
    else:
        kf_ref[...] = k
        vf_ref[...] = v
    q_ref[...] = (q * (HEAD_DIM ** -0.5)).astype(BF16)
    k_ref[...] = k.astype(BF16)
    v_ref[...] = v.astype(BF16)


def _project(x2d, mod, norm1, w_in, dft_c, rope_tabs, tm, tiles_per_batch, row_fn):
    n = x2d.shape[0]
    latent = rope_tabs is not None
    tok = lambda w: pl.BlockSpec((tm, w), lambda i: (i, 0))
    full = lambda a: pl.BlockSpec(a.shape, lambda i: (0,) * a.ndim)
    in_specs = [tok(D_MODEL), _mod_spec(0, row_fn), _mod_spec(1, row_fn), full(norm1), full(w_in), full(dft_c)]
    args = [x2d, mod, mod, norm1, w_in, dft_c]
    out_specs = [tok(F_DIM), tok(F_DIM), tok(ATTN_DIM), tok(KV_DIM), tok(KV_DIM)]
    out_shape = [jax.ShapeDtypeStruct((n, F_DIM), F32), jax.ShapeDtypeStruct((n, F_DIM), F32),
                 jax.ShapeDtypeStruct((n, ATTN_DIM), BF16), jax.ShapeDtypeStruct((n, KV_DIM), BF16),
                 jax.ShapeDtypeStruct((n, KV_DIM), BF16)]
    if latent:
        pos = pl.BlockSpec((tm, LANES), lambda i: (i % tiles_per_batch, 0))
        in_specs += [pos, pos, pos]
        args += list(rope_tabs)
    else:
        out_specs += [tok(KV_DIM), tok(KV_DIM)]
        out_shape += [jax.ShapeDtypeStruct((n, KV_DIM), F32), jax.ShapeDtypeStruct((n, KV_DIM), F32)]
    return pl.pallas_call(
        functools.partial(_proj_kernel, latent=latent),
        grid=(n // tm,),
        in_specs=in_specs, out_specs=out_specs, out_shape=out_shape,
        compiler_params=_cparams(("parallel",)),
        name="project_latent" if latent else "project_context",
    )(*args)


def _fourier_kernel(wr_ref, wi_ref, a_ref, b_ref, wf_ref, o_ref, yr_ref, yi_ref, z_ref, *, t1, t2, scale):
    for j in range(t2):
        rows = pl.ds(j, t1, stride=t2)
        xin = jnp.concatenate([wr_ref[rows, :], wi_ref[rows, :]], axis=0)
        y = _bdot(a_ref[j], xin)
        yr_ref[rows, :] = y[:t1]
        yi_ref[rows, :] = y[t1:]
    bm = b_ref[...]
    for k1 in range(t1):
        rows = pl.ds(k1 * t2, t2)
        yin = jnp.concatenate([yr_ref[rows, :], yi_ref[rows, :]], axis=0)
        z_ref[pl.ds(k1, t2, stride=t1), :] = _bdot(bm, yin)
    o_ref[...] = (_bdot(z_ref[...], wf_ref[...]) * scale).astype(BF16)


def _dft_tables(t1, t2):
    t = t1 * t2
    k1 = np.arange(t1)[None, :, None]
    pos = (t2 * np.arange(t1)[None, None, :] + np.arange(t2)[:, None, None])
    ang = 2.0 * np.pi * ((k1 * pos) % t) / t
    c, s = np.cos(ang), np.sin(ang)
    a = np.concatenate([np.concatenate([c, s], axis=2), np.concatenate([-s, c], axis=2)], axis=1)
    ang2 = 2.0 * np.pi * ((np.arange(t2)[:, None] * np.arange(t2)[None, :]) % t2) / t2
    b = np.concatenate([np.cos(ang2), np.sin(ang2)], axis=1)
    return a.astype(np.float32), b.astype(np.float32)


def _fourier(wr, wi, wf_bd, batch, t, t1, t2):
    a_np, b_np = _dft_tables(t1, t2)
    a = jnp.asarray(a_np).astype(BF16)
    b = jnp.asarray(b_np).astype(BF16)
    cw = LANES
    blk = pl.BlockSpec((None, t, cw), lambda bi, ci: (bi, 0, ci))
    return pl.pallas_call(
        functools.partial(_fourier_kernel, t1=t1, t2=t2, scale=float((t * F_HD) ** -0.5)),
        grid=(batch, F_DIM // cw),
        in_specs=[blk, blk,
                  pl.BlockSpec(a.shape, lambda bi, ci: (0, 0, 0)),
                  pl.BlockSpec(b.shape, lambda bi, ci: (0, 0)),
                  pl.BlockSpec((None, cw, cw), lambda bi, ci: (ci, 0, 0))],
        out_specs=blk,
        out_shape=jax.ShapeDtypeStruct((batch, t, F_DIM), BF16),
        scratch_shapes=[pltpu.VMEM((t, cw), F32), pltpu.VMEM((t, cw), F32), pltpu.VMEM((t, cw), F32)],
        compiler_params=_cparams(("parallel", "parallel")),
        name="fourier_%d" % t,
    )(wr.reshape(batch, t, F_DIM), wi.reshape(batch, t, F_DIM), a, b, wf_bd)


def _attn_kernel(*refs, windowed, n_blocks):
    if windowed:
        q_ref, kp_ref, kc_ref, kn_ref, vp_ref, vc_ref, vn_ref, kx_ref, vx_ref, sink_ref, o_ref, s_ref, p_ref = refs
    else:
        q_ref, kx_ref, vx_ref, sink_ref, o_ref, s_ref, p_ref = refs
    nt = (((1,), (1,)), ((), ()))
    parts = [(kx_ref, vx_ref, None)]
    if windowed:
        i = pl.program_id(1)
        a = lax.broadcasted_iota(jnp.int32, (Q_BLOCK, Q_BLOCK), 0)
        j = lax.broadcasted_iota(jnp.int32, (Q_BLOCK, Q_BLOCK), 1)
        prev_ok = (j >= a) & (i > 0)
        next_ok = (j <= a) & (i < n_blocks - 1)
        parts = [(kp_ref, vp_ref, prev_ok), (kc_ref, vc_ref, None), (kn_ref, vn_ref, next_ok)] + parts
    n_keys = sum(k_ref.shape[0] for k_ref, _, _ in parts)
    outs = []
    for kv in range(N_KV_HEADS):
        heads = range(kv * GQA_GROUP, (kv + 1) * GQA_GROUP)
        hs = slice(kv * HEAD_DIM, (kv + 1) * HEAD_DIM)
        qs = jnp.concatenate([q_ref[:, h * HEAD_DIM:(h + 1) * HEAD_DIM] for h in heads], axis=0)
        c0 = 0
        for k_ref, _, ok in parts:
            s = lax.dot_general(qs, k_ref[:, hs], nt, preferred_element_type=F32)
            if ok is not None:
                s = jnp.where(jnp.concatenate([ok] * GQA_GROUP, axis=0), s, -jnp.inf)
            s_ref[:, c0:c0 + s.shape[1]] = s
            c0 += s.shape[1]
        invs = []
        for r0 in range(0, GQA_GROUP * Q_BLOCK, SOFTMAX_ROWS):
            s = s_ref[r0:r0 + SOFTMAX_ROWS, :]
            sk = sink_ref[kv * GQA_GROUP + r0 // Q_BLOCK]
            slabs = [s[:, c:c + LANES] for c in range(0, n_keys, LANES)]
            m = jnp.maximum(jnp.max(functools.reduce(jnp.maximum, slabs), axis=1, keepdims=True), sk)
            p = jnp.exp(s - m)
            pslabs = [p[:, c:c + LANES] for c in range(0, n_keys, LANES)]
            den = jnp.sum(functools.reduce(jnp.add, pslabs), axis=1, keepdims=True) + jnp.exp(sk - m)
            p_ref[r0:r0 + SOFTMAX_ROWS, :] = p.astype(BF16)
            invs.append(1.0 / den)
        o = None
        c0 = 0
        for k_ref, v_ref, _ in parts:
            nk = k_ref.shape[0]
            d = jnp.dot(p_ref[:, c0:c0 + nk], v_ref[:, hs], preferred_element_type=F32)
            o = d if o is None else o + d
            c0 += nk
        o = o * jnp.concatenate(invs, axis=0)
        outs += [o[g * Q_BLOCK:(g + 1) * Q_BLOCK] for g in range(GQA_GROUP)]
    o_ref[...] = jnp.concatenate(outs, axis=1).astype(BF16)


def _attention(q, k, v, kx, vx, sink_col, batch, t, windowed):
    nb = t // Q_BLOCK
    n_ctx = kx.shape[1]
    n_keys = n_ctx + (3 * Q_BLOCK if windowed else 0)
    qspec = pl.BlockSpec((None, Q_BLOCK, ATTN_DIM), lambda b, i: (b, i, 0))
    xspec = pl.BlockSpec((None, n_ctx, KV_DIM), lambda b, i: (b, 0, 0))
    sspec = pl.BlockSpec(memory_space=pltpu.SMEM)
    if windowed:
        prev = pl.BlockSpec((None, Q_BLOCK, KV_DIM), lambda b, i: (b, jnp.maximum(i - 1, 0), 0))
        cur = pl.BlockSpec((None, Q_BLOCK, KV_DIM), lambda b, i: (b, i, 0))
        nxt = pl.BlockSpec((None, Q_BLOCK, KV_DIM), lambda b, i: (b, jnp.minimum(i + 1, nb - 1), 0))
        in_specs = [qspec, prev, cur, nxt, prev, cur, nxt, xspec, xspec, sspec]
        args = [q, k, k, k, v, v, v, kx, vx, sink_col]
    else:
        in_specs = [qspec, xspec, xspec, sspec]
        args = [q, kx, vx, sink_col]
    return pl.pallas_call(
        functools.partial(_attn_kernel, windowed=windowed, n_blocks=nb),
        grid=(batch, nb),
        in_specs=in_specs, out_specs=qspec,
        out_shape=jax.ShapeDtypeStruct((batch, t, ATTN_DIM), BF16),
        scratch_shapes=[pltpu.VMEM((GQA_GROUP * Q_BLOCK, n_keys), F32),
                        pltpu.VMEM((GQA_GROUP * Q_BLOCK, n_keys), BF16)],
        compiler_params=_cparams(("parallel", "parallel")),
        name="attention_latent" if windowed else "attention_context",
    )(*args)


def _outproj_kernel(xc_ref, mfc_ref, mac_ref, xl_ref, mfl_ref, mal_ref,
                    wof_ref, woa_ref, g1_ref, sh_ref, sc_ref, n2_ref, wr_ref,
                    x1_ref, h_ref, lg_ref, *, ctx_tiles):
    def body(x_ref, mf_ref, ma_ref):
        o = jnp.dot(mf_ref[...], wof_ref[...], preferred_element_type=F32)
        o = o + jnp.dot(ma_ref[...], woa_ref[...], preferred_element_type=F32)
        x1 = x_ref[...] + g1_ref[...] * o
        x1_ref[...] = x1
        h = (_rms(x1, n2_ref[...]) * (1.0 + sc_ref[...]) + sh_ref[...]).astype(BF16)
        h_ref[...] = h
        lg_ref[...] = lax.dot_general(wr_ref[...], h, (((1,), (1,)), ((), ())), preferred_element_type=F32)

    is_ctx = pl.program_id(0) < ctx_tiles
    pl.when(is_ctx)(lambda: body(xc_ref, mfc_ref, mac_ref))
    pl.when(jnp.logical_not(is_ctx))(lambda: body(xl_ref, mfl_ref, mal_ref))


def _outproj(ctx, lat, wo_f, wo_a, mod, norm2, w_router_t, tm, row_fn):
    n_c, n_l = ctx[0].shape[0], lat[0].shape[0]
    ctx_tiles = n_c // tm
    n = n_c + n_l
    ctok = lambda w: pl.BlockSpec((tm, w), lambda i: (jnp.minimum(i, ctx_tiles - 1), 0))
    ltok = lambda w: pl.BlockSpec((tm, w), lambda i: (jnp.maximum(i - ctx_tiles, 0), 0))
    tok = lambda w: pl.BlockSpec((tm, w), lambda i: (i, 0))
    full = lambda a: pl.BlockSpec(a.shape, lambda i: (0,) * a.ndim)
    widths = (D_MODEL, F_DIM, ATTN_DIM)
    return pl.pallas_call(
        functools.partial(_outproj_kernel, ctx_tiles=ctx_tiles),
        grid=(n // tm,),
        in_specs=[ctok(w) for w in widths] + [ltok(w) for w in widths] + [
            full(wo_f), full(wo_a), _mod_spec(2, row_fn), _mod_spec(3, row_fn), _mod_spec(4, row_fn),
            full(norm2), full(w_router_t)],
        out_specs=[tok(D_MODEL), tok(D_MODEL), pl.BlockSpec((N_EXPERTS, tm), lambda i: (0, i))],
        out_shape=[jax.ShapeDtypeStruct((n, D_MODEL), F32), jax.ShapeDtypeStruct((n, D_MODEL), BF16),
                   jax.ShapeDtypeStruct((N_EXPERTS, n), F32)],
        compiler_params=_cparams(("parallel",)),
        name="outproj",
    )(*ctx, *lat, wo_f, wo_a, mod, mod, mod, norm2, w_router_t)


def _top_rows(vals, k):
    n = vals.shape[0]
    idx = lax.broadcasted_iota(jnp.int32, vals.shape, 0)
    picked = jnp.zeros(vals.shape, F32)
    for _ in range(k):
        best = jnp.max(vals, axis=0, keepdims=True)
        first = jnp.min(jnp.where(vals == best, idx, n), axis=0, keepdims=True)
        hit = idx == first
        picked = jnp.where(hit, 1.0, picked)
        vals = jnp.where(hit, -jnp.inf, vals)
    return picked


def _route_chunk(logits, bias, lo, up):
    s = jax.nn.sigmoid(logits)
    sc = s + bias
    tn = s.shape[1]
    per = N_EXPERTS // N_EXPERT_GROUPS
    g3 = sc.reshape(N_EXPERT_GROUPS, per, tn)
    member = lax.broadcasted_iota(jnp.int32, g3.shape, 1)
    m1 = jnp.max(g3, axis=1, keepdims=True)
    first = jnp.min(jnp.where(g3 == m1, member, per), axis=1, keepdims=True)
    m2 = jnp.max(jnp.where(member == first, -jnp.inf, g3), axis=1, keepdims=True)
    gscore = (m1 + m2).reshape(N_EXPERT_GROUPS, tn)
    gsel = _top_rows(gscore, TOPK_GROUPS)
    emask = jnp.broadcast_to(gsel.reshape(N_EXPERT_GROUPS, 1, tn), g3.shape).reshape(N_EXPERTS, tn)
    masked = jnp.where(emask > 0.5, sc, -jnp.inf)
    self = _top_rows(masked, TOP_K)
    sel = self > 0.5
    w = jnp.where(sel, s, 0.0)
    gate = w / jnp.sum(w, axis=0, keepdims=True) * ROUTED_SCALE

    selb = self.astype(BF16)
    cnt = jnp.sum(self, axis=1, keepdims=True)
    pad = jnp.maximum(jnp.floor((cnt + (RUN_ALIGN - 1)) * (1.0 / RUN_ALIGN)), 1.0) * RUN_ALIGN
    soff = jnp.dot(lo, jnp.broadcast_to(pad, (N_EXPERTS, LANES)).astype(BF16), preferred_element_type=F32)[:, :1]
    rank = jnp.dot(selb, up, preferred_element_type=F32)
    kidx = jnp.dot(lo, selb, preferred_element_type=F32)
    pos_e = jnp.where(sel, soff + rank, 0.0)
    rows_p, rows_g = [], []
    for k in range(TOP_K):
        m = kidx == k
        rows_p.append(jnp.sum(jnp.where(m, pos_e, 0.0), axis=0, keepdims=True))
        rows_g.append(jnp.sum(jnp.where(m, gate, 0.0), axis=0, keepdims=True))
    return jnp.concatenate(rows_p, axis=0).astype(jnp.int32), jnp.concatenate(rows_g, axis=0), pad.astype(jnp.int32)


def _route_kernel(lg_ref, bias_ref, lo_ref, up_ref, pos_ref, gs_ref, cnt_ref, *, tc):
    for j in range(ROUTE_CHUNKS):
        cols = slice(j * tc, (j + 1) * tc)
        pos, gates, pad = _route_chunk(lg_ref[:, cols], bias_ref[...], lo_ref[...], up_ref[...])
        pos_ref[:, cols] = pos
        gs_ref[:, cols] = gates
        cnt_ref[j] = pad


def _route(logits_t, bias_col, tc):
    n = logits_t.shape[1]
    lo = jnp.asarray(np.tril(np.ones((N_EXPERTS, N_EXPERTS), np.float32), -1)).astype(BF16)
    up = jnp.asarray(np.triu(np.ones((tc, tc), np.float32), 1)).astype(BF16)
    tn = tc * ROUTE_CHUNKS
    slot = pl.BlockSpec((TOP_K, tn), lambda i: (0, i))
    return pl.pallas_call(
        functools.partial(_route_kernel, tc=tc),
        grid=(n // tn,),
        in_specs=[pl.BlockSpec((N_EXPERTS, tn), lambda i: (0, i)),
                  pl.BlockSpec((N_EXPERTS, 1), lambda i: (0, 0)),
                  pl.BlockSpec(lo.shape, lambda i: (0, 0)),
                  pl.BlockSpec(up.shape, lambda i: (0, 0))],
        out_specs=[slot, slot, pl.BlockSpec((ROUTE_CHUNKS, N_EXPERTS, 1), lambda i: (i, 0, 0))],
        out_shape=[jax.ShapeDtypeStruct((TOP_K, n), jnp.int32), jax.ShapeDtypeStruct((TOP_K, n), F32),
                   jax.ShapeDtypeStruct((n // tc, N_EXPERTS, 1), jnp.int32)],
        compiler_params=_cparams(("parallel",)),
        name="route",
    )(logits_t, bias_col, lo, up)


def _aligned(x):
    return pl.multiple_of(x, RUN_ALIGN)


def _block_rows(tc):
    return lax.broadcasted_iota(jnp.int32, (PERM_BLOCK, tc), 0).astype(F32).astype(BF16)


def _block_relative(pos, r0):
    return (pos - r0).astype(F32).astype(BF16)


def _gather_kernel(src_ref, cnt_ref, dst_ref, tot_ref,
                   pos_ref, x_ref, xs_ref, z_ref, sem, *, tc, n_chunks):
    c = pl.program_id(0)
    slot = c % 2

    def wait_chunk(ci, s):
        n = _aligned(tot_ref[ci])
        pltpu.make_async_copy(z_ref.at[s, pl.ds(0, n)], xs_ref.at[pl.ds(0, n)], sem.at[s]).wait()

    def start_runs(ci, s, first, count):
        for j in range(count):
            i = ci * N_EXPERTS + first + j
            n = _aligned(cnt_ref[i])
            pltpu.make_async_copy(z_ref.at[s, pl.ds(_aligned(src_ref[i]), n)],
                                  xs_ref.at[pl.ds(_aligned(dst_ref[i]), n)], sem.at[s]).start()

    @pl.when(c >= 2)
    def _():
        wait_chunk(c - 2, slot)

    rows = _block_rows(tc)

    def permute_block(b):
        r0 = _block_start(b)
        rel = _block_relative(pos_ref[...], r0)
        onehot = jnp.zeros((PERM_BLOCK, tc), BF16)
        for k in range(TOP_K):
            onehot = jnp.where(rel[k:k + 1, :] == rows, jnp.ones((), BF16), onehot)
        z_ref[slot, pl.ds(r0, PERM_BLOCK), :] = jnp.dot(
            onehot, x_ref[...], preferred_element_type=F32).astype(BF16)

    for b in range(_sure_blocks(tc)):
        permute_block(b)

    def tail_block(b, carry):
        permute_block(b)
        return carry
    lax.fori_loop(_sure_blocks(tc), _perm_blocks(tot_ref[c]), tail_block, 0)

    def e_body(e8, carry):
        start_runs(c, slot, e8 * DMA_UNROLL, DMA_UNROLL)
        return carry
    lax.fori_loop(0, N_EXPERTS // DMA_UNROLL, e_body, 0)

    @pl.when(c == n_chunks - 1)
    def _():
        wait_chunk(c, slot)
        if n_chunks > 1:
            wait_chunk(c - 1, 1 - slot)


def _perm_blocks(rows):
    return (rows + PERM_BLOCK - 1) // PERM_BLOCK


def _sure_blocks(tc):
    return min(TOP_K * tc // PERM_BLOCK + STRAIGHT_EXTRA_BLOCKS, _local_cap(tc) // PERM_BLOCK)


def _block_start(b):
    return b * PERM_BLOCK if isinstance(b, int) else pl.multiple_of(b * PERM_BLOCK, PERM_BLOCK)


def _chunk_rows_max(tc):
    return TOP_K * tc + N_EXPERTS * RUN_ALIGN


def _local_cap(tc):
    return -(-_chunk_rows_max(tc) // PERM_BLOCK) * PERM_BLOCK


def _gather(tabs, pos, h2, r_max, tc):
    n = h2.shape[0]
    n_chunks = n // tc
    return pl.pallas_call(
        functools.partial(_gather_kernel, tc=tc, n_chunks=n_chunks),
        grid_spec=pltpu.PrefetchScalarGridSpec(
            num_scalar_prefetch=4, grid=(n_chunks,),
            in_specs=[pl.BlockSpec((TOP_K, tc), lambda c, *_: (0, c)),
                      pl.BlockSpec((tc, D_MODEL), lambda c, *_: (c, 0))],
            out_specs=pl.BlockSpec(memory_space=pl.ANY),
            scratch_shapes=[pltpu.VMEM((2, _local_cap(tc), D_MODEL), BF16),
                            pltpu.SemaphoreType.DMA((2,))]),
        out_shape=jax.ShapeDtypeStruct((r_max, D_MODEL), BF16),
        compiler_params=_cparams(("arbitrary",), manual_dma=True),
        name="moe_gather",
    )(tabs["src"], tabs["cnt"], tabs["dst"], tabs["tot"], pos, h2)


def _ffn_kernel(first_ref, ntile_ref, tstart_ref, trows_ref, total_ref, xs_ref, wg_ref, wu_ref, wd_ref, ys_ref,
                xbuf, ybuf, wgb, wub, wdb, sem_in, sem_out):
    e = pl.program_id(0)
    total = total_ref[0]

    def in_copy(g):
        s = g % FFN_IN_SLOTS
        n = _aligned(trows_ref[g])
        return pltpu.make_async_copy(xs_ref.at[pl.ds(_aligned(tstart_ref[g]), n)],
                                     xbuf.at[s, pl.ds(0, n)], sem_in.at[s])

    def out_copy(g):
        s = g % 2
        n = _aligned(trows_ref[g])
        return pltpu.make_async_copy(ybuf.at[s, pl.ds(0, n)],
                                     ys_ref.at[pl.ds(_aligned(tstart_ref[g]), n)], sem_out.at[s])

    @pl.when(e == 0)
    def _():
        xbuf[...] = jnp.zeros(xbuf.shape, xbuf.dtype)
        for g in range(FFN_LOOKAHEAD):
            @pl.when(g < total)
            def _():
                in_copy(g).start()

    wgb[...] = wg_ref[...].astype(BF16)
    wub[...] = wu_ref[...].astype(BF16)
    wdb[...] = wd_ref[...].astype(BF16)

    def body(g, carry):
        @pl.when(g + FFN_LOOKAHEAD < total)
        def _():
            in_copy(g + FFN_LOOKAHEAD).start()
        in_copy(g).wait()

        @pl.when(g >= 2)
        def _():
            out_copy(g - 2).wait()
        x = xbuf[g % FFN_IN_SLOTS]
        a = jnp.dot(x, wgb[...], preferred_element_type=F32)
        u = jnp.dot(x, wub[...], preferred_element_type=F32)
        h = (a * jax.nn.sigmoid(a) * u).astype(BF16)
        ybuf[g % 2] = jnp.dot(h, wdb[...], preferred_element_type=F32).astype(BF16)
        out_copy(g).start()
        return carry
    lax.fori_loop(first_ref[e], first_ref[e] + ntile_ref[e], body, 0)

    @pl.when(e == pl.num_programs(0) - 1)
    def _():
        @pl.when(total >= 2)
        def _():
            out_copy(total - 2).wait()
        out_copy(total - 1).wait()


def _ffn(tabs, xs, wg, wu, wd):
    ew = lambda a: pl.BlockSpec((None,) + a.shape[1:], lambda e, *_: (e, 0, 0))
    return pl.pallas_call(
        _ffn_kernel,
        grid_spec=pltpu.PrefetchScalarGridSpec(
            num_scalar_prefetch=5, grid=(N_EXPERTS,),
            in_specs=[pl.BlockSpec(memory_space=pl.ANY), ew(wg), ew(wu), ew(wd)],
            out_specs=pl.BlockSpec(memory_space=pl.ANY),
            scratch_shapes=[pltpu.VMEM((FFN_IN_SLOTS, FFN_BLOCK, D_MODEL), BF16),
                            pltpu.VMEM((2, FFN_BLOCK, D_MODEL), BF16),
                            pltpu.VMEM(wg.shape[1:], BF16), pltpu.VMEM(wu.shape[1:], BF16),
                            pltpu.VMEM(wd.shape[1:], BF16),
                            pltpu.SemaphoreType.DMA((FFN_IN_SLOTS,)), pltpu.SemaphoreType.DMA((2,))]),
        out_shape=jax.ShapeDtypeStruct(xs.shape, BF16),
        compiler_params=_cparams(("arbitrary",), manual_dma=True),
        name="moe_ffn",
    )(tabs["first_tile"], tabs["n_tiles"], tabs["tile_start"], tabs["tile_rows"], tabs["total_tiles"],
      xs, wg, wu, wd)


def _combine_kernel(src_ref, cnt_ref, dst_ref, tot_ref,
                    pos_ref, gs_ref, x1_ref, h_ref, sg_ref, su_ref, sd_ref, g2_ref, nf_ref, ys_ref,
                    yc_ref, yl_ref, ybuf, acc_ref, sem, *, tc, n_chunks, ctx_chunks):
    c = pl.program_id(0)
    slot = c % 2

    def issue(ci, s):
        def e_body(e8, carry):
            for j in range(DMA_UNROLL):
                i = ci * N_EXPERTS + e8 * DMA_UNROLL + j
                n = _aligned(cnt_ref[i])
                pltpu.make_async_copy(ys_ref.at[pl.ds(_aligned(dst_ref[i]), n)],
                                      ybuf.at[s, pl.ds(_aligned(src_ref[i]), n)], sem.at[s]).start()
            return carry
        lax.fori_loop(0, N_EXPERTS // DMA_UNROLL, e_body, 0)

    @pl.when(c == 0)
    def _():
        ybuf[...] = jnp.zeros(ybuf.shape, ybuf.dtype)
        issue(0, 0)

    @pl.when(c + 1 < n_chunks)
    def _():
        issue(c + 1, 1 - slot)

    h = h_ref[...]
    a = jnp.dot(h, sg_ref[...], preferred_element_type=F32)
    u = jnp.dot(h, su_ref[...], preferred_element_type=F32)
    acc_ref[...] = jnp.dot((a * jax.nn.sigmoid(a) * u).astype(BF16), sd_ref[...], preferred_element_type=F32)

    n = _aligned(tot_ref[c])
    pltpu.make_async_copy(ys_ref.at[pl.ds(0, n)], ybuf.at[slot, pl.ds(0, n)], sem.at[slot]).wait()

    tn = (((0,), (0,)), ((), ()))
    rows = _block_rows(tc)
    gates = gs_ref[...].astype(BF16)

    def gate_block(b):
        rel = _block_relative(pos_ref[...], _block_start(b))
        wt = jnp.zeros((PERM_BLOCK, tc), BF16)
        for k in range(TOP_K):
            wt = jnp.where(rel[k:k + 1, :] == rows, gates[k:k + 1, :], wt)
        return wt

    sure_rows = _sure_blocks(tc) * PERM_BLOCK
    wt_sure = jnp.concatenate([gate_block(b) for b in range(_sure_blocks(tc))], axis=0)
    acc_ref[...] += lax.dot_general(wt_sure, ybuf[slot, pl.ds(0, sure_rows), :], tn, preferred_element_type=F32)

    def tail_block(b, carry):
        acc_ref[...] += lax.dot_general(gate_block(b), ybuf[slot, pl.ds(_block_start(b), PERM_BLOCK), :], tn,
                                        preferred_element_type=F32)
        return carry
    lax.fori_loop(_sure_blocks(tc), _perm_blocks(tot_ref[c]), tail_block, 0)

    x2 = x1_ref[...] + g2_ref[...] * acc_ref[...]
    y = _rms(x2, nf_ref[...])

    @pl.when(c < ctx_chunks)
    def _():
        yc_ref[...] = y

    @pl.when(c >= ctx_chunks)
    def _():
        yl_ref[...] = y


def _combine(tabs, pos, gslot, x1, h2, sg, su, sd, mod, norm_f, ys, tc, row_fn, n_ctx):
    n = x1.shape[0]
    n_chunks = n // tc
    ctx_chunks = n_ctx // tc
    tok = lambda w: pl.BlockSpec((tc, w), lambda c, *_: (c, 0))
    slot = pl.BlockSpec((TOP_K, tc), lambda c, *_: (0, c))
    full = lambda a: pl.BlockSpec(a.shape, lambda c, *_: (0,) * a.ndim)
    return pl.pallas_call(
        functools.partial(_combine_kernel, tc=tc, n_chunks=n_chunks, ctx_chunks=ctx_chunks),
        grid_spec=pltpu.PrefetchScalarGridSpec(
            num_scalar_prefetch=4, grid=(n_chunks,),
            in_specs=[slot, slot, tok(D_MODEL), tok(D_MODEL), full(sg), full(su), full(sd),
                      _mod_spec(5, lambda c, *_: row_fn(c)), full(norm_f), pl.BlockSpec(memory_space=pl.ANY)],
            out_specs=[pl.BlockSpec((tc, D_MODEL), lambda c, *_: (jnp.minimum(c, ctx_chunks - 1), 0)),
                       pl.BlockSpec((tc, D_MODEL), lambda c, *_: (jnp.maximum(c - ctx_chunks, 0), 0))],
            scratch_shapes=[pltpu.VMEM((2, _local_cap(tc), D_MODEL), BF16),
                            pltpu.VMEM((tc, D_MODEL), F32),
                            pltpu.SemaphoreType.DMA((2,))]),
        out_shape=[jax.ShapeDtypeStruct((n_ctx, D_MODEL), F32), jax.ShapeDtypeStruct((n - n_ctx, D_MODEL), F32)],
        compiler_params=_cparams(("arbitrary",), manual_dma=True),
        name="moe_combine",
    )(tabs["src"], tabs["cnt"], tabs["dst"], tabs["tot"], pos, gslot, x1, h2, sg, su, sd, mod, norm_f, ys)


def _dispatch_tables(pad_cnt, r_max):
    i32 = jnp.int32
    src = jnp.cumsum(pad_cnt, axis=1) - pad_cnt
    tot = jnp.sum(pad_cnt, axis=1)
    e_rows = jnp.sum(pad_cnt, axis=0)
    e_start = jnp.cumsum(e_rows) - e_rows
    dst = e_start[None, :] + jnp.cumsum(pad_cnt, axis=0) - pad_cnt
    n_tiles = (e_rows + FFN_BLOCK - 1) // FFN_BLOCK
    tile_end = jnp.cumsum(n_tiles)
    first_tile = tile_end - n_tiles
    g = jnp.arange(r_max // FFN_BLOCK + N_EXPERTS)
    before = tile_end[None, :] <= g[:, None]
    owner = jnp.concatenate([jnp.ones_like(before[:, :1]), before[:, :-1]], axis=1) & ~before
    offset = (g - jnp.sum(jnp.where(before, n_tiles, 0), axis=1)) * FFN_BLOCK
    tile_start = jnp.clip(jnp.sum(jnp.where(before, e_rows, 0), axis=1) + offset, 0, r_max - FFN_BLOCK)
    tile_rows = jnp.clip(jnp.sum(jnp.where(owner, e_rows, 0), axis=1) - offset, RUN_ALIGN, FFN_BLOCK)
    return {"src": src.reshape(-1).astype(i32), "cnt": pad_cnt.reshape(-1).astype(i32),
            "dst": dst.reshape(-1).astype(i32), "tot": tot.astype(i32),
            "first_tile": first_tile.astype(i32), "n_tiles": n_tiles.astype(i32),
            "tile_start": tile_start.astype(i32), "tile_rows": tile_rows.astype(i32),
            "total_tiles": tile_end[-1:].astype(i32)}


def _moe(x1, h2, logits_t, bias_col, wg, wu, wd, sg, su, sd, mod, norm_f, row_fn, n_ctx):
    n = x1.shape[0]
    tc = MOE_CHUNK
    pos, gslot, pad_cnt = _route(logits_t, bias_col, tc)
    r_max = (n // tc) * _chunk_rows_max(tc)
    tabs = _dispatch_tables(pad_cnt[:, :, 0], r_max)
    xs = _gather(tabs, pos, h2, r_max, tc)
    ys = _ffn(tabs, xs, wg, wu, wd)
    return _combine(tabs, pos, gslot, x1, h2, sg, su, sd, mod, norm_f, ys, tc, row_fn, n_ctx)


def _channel_dft():
    ang = 2.0 * np.pi * ((np.arange(F_HD)[:, None] * np.arange(F_HD)[None, :]) % F_HD) / F_HD
    eye = np.eye(F_GROUPS)
    return np.concatenate([np.kron(eye, np.cos(ang)), np.kron(eye, -np.sin(ang))], axis=1).astype(np.float32)


def _rope_tables(t):
    pos = np.arange(t)
    row, col = pos // GRID_W, pos % GRID_W
    n_freq = HEAD_DIM // 4
    inv = ROPE_THETA ** (-np.arange(n_freq, dtype=np.float64) / n_freq)
    lane = np.arange(LANES)
    hd = lane % HEAD_DIM
    within = hd % (HEAD_DIM // 2)
    freq = within % n_freq
    first = within < n_freq
    p = np.where((hd < HEAD_DIM // 2)[None, :], row[:, None], col[:, None]).astype(np.float64)
    ang = p * inv[freq][None, :]
    cos, sin = np.cos(ang), np.sin(ang)
    sin_up = np.where(first[None, :], -sin, 0.0)
    sin_dn = np.where(first[None, :], 0.0, sin)
    return [jnp.asarray(a.astype(np.float32)) for a in (cos, sin_up, sin_dn)]


def _block_diag_pairs(wf):
    per = LANES // F_HD
    z = jnp.zeros((F_HD, F_HD), wf.dtype)
    blocks = []
    for c in range(F_GROUPS // per):
        rows = [jnp.concatenate([wf[c * per + r] if r == cc else z for cc in range(per)], axis=1) for r in range(per)]
        blocks.append(jnp.concatenate(rows, axis=0))
    return jnp.stack(blocks)


def kernel(x_prompt, x_sample, cache_k, cache_v, c, c_ctx, w_mod, b_mod, norm1, w_in, w_fourier, sink,
           w_out, norm2, w_router, router_bias, w_gate, w_up, w_down, ws_gate, ws_up, ws_down, norm_f):
    nb_ctx, t_ctx, _ = x_prompt.shape
    nb_lat, t_lat, _ = x_sample.shape
    l = 0
    cvec = jnp.concatenate([c_ctx[None, :], c, jnp.zeros((MOD_ROWS - 1 - nb_lat, D_MODEL), F32)], axis=0)
    mod = _modulation(cvec, w_mod[l], b_mod[l]).reshape(MOD_ROWS, 6, 1, D_MODEL)

    w_in_b = w_in[l].astype(BF16)
    dft_c = jnp.asarray(_channel_dft()).astype(BF16)
    wf_bd = _block_diag_pairs(w_fourier[l]).astype(BF16)
    wo_f = w_out[l][:F_DIM].astype(BF16)
    wo_a = w_out[l][F_DIM:].astype(BF16)
    w_router_t = w_router[l].T.astype(BF16)
    bias_col = router_bias[l].reshape(N_EXPERTS, 1)
    n1 = norm1[l].reshape(1, D_MODEL)
    n2 = norm2[l].reshape(1, D_MODEL)
    nf = norm_f.reshape(1, D_MODEL)
    sink_col = sink[l]
    wg, wu, wd = w_gate[l], w_up[l], w_down[l]
    sg, su, sd = ws_gate[l].astype(BF16), ws_up[l].astype(BF16), ws_down[l].astype(BF16)

    def mixers(x, batch, t, latent, kx, vx, t1, t2, tm):
        n = batch * t
        x2d = x.reshape(n, D_MODEL)
        tiles = t // tm
        row_fn = (lambda i: 1 + i // tiles) if latent else (lambda i: 0)
        rope_tabs = _rope_tables(t) if latent else None
        outs = _project(x2d, mod, n1, w_in_b, dft_c, rope_tabs, tm, tiles, row_fn)
        wr, wi, q, k, v = outs[:5]
        mixf = _fourier(wr, wi, wf_bd, batch, t, t1, t2)
        q3, k3, v3 = (a.reshape(batch, t, -1) for a in (q, k, v))
        if latent:
            mixa = _attention(q3, k3, v3, kx, vx, sink_col, batch, t, True)
        else:
            mixa = _attention(q3, None, None, k3, v3, sink_col, batch, t, False)
        return (x2d, mixf.reshape(n, F_DIM), mixa.reshape(n, ATTN_DIM)), outs[5:]

    ctx, (kf, vf) = mixers(x_prompt, nb_ctx, t_ctx, False, None, None, 16, 16, 256)
    kx = cache_k[:, l].reshape(nb_lat, -1, KV_DIM).astype(BF16)
    vx = cache_v[:, l].reshape(nb_lat, -1, KV_DIM).astype(BF16)
    lat, _ = mixers(x_sample, nb_lat, t_lat, True, kx, vx, 64, 64, 512)

    n_ctx = nb_ctx * t_ctx

    def tile_row(tile):
        ctx_tiles, lat_tiles = n_ctx // tile, t_lat // tile
        return lambda i: jnp.where(i < ctx_tiles, 0, 1 + (i - ctx_tiles) // lat_tiles)

    x1, h2, logits_t = _outproj(ctx, lat, wo_f, wo_a, mod, n2, w_router_t, OUTPROJ_TILE, tile_row(OUTPROJ_TILE))
    y_prompt, y_sample = _moe(x1, h2, logits_t, bias_col, wg, wu, wd, sg, su, sd, mod, nf, tile_row(MOE_CHUNK),
                              n_ctx)
    new_k = kf.reshape(nb_ctx, 1, t_ctx, N_KV_HEADS, HEAD_DIM)
    new_v = vf.reshape(nb_ctx, 1, t_ctx, N_KV_HEADS, HEAD_DIM)
    return (y_prompt.reshape(x_prompt.shape), y_sample.reshape(x_sample.shape), new_k, new_v)
```

```python
import functools

import numpy as np
import jax
import jax.numpy as jnp
from jax import lax
from jax.experimental import pallas as pl
from jax.experimental.pallas import tpu as pltpu

F32 = jnp.float32
BF16 = jnp.bfloat16

D_MODEL = 1024
GRID_W = 64
HEAD_DIM = 64
N_HEADS = 12
N_KV_HEADS = 4
GQA_GROUP = N_HEADS // N_KV_HEADS
ATTN_DIM = N_HEADS * HEAD_DIM
KV_DIM = N_KV_HEADS * HEAD_DIM
F_GROUPS = 4
F_HD = 64
F_DIM = F_GROUPS * F_HD
IN_DIM = F_DIM + ATTN_DIM + 2 * KV_DIM
WINDOW = 128
Q_BLOCK = 128
ROPE_THETA = 10000.0
N_EXPERTS = 64
TOP_K = 8
N_EXPERT_GROUPS = 8
TOPK_GROUPS = 4
EXPERT_DIM = 256
SHARED_DIM = 256
ROUTED_SCALE = 2.5
EPS = 1e-6

LANES = 128
MOD_ROWS = 8
RUN_ALIGN = 16
PERM_BLOCK = 256
FFN_BLOCK = 512
FFN_LOOKAHEAD = 3
FFN_IN_SLOTS = FFN_LOOKAHEAD + 1
MOE_CHUNK = 256
OUTPROJ_TILE = 512
ROUTE_CHUNKS = 4
SOFTMAX_ROWS = 32
DMA_UNROLL = 8
STRAIGHT_EXTRA_BLOCKS = 2
VMEM_LIMIT = 56 * 1024 * 1024


def _cparams(sem, manual_dma=False):
    return pltpu.CompilerParams(dimension_semantics=sem, vmem_limit_bytes=VMEM_LIMIT,
                                disable_bounds_checks=manual_dma)


def _bdot(a, b):
    return jnp.dot(a.astype(BF16), b.astype(BF16), preferred_element_type=F32)


def _rms(x, g):
    return x * lax.rsqrt(jnp.mean(x * x, axis=-1, keepdims=True) + EPS) * g


def _mod_kernel(c_ref, w_ref, b_ref, o_ref):
    c = c_ref[...]
    a = c * jax.nn.sigmoid(c)
    o_ref[...] = _bdot(a, w_ref[...]) + b_ref[...]


def _modulation(cvec, w_mod, b_mod):
    n = w_mod.shape[1]
    tn = 1024
    return pl.pallas_call(
        _mod_kernel,
        grid=(n // tn,),
        in_specs=[pl.BlockSpec((MOD_ROWS, D_MODEL), lambda j: (0, 0)),
                  pl.BlockSpec((D_MODEL, tn), lambda j: (0, j)),
                  pl.BlockSpec((1, tn), lambda j: (0, j))],
        out_specs=pl.BlockSpec((MOD_ROWS, tn), lambda j: (0, j)),
        out_shape=jax.ShapeDtypeStruct((MOD_ROWS, n), F32),
        compiler_params=_cparams(("parallel",)),
        name="modulation",
    )(cvec, w_mod, b_mod.reshape(1, n))


def _mod_spec(piece, row_fn):
    return pl.BlockSpec((None, None, 1, D_MODEL), lambda *idx: (row_fn(*idx), piece, 0, 0))


def _rope(x, cos, sin_up, sin_dn):
    outs = []
    for j in range(x.shape[1] // LANES):
        xj = x[:, j * LANES:(j + 1) * LANES]
        up = pltpu.roll(xj, LANES - 16, axis=1)
        dn = pltpu.roll(xj, 16, axis=1)
        outs.append(xj * cos + up * sin_up + dn * sin_dn)
    return jnp.concatenate(outs, axis=1)


def _proj_kernel(*refs, latent):
    if latent:
        (x_ref, sh_ref, sc_ref, g_ref, w_ref, dft_ref, cos_ref, sup_ref, sdn_ref,
         wr_ref, wi_ref, q_ref, k_ref, v_ref) = refs
    else:
        (x_ref, sh_ref, sc_ref, g_ref, w_ref, dft_ref,
         wr_ref, wi_ref, q_ref, k_ref, v_ref, kf_ref, vf_ref) = refs
    x = x_ref[...]
    h = _rms(x, g_ref[...]) * (1.0 + sc_ref[...]) + sh_ref[...]
    p = _bdot(h, w_ref[...])
    u = p[:, :F_DIM]
    q = p[:, F_DIM:F_DIM + ATTN_DIM]
    k = p[:, F_DIM + ATTN_DIM:F_DIM + ATTN_DIM + KV_DIM]
    v = p[:, F_DIM + ATTN_DIM + KV_DIM:]
    w = _bdot(u, dft_ref[...])
    wr_ref[...] = w[:, :F_DIM]
    wi_ref[...] = w[:, F_DIM:]
    if latent:
        cos, sup, sdn = cos_ref[...], sup_ref[...], sdn_ref[...]
        q = _rope(q, cos, sup, sdn)
        k = _rope(k, cos, sup, sdn)
    else:
        kf_ref[...] = k
        vf_ref[...] = v
    q_ref[...] = (q * (HEAD_DIM ** -0.5)).astype(BF16)
    k_ref[...] = k.astype(BF16)
    v_ref[...] = v.astype(BF16)


def _project(x2d, mod, norm1, w_in, dft_c, rope_tabs, tm, tiles_per_batch, row_fn):
    n = x2d.shape[0]
    latent = rope_tabs is not None
    tok = lambda w: pl.BlockSpec((tm, w), lambda i: (i, 0))
    full = lambda a: pl.BlockSpec(a.shape, lambda i: (0,) * a.ndim)
    in_specs = [tok(D_MODEL), _mod_spec(0, row_fn), _mod_spec(1, row_fn), full(norm1), full(w_in), full(dft_c)]
    args = [x2d, mod, mod, norm1, w_in, dft_c]
    out_specs = [tok(F_DIM), tok(F_DIM), tok(ATTN_DIM), tok(KV_DIM), tok(KV_DIM)]
    out_shape = [jax.ShapeDtypeStruct((n, F_DIM), F32), jax.ShapeDtypeStruct((n, F_DIM), F32),
                 jax.ShapeDtypeStruct((n, ATTN_DIM), BF16), jax.ShapeDtypeStruct((n, KV_DIM), BF16),
                 jax.ShapeDtypeStruct((n, KV_DIM), BF16)]
    if latent:
        pos = pl.BlockSpec((tm, LANES), lambda i: (i % tiles_per_batch, 0))
        in_specs += [pos, pos, pos]
        args += list(rope_tabs)
    else:
        out_specs += [tok(KV_DIM), tok(KV_DIM)]
        out_shape += [jax.ShapeDtypeStruct((n, KV_DIM), F32), jax.ShapeDtypeStruct((n, KV_DIM), F32)]
    return pl.pallas_call(
        functools.partial(_proj_kernel, latent=latent),
        grid=(n // tm,),
        in_specs=in_specs, out_specs=out_specs, out_shape=out_shape,
        compiler_params=_cparams(("parallel",)),
        name="project_latent" if latent else "project_context",
    )(*args)


def _fourier_kernel(wr_ref, wi_ref, a_ref, b_ref, wf_ref, o_ref, yr_ref, yi_ref, z_ref, *, t1, t2, scale):
    for j in range(t2):
        rows = pl.ds(j, t1, stride=t2)
        xin = jnp.concatenate([wr_ref[rows, :], wi_ref[rows, :]], axis=0)
        y = _bdot(a_ref[j], xin)
        yr_ref[rows, :] = y[:t1]
        yi_ref[rows, :] = y[t1:]
    bm = b_ref[...]
    for k1 in range(t1):
        rows = pl.ds(k1 * t2, t2)
        yin = jnp.concatenate([yr_ref[rows, :], yi_ref[rows, :]], axis=0)
        z_ref[pl.ds(k1, t2, stride=t1), :] = _bdot(bm, yin)
    o_ref[...] = (_bdot(z_ref[...], wf_ref[...]) * scale).astype(BF16)


def _dft_tables(t1, t2):
    t = t1 * t2
    k1 = np.arange(t1)[None, :, None]
    pos = (t2 * np.arange(t1)[None, None, :] + np.arange(t2)[:, None, None])
    ang = 2.0 * np.pi * ((k1 * pos) % t) / t
    c, s = np.cos(ang), np.sin(ang)
    a = np.concatenate([np.concatenate([c, s], axis=2), np.concatenate([-s, c], axis=2)], axis=1)
    ang2 = 2.0 * np.pi * ((np.arange(t2)[:, None] * np.arange(t2)[None, :]) % t2) / t2
    b = np.concatenate([np.cos(ang2), np.sin(ang2)], axis=1)
    return a.astype(np.float32), b.astype(np.float32)


def _fourier(wr, wi, wf_bd, batch, t, t1, t2):
    a_np, b_np = _dft_tables(t1, t2)
    a = jnp.asarray(a_np).astype(BF16)
    b = jnp.asarray(b_np).astype(BF16)
    cw = LANES
    blk = pl.BlockSpec((None, t, cw), lambda bi, ci: (bi, 0, ci))
    return pl.pallas_call(
        functools.partial(_fourier_kernel, t1=t1, t2=t2, scale=float((t * F_HD) ** -0.5)),
        grid=(batch, F_DIM // cw),
        in_specs=[blk, blk,
                  pl.BlockSpec(a.shape, lambda bi, ci: (0, 0, 0)),
                  pl.BlockSpec(b.shape, lambda bi, ci: (0, 0)),
                  pl.BlockSpec((None, cw, cw), lambda bi, ci: (ci, 0, 0))],
        out_specs=blk,
        out_shape=jax.ShapeDtypeStruct((batch, t, F_DIM), BF16),
        scratch_shapes=[pltpu.VMEM((t, cw), F32), pltpu.VMEM((t, cw), F32), pltpu.VMEM((t, cw), F32)],
        compiler_params=_cparams(("parallel", "parallel")),
        name="fourier_%d" % t,
    )(wr.reshape(batch, t, F_DIM), wi.reshape(batch, t, F_DIM), a, b, wf_bd)


def _attend(q_ref, row0, parts, sink_ref, s_ref, p_ref):
    nt = (((1,), (1,)), ((), ()))
    n_keys = sum(k.shape[0] for k, _, _ in parts)
    outs = []
    for kv in range(N_KV_HEADS):
        heads = range(kv * GQA_GROUP, (kv + 1) * GQA_GROUP)
        hs = slice(kv * HEAD_DIM, (kv + 1) * HEAD_DIM)
        qs = jnp.concatenate([q_ref[row0:row0 + Q_BLOCK, h * HEAD_DIM:(h + 1) * HEAD_DIM] for h in heads], axis=0)
        c0 = 0
        for k, _, ok in parts:
            s = lax.dot_general(qs, k[:, hs], nt, preferred_element_type=F32)
            if ok is not None:
                s = jnp.where(jnp.concatenate([ok] * GQA_GROUP, axis=0), s, -jnp.inf)
            s_ref[:, c0:c0 + s.shape[1]] = s
            c0 += s.shape[1]
        invs = []
        for r0 in range(0, GQA_GROUP * Q_BLOCK, SOFTMAX_ROWS):
            s = s_ref[r0:r0 + SOFTMAX_ROWS, :]
            sk = sink_ref[kv * GQA_GROUP + r0 // Q_BLOCK]
            slabs = [s[:, c:c + LANES] for c in range(0, n_keys, LANES)]
            m = jnp.maximum(jnp.max(functools.reduce(jnp.maximum, slabs), axis=1, keepdims=True), sk)
            p = jnp.exp(s - m)
            pslabs = [p[:, c:c + LANES] for c in range(0, n_keys, LANES)]
            den = jnp.sum(functools.reduce(jnp.add, pslabs), axis=1, keepdims=True) + jnp.exp(sk - m)
            p_ref[r0:r0 + SOFTMAX_ROWS, :] = p.astype(BF16)
            invs.append(1.0 / den)
        o = None
        c0 = 0
        for k, v, _ in parts:
            nk = k.shape[0]
            d = jnp.dot(p_ref[:, c0:c0 + nk], v[:, hs], preferred_element_type=F32)
            o = d if o is None else o + d
            c0 += nk
        o = o * jnp.concatenate(invs, axis=0)
        outs += [o[g * Q_BLOCK:(g + 1) * Q_BLOCK] for g in range(GQA_GROUP)]
    return jnp.concatenate(outs, axis=1).astype(BF16)


def _attn_kernel(*refs, windowed, n_pairs):
    if not windowed:
        q_ref, kx_ref, vx_ref, sink_ref, o_ref, s0, p0 = refs
        o_ref[...] = _attend(q_ref, 0, [(kx_ref, vx_ref, None)], sink_ref, s0, p0)
        return
    q_ref, kp_ref, kc_ref, kn_ref, vp_ref, vc_ref, vn_ref, kx_ref, vx_ref, sink_ref, o_ref, s0, p0, s1, p1 = refs
    always = (kx_ref, vx_ref, None)
    j = pl.program_id(1)
    a = lax.broadcasted_iota(jnp.int32, (Q_BLOCK, Q_BLOCK), 0)
    b = lax.broadcasted_iota(jnp.int32, (Q_BLOCK, Q_BLOCK), 1)
    later, earlier = b >= a, b <= a
    lo_k, hi_k = kc_ref.at[0:Q_BLOCK], kc_ref.at[Q_BLOCK:2 * Q_BLOCK]
    lo_v, hi_v = vc_ref.at[0:Q_BLOCK], vc_ref.at[Q_BLOCK:2 * Q_BLOCK]
    parts_a = [(kp_ref, vp_ref, later & (j > 0)), (lo_k, lo_v, None), (hi_k, hi_v, earlier), always]
    parts_b = [(lo_k, lo_v, later), (hi_k, hi_v, None), (kn_ref, vn_ref, earlier & (j < n_pairs - 1)), always]
    o_ref[0:Q_BLOCK, :] = _attend(q_ref, 0, parts_a, sink_ref, s0, p0)
    o_ref[Q_BLOCK:2 * Q_BLOCK, :] = _attend(q_ref, Q_BLOCK, parts_b, sink_ref, s1, p1)


def _attention(q, k, v, kx, vx, sink_col, batch, t, windowed):
    nb = t // Q_BLOCK
    per_step = 2 if windowed else 1
    n_ctx = kx.shape[1]
    n_keys = n_ctx + (3 * Q_BLOCK if windowed else 0)
    qspec = pl.BlockSpec((None, per_step * Q_BLOCK, ATTN_DIM), lambda b, j: (b, j, 0))
    xspec = pl.BlockSpec((None, n_ctx, KV_DIM), lambda b, j: (b, 0, 0))
    sspec = pl.BlockSpec(memory_space=pltpu.SMEM)
    if windowed:
        prev = pl.BlockSpec((None, Q_BLOCK, KV_DIM), lambda b, j: (b, jnp.maximum(2 * j - 1, 0), 0))
        pair = pl.BlockSpec((None, 2 * Q_BLOCK, KV_DIM), lambda b, j: (b, j, 0))
        nxt = pl.BlockSpec((None, Q_BLOCK, KV_DIM), lambda b, j: (b, jnp.minimum(2 * j + 2, nb - 1), 0))
        in_specs = [qspec, prev, pair, nxt, prev, pair, nxt, xspec, xspec, sspec]
        args = [q, k, k, k, v, v, v, kx, vx, sink_col]
    else:
        in_specs = [qspec, xspec, xspec, sspec]
        args = [q, kx, vx, sink_col]
    scores = pltpu.VMEM((GQA_GROUP * Q_BLOCK, n_keys), F32)
    probs = pltpu.VMEM((GQA_GROUP * Q_BLOCK, n_keys), BF16)
    return pl.pallas_call(
        functools.partial(_attn_kernel, windowed=windowed, n_pairs=nb // 2),
        grid=(batch, nb // per_step),
        in_specs=in_specs, out_specs=qspec,
        out_shape=jax.ShapeDtypeStruct((batch, t, ATTN_DIM), BF16),
        scratch_shapes=[scores, probs] * per_step,
        compiler_params=_cparams(("parallel", "parallel")),
        name="attention_latent" if windowed else "attention_context",
    )(*args)


def _outproj_kernel(xc_ref, mfc_ref, mac_ref, xl_ref, mfl_ref, mal_ref,
                    wof_ref, woa_ref, g1_ref, sh_ref, sc_ref, n2_ref, wr_ref,
                    x1_ref, h_ref, lg_ref, *, ctx_tiles):
    def body(x_ref, mf_ref, ma_ref):
        o = jnp.dot(mf_ref[...], wof_ref[...], preferred_element_type=F32)
        o = o + jnp.dot(ma_ref[...], woa_ref[...], preferred_element_type=F32)
        x1 = x_ref[...] + g1_ref[...] * o
        x1_ref[...] = x1
        h = (_rms(x1, n2_ref[...]) * (1.0 + sc_ref[...]) + sh_ref[...]).astype(BF16)
        h_ref[...] = h
        lg_ref[...] = lax.dot_general(wr_ref[...], h, (((1,), (1,)), ((), ())), preferred_element_type=F32)

    is_ctx = pl.program_id(0) < ctx_tiles
    pl.when(is_ctx)(lambda: body(xc_ref, mfc_ref, mac_ref))
    pl.when(jnp.logical_not(is_ctx))(lambda: body(xl_ref, mfl_ref, mal_ref))


def _outproj(ctx, lat, wo_f, wo_a, mod, norm2, w_router_t, tm, row_fn):
    n_c, n_l = ctx[0].shape[0], lat[0].shape[0]
    ctx_tiles = n_c // tm
    n = n_c + n_l
    ctok = lambda w: pl.BlockSpec((tm, w), lambda i: (jnp.minimum(i, ctx_tiles - 1), 0))
    ltok = lambda w: pl.BlockSpec((tm, w), lambda i: (jnp.maximum(i - ctx_tiles, 0), 0))
    tok = lambda w: pl.BlockSpec((tm, w), lambda i: (i, 0))
    full = lambda a: pl.BlockSpec(a.shape, lambda i: (0,) * a.ndim)
    widths = (D_MODEL, F_DIM, ATTN_DIM)
    return pl.pallas_call(
        functools.partial(_outproj_kernel, ctx_tiles=ctx_tiles),
        grid=(n // tm,),
        in_specs=[ctok(w) for w in widths] + [ltok(w) for w in widths] + [
            full(wo_f), full(wo_a), _mod_spec(2, row_fn), _mod_spec(3, row_fn), _mod_spec(4, row_fn),
            full(norm2), full(w_router_t)],
        out_specs=[tok(D_MODEL), tok(D_MODEL), pl.BlockSpec((N_EXPERTS, tm), lambda i: (0, i))],
        out_shape=[jax.ShapeDtypeStruct((n, D_MODEL), F32), jax.ShapeDtypeStruct((n, D_MODEL), BF16),
                   jax.ShapeDtypeStruct((N_EXPERTS, n), F32)],
        compiler_params=_cparams(("parallel",)),
        name="outproj",
    )(*ctx, *lat, wo_f, wo_a, mod, mod, mod, norm2, w_router_t)


def _top_rows(vals, k):
    n = vals.shape[0]
    idx = lax.broadcasted_iota(jnp.int32, vals.shape, 0)
    picked = jnp.zeros(vals.shape, F32)
    for _ in range(k):
        best = jnp.max(vals, axis=0, keepdims=True)
        first = jnp.min(jnp.where(vals == best, idx, n), axis=0, keepdims=True)
        hit = idx == first
        picked = jnp.where(hit, 1.0, picked)
        vals = jnp.where(hit, -jnp.inf, vals)
    return picked


def _route_chunk(logits, bias, lo, up):
    s = jax.nn.sigmoid(logits)
    sc = s + bias
    tn = s.shape[1]
    per = N_EXPERTS // N_EXPERT_GROUPS
    g3 = sc.reshape(N_EXPERT_GROUPS, per, tn)
    member = lax.broadcasted_iota(jnp.int32, g3.shape, 1)
    m1 = jnp.max(g3, axis=1, keepdims=True)
    first = jnp.min(jnp.where(g3 == m1, member, per), axis=1, keepdims=True)
    m2 = jnp.max(jnp.where(member == first, -jnp.inf, g3), axis=1, keepdims=True)
    gscore = (m1 + m2).reshape(N_EXPERT_GROUPS, tn)
    gsel = _top_rows(gscore, TOPK_GROUPS)
    emask = jnp.broadcast_to(gsel.reshape(N_EXPERT_GROUPS, 1, tn), g3.shape).reshape(N_EXPERTS, tn)
    masked = jnp.where(emask > 0.5, sc, -jnp.inf)
    self = _top_rows(masked, TOP_K)
    sel = self > 0.5
    w = jnp.where(sel, s, 0.0)
    gate = w / jnp.sum(w, axis=0, keepdims=True) * ROUTED_SCALE

    selb = self.astype(BF16)
    cnt = jnp.sum(self, axis=1, keepdims=True)
    pad = jnp.maximum(jnp.floor((cnt + (RUN_ALIGN - 1)) * (1.0 / RUN_ALIGN)), 1.0) * RUN_ALIGN
    soff = jnp.dot(lo, jnp.broadcast_to(pad, (N_EXPERTS, LANES)).astype(BF16), preferred_element_type=F32)[:, :1]
    rank = jnp.dot(selb, up, preferred_element_type=F32)
    kidx = jnp.dot(lo, selb, preferred_element_type=F32)
    pos_e = jnp.where(sel, soff + rank, 0.0)
    rows_p, rows_g = [], []
    for k in range(TOP_K):
        m = kidx == k
        rows_p.append(jnp.sum(jnp.where(m, pos_e, 0.0), axis=0, keepdims=True))
        rows_g.append(jnp.sum(jnp.where(m, gate, 0.0), axis=0, keepdims=True))
    return jnp.concatenate(rows_p, axis=0).astype(jnp.int32), jnp.concatenate(rows_g, axis=0), pad.astype(jnp.int32)


def _route_kernel(lg_ref, bias_ref, lo_ref, up_ref, pos_ref, gs_ref, cnt_ref, *, tc):
    for j in range(ROUTE_CHUNKS):
        cols = slice(j * tc, (j + 1) * tc)
        pos, gates, pad = _route_chunk(lg_ref[:, cols], bias_ref[...], lo_ref[...], up_ref[...])
        pos_ref[:, cols] = pos
        gs_ref[:, cols] = gates
        cnt_ref[j] = pad


def _route(logits_t, bias_col, tc):
    n = logits_t.shape[1]
    lo = jnp.asarray(np.tril(np.ones((N_EXPERTS, N_EXPERTS), np.float32), -1)).astype(BF16)
    up = jnp.asarray(np.triu(np.ones((tc, tc), np.float32), 1)).astype(BF16)
    tn = tc * ROUTE_CHUNKS
    slot = pl.BlockSpec((TOP_K, tn), lambda i: (0, i))
    return pl.pallas_call(
        functools.partial(_route_kernel, tc=tc),
        grid=(n // tn,),
        in_specs=[pl.BlockSpec((N_EXPERTS, tn), lambda i: (0, i)),
                  pl.BlockSpec((N_EXPERTS, 1), lambda i: (0, 0)),
                  pl.BlockSpec(lo.shape, lambda i: (0, 0)),
                  pl.BlockSpec(up.shape, lambda i: (0, 0))],
        out_specs=[slot, slot, pl.BlockSpec((ROUTE_CHUNKS, N_EXPERTS, 1), lambda i: (i, 0, 0))],
        out_shape=[jax.ShapeDtypeStruct((TOP_K, n), jnp.int32), jax.ShapeDtypeStruct((TOP_K, n), F32),
                   jax.ShapeDtypeStruct((n // tc, N_EXPERTS, 1), jnp.int32)],
        compiler_params=_cparams(("parallel",)),
        name="route",
    )(logits_t, bias_col, lo, up)


def _aligned(x):
    return pl.multiple_of(x, RUN_ALIGN)


def _block_rows(tc):
    return lax.broadcasted_iota(jnp.int32, (PERM_BLOCK, tc), 0).astype(F32).astype(BF16)


def _block_relative(pos, r0):
    return (pos - r0).astype(F32).astype(BF16)


def _gather_kernel(src_ref, cnt_ref, dst_ref, tot_ref,
                   pos_ref, x_ref, xs_ref, z_ref, sem, *, tc, n_chunks):
    c = pl.program_id(0)
    slot = c % 2

    def wait_chunk(ci, s):
        n = _aligned(tot_ref[ci])
        pltpu.make_async_copy(z_ref.at[s, pl.ds(0, n)], xs_ref.at[pl.ds(0, n)], sem.at[s]).wait()

    def start_runs(ci, s, first, count):
        for j in range(count):
            i = ci * N_EXPERTS + first + j
            n = _aligned(cnt_ref[i])
            pltpu.make_async_copy(z_ref.at[s, pl.ds(_aligned(src_ref[i]), n)],
                                  xs_ref.at[pl.ds(_aligned(dst_ref[i]), n)], sem.at[s]).start()

    @pl.when(c >= 2)
    def _():
        wait_chunk(c - 2, slot)

    rows = _block_rows(tc)

    def permute_block(b):
        r0 = _block_start(b)
        rel = _block_relative(pos_ref[...], r0)
        onehot = jnp.zeros((PERM_BLOCK, tc), BF16)
        for k in range(TOP_K):
            onehot = jnp.where(rel[k:k + 1, :] == rows, jnp.ones((), BF16), onehot)
        z_ref[slot, pl.ds(r0, PERM_BLOCK), :] = jnp.dot(
            onehot, x_ref[...], preferred_element_type=F32).astype(BF16)

    for b in range(_sure_blocks(tc)):
        permute_block(b)

    def tail_block(b, carry):
        permute_block(b)
        return carry
    lax.fori_loop(_sure_blocks(tc), _perm_blocks(tot_ref[c]), tail_block, 0)

    def e_body(e8, carry):
        start_runs(c, slot, e8 * DMA_UNROLL, DMA_UNROLL)
        return carry
    lax.fori_loop(0, N_EXPERTS // DMA_UNROLL, e_body, 0)

    @pl.when(c == n_chunks - 1)
    def _():
        wait_chunk(c, slot)
        if n_chunks > 1:
            wait_chunk(c - 1, 1 - slot)


def _perm_blocks(rows):
    return (rows + PERM_BLOCK - 1) // PERM_BLOCK


def _sure_blocks(tc):
    return min(TOP_K * tc // PERM_BLOCK + STRAIGHT_EXTRA_BLOCKS, _local_cap(tc) // PERM_BLOCK)


def _block_start(b):
    return b * PERM_BLOCK if isinstance(b, int) else pl.multiple_of(b * PERM_BLOCK, PERM_BLOCK)


def _chunk_rows_max(tc):
    return TOP_K * tc + N_EXPERTS * RUN_ALIGN


def _local_cap(tc):
    return -(-_chunk_rows_max(tc) // PERM_BLOCK) * PERM_BLOCK


def _gather(tabs, pos, h2, r_max, tc):
    n = h2.shape[0]
    n_chunks = n // tc
    return pl.pallas_call(
        functools.partial(_gather_kernel, tc=tc, n_chunks=n_chunks),
        grid_spec=pltpu.PrefetchScalarGridSpec(
            num_scalar_prefetch=4, grid=(n_chunks,),
            in_specs=[pl.BlockSpec((TOP_K, tc), lambda c, *_: (0, c)),
                      pl.BlockSpec((tc, D_MODEL), lambda c, *_: (c, 0))],
            out_specs=pl.BlockSpec(memory_space=pl.ANY),
            scratch_shapes=[pltpu.VMEM((2, _local_cap(tc), D_MODEL), BF16),
                            pltpu.SemaphoreType.DMA((2,))]),
        out_shape=jax.ShapeDtypeStruct((r_max, D_MODEL), BF16),
        compiler_params=_cparams(("arbitrary",), manual_dma=True),
        name="moe_gather",
    )(tabs["src"], tabs["cnt"], tabs["dst"], tabs["tot"], pos, h2)


def _ffn_kernel(first_ref, ntile_ref, tstart_ref, trows_ref, total_ref, xs_ref, wg_ref, wu_ref, wd_ref, ys_ref,
                xbuf, ybuf, wgb, wub, wdb, sem_in, sem_out):
    e = pl.program_id(0)
    total = total_ref[0]

    def in_copy(g):
        s = g % FFN_IN_SLOTS
        n = _aligned(trows_ref[g])
        return pltpu.make_async_copy(xs_ref.at[pl.ds(_aligned(tstart_ref[g]), n)],
                                     xbuf.at[s, pl.ds(0, n)], sem_in.at[s])

    def out_copy(g):
        s = g % 2
        n = _aligned(trows_ref[g])
        return pltpu.make_async_copy(ybuf.at[s, pl.ds(0, n)],
                                     ys_ref.at[pl.ds(_aligned(tstart_ref[g]), n)], sem_out.at[s])

    @pl.when(e == 0)
    def _():
        xbuf[...] = jnp.zeros(xbuf.shape, xbuf.dtype)
        for g in range(FFN_LOOKAHEAD):
            @pl.when(g < total)
            def _():
                in_copy(g).start()

    wgb[...] = wg_ref[...].astype(BF16)
    wub[...] = wu_ref[...].astype(BF16)
    wdb[...] = wd_ref[...].astype(BF16)

    def body(g, carry):
        @pl.when(g + FFN_LOOKAHEAD < total)
        def _():
            in_copy(g + FFN_LOOKAHEAD).start()
        in_copy(g).wait()

        @pl.when(g >= 2)
        def _():
            out_copy(g - 2).wait()
        x = xbuf[g % FFN_IN_SLOTS]
        a = jnp.dot(x, wgb[...], preferred_element_type=F32)
        u = jnp.dot(x, wub[...], preferred_element_type=F32)
        h = (a * jax.nn.sigmoid(a) * u).astype(BF16)
        ybuf[g % 2] = jnp.dot(h, wdb[...], preferred_element_type=F32).astype(BF16)
        out_copy(g).start()
        return carry
    lax.fori_loop(first_ref[e], first_ref[e] + ntile_ref[e], body, 0)

    @pl.when(e == pl.num_programs(0) - 1)
    def _():
        @pl.when(total >= 2)
        def _():
            out_copy(total - 2).wait()
        out_copy(total - 1).wait()


def _ffn(tabs, xs, wg, wu, wd):
    ew = lambda a: pl.BlockSpec((None,) + a.shape[1:], lambda e, *_: (e, 0, 0))
    return pl.pallas_call(
        _ffn_kernel,
        grid_spec=pltpu.PrefetchScalarGridSpec(
            num_scalar_prefetch=5, grid=(N_EXPERTS,),
            in_specs=[pl.BlockSpec(memory_space=pl.ANY), ew(wg), ew(wu), ew(wd)],
            out_specs=pl.BlockSpec(memory_space=pl.ANY),
            scratch_shapes=[pltpu.VMEM((FFN_IN_SLOTS, FFN_BLOCK, D_MODEL), BF16),
                            pltpu.VMEM((2, FFN_BLOCK, D_MODEL), BF16),
                            pltpu.VMEM(wg.shape[1:], BF16), pltpu.VMEM(wu.shape[1:], BF16),
                            pltpu.VMEM(wd.shape[1:], BF16),
                            pltpu.SemaphoreType.DMA((FFN_IN_SLOTS,)), pltpu.SemaphoreType.DMA((2,))]),
        out_shape=jax.ShapeDtypeStruct(xs.shape, BF16),
        compiler_params=_cparams(("arbitrary",), manual_dma=True),
        name="moe_ffn",
    )(tabs["first_tile"], tabs["n_tiles"], tabs["tile_start"], tabs["tile_rows"], tabs["total_tiles"],
      xs, wg, wu, wd)


def _combine_kernel(src_ref, cnt_ref, dst_ref, tot_ref,
                    pos_ref, gs_ref, x1_ref, h_ref, sg_ref, su_ref, sd_ref, g2_ref, nf_ref, ys_ref,
                    yc_ref, yl_ref, ybuf, acc_ref, sem, *, tc, n_chunks, ctx_chunks):
    c = pl.program_id(0)
    slot = c % 2

    def issue(ci, s):
        def e_body(e8, carry):
            for j in range(DMA_UNROLL):
                i = ci * N_EXPERTS + e8 * DMA_UNROLL + j
                n = _aligned(cnt_ref[i])
                pltpu.make_async_copy(ys_ref.at[pl.ds(_aligned(dst_ref[i]), n)],
                                      ybuf.at[s, pl.ds(_aligned(src_ref[i]), n)], sem.at[s]).start()
            return carry
        lax.fori_loop(0, N_EXPERTS // DMA_UNROLL, e_body, 0)

    @pl.when(c == 0)
    def _():
        ybuf[...] = jnp.zeros(ybuf.shape, ybuf.dtype)
        issue(0, 0)

    @pl.when(c + 1 < n_chunks)
    def _():
        issue(c + 1, 1 - slot)

    h = h_ref[...]
    a = jnp.dot(h, sg_ref[...], preferred_element_type=F32)
    u = jnp.dot(h, su_ref[...], preferred_element_type=F32)
    acc_ref[...] = jnp.dot((a * jax.nn.sigmoid(a) * u).astype(BF16), sd_ref[...], preferred_element_type=F32)

    n = _aligned(tot_ref[c])
    pltpu.make_async_copy(ys_ref.at[pl.ds(0, n)], ybuf.at[slot, pl.ds(0, n)], sem.at[slot]).wait()

    tn = (((0,), (0,)), ((), ()))
    rows = _block_rows(tc)
    gates = gs_ref[...].astype(BF16)

    def gate_block(b):
        rel = _block_relative(pos_ref[...], _block_start(b))
        wt = jnp.zeros((PERM_BLOCK, tc), BF16)
        for k in range(TOP_K):
            wt = jnp.where(rel[k:k + 1, :] == rows, gates[k:k + 1, :], wt)
        return wt

    sure_rows = _sure_blocks(tc) * PERM_BLOCK
    wt_sure = jnp.concatenate([gate_block(b) for b in range(_sure_blocks(tc))], axis=0)
    acc_ref[...] += lax.dot_general(wt_sure, ybuf[slot, pl.ds(0, sure_rows), :], tn, preferred_element_type=F32)

    def tail_block(b, carry):
        acc_ref[...] += lax.dot_general(gate_block(b), ybuf[slot, pl.ds(_block_start(b), PERM_BLOCK), :], tn,
                                        preferred_element_type=F32)
        return carry
    lax.fori_loop(_sure_blocks(tc), _perm_blocks(tot_ref[c]), tail_block, 0)

    x2 = x1_ref[...] + g2_ref[...] * acc_ref[...]
    y = _rms(x2, nf_ref[...])

    @pl.when(c < ctx_chunks)
    def _():
        yc_ref[...] = y

    @pl.when(c >= ctx_chunks)
    def _():
        yl_ref[...] = y


def _combine(tabs, pos, gslot, x1, h2, sg, su, sd, mod, norm_f, ys, tc, row_fn, n_ctx):
    n = x1.shape[0]
    n_chunks = n // tc
    ctx_chunks = n_ctx // tc
    tok = lambda w: pl.BlockSpec((tc, w), lambda c, *_: (c, 0))
    slot = pl.BlockSpec((TOP_K, tc), lambda c, *_: (0, c))
    full = lambda a: pl.BlockSpec(a.shape, lambda c, *_: (0,) * a.ndim)
    return pl.pallas_call(
        functools.partial(_combine_kernel, tc=tc, n_chunks=n_chunks, ctx_chunks=ctx_chunks),
        grid_spec=pltpu.PrefetchScalarGridSpec(
            num_scalar_prefetch=4, grid=(n_chunks,),
            in_specs=[slot, slot, tok(D_MODEL), tok(D_MODEL), full(sg), full(su), full(sd),
                      _mod_spec(5, lambda c, *_: row_fn(c)), full(norm_f), pl.BlockSpec(memory_space=pl.ANY)],
            out_specs=[pl.BlockSpec((tc, D_MODEL), lambda c, *_: (jnp.minimum(c, ctx_chunks - 1), 0)),
                       pl.BlockSpec((tc, D_MODEL), lambda c, *_: (jnp.maximum(c - ctx_chunks, 0), 0))],
            scratch_shapes=[pltpu.VMEM((2, _local_cap(tc), D_MODEL), BF16),
                            pltpu.VMEM((tc, D_MODEL), F32),
                            pltpu.SemaphoreType.DMA((2,))]),
        out_shape=[jax.ShapeDtypeStruct((n_ctx, D_MODEL), F32), jax.ShapeDtypeStruct((n - n_ctx, D_MODEL), F32)],
        compiler_params=_cparams(("arbitrary",), manual_dma=True),
        name="moe_combine",
    )(tabs["src"], tabs["cnt"], tabs["dst"], tabs["tot"], pos, gslot, x1, h2, sg, su, sd, mod, norm_f, ys)


def _dispatch_tables(pad_cnt, r_max):
    i32 = jnp.int32
    src = jnp.cumsum(pad_cnt, axis=1) - pad_cnt
    tot = jnp.sum(pad_cnt, axis=1)
    e_rows = jnp.sum(pad_cnt, axis=0)
    e_start = jnp.cumsum(e_rows) - e_rows
    dst = e_start[None, :] + jnp.cumsum(pad_cnt, axis=0) - pad_cnt
    n_tiles = (e_rows + FFN_BLOCK - 1) // FFN_BLOCK
    tile_end = jnp.cumsum(n_tiles)
    first_tile = tile_end - n_tiles
    g = jnp.arange(r_max // FFN_BLOCK + N_EXPERTS)
    before = tile_end[None, :] <= g[:, None]
    owner = jnp.concatenate([jnp.ones_like(before[:, :1]), before[:, :-1]], axis=1) & ~before
    offset = (g - jnp.sum(jnp.where(before, n_tiles, 0), axis=1)) * FFN_BLOCK
    tile_start = jnp.clip(jnp.sum(jnp.where(before, e_rows, 0), axis=1) + offset, 0, r_max - FFN_BLOCK)
    tile_rows = jnp.clip(jnp.sum(jnp.where(owner, e_rows, 0), axis=1) - offset, RUN_ALIGN, FFN_BLOCK)
    return {"src": src.reshape(-1).astype(i32), "cnt": pad_cnt.reshape(-1).astype(i32),
            "dst": dst.reshape(-1).astype(i32), "tot": tot.astype(i32),
            "first_tile": first_tile.astype(i32), "n_tiles": n_tiles.astype(i32),
            "tile_start": tile_start.astype(i32), "tile_rows": tile_rows.astype(i32),
            "total_tiles": tile_end[-1:].astype(i32)}


def _moe(x1, h2, logits_t, bias_col, wg, wu, wd, sg, su, sd, mod, norm_f, row_fn, n_ctx):
    n = x1.shape[0]
    tc = MOE_CHUNK
    pos, gslot, pad_cnt = _route(logits_t, bias_col, tc)
    r_max = (n // tc) * _chunk_rows_max(tc)
    tabs = _dispatch_tables(pad_cnt[:, :, 0], r_max)
    xs = _gather(tabs, pos, h2, r_max, tc)
    ys = _ffn(tabs, xs, wg, wu, wd)
    return _combine(tabs, pos, gslot, x1, h2, sg, su, sd, mod, norm_f, ys, tc, row_fn, n_ctx)


def _channel_dft():
    ang = 2.0 * np.pi * ((np.arange(F_HD)[:, None] * np.arange(F_HD)[None, :]) % F_HD) / F_HD
    eye = np.eye(F_GROUPS)
    return np.concatenate([np.kron(eye, np.cos(ang)), np.kron(eye, -np.sin(ang))], axis=1).astype(np.float32)


def _rope_tables(t):
    pos = np.arange(t)
    row, col = pos // GRID_W, pos % GRID_W
    n_freq = HEAD_DIM // 4
    inv = ROPE_THETA ** (-np.arange(n_freq, dtype=np.float64) / n_freq)
    lane = np.arange(LANES)
    hd = lane % HEAD_DIM
    within = hd % (HEAD_DIM // 2)
    freq = within % n_freq
    first = within < n_freq
    p = np.where((hd < HEAD_DIM // 2)[None, :], row[:, None], col[:, None]).astype(np.float64)
    ang = p * inv[freq][None, :]
    cos, sin = np.cos(ang), np.sin(ang)
    sin_up = np.where(first[None, :], -sin, 0.0)
    sin_dn = np.where(first[None, :], 0.0, sin)
    return [jnp.asarray(a.astype(np.float32)) for a in (cos, sin_up, sin_dn)]


def _block_diag_pairs(wf):
    per = LANES // F_HD
    z = jnp.zeros((F_HD, F_HD), wf.dtype)
    blocks = []
    for c in range(F_GROUPS // per):
        rows = [jnp.concatenate([wf[c * per + r] if r == cc else z for cc in range(per)], axis=1) for r in range(per)]
        blocks.append(jnp.concatenate(rows, axis=0))
    return jnp.stack(blocks)


def kernel(x_prompt, x_sample, cache_k, cache_v, c, c_ctx, w_mod, b_mod, norm1, w_in, w_fourier, sink,
           w_out, norm2, w_router, router_bias, w_gate, w_up, w_down, ws_gate, ws_up, ws_down, norm_f):
    nb_ctx, t_ctx, _ = x_prompt.shape
    nb_lat, t_lat, _ = x_sample.shape
    l = 0
    cvec = jnp.concatenate([c_ctx[None, :], c, jnp.zeros((MOD_ROWS - 1 - nb_lat, D_MODEL), F32)], axis=0)
    mod = _modulation(cvec, w_mod[l], b_mod[l]).reshape(MOD_ROWS, 6, 1, D_MODEL)

    w_in_b = w_in[l].astype(BF16)
    dft_c = jnp.asarray(_channel_dft()).astype(BF16)
    wf_bd = _block_diag_pairs(w_fourier[l]).astype(BF16)
    wo_f = w_out[l][:F_DIM].astype(BF16)
    wo_a = w_out[l][F_DIM:].astype(BF16)
    w_router_t = w_router[l].T.astype(BF16)
    bias_col = router_bias[l].reshape(N_EXPERTS, 1)
    n1 = norm1[l].reshape(1, D_MODEL)
    n2 = norm2[l].reshape(1, D_MODEL)
    nf = norm_f.reshape(1, D_MODEL)
    sink_col = sink[l]
    wg, wu, wd = w_gate[l], w_up[l], w_down[l]
    sg, su, sd = ws_gate[l].astype(BF16), ws_up[l].astype(BF16), ws_down[l].astype(BF16)

    def mixers(x, batch, t, latent, kx, vx, t1, t2, tm):
        n = batch * t
        x2d = x.reshape(n, D_MODEL)
        tiles = t // tm
        row_fn = (lambda i: 1 + i // tiles) if latent else (lambda i: 0)
        rope_tabs = _rope_tables(t) if latent else None
        outs = _project(x2d, mod, n1, w_in_b, dft_c, rope_tabs, tm, tiles, row_fn)
        wr, wi, q, k, v = outs[:5]
        mixf = _fourier(wr, wi, wf_bd, batch, t, t1, t2)
        q3, k3, v3 = (a.reshape(batch, t, -1) for a in (q, k, v))
        if latent:
            mixa = _attention(q3, k3, v3, kx, vx, sink_col, batch, t, True)
        else:
            mixa = _attention(q3, None, None, k3, v3, sink_col, batch, t, False)
        return (x2d, mixf.reshape(n, F_DIM), mixa.reshape(n, ATTN_DIM)), outs[5:]

    ctx, (kf, vf) = mixers(x_prompt, nb_ctx, t_ctx, False, None, None, 16, 16, 256)
    kx = cache_k[:, l].reshape(nb_lat, -1, KV_DIM).astype(BF16)
    vx = cache_v[:, l].reshape(nb_lat, -1, KV_DIM).astype(BF16)
    lat, _ = mixers(x_sample, nb_lat, t_lat, True, kx, vx, 64, 64, 512)

    n_ctx = nb_ctx * t_ctx

    def tile_row(tile):
        ctx_tiles, lat_tiles = n_ctx // tile, t_lat // tile
        return lambda i: jnp.where(i < ctx_tiles, 0, 1 + (i - ctx_tiles) // lat_tiles)

    x1, h2, logits_t = _outproj(ctx, lat, wo_f, wo_a, mod, n2, w_router_t, OUTPROJ_TILE, tile_row(OUTPROJ_TILE))
    y_prompt, y_sample = _moe(x1, h2, logits_t, bias_col, wg, wu, wd, sg, su, sd, mod, nf, tile_row(MOE_CHUNK),
                              n_ctx)
    new_k = kf.reshape(nb_ctx, 1, t_ctx, N_KV_HEADS, HEAD_DIM)
    new_v = vf.reshape(nb_ctx, 1, t_ctx, N_KV_HEADS, HEAD_DIM)
    return (y_prompt.reshape(x_prompt.shape), y_sample.reshape(x_sample.shape), new_k, new_v)
```

```python
import functools

import numpy as np
import jax
import jax.numpy as jnp
from jax import lax
from jax.experimental import pallas as pl
from jax.experimental.pallas import tpu as pltpu

F32 = jnp.float32
BF16 = jnp.bfloat16

D_MODEL = 1024
GRID_W = 64
HEAD_DIM = 64
N_HEADS = 12
N_KV_HEADS = 4
GQA_GROUP = N_HEADS // N_KV_HEADS
ATTN_DIM = N_HEADS * HEAD_DIM
KV_DIM = N_KV_HEADS * HEAD_DIM
F_GROUPS = 4
F_HD = 64
F_DIM = F_GROUPS * F_HD
IN_DIM = F_DIM + ATTN_DIM + 2 * KV_DIM
WINDOW = 128
Q_BLOCK = 128
ROPE_THETA = 10000.0
N_EXPERTS = 64
TOP_K = 8
N_EXPERT_GROUPS = 8
TOPK_GROUPS = 4
EXPERT_DIM = 256
SHARED_DIM = 256
ROUTED_SCALE = 2.5
EPS = 1e-6

LANES = 128
MOD_ROWS = 8
RUN_ALIGN = 16
PERM_BLOCK = 256
FFN_BLOCK = 512
FFN_LOOKAHEAD = 3
FFN_IN_SLOTS = FFN_LOOKAHEAD + 1
MOE_CHUNK = 256
OUTPROJ_TILE = 512
ROUTE_CHUNKS = 4
SOFTMAX_ROWS = 32
ATTN_BLOCKS = 4
DMA_UNROLL = 8
STRAIGHT_EXTRA_BLOCKS = 2
VMEM_LIMIT = 56 * 1024 * 1024


def _cparams(sem, manual_dma=False):
    return pltpu.CompilerParams(dimension_semantics=sem, vmem_limit_bytes=VMEM_LIMIT,
                                disable_bounds_checks=manual_dma)


def _bdot(a, b):
    return jnp.dot(a.astype(BF16), b.astype(BF16), preferred_element_type=F32)


def _rms(x, g):
    return x * lax.rsqrt(jnp.mean(x * x, axis=-1, keepdims=True) + EPS) * g


def _mod_kernel(c_ref, w_ref, b_ref, o_ref):
    c = c_ref[...]
    a = c * jax.nn.sigmoid(c)
    o_ref[...] = _bdot(a, w_ref[...]) + b_ref[...]


def _modulation(cvec, w_mod, b_mod):
    n = w_mod.shape[1]
    tn = 1024
    return pl.pallas_call(
        _mod_kernel,
        grid=(n // tn,),
        in_specs=[pl.BlockSpec((MOD_ROWS, D_MODEL), lambda j: (0, 0)),
                  pl.BlockSpec((D_MODEL, tn), lambda j: (0, j)),
                  pl.BlockSpec((1, tn), lambda j: (0, j))],
        out_specs=pl.BlockSpec((MOD_ROWS, tn), lambda j: (0, j)),
        out_shape=jax.ShapeDtypeStruct((MOD_ROWS, n), F32),
        compiler_params=_cparams(("parallel",)),
        name="modulation",
    )(cvec, w_mod, b_mod.reshape(1, n))


def _mod_spec(piece, row_fn):
    return pl.BlockSpec((None, None, 1, D_MODEL), lambda *idx: (row_fn(*idx), piece, 0, 0))


def _rope(x, cos, sin_up, sin_dn):
    outs = []
    for j in range(x.shape[1] // LANES):
        xj = x[:, j * LANES:(j + 1) * LANES]
        up = pltpu.roll(xj, LANES - 16, axis=1)
        dn = pltpu.roll(xj, 16, axis=1)
        outs.append(xj * cos + up * sin_up + dn * sin_dn)
    return jnp.concatenate(outs, axis=1)


def _proj_kernel(*refs, latent):
    if latent:
        (x_ref, sh_ref, sc_ref, g_ref, w_ref, dft_ref, cos_ref, sup_ref, sdn_ref,
         wr_ref, wi_ref, q_ref, k_ref, v_ref) = refs
    else:
        (x_ref, sh_ref, sc_ref, g_ref, w_ref, dft_ref,
         wr_ref, wi_ref, q_ref, k_ref, v_ref, kf_ref, vf_ref) = refs
    x = x_ref[...]
    h = _rms(x, g_ref[...]) * (1.0 + sc_ref[...]) + sh_ref[...]
    p = _bdot(h, w_ref[...])
    u = p[:, :F_DIM]
    q = p[:, F_DIM:F_DIM + ATTN_DIM]
    k = p[:, F_DIM + ATTN_DIM:F_DIM + ATTN_DIM + KV_DIM]
    v = p[:, F_DIM + ATTN_DIM + KV_DIM:]
    w = _bdot(u, dft_ref[...])
    wr_ref[...] = w[:, :F_DIM]
    wi_ref[...] = w[:, F_DIM:]
    if latent:
        cos, sup, sdn = cos_ref[...], sup_ref[...], sdn_ref[...]
        q = _rope(q, cos, sup, sdn)
        k = _rope(k, cos, sup, sdn)
    else:
        kf_ref[...] = k
        vf_ref[...] = v
    q_ref[...] = (q * (HEAD_DIM ** -0.5)).astype(BF16)
    k_ref[...] = k.astype(BF16)
    v_ref[...] = v.astype(BF16)


def _project(x2d, mod, norm1, w_in, dft_c, rope_tabs, tm, tiles_per_batch, row_fn):
    n = x2d.shape[0]
    latent = rope_tabs is not None
    tok = lambda w: pl.BlockSpec((tm, w), lambda i: (i, 0))
    full = lambda a: pl.BlockSpec(a.shape, lambda i: (0,) * a.ndim)
    in_specs = [tok(D_MODEL), _mod_spec(0, row_fn), _mod_spec(1, row_fn), full(norm1), full(w_in), full(dft_c)]
    args = [x2d, mod, mod, norm1, w_in, dft_c]
    out_specs = [tok(F_DIM), tok(F_DIM), tok(ATTN_DIM), tok(KV_DIM), tok(KV_DIM)]
    out_shape = [jax.ShapeDtypeStruct((n, F_DIM), F32), jax.ShapeDtypeStruct((n, F_DIM), F32),
                 jax.ShapeDtypeStruct((n, ATTN_DIM), BF16), jax.ShapeDtypeStruct((n, KV_DIM), BF16),
                 jax.ShapeDtypeStruct((n, KV_DIM), BF16)]
    if latent:
        pos = pl.BlockSpec((tm, LANES), lambda i: (i % tiles_per_batch, 0))
        in_specs += [pos, pos, pos]
        args += list(rope_tabs)
    else:
        out_specs += [tok(KV_DIM), tok(KV_DIM)]
        out_shape += [jax.ShapeDtypeStruct((n, KV_DIM), F32), jax.ShapeDtypeStruct((n, KV_DIM), F32)]
    return pl.pallas_call(
        functools.partial(_proj_kernel, latent=latent),
        grid=(n // tm,),
        in_specs=in_specs, out_specs=out_specs, out_shape=out_shape,
        compiler_params=_cparams(("parallel",)),
        name="project_latent" if latent else "project_context",
    )(*args)


def _fourier_kernel(wr_ref, wi_ref, a_ref, b_ref, wf_ref, o_ref, yr_ref, yi_ref, z_ref, *, t1, t2, scale):
    for j in range(t2):
        rows = pl.ds(j, t1, stride=t2)
        xin = jnp.concatenate([wr_ref[rows, :], wi_ref[rows, :]], axis=0)
        y = _bdot(a_ref[j], xin)
        yr_ref[rows, :] = y[:t1]
        yi_ref[rows, :] = y[t1:]
    bm = b_ref[...]
    for k1 in range(t1):
        rows = pl.ds(k1 * t2, t2)
        yin = jnp.concatenate([yr_ref[rows, :], yi_ref[rows, :]], axis=0)
        z_ref[pl.ds(k1, t2, stride=t1), :] = _bdot(bm, yin)
    o_ref[...] = (_bdot(z_ref[...], wf_ref[...]) * scale).astype(BF16)


def _dft_tables(t1, t2):
    t = t1 * t2
    k1 = np.arange(t1)[None, :, None]
    pos = (t2 * np.arange(t1)[None, None, :] + np.arange(t2)[:, None, None])
    ang = 2.0 * np.pi * ((k1 * pos) % t) / t
    c, s = np.cos(ang), np.sin(ang)
    a = np.concatenate([np.concatenate([c, s], axis=2), np.concatenate([-s, c], axis=2)], axis=1)
    ang2 = 2.0 * np.pi * ((np.arange(t2)[:, None] * np.arange(t2)[None, :]) % t2) / t2
    b = np.concatenate([np.cos(ang2), np.sin(ang2)], axis=1)
    return a.astype(np.float32), b.astype(np.float32)


def _fourier(wr, wi, wf_bd, batch, t, t1, t2):
    a_np, b_np = _dft_tables(t1, t2)
    a = jnp.asarray(a_np).astype(BF16)
    b = jnp.asarray(b_np).astype(BF16)
    cw = LANES
    blk = pl.BlockSpec((None, t, cw), lambda bi, ci: (bi, 0, ci))
    return pl.pallas_call(
        functools.partial(_fourier_kernel, t1=t1, t2=t2, scale=float((t * F_HD) ** -0.5)),
        grid=(batch, F_DIM // cw),
        in_specs=[blk, blk,
                  pl.BlockSpec(a.shape, lambda bi, ci: (0, 0, 0)),
                  pl.BlockSpec(b.shape, lambda bi, ci: (0, 0)),
                  pl.BlockSpec((None, cw, cw), lambda bi, ci: (ci, 0, 0))],
        out_specs=blk,
        out_shape=jax.ShapeDtypeStruct((batch, t, F_DIM), BF16),
        scratch_shapes=[pltpu.VMEM((t, cw), F32), pltpu.VMEM((t, cw), F32), pltpu.VMEM((t, cw), F32)],
        compiler_params=_cparams(("parallel", "parallel")),
        name="fourier_%d" % t,
    )(wr.reshape(batch, t, F_DIM), wi.reshape(batch, t, F_DIM), a, b, wf_bd)


def _attend(q_ref, row0, parts, sink_ref, s_ref, p_ref):
    nt = (((1,), (1,)), ((), ()))
    n_keys = sum(k.shape[0] for k, _, _ in parts)
    outs = []
    for kv in range(N_KV_HEADS):
        heads = range(kv * GQA_GROUP, (kv + 1) * GQA_GROUP)
        hs = slice(kv * HEAD_DIM, (kv + 1) * HEAD_DIM)
        qs = jnp.concatenate([q_ref[row0:row0 + Q_BLOCK, h * HEAD_DIM:(h + 1) * HEAD_DIM] for h in heads], axis=0)
        c0 = 0
        for k, _, ok in parts:
            s = lax.dot_general(qs, k[:, hs], nt, preferred_element_type=F32)
            if ok is not None:
                s = jnp.where(jnp.concatenate([ok] * GQA_GROUP, axis=0), s, -jnp.inf)
            s_ref[:, c0:c0 + s.shape[1]] = s
            c0 += s.shape[1]
        invs = []
        for r0 in range(0, GQA_GROUP * Q_BLOCK, SOFTMAX_ROWS):
            s = s_ref[r0:r0 + SOFTMAX_ROWS, :]
            sk = sink_ref[kv * GQA_GROUP + r0 // Q_BLOCK]
            slabs = [s[:, c:c + LANES] for c in range(0, n_keys, LANES)]
            m = jnp.maximum(jnp.max(functools.reduce(jnp.maximum, slabs), axis=1, keepdims=True), sk)
            p = jnp.exp(s - m)
            pslabs = [p[:, c:c + LANES] for c in range(0, n_keys, LANES)]
            den = jnp.sum(functools.reduce(jnp.add, pslabs), axis=1, keepdims=True) + jnp.exp(sk - m)
            p_ref[r0:r0 + SOFTMAX_ROWS, :] = p.astype(BF16)
            invs.append(1.0 / den)
        o = None
        c0 = 0
        for k, v, _ in parts:
            nk = k.shape[0]
            d = jnp.dot(p_ref[:, c0:c0 + nk], v[:, hs], preferred_element_type=F32)
            o = d if o is None else o + d
            c0 += nk
        o = o * jnp.concatenate(invs, axis=0)
        outs += [o[g * Q_BLOCK:(g + 1) * Q_BLOCK] for g in range(GQA_GROUP)]
    return jnp.concatenate(outs, axis=1).astype(BF16)


def _attn_kernel(*refs, windowed, n_steps):
    if not windowed:
        q_ref, kx_ref, vx_ref, sink_ref, o_ref, s0, p0 = refs
        o_ref[...] = _attend(q_ref, 0, [(kx_ref, vx_ref, None)], sink_ref, s0, p0)
        return
    q_ref, kp_ref, kc_ref, kn_ref, vp_ref, vc_ref, vn_ref, kx_ref, vx_ref, sink_ref, o_ref = refs[:11]
    scratch = refs[11:]
    j = pl.program_id(1)
    a = lax.broadcasted_iota(jnp.int32, (Q_BLOCK, Q_BLOCK), 0)
    b = lax.broadcasted_iota(jnp.int32, (Q_BLOCK, Q_BLOCK), 1)
    later, earlier = b >= a, b <= a
    own = [(kc_ref.at[i * Q_BLOCK:(i + 1) * Q_BLOCK], vc_ref.at[i * Q_BLOCK:(i + 1) * Q_BLOCK])
           for i in range(ATTN_BLOCKS)]
    for i in range(ATTN_BLOCKS):
        prev = own[i - 1] + (later,) if i > 0 else (kp_ref, vp_ref, later & (j > 0))
        nxt = own[i + 1] + (earlier,) if i < ATTN_BLOCKS - 1 else (kn_ref, vn_ref, earlier & (j < n_steps - 1))
        parts = [prev, own[i] + (None,), nxt, (kx_ref, vx_ref, None)]
        o_ref[i * Q_BLOCK:(i + 1) * Q_BLOCK, :] = _attend(q_ref, i * Q_BLOCK, parts, sink_ref,
                                                           scratch[2 * i], scratch[2 * i + 1])


def _attention(q, k, v, kx, vx, sink_col, batch, t, windowed):
    nb = t // Q_BLOCK
    per_step = ATTN_BLOCKS if windowed else 1
    n_ctx = kx.shape[1]
    n_keys = n_ctx + (3 * Q_BLOCK if windowed else 0)
    qspec = pl.BlockSpec((None, per_step * Q_BLOCK, ATTN_DIM), lambda b, j: (b, j, 0))
    xspec = pl.BlockSpec((None, n_ctx, KV_DIM), lambda b, j: (b, 0, 0))
    sspec = pl.BlockSpec(memory_space=pltpu.SMEM)
    if windowed:
        prev = pl.BlockSpec((None, Q_BLOCK, KV_DIM), lambda b, j: (b, jnp.maximum(per_step * j - 1, 0), 0))
        own = pl.BlockSpec((None, per_step * Q_BLOCK, KV_DIM), lambda b, j: (b, j, 0))
        nxt = pl.BlockSpec((None, Q_BLOCK, KV_DIM), lambda b, j: (b, jnp.minimum(per_step * (j + 1), nb - 1), 0))
        in_specs = [qspec, prev, own, nxt, prev, own, nxt, xspec, xspec, sspec]
        args = [q, k, k, k, v, v, v, kx, vx, sink_col]
    else:
        in_specs = [qspec, xspec, xspec, sspec]
        args = [q, kx, vx, sink_col]
    scores = pltpu.VMEM((GQA_GROUP * Q_BLOCK, n_keys), F32)
    probs = pltpu.VMEM((GQA_GROUP * Q_BLOCK, n_keys), BF16)
    return pl.pallas_call(
        functools.partial(_attn_kernel, windowed=windowed, n_steps=nb // per_step),
        grid=(batch, nb // per_step),
        in_specs=in_specs, out_specs=qspec,
        out_shape=jax.ShapeDtypeStruct((batch, t, ATTN_DIM), BF16),
        scratch_shapes=[scores, probs] * per_step,
        compiler_params=_cparams(("parallel", "parallel")),
        name="attention_latent" if windowed else "attention_context",
    )(*args)


def _outproj_kernel(xc_ref, mfc_ref, mac_ref, xl_ref, mfl_ref, mal_ref,
                    wof_ref, woa_ref, g1_ref, sh_ref, sc_ref, n2_ref, wr_ref,
                    x1_ref, h_ref, lg_ref, *, ctx_tiles):
    def body(x_ref, mf_ref, ma_ref):
        o = jnp.dot(mf_ref[...], wof_ref[...], preferred_element_type=F32)
        o = o + jnp.dot(ma_ref[...], woa_ref[...], preferred_element_type=F32)
        x1 = x_ref[...] + g1_ref[...] * o
        x1_ref[...] = x1
        h = (_rms(x1, n2_ref[...]) * (1.0 + sc_ref[...]) + sh_ref[...]).astype(BF16)
        h_ref[...] = h
        lg_ref[...] = lax.dot_general(wr_ref[...], h, (((1,), (1,)), ((), ())), preferred_element_type=F32)

    is_ctx = pl.program_id(0) < ctx_tiles
    pl.when(is_ctx)(lambda: body(xc_ref, mfc_ref, mac_ref))
    pl.when(jnp.logical_not(is_ctx))(lambda: body(xl_ref, mfl_ref, mal_ref))


def _outproj(ctx, lat, wo_f, wo_a, mod, norm2, w_router_t, tm, row_fn):
    n_c, n_l = ctx[0].shape[0], lat[0].shape[0]
    ctx_tiles = n_c // tm
    n = n_c + n_l
    ctok = lambda w: pl.BlockSpec((tm, w), lambda i: (jnp.minimum(i, ctx_tiles - 1), 0))
    ltok = lambda w: pl.BlockSpec((tm, w), lambda i: (jnp.maximum(i - ctx_tiles, 0), 0))
    tok = lambda w: pl.BlockSpec((tm, w), lambda i: (i, 0))
    full = lambda a: pl.BlockSpec(a.shape, lambda i: (0,) * a.ndim)
    widths = (D_MODEL, F_DIM, ATTN_DIM)
    return pl.pallas_call(
        functools.partial(_outproj_kernel, ctx_tiles=ctx_tiles),
        grid=(n // tm,),
        in_specs=[ctok(w) for w in widths] + [ltok(w) for w in widths] + [
            full(wo_f), full(wo_a), _mod_spec(2, row_fn), _mod_spec(3, row_fn), _mod_spec(4, row_fn),
            full(norm2), full(w_router_t)],
        out_specs=[tok(D_MODEL), tok(D_MODEL), pl.BlockSpec((N_EXPERTS, tm), lambda i: (0, i))],
        out_shape=[jax.ShapeDtypeStruct((n, D_MODEL), F32), jax.ShapeDtypeStruct((n, D_MODEL), BF16),
                   jax.ShapeDtypeStruct((N_EXPERTS, n), F32)],
        compiler_params=_cparams(("parallel",)),
        name="outproj",
    )(*ctx, *lat, wo_f, wo_a, mod, mod, mod, norm2, w_router_t)


def _top_rows(vals, k):
    n = vals.shape[0]
    idx = lax.broadcasted_iota(jnp.int32, vals.shape, 0)
    picked = jnp.zeros(vals.shape, F32)
    for _ in range(k):
        best = jnp.max(vals, axis=0, keepdims=True)
        first = jnp.min(jnp.where(vals == best, idx, n), axis=0, keepdims=True)
        hit = idx == first
        picked = jnp.where(hit, 1.0, picked)
        vals = jnp.where(hit, -jnp.inf, vals)
    return picked


def _route_chunk(logits, bias, lo, up):
    s = jax.nn.sigmoid(logits)
    sc = s + bias
    tn = s.shape[1]
    per = N_EXPERTS // N_EXPERT_GROUPS
    g3 = sc.reshape(N_EXPERT_GROUPS, per, tn)
    member = lax.broadcasted_iota(jnp.int32, g3.shape, 1)
    m1 = jnp.max(g3, axis=1, keepdims=True)
    first = jnp.min(jnp.where(g3 == m1, member, per), axis=1, keepdims=True)
    m2 = jnp.max(jnp.where(member == first, -jnp.inf, g3), axis=1, keepdims=True)
    gscore = (m1 + m2).reshape(N_EXPERT_GROUPS, tn)
    gsel = _top_rows(gscore, TOPK_GROUPS)
    emask = jnp.broadcast_to(gsel.reshape(N_EXPERT_GROUPS, 1, tn), g3.shape).reshape(N_EXPERTS, tn)
    masked = jnp.where(emask > 0.5, sc, -jnp.inf)
    self = _top_rows(masked, TOP_K)
    sel = self > 0.5
    w = jnp.where(sel, s, 0.0)
    gate = w / jnp.sum(w, axis=0, keepdims=True) * ROUTED_SCALE

    selb = self.astype(BF16)
    cnt = jnp.sum(self, axis=1, keepdims=True)
    pad = jnp.maximum(jnp.floor((cnt + (RUN_ALIGN - 1)) * (1.0 / RUN_ALIGN)), 1.0) * RUN_ALIGN
    soff = jnp.dot(lo, jnp.broadcast_to(pad, (N_EXPERTS, LANES)).astype(BF16), preferred_element_type=F32)[:, :1]
    rank = jnp.dot(selb, up, preferred_element_type=F32)
    kidx = jnp.dot(lo, selb, preferred_element_type=F32)
    pos_e = jnp.where(sel, soff + rank, 0.0)
    rows_p, rows_g = [], []
    for k in range(TOP_K):
        m = kidx == k
        rows_p.append(jnp.sum(jnp.where(m, pos_e, 0.0), axis=0, keepdims=True))
        rows_g.append(jnp.sum(jnp.where(m, gate, 0.0), axis=0, keepdims=True))
    return jnp.concatenate(rows_p, axis=0).astype(jnp.int32), jnp.concatenate(rows_g, axis=0), pad.astype(jnp.int32)


def _route_kernel(lg_ref, bias_ref, lo_ref, up_ref, pos_ref, gs_ref, cnt_ref, *, tc):
    for j in range(ROUTE_CHUNKS):
        cols = slice(j * tc, (j + 1) * tc)
        pos, gates, pad = _route_chunk(lg_ref[:, cols], bias_ref[...], lo_ref[...], up_ref[...])
        pos_ref[:, cols] = pos
        gs_ref[:, cols] = gates
        cnt_ref[j] = pad


def _route(logits_t, bias_col, tc):
    n = logits_t.shape[1]
    lo = jnp.asarray(np.tril(np.ones((N_EXPERTS, N_EXPERTS), np.float32), -1)).astype(BF16)
    up = jnp.asarray(np.triu(np.ones((tc, tc), np.float32), 1)).astype(BF16)
    tn = tc * ROUTE_CHUNKS
    slot = pl.BlockSpec((TOP_K, tn), lambda i: (0, i))
    return pl.pallas_call(
        functools.partial(_route_kernel, tc=tc),
        grid=(n // tn,),
        in_specs=[pl.BlockSpec((N_EXPERTS, tn), lambda i: (0, i)),
                  pl.BlockSpec((N_EXPERTS, 1), lambda i: (0, 0)),
                  pl.BlockSpec(lo.shape, lambda i: (0, 0)),
                  pl.BlockSpec(up.shape, lambda i: (0, 0))],
        out_specs=[slot, slot, pl.BlockSpec((ROUTE_CHUNKS, N_EXPERTS, 1), lambda i: (i, 0, 0))],
        out_shape=[jax.ShapeDtypeStruct((TOP_K, n), jnp.int32), jax.ShapeDtypeStruct((TOP_K, n), F32),
                   jax.ShapeDtypeStruct((n // tc, N_EXPERTS, 1), jnp.int32)],
        compiler_params=_cparams(("parallel",)),
        name="route",
    )(logits_t, bias_col, lo, up)


def _aligned(x):
    return pl.multiple_of(x, RUN_ALIGN)


def _block_rows(tc):
    return lax.broadcasted_iota(jnp.int32, (PERM_BLOCK, tc), 0).astype(F32).astype(BF16)


def _block_relative(pos, r0):
    return (pos - r0).astype(F32).astype(BF16)


def _gather_kernel(src_ref, cnt_ref, dst_ref, tot_ref,
                   pos_ref, x_ref, xs_ref, z_ref, sem, *, tc, n_chunks):
    c = pl.program_id(0)
    slot = c % 2

    def wait_chunk(ci, s):
        n = _aligned(tot_ref[ci])
        pltpu.make_async_copy(z_ref.at[s, pl.ds(0, n)], xs_ref.at[pl.ds(0, n)], sem.at[s]).wait()

    def start_runs(ci, s, first, count):
        for j in range(count):
            i = ci * N_EXPERTS + first + j
            n = _aligned(cnt_ref[i])
            pltpu.make_async_copy(z_ref.at[s, pl.ds(_aligned(src_ref[i]), n)],
                                  xs_ref.at[pl.ds(_aligned(dst_ref[i]), n)], sem.at[s]).start()

    @pl.when(c >= 2)
    def _():
        wait_chunk(c - 2, slot)

    rows = _block_rows(tc)

    def permute_block(b):
        r0 = _block_start(b)
        rel = _block_relative(pos_ref[...], r0)
        onehot = jnp.zeros((PERM_BLOCK, tc), BF16)
        for k in range(TOP_K):
            onehot = jnp.where(rel[k:k + 1, :] == rows, jnp.ones((), BF16), onehot)
        z_ref[slot, pl.ds(r0, PERM_BLOCK), :] = jnp.dot(
            onehot, x_ref[...], preferred_element_type=F32).astype(BF16)

    for b in range(_sure_blocks(tc)):
        permute_block(b)

    def tail_block(b, carry):
        permute_block(b)
        return carry
    lax.fori_loop(_sure_blocks(tc), _perm_blocks(tot_ref[c]), tail_block, 0)

    def e_body(e8, carry):
        start_runs(c, slot, e8 * DMA_UNROLL, DMA_UNROLL)
        return carry
    lax.fori_loop(0, N_EXPERTS // DMA_UNROLL, e_body, 0)

    @pl.when(c == n_chunks - 1)
    def _():
        wait_chunk(c, slot)
        if n_chunks > 1:
            wait_chunk(c - 1, 1 - slot)


def _perm_blocks(rows):
    return (rows + PERM_BLOCK - 1) // PERM_BLOCK


def _sure_blocks(tc):
    return min(TOP_K * tc // PERM_BLOCK + STRAIGHT_EXTRA_BLOCKS, _local_cap(tc) // PERM_BLOCK)


def _block_start(b):
    return b * PERM_BLOCK if isinstance(b, int) else pl.multiple_of(b * PERM_BLOCK, PERM_BLOCK)


def _chunk_rows_max(tc):
    return TOP_K * tc + N_EXPERTS * RUN_ALIGN


def _local_cap(tc):
    return -(-_chunk_rows_max(tc) // PERM_BLOCK) * PERM_BLOCK


def _gather(tabs, pos, h2, r_max, tc):
    n = h2.shape[0]
    n_chunks = n // tc
    return pl.pallas_call(
        functools.partial(_gather_kernel, tc=tc, n_chunks=n_chunks),
        grid_spec=pltpu.PrefetchScalarGridSpec(
            num_scalar_prefetch=4, grid=(n_chunks,),
            in_specs=[pl.BlockSpec((TOP_K, tc), lambda c, *_: (0, c)),
                      pl.BlockSpec((tc, D_MODEL), lambda c, *_: (c, 0))],
            out_specs=pl.BlockSpec(memory_space=pl.ANY),
            scratch_shapes=[pltpu.VMEM((2, _local_cap(tc), D_MODEL), BF16),
                            pltpu.SemaphoreType.DMA((2,))]),
        out_shape=jax.ShapeDtypeStruct((r_max, D_MODEL), BF16),
        compiler_params=_cparams(("arbitrary",), manual_dma=True),
        name="moe_gather",
    )(tabs["src"], tabs["cnt"], tabs["dst"], tabs["tot"], pos, h2)


def _ffn_kernel(first_ref, ntile_ref, tstart_ref, trows_ref, total_ref, xs_ref, wg_ref, wu_ref, wd_ref, ys_ref,
                xbuf, ybuf, wgb, wub, wdb, sem_in, sem_out):
    e = pl.program_id(0)
    total = total_ref[0]

    def in_copy(g):
        s = g % FFN_IN_SLOTS
        n = _aligned(trows_ref[g])
        return pltpu.make_async_copy(xs_ref.at[pl.ds(_aligned(tstart_ref[g]), n)],
                                     xbuf.at[s, pl.ds(0, n)], sem_in.at[s])

    def out_copy(g):
        s = g % 2
        n = _aligned(trows_ref[g])
        return pltpu.make_async_copy(ybuf.at[s, pl.ds(0, n)],
                                     ys_ref.at[pl.ds(_aligned(tstart_ref[g]), n)], sem_out.at[s])

    @pl.when(e == 0)
    def _():
        xbuf[...] = jnp.zeros(xbuf.shape, xbuf.dtype)
        for g in range(FFN_LOOKAHEAD):
            @pl.when(g < total)
            def _():
                in_copy(g).start()

    wgb[...] = wg_ref[...].astype(BF16)
    wub[...] = wu_ref[...].astype(BF16)
    wdb[...] = wd_ref[...].astype(BF16)

    def body(g, carry):
        @pl.when(g + FFN_LOOKAHEAD < total)
        def _():
            in_copy(g + FFN_LOOKAHEAD).start()
        in_copy(g).wait()

        @pl.when(g >= 2)
        def _():
            out_copy(g - 2).wait()
        x = xbuf[g % FFN_IN_SLOTS]
        a = jnp.dot(x, wgb[...], preferred_element_type=F32)
        u = jnp.dot(x, wub[...], preferred_element_type=F32)
        h = (a * jax.nn.sigmoid(a) * u).astype(BF16)
        ybuf[g % 2] = jnp.dot(h, wdb[...], preferred_element_type=F32).astype(BF16)
        out_copy(g).start()
        return carry
    lax.fori_loop(first_ref[e], first_ref[e] + ntile_ref[e], body, 0)

    @pl.when(e == pl.num_programs(0) - 1)
    def _():
        @pl.when(total >= 2)
        def _():
            out_copy(total - 2).wait()
        out_copy(total - 1).wait()


def _ffn(tabs, xs, wg, wu, wd):
    ew = lambda a: pl.BlockSpec((None,) + a.shape[1:], lambda e, *_: (e, 0, 0))
    return pl.pallas_call(
        _ffn_kernel,
        grid_spec=pltpu.PrefetchScalarGridSpec(
            num_scalar_prefetch=5, grid=(N_EXPERTS,),
            in_specs=[pl.BlockSpec(memory_space=pl.ANY), ew(wg), ew(wu), ew(wd)],
            out_specs=pl.BlockSpec(memory_space=pl.ANY),
            scratch_shapes=[pltpu.VMEM((FFN_IN_SLOTS, FFN_BLOCK, D_MODEL), BF16),
                            pltpu.VMEM((2, FFN_BLOCK, D_MODEL), BF16),
                            pltpu.VMEM(wg.shape[1:], BF16), pltpu.VMEM(wu.shape[1:], BF16),
                            pltpu.VMEM(wd.shape[1:], BF16),
                            pltpu.SemaphoreType.DMA((FFN_IN_SLOTS,)), pltpu.SemaphoreType.DMA((2,))]),
        out_shape=jax.ShapeDtypeStruct(xs.shape, BF16),
        compiler_params=_cparams(("arbitrary",), manual_dma=True),
        name="moe_ffn",
    )(tabs["first_tile"], tabs["n_tiles"], tabs["tile_start"], tabs["tile_rows"], tabs["total_tiles"],
      xs, wg, wu, wd)


def _combine_kernel(src_ref, cnt_ref, dst_ref, tot_ref,
                    pos_ref, gs_ref, x1_ref, h_ref, sg_ref, su_ref, sd_ref, g2_ref, nf_ref, ys_ref,
                    yc_ref, yl_ref, ybuf, acc_ref, sem, *, tc, n_chunks, ctx_chunks):
    c = pl.program_id(0)
    slot = c % 2

    def issue(ci, s):
        def e_body(e8, carry):
            for j in range(DMA_UNROLL):
                i = ci * N_EXPERTS + e8 * DMA_UNROLL + j
                n = _aligned(cnt_ref[i])
                pltpu.make_async_copy(ys_ref.at[pl.ds(_aligned(dst_ref[i]), n)],
                                      ybuf.at[s, pl.ds(_aligned(src_ref[i]), n)], sem.at[s]).start()
            return carry
        lax.fori_loop(0, N_EXPERTS // DMA_UNROLL, e_body, 0)

    @pl.when(c == 0)
    def _():
        ybuf[...] = jnp.zeros(ybuf.shape, ybuf.dtype)
        issue(0, 0)

    @pl.when(c + 1 < n_chunks)
    def _():
        issue(c + 1, 1 - slot)

    h = h_ref[...]
    a = jnp.dot(h, sg_ref[...], preferred_element_type=F32)
    u = jnp.dot(h, su_ref[...], preferred_element_type=F32)
    acc_ref[...] = jnp.dot((a * jax.nn.sigmoid(a) * u).astype(BF16), sd_ref[...], preferred_element_type=F32)

    n = _aligned(tot_ref[c])
    pltpu.make_async_copy(ys_ref.at[pl.ds(0, n)], ybuf.at[slot, pl.ds(0, n)], sem.at[slot]).wait()

    tn = (((0,), (0,)), ((), ()))
    rows = _block_rows(tc)
    gates = gs_ref[...].astype(BF16)

    def gate_block(b):
        rel = _block_relative(pos_ref[...], _block_start(b))
        wt = jnp.zeros((PERM_BLOCK, tc), BF16)
        for k in range(TOP_K):
            wt = jnp.where(rel[k:k + 1, :] == rows, gates[k:k + 1, :], wt)
        return wt

    sure_rows = _sure_blocks(tc) * PERM_BLOCK
    wt_sure = jnp.concatenate([gate_block(b) for b in range(_sure_blocks(tc))], axis=0)
    acc_ref[...] += lax.dot_general(wt_sure, ybuf[slot, pl.ds(0, sure_rows), :], tn, preferred_element_type=F32)

    def tail_block(b, carry):
        acc_ref[...] += lax.dot_general(gate_block(b), ybuf[slot, pl.ds(_block_start(b), PERM_BLOCK), :], tn,
                                        preferred_element_type=F32)
        return carry
    lax.fori_loop(_sure_blocks(tc), _perm_blocks(tot_ref[c]), tail_block, 0)

    x2 = x1_ref[...] + g2_ref[...] * acc_ref[...]
    y = _rms(x2, nf_ref[...])

    @pl.when(c < ctx_chunks)
    def _():
        yc_ref[...] = y

    @pl.when(c >= ctx_chunks)
    def _():
        yl_ref[...] = y


def _combine(tabs, pos, gslot, x1, h2, sg, su, sd, mod, norm_f, ys, tc, row_fn, n_ctx):
    n = x1.shape[0]
    n_chunks = n // tc
    ctx_chunks = n_ctx // tc
    tok = lambda w: pl.BlockSpec((tc, w), lambda c, *_: (c, 0))
    slot = pl.BlockSpec((TOP_K, tc), lambda c, *_: (0, c))
    full = lambda a: pl.BlockSpec(a.shape, lambda c, *_: (0,) * a.ndim)
    return pl.pallas_call(
        functools.partial(_combine_kernel, tc=tc, n_chunks=n_chunks, ctx_chunks=ctx_chunks),
        grid_spec=pltpu.PrefetchScalarGridSpec(
            num_scalar_prefetch=4, grid=(n_chunks,),
            in_specs=[slot, slot, tok(D_MODEL), tok(D_MODEL), full(sg), full(su), full(sd),
                      _mod_spec(5, lambda c, *_: row_fn(c)), full(norm_f), pl.BlockSpec(memory_space=pl.ANY)],
            out_specs=[pl.BlockSpec((tc, D_MODEL), lambda c, *_: (jnp.minimum(c, ctx_chunks - 1), 0)),
                       pl.BlockSpec((tc, D_MODEL), lambda c, *_: (jnp.maximum(c - ctx_chunks, 0), 0))],
            scratch_shapes=[pltpu.VMEM((2, _local_cap(tc), D_MODEL), BF16),
                            pltpu.VMEM((tc, D_MODEL), F32),
                            pltpu.SemaphoreType.DMA((2,))]),
        out_shape=[jax.ShapeDtypeStruct((n_ctx, D_MODEL), F32), jax.ShapeDtypeStruct((n - n_ctx, D_MODEL), F32)],
        compiler_params=_cparams(("arbitrary",), manual_dma=True),
        name="moe_combine",
    )(tabs["src"], tabs["cnt"], tabs["dst"], tabs["tot"], pos, gslot, x1, h2, sg, su, sd, mod, norm_f, ys)


def _dispatch_tables(pad_cnt, r_max):
    i32 = jnp.int32
    src = jnp.cumsum(pad_cnt, axis=1) - pad_cnt
    tot = jnp.sum(pad_cnt, axis=1)
    e_rows = jnp.sum(pad_cnt, axis=0)
    e_start = jnp.cumsum(e_rows) - e_rows
    dst = e_start[None, :] + jnp.cumsum(pad_cnt, axis=0) - pad_cnt
    n_tiles = (e_rows + FFN_BLOCK - 1) // FFN_BLOCK
    tile_end = jnp.cumsum(n_tiles)
    first_tile = tile_end - n_tiles
    g = jnp.arange(r_max // FFN_BLOCK + N_EXPERTS)
    before = tile_end[None, :] <= g[:, None]
    owner = jnp.concatenate([jnp.ones_like(before[:, :1]), before[:, :-1]], axis=1) & ~before
    offset = (g - jnp.sum(jnp.where(before, n_tiles, 0), axis=1)) * FFN_BLOCK
    tile_start = jnp.clip(jnp.sum(jnp.where(before, e_rows, 0), axis=1) + offset, 0, r_max - FFN_BLOCK)
    tile_rows = jnp.clip(jnp.sum(jnp.where(owner, e_rows, 0), axis=1) - offset, RUN_ALIGN, FFN_BLOCK)
    return {"src": src.reshape(-1).astype(i32), "cnt": pad_cnt.reshape(-1).astype(i32),
            "dst": dst.reshape(-1).astype(i32), "tot": tot.astype(i32),
            "first_tile": first_tile.astype(i32), "n_tiles": n_tiles.astype(i32),
            "tile_start": tile_start.astype(i32), "tile_rows": tile_rows.astype(i32),
            "total_tiles": tile_end[-1:].astype(i32)}


def _moe(x1, h2, logits_t, bias_col, wg, wu, wd, sg, su, sd, mod, norm_f, row_fn, n_ctx):
    n = x1.shape[0]
    tc = MOE_CHUNK
    pos, gslot, pad_cnt = _route(logits_t, bias_col, tc)
    r_max = (n // tc) * _chunk_rows_max(tc)
    tabs = _dispatch_tables(pad_cnt[:, :, 0], r_max)
    xs = _gather(tabs, pos, h2, r_max, tc)
    ys = _ffn(tabs, xs, wg, wu, wd)
    return _combine(tabs, pos, gslot, x1, h2, sg, su, sd, mod, norm_f, ys, tc, row_fn, n_ctx)


def _channel_dft():
    ang = 2.0 * np.pi * ((np.arange(F_HD)[:, None] * np.arange(F_HD)[None, :]) % F_HD) / F_HD
    eye = np.eye(F_GROUPS)
    return np.concatenate([np.kron(eye, np.cos(ang)), np.kron(eye, -np.sin(ang))], axis=1).astype(np.float32)


def _rope_tables(t):
    pos = np.arange(t)
    row, col = pos // GRID_W, pos % GRID_W
    n_freq = HEAD_DIM // 4
    inv = ROPE_THETA ** (-np.arange(n_freq, dtype=np.float64) / n_freq)
    lane = np.arange(LANES)
    hd = lane % HEAD_DIM
    within = hd % (HEAD_DIM // 2)
    freq = within % n_freq
    first = within < n_freq
    p = np.where((hd < HEAD_DIM // 2)[None, :], row[:, None], col[:, None]).astype(np.float64)
    ang = p * inv[freq][None, :]
    cos, sin = np.cos(ang), np.sin(ang)
    sin_up = np.where(first[None, :], -sin, 0.0)
    sin_dn = np.where(first[None, :], 0.0, sin)
    return [jnp.asarray(a.astype(np.float32)) for a in (cos, sin_up, sin_dn)]


def _block_diag_pairs(wf):
    per = LANES // F_HD
    z = jnp.zeros((F_HD, F_HD), wf.dtype)
    blocks = []
    for c in range(F_GROUPS // per):
        rows = [jnp.concatenate([wf[c * per + r] if r == cc else z for cc in range(per)], axis=1) for r in range(per)]
        blocks.append(jnp.concatenate(rows, axis=0))
    return jnp.stack(blocks)


def kernel(x_prompt, x_sample, cache_k, cache_v, c, c_ctx, w_mod, b_mod, norm1, w_in, w_fourier, sink,
           w_out, norm2, w_router, router_bias, w_gate, w_up, w_down, ws_gate, ws_up, ws_down, norm_f):
    nb_ctx, t_ctx, _ = x_prompt.shape
    nb_lat, t_lat, _ = x_sample.shape
    l = 0
    cvec = jnp.concatenate([c_ctx[None, :], c, jnp.zeros((MOD_ROWS - 1 - nb_lat, D_MODEL), F32)], axis=0)
    mod = _modulation(cvec, w_mod[l], b_mod[l]).reshape(MOD_ROWS, 6, 1, D_MODEL)

    w_in_b = w_in[l].astype(BF16)
    dft_c = jnp.asarray(_channel_dft()).astype(BF16)
    wf_bd = _block_diag_pairs(w_fourier[l]).astype(BF16)
    wo_f = w_out[l][:F_DIM].astype(BF16)
    wo_a = w_out[l][F_DIM:].astype(BF16)
    w_router_t = w_router[l].T.astype(BF16)
    bias_col = router_bias[l].reshape(N_EXPERTS, 1)
    n1 = norm1[l].reshape(1, D_MODEL)
    n2 = norm2[l].reshape(1, D_MODEL)
    nf = norm_f.reshape(1, D_MODEL)
    sink_col = sink[l]
    wg, wu, wd = w_gate[l], w_up[l], w_down[l]
    sg, su, sd = ws_gate[l].astype(BF16), ws_up[l].astype(BF16), ws_down[l].astype(BF16)

    def mixers(x, batch, t, latent, kx, vx, t1, t2, tm):
        n = batch * t
        x2d = x.reshape(n, D_MODEL)
        tiles = t // tm
        row_fn = (lambda i: 1 + i // tiles) if latent else (lambda i: 0)
        rope_tabs = _rope_tables(t) if latent else None
        outs = _project(x2d, mod, n1, w_in_b, dft_c, rope_tabs, tm, tiles, row_fn)
        wr, wi, q, k, v = outs[:5]
        mixf = _fourier(wr, wi, wf_bd, batch, t, t1, t2)
        q3, k3, v3 = (a.reshape(batch, t, -1) for a in (q, k, v))
        if latent:
            mixa = _attention(q3, k3, v3, kx, vx, sink_col, batch, t, True)
        else:
            mixa = _attention(q3, None, None, k3, v3, sink_col, batch, t, False)
        return (x2d, mixf.reshape(n, F_DIM), mixa.reshape(n, ATTN_DIM)), outs[5:]

    ctx, (kf, vf) = mixers(x_prompt, nb_ctx, t_ctx, False, None, None, 16, 16, 256)
    kx = cache_k[:, l].reshape(nb_lat, -1, KV_DIM).astype(BF16)
    vx = cache_v[:, l].reshape(nb_lat, -1, KV_DIM).astype(BF16)
    lat, _ = mixers(x_sample, nb_lat, t_lat, True, kx, vx, 64, 64, 1024)

    n_ctx = nb_ctx * t_ctx

    def tile_row(tile):
        ctx_tiles, lat_tiles = n_ctx // tile, t_lat // tile
        return lambda i: jnp.where(i < ctx_tiles, 0, 1 + (i - ctx_tiles) // lat_tiles)

    x1, h2, logits_t = _outproj(ctx, lat, wo_f, wo_a, mod, n2, w_router_t, OUTPROJ_TILE, tile_row(OUTPROJ_TILE))
    y_prompt, y_sample = _moe(x1, h2, logits_t, bias_col, wg, wu, wd, sg, su, sd, mod, nf, tile_row(MOE_CHUNK),
                              n_ctx)
    new_k = kf.reshape(nb_ctx, 1, t_ctx, N_KV_HEADS, HEAD_DIM)
    new_v = vf.reshape(nb_ctx, 1, t_ctx, N_KV_HEADS, HEAD_DIM)
    return (y_prompt.reshape(x_prompt.shape), y_sample.reshape(x_sample.shape), new_k, new_v)
```

```python
import functools

import numpy as np
import jax
import jax.numpy as jnp
from jax import lax
from jax.experimental import pallas as pl
from jax.experimental.pallas import tpu as pltpu

F32 = jnp.float32
BF16 = jnp.bfloat16

D_MODEL = 1024
GRID_W = 64
HEAD_DIM = 64
N_HEADS = 12
N_KV_HEADS = 4
GQA_GROUP = N_HEADS // N_KV_HEADS
ATTN_DIM = N_HEADS * HEAD_DIM
KV_DIM = N_KV_HEADS * HEAD_DIM
F_GROUPS = 4
F_HD = 64
F_DIM = F_GROUPS * F_HD
IN_DIM = F_DIM + ATTN_DIM + 2 * KV_DIM
WINDOW = 128
Q_BLOCK = 128
ROPE_THETA = 10000.0
N_EXPERTS = 64
TOP_K = 8
N_EXPERT_GROUPS = 8
TOPK_GROUPS = 4
EXPERT_DIM = 256
SHARED_DIM = 256
ROUTED_SCALE = 2.5
EPS = 1e-6

LANES = 128
MOD_ROWS = 8
RUN_ALIGN = 16
PERM_BLOCK = 256
FFN_BLOCK = 512
FFN_LOOKAHEAD = 5
FFN_IN_SLOTS = FFN_LOOKAHEAD + 1
MOE_CHUNK = 256
CTX_PROJ_TILE = 1024
LAT_PROJ_TILE = 1024
LAT_DFT_FACTORS = (64, 64)
F_SLABS = F_DIM // LANES
OUTPROJ_TILE = 1024
ROUTE_CHUNKS = 4
SOFTMAX_ROWS = 32
ATTN_BLOCKS = 4
DMA_UNROLL = 8
GATHER_SLOTS = 3
STRAIGHT_EXTRA_BLOCKS = 2
VMEM_LIMIT = 56 * 1024 * 1024


def _cparams(sem):
    return pltpu.CompilerParams(dimension_semantics=sem, vmem_limit_bytes=VMEM_LIMIT)


def _bdot(a, b):
    return jnp.dot(a.astype(BF16), b.astype(BF16), preferred_element_type=F32)


def _rms(x, g):
    return x * lax.rsqrt(jnp.mean(x * x, axis=-1, keepdims=True) + EPS) * g


def _mod_kernel(c_ref, w_ref, b_ref, o_ref):
    c = c_ref[...]
    a = c * jax.nn.sigmoid(c)
    o_ref[...] = _bdot(a, w_ref[...]) + b_ref[...]


def _modulation(cvec, w_mod, b_mod):
    n = w_mod.shape[1]
    tn = 1024
    return pl.pallas_call(
        _mod_kernel,
        grid=(n // tn,),
        in_specs=[pl.BlockSpec((MOD_ROWS, D_MODEL), lambda j: (0, 0)),
                  pl.BlockSpec((D_MODEL, tn), lambda j: (0, j)),
                  pl.BlockSpec((1, tn), lambda j: (0, j))],
        out_specs=pl.BlockSpec((MOD_ROWS, tn), lambda j: (0, j)),
        out_shape=jax.ShapeDtypeStruct((MOD_ROWS, n), F32),
        compiler_params=_cparams(("parallel",)),
        name="modulation",
    )(cvec, w_mod, b_mod.reshape(1, n))


def _mod_spec(piece, row_fn):
    return pl.BlockSpec((None, None, 1, D_MODEL), lambda *idx: (row_fn(*idx), piece, 0, 0))


def _rope(x, cos, sin_up, sin_dn):
    outs = []
    for j in range(x.shape[1] // LANES):
        xj = x[:, j * LANES:(j + 1) * LANES]
        up = pltpu.roll(xj, LANES - 16, axis=1)
        dn = pltpu.roll(xj, 16, axis=1)
        outs.append(xj * cos + up * sin_up + dn * sin_dn)
    return jnp.concatenate(outs, axis=1)


def _proj_kernel(*refs, latent):
    if latent:
        (x_ref, sh_ref, sc_ref, g_ref, w_ref, dft_ref, cos_ref, sup_ref, sdn_ref,
         wr_ref, wi_ref, q_ref, k_ref, v_ref) = refs
    else:
        (x_ref, sh_ref, sc_ref, g_ref, w_ref, dft_ref,
         wr_ref, wi_ref, q_ref, k_ref, v_ref, kf_ref, vf_ref) = refs
    x = x_ref[...]
    h = _rms(x, g_ref[...]) * (1.0 + sc_ref[...]) + sh_ref[...]
    p = _bdot(h, w_ref[...])
    u = p[:, :F_DIM]
    q = p[:, F_DIM:F_DIM + ATTN_DIM]
    k = p[:, F_DIM + ATTN_DIM:F_DIM + ATTN_DIM + KV_DIM]
    v = p[:, F_DIM + ATTN_DIM + KV_DIM:]
    w = _bdot(u, dft_ref[...])
    wr_ref[...] = w[:, :F_DIM]
    wi_ref[...] = w[:, F_DIM:]
    if latent:
        cos, sup, sdn = cos_ref[...], sup_ref[...], sdn_ref[...]
        q = _rope(q, cos, sup, sdn)
        k = _rope(k, cos, sup, sdn)
    else:
        kf_ref[...] = k
        vf_ref[...] = v
    q_ref[...] = (q * (HEAD_DIM ** -0.5)).astype(BF16)
    k_ref[...] = k.astype(BF16)
    v_ref[...] = v.astype(BF16)


def _project(x2d, mod, norm1, w_in, dft_c, rope_tabs, tm, tiles_per_batch, row_fn):
    n = x2d.shape[0]
    latent = rope_tabs is not None
    tok = lambda w: pl.BlockSpec((tm, w), lambda i: (i, 0))
    full = lambda a: pl.BlockSpec(a.shape, lambda i: (0,) * a.ndim)
    in_specs = [tok(D_MODEL), _mod_spec(0, row_fn), _mod_spec(1, row_fn), full(norm1), full(w_in), full(dft_c)]
    args = [x2d, mod, mod, norm1, w_in, dft_c]
    out_specs = [tok(F_DIM), tok(F_DIM), tok(ATTN_DIM), tok(KV_DIM), tok(KV_DIM)]
    out_shape = [jax.ShapeDtypeStruct((n, F_DIM), F32), jax.ShapeDtypeStruct((n, F_DIM), F32),
                 jax.ShapeDtypeStruct((n, ATTN_DIM), BF16), jax.ShapeDtypeStruct((n, KV_DIM), BF16),
                 jax.ShapeDtypeStruct((n, KV_DIM), BF16)]
    if latent:
        pos = pl.BlockSpec((tm, LANES), lambda i: (i % tiles_per_batch, 0))
        in_specs += [pos, pos, pos]
        args += list(rope_tabs)
    else:
        out_specs += [tok(KV_DIM), tok(KV_DIM)]
        out_shape += [jax.ShapeDtypeStruct((n, KV_DIM), F32), jax.ShapeDtypeStruct((n, KV_DIM), F32)]
    return pl.pallas_call(
        functools.partial(_proj_kernel, latent=latent),
        grid=(n // tm,),
        in_specs=in_specs, out_specs=out_specs, out_shape=out_shape,
        compiler_params=_cparams(("parallel",)),
        name="project_latent" if latent else "project_context",
    )(*args)


def _fourier_kernel(wr_ref, wi_ref, a_ref, b_ref, wf_ref, o_ref, yr_ref, yi_ref, z_ref, *, t1, t2, scale):
    for j in range(t2):
        rows = pl.ds(j, t1, stride=t2)
        xin = jnp.concatenate([wr_ref[rows, :], wi_ref[rows, :]], axis=0)
        y = _bdot(a_ref[j], xin)
        yr_ref[rows, :] = y[:t1]
        yi_ref[rows, :] = y[t1:]
    bm = b_ref[...]
    for k1 in range(t1):
        rows = pl.ds(k1 * t2, t2)
        yin = jnp.concatenate([yr_ref[rows, :], yi_ref[rows, :]], axis=0)
        z_ref[pl.ds(k1, t2, stride=t1), :] = _bdot(bm, yin)
    o_ref[...] = (_bdot(z_ref[...], wf_ref[...]) * scale).astype(BF16)


def _dense_fourier_kernel(wr_ref, wi_ref, m_ref, wf_ref, o_ref, *, scale):
    f = _bdot(m_ref[...], jnp.concatenate([wr_ref[...], wi_ref[...]], axis=0))
    for h in range(F_SLABS):
        cols = slice(h * LANES, (h + 1) * LANES)
        o_ref[:, cols] = (_bdot(f[:, cols], wf_ref[h]) * scale).astype(BF16)


def _dft_tables(t1, t2):
    t = t1 * t2
    k1 = np.arange(t1)[None, :, None]
    pos = (t2 * np.arange(t1)[None, None, :] + np.arange(t2)[:, None, None])
    ang = 2.0 * np.pi * ((k1 * pos) % t) / t
    c, s = np.cos(ang), np.sin(ang)
    a = np.concatenate([np.concatenate([c, s], axis=2), np.concatenate([-s, c], axis=2)], axis=1)
    ang2 = 2.0 * np.pi * ((np.arange(t2)[:, None] * np.arange(t2)[None, :]) % t2) / t2
    b = np.concatenate([np.cos(ang2), np.sin(ang2)], axis=1)
    return a.astype(np.float32), b.astype(np.float32)


def _fourier(wr, wi, wf_bd, batch, t, factors):
    wr, wi = wr.reshape(batch, t, F_DIM), wi.reshape(batch, t, F_DIM)
    scale = float((t * F_HD) ** -0.5)
    full = lambda a: pl.BlockSpec(a.shape, lambda bi: (0,) * a.ndim)
    whole = pl.BlockSpec((None, t, F_DIM), lambda bi: (bi, 0, 0))
    if factors is None:
        ang = 2.0 * np.pi * ((np.arange(t)[:, None] * np.arange(t)[None, :]) % t) / t
        m = jnp.asarray(np.concatenate([np.cos(ang), np.sin(ang)], axis=1).astype(np.float32)).astype(BF16)
        return pl.pallas_call(
            functools.partial(_dense_fourier_kernel, scale=scale),
            grid=(batch,),
            in_specs=[whole, whole, full(m), full(wf_bd)],
            out_specs=whole,
            out_shape=jax.ShapeDtypeStruct((batch, t, F_DIM), BF16),
            compiler_params=_cparams(("parallel",)),
            name="fourier_dense_%d" % t,
        )(wr, wi, m, wf_bd)
    t1, t2 = factors
    a_np, b_np = _dft_tables(t1, t2)
    a = jnp.asarray(a_np).astype(BF16)
    b = jnp.asarray(b_np).astype(BF16)
    slab = pl.BlockSpec((None, t, LANES), lambda bi, ci: (bi, 0, ci))
    const = lambda x: pl.BlockSpec(x.shape, lambda bi, ci: (0,) * x.ndim)
    return pl.pallas_call(
        functools.partial(_fourier_kernel, t1=t1, t2=t2, scale=scale),
        grid=(batch, F_SLABS),
        in_specs=[slab, slab, const(a), const(b), pl.BlockSpec((None, LANES, LANES), lambda bi, ci: (ci, 0, 0))],
        out_specs=slab,
        out_shape=jax.ShapeDtypeStruct((batch, t, F_DIM), BF16),
        scratch_shapes=[pltpu.VMEM((t, LANES), F32)] * 3,
        compiler_params=_cparams(("parallel", "parallel")),
        name="fourier_%d" % t,
    )(wr, wi, a, b, wf_bd)


def _attend(q_ref, row0, parts, sink_ref, s_ref, p_ref):
    nt = (((1,), (1,)), ((), ()))
    n_keys = sum(k.shape[0] for k, _, _ in parts)
    outs = []
    for kv in range(N_KV_HEADS):
        heads = range(kv * GQA_GROUP, (kv + 1) * GQA_GROUP)
        hs = slice(kv * HEAD_DIM, (kv + 1) * HEAD_DIM)
        qs = jnp.concatenate([q_ref[row0:row0 + Q_BLOCK, h * HEAD_DIM:(h + 1) * HEAD_DIM] for h in heads], axis=0)
        c0 = 0
        for k, _, ok in parts:
            s = lax.dot_general(qs, k[:, hs], nt, preferred_element_type=F32)
            if ok is not None:
                s = jnp.where(jnp.concatenate([ok] * GQA_GROUP, axis=0), s, -jnp.inf)
            s_ref[:, c0:c0 + s.shape[1]] = s
            c0 += s.shape[1]
        invs = []
        for r0 in range(0, GQA_GROUP * Q_BLOCK, SOFTMAX_ROWS):
            s = s_ref[r0:r0 + SOFTMAX_ROWS, :]
            sk = sink_ref[kv * GQA_GROUP + r0 // Q_BLOCK]
            slabs = [s[:, c:c + LANES] for c in range(0, n_keys, LANES)]
            m = jnp.maximum(jnp.max(functools.reduce(jnp.maximum, slabs), axis=1, keepdims=True), sk)
            p = jnp.exp(s - m)
            pslabs = [p[:, c:c + LANES] for c in range(0, n_keys, LANES)]
            den = jnp.sum(functools.reduce(jnp.add, pslabs), axis=1, keepdims=True) + jnp.exp(sk - m)
            p_ref[r0:r0 + SOFTMAX_ROWS, :] = p.astype(BF16)
            invs.append(1.0 / den)
        o = None
        c0 = 0
        for k, v, _ in parts:
            nk = k.shape[0]
            d = jnp.dot(p_ref[:, c0:c0 + nk], v[:, hs], preferred_element_type=F32)
            o = d if o is None else o + d
            c0 += nk
        o = o * jnp.concatenate(invs, axis=0)
        outs += [o[g * Q_BLOCK:(g + 1) * Q_BLOCK] for g in range(GQA_GROUP)]
    return jnp.concatenate(outs, axis=1).astype(BF16)


def _attn_kernel(*refs, windowed, n_steps):
    if not windowed:
        q_ref, kx_ref, vx_ref, sink_ref, o_ref, s0, p0 = refs
        o_ref[...] = _attend(q_ref, 0, [(kx_ref, vx_ref, None)], sink_ref, s0, p0)
        return
    q_ref, kp_ref, kc_ref, kn_ref, vp_ref, vc_ref, vn_ref, kx_ref, vx_ref, sink_ref, o_ref = refs[:11]
    scratch = refs[11:]
    j = pl.program_id(1)
    a = lax.broadcasted_iota(jnp.int32, (Q_BLOCK, Q_BLOCK), 0)
    b = lax.broadcasted_iota(jnp.int32, (Q_BLOCK, Q_BLOCK), 1)
    later, earlier = b >= a, b <= a
    own = [(kc_ref.at[i * Q_BLOCK:(i + 1) * Q_BLOCK], vc_ref.at[i * Q_BLOCK:(i + 1) * Q_BLOCK])
           for i in range(ATTN_BLOCKS)]
    for i in range(ATTN_BLOCKS):
        prev = own[i - 1] + (later,) if i > 0 else (kp_ref, vp_ref, later & (j > 0))
        nxt = own[i + 1] + (earlier,) if i < ATTN_BLOCKS - 1 else (kn_ref, vn_ref, earlier & (j < n_steps - 1))
        parts = [prev, own[i] + (None,), nxt, (kx_ref, vx_ref, None)]
        o_ref[i * Q_BLOCK:(i + 1) * Q_BLOCK, :] = _attend(q_ref, i * Q_BLOCK, parts, sink_ref,
                                                           scratch[2 * i], scratch[2 * i + 1])


def _attention(q, k, v, kx, vx, sink_col, batch, t, windowed):
    nb = t // Q_BLOCK
    per_step = ATTN_BLOCKS if windowed else 1
    n_ctx = kx.shape[1]
    n_keys = n_ctx + (3 * Q_BLOCK if windowed else 0)
    qspec = pl.BlockSpec((None, per_step * Q_BLOCK, ATTN_DIM), lambda b, j: (b, j, 0))
    xspec = pl.BlockSpec((None, n_ctx, KV_DIM), lambda b, j: (b, 0, 0))
    sspec = pl.BlockSpec(memory_space=pltpu.SMEM)
    if windowed:
        prev = pl.BlockSpec((None, Q_BLOCK, KV_DIM), lambda b, j: (b, jnp.maximum(per_step * j - 1, 0), 0))
        own = pl.BlockSpec((None, per_step * Q_BLOCK, KV_DIM), lambda b, j: (b, j, 0))
        nxt = pl.BlockSpec((None, Q_BLOCK, KV_DIM), lambda b, j: (b, jnp.minimum(per_step * (j + 1), nb - 1), 0))
        in_specs = [qspec, prev, own, nxt, prev, own, nxt, xspec, xspec, sspec]
        args = [q, k, k, k, v, v, v, kx, vx, sink_col]
    else:
        in_specs = [qspec, xspec, xspec, sspec]
        args = [q, kx, vx, sink_col]
    scores = pltpu.VMEM((GQA_GROUP * Q_BLOCK, n_keys), F32)
    probs = pltpu.VMEM((GQA_GROUP * Q_BLOCK, n_keys), BF16)
    return pl.pallas_call(
        functools.partial(_attn_kernel, windowed=windowed, n_steps=nb // per_step),
        grid=(batch, nb // per_step),
        in_specs=in_specs, out_specs=qspec,
        out_shape=jax.ShapeDtypeStruct((batch, t, ATTN_DIM), BF16),
        scratch_shapes=[scores, probs] * per_step,
        compiler_params=_cparams(("parallel", "parallel")),
        name="attention_latent" if windowed else "attention_context",
    )(*args)


def _outproj_kernel(xc_ref, mfc_ref, mac_ref, xl_ref, mfl_ref, mal_ref,
                    wof_ref, woa_ref, g1_ref, sh_ref, sc_ref, n2_ref, wr_ref,
                    x1_ref, h_ref, lg_ref, *, ctx_tiles):
    def body(x_ref, mf_ref, ma_ref):
        o = jnp.dot(mf_ref[...], wof_ref[...], preferred_element_type=F32)
        o = o + jnp.dot(ma_ref[...], woa_ref[...], preferred_element_type=F32)
        x1 = x_ref[...] + g1_ref[...] * o
        x1_ref[...] = x1
        h = (_rms(x1, n2_ref[...]) * (1.0 + sc_ref[...]) + sh_ref[...]).astype(BF16)
        h_ref[...] = h
        lg_ref[...] = lax.dot_general(wr_ref[...], h, (((1,), (1,)), ((), ())), preferred_element_type=F32)

    is_ctx = pl.program_id(0) < ctx_tiles
    pl.when(is_ctx)(lambda: body(xc_ref, mfc_ref, mac_ref))
    pl.when(jnp.logical_not(is_ctx))(lambda: body(xl_ref, mfl_ref, mal_ref))


def _outproj(ctx, lat, wo_f, wo_a, mod, norm2, w_router_t, tm, row_fn):
    n_c, n_l = ctx[0].shape[0], lat[0].shape[0]
    ctx_tiles = n_c // tm
    n = n_c + n_l
    ctok = lambda w: pl.BlockSpec((tm, w), lambda i: (jnp.minimum(i, ctx_tiles - 1), 0))
    ltok = lambda w: pl.BlockSpec((tm, w), lambda i: (jnp.maximum(i - ctx_tiles, 0), 0))
    tok = lambda w: pl.BlockSpec((tm, w), lambda i: (i, 0))
    full = lambda a: pl.BlockSpec(a.shape, lambda i: (0,) * a.ndim)
    widths = (D_MODEL, F_DIM, ATTN_DIM)
    return pl.pallas_call(
        functools.partial(_outproj_kernel, ctx_tiles=ctx_tiles),
        grid=(n // tm,),
        in_specs=[ctok(w) for w in widths] + [ltok(w) for w in widths] + [
            full(wo_f), full(wo_a), _mod_spec(2, row_fn), _mod_spec(3, row_fn), _mod_spec(4, row_fn),
            full(norm2), full(w_router_t)],
        out_specs=[tok(D_MODEL), tok(D_MODEL), pl.BlockSpec((N_EXPERTS, tm), lambda i: (0, i))],
        out_shape=[jax.ShapeDtypeStruct((n, D_MODEL), F32), jax.ShapeDtypeStruct((n, D_MODEL), BF16),
                   jax.ShapeDtypeStruct((N_EXPERTS, n), F32)],
        compiler_params=_cparams(("parallel",)),
        name="outproj",
    )(*ctx, *lat, wo_f, wo_a, mod, mod, mod, norm2, w_router_t)


def _top_rows(vals, k):
    n = vals.shape[0]
    idx = lax.broadcasted_iota(jnp.int32, vals.shape, 0)
    picked = jnp.zeros(vals.shape, F32)
    for _ in range(k):
        best = jnp.max(vals, axis=0, keepdims=True)
        first = jnp.min(jnp.where(vals == best, idx, n), axis=0, keepdims=True)
        hit = idx == first
        picked = jnp.where(hit, 1.0, picked)
        vals = jnp.where(hit, -jnp.inf, vals)
    return picked


def _route_chunk(logits, bias, lo, up):
    s = jax.nn.sigmoid(logits)
    sc = s + bias
    tn = s.shape[1]
    per = N_EXPERTS // N_EXPERT_GROUPS
    g3 = sc.reshape(N_EXPERT_GROUPS, per, tn)
    member = lax.broadcasted_iota(jnp.int32, g3.shape, 1)
    m1 = jnp.max(g3, axis=1, keepdims=True)
    first = jnp.min(jnp.where(g3 == m1, member, per), axis=1, keepdims=True)
    m2 = jnp.max(jnp.where(member == first, -jnp.inf, g3), axis=1, keepdims=True)
    gscore = (m1 + m2).reshape(N_EXPERT_GROUPS, tn)
    gsel = _top_rows(gscore, TOPK_GROUPS)
    emask = jnp.broadcast_to(gsel.reshape(N_EXPERT_GROUPS, 1, tn), g3.shape).reshape(N_EXPERTS, tn)
    masked = jnp.where(emask > 0.5, sc, -jnp.inf)
    self = _top_rows(masked, TOP_K)
    sel = self > 0.5
    w = jnp.where(sel, s, 0.0)
    gate = w / jnp.sum(w, axis=0, keepdims=True) * ROUTED_SCALE

    selb = self.astype(BF16)
    cnt = jnp.sum(self, axis=1, keepdims=True)
    pad = jnp.maximum(jnp.floor((cnt + (RUN_ALIGN - 1)) * (1.0 / RUN_ALIGN)), 1.0) * RUN_ALIGN
    soff = jnp.dot(lo, jnp.broadcast_to(pad, (N_EXPERTS, LANES)).astype(BF16), preferred_element_type=F32)[:, :1]
    rank = jnp.dot(selb, up, preferred_element_type=F32)
    kidx = jnp.dot(lo, selb, preferred_element_type=F32)
    pos_e = jnp.where(sel, soff + rank, 0.0)
    rows_p, rows_g = [], []
    for k in range(TOP_K):
        m = kidx == k
        rows_p.append(jnp.sum(jnp.where(m, pos_e, 0.0), axis=0, keepdims=True))
        rows_g.append(jnp.sum(jnp.where(m, gate, 0.0), axis=0, keepdims=True))
    return jnp.concatenate(rows_p, axis=0).astype(jnp.int32), jnp.concatenate(rows_g, axis=0), pad.astype(jnp.int32)


def _route_kernel(lg_ref, bias_ref, lo_ref, up_ref, pos_ref, gs_ref, cnt_ref, *, tc):
    for j in range(ROUTE_CHUNKS):
        cols = slice(j * tc, (j + 1) * tc)
        pos, gates, pad = _route_chunk(lg_ref[:, cols], bias_ref[...], lo_ref[...], up_ref[...])
        pos_ref[:, cols] = pos
        gs_ref[:, cols] = gates
        cnt_ref[j] = pad


def _route(logits_t, bias_col, tc):
    n = logits_t.shape[1]
    lo = jnp.asarray(np.tril(np.ones((N_EXPERTS, N_EXPERTS), np.float32), -1)).astype(BF16)
    up = jnp.asarray(np.triu(np.ones((tc, tc), np.float32), 1)).astype(BF16)
    tn = tc * ROUTE_CHUNKS
    slot = pl.BlockSpec((TOP_K, tn), lambda i: (0, i))
    return pl.pallas_call(
        functools.partial(_route_kernel, tc=tc),
        grid=(n // tn,),
        in_specs=[pl.BlockSpec((N_EXPERTS, tn), lambda i: (0, i)),
                  pl.BlockSpec((N_EXPERTS, 1), lambda i: (0, 0)),
                  pl.BlockSpec(lo.shape, lambda i: (0, 0)),
                  pl.BlockSpec(up.shape, lambda i: (0, 0))],
        out_specs=[slot, slot, pl.BlockSpec((ROUTE_CHUNKS, N_EXPERTS, 1), lambda i: (i, 0, 0))],
        out_shape=[jax.ShapeDtypeStruct((TOP_K, n), jnp.int32), jax.ShapeDtypeStruct((TOP_K, n), F32),
                   jax.ShapeDtypeStruct((n // tc, N_EXPERTS, 1), jnp.int32)],
        compiler_params=_cparams(("parallel",)),
        name="route",
    )(logits_t, bias_col, lo, up)


def _aligned(x):
    return pl.multiple_of(x, RUN_ALIGN)


def _block_rows(tc):
    return lax.broadcasted_iota(jnp.int32, (PERM_BLOCK, tc), 0).astype(F32).astype(BF16)


def _block_relative(pos, r0):
    return (pos - r0).astype(F32).astype(BF16)


def _gather_kernel(src_ref, cnt_ref, dst_ref, tot_ref,
                   pos_ref, x_ref, xs_ref, z_ref, sem, *, tc, n_chunks):
    c = pl.program_id(0)
    slot = c % GATHER_SLOTS

    def wait_chunk(ci, s):
        n = _aligned(tot_ref[ci])
        pltpu.make_async_copy(z_ref.at[s, pl.ds(0, n)], xs_ref.at[pl.ds(0, n)], sem.at[s]).wait()

    def start_runs(ci, s, first, count):
        for j in range(count):
            i = ci * N_EXPERTS + first + j
            n = _aligned(cnt_ref[i])
            pltpu.make_async_copy(z_ref.at[s, pl.ds(_aligned(src_ref[i]), n)],
                                  xs_ref.at[pl.ds(_aligned(dst_ref[i]), n)], sem.at[s]).start()

    @pl.when(c >= GATHER_SLOTS)
    def _():
        wait_chunk(c - GATHER_SLOTS, slot)

    rows = _block_rows(tc)

    def permute_block(b):
        r0 = _block_start(b)
        rel = _block_relative(pos_ref[...], r0)
        onehot = jnp.zeros((PERM_BLOCK, tc), BF16)
        for k in range(TOP_K):
            onehot = jnp.where(rel[k:k + 1, :] == rows, jnp.ones((), BF16), onehot)
        z_ref[slot, pl.ds(r0, PERM_BLOCK), :] = jnp.dot(
            onehot, x_ref[...], preferred_element_type=F32).astype(BF16)

    for b in range(_sure_blocks(tc)):
        permute_block(b)

    def tail_block(b, carry):
        permute_block(b)
        return carry
    lax.fori_loop(_sure_blocks(tc), _perm_blocks(tot_ref[c]), tail_block, 0)

    def e_body(e8, carry):
        start_runs(c, slot, e8 * DMA_UNROLL, DMA_UNROLL)
        return carry
    lax.fori_loop(0, N_EXPERTS // DMA_UNROLL, e_body, 0)

    @pl.when(c == n_chunks - 1)
    def _():
        for back in range(min(GATHER_SLOTS, n_chunks)):
            wait_chunk(c - back, (c - back) % GATHER_SLOTS)


def _perm_blocks(rows):
    return (rows + PERM_BLOCK - 1) // PERM_BLOCK


def _sure_blocks(tc):
    return min(TOP_K * tc // PERM_BLOCK + STRAIGHT_EXTRA_BLOCKS, _local_cap(tc) // PERM_BLOCK)


def _block_start(b):
    return b * PERM_BLOCK if isinstance(b, int) else pl.multiple_of(b * PERM_BLOCK, PERM_BLOCK)


def _chunk_rows_max(tc):
    return TOP_K * tc + N_EXPERTS * RUN_ALIGN


def _local_cap(tc):
    return -(-_chunk_rows_max(tc) // PERM_BLOCK) * PERM_BLOCK


def _gather(tabs, pos, h2, r_max, tc):
    n = h2.shape[0]
    n_chunks = n // tc
    return pl.pallas_call(
        functools.partial(_gather_kernel, tc=tc, n_chunks=n_chunks),
        grid_spec=pltpu.PrefetchScalarGridSpec(
            num_scalar_prefetch=4, grid=(n_chunks,),
            in_specs=[pl.BlockSpec((TOP_K, tc), lambda c, *_: (0, c)),
                      pl.BlockSpec((tc, D_MODEL), lambda c, *_: (c, 0))],
            out_specs=pl.BlockSpec(memory_space=pl.ANY),
            scratch_shapes=[pltpu.VMEM((GATHER_SLOTS, _local_cap(tc), D_MODEL), BF16),
                            pltpu.SemaphoreType.DMA((GATHER_SLOTS,))]),
        out_shape=jax.ShapeDtypeStruct((r_max, D_MODEL), BF16),
        compiler_params=_cparams(("arbitrary",)),
        name="moe_gather",
    )(tabs["src"], tabs["cnt"], tabs["dst"], tabs["tot"], pos, h2)


def _ffn_kernel(first_ref, ntile_ref, tstart_ref, trows_ref, total_ref, xs_ref, wg_ref, wu_ref, wd_ref, ys_ref,
                xbuf, ybuf, sem_in, sem_out):
    e = pl.program_id(0)
    total = total_ref[0]

    def in_copy(g):
        s = g % FFN_IN_SLOTS
        n = _aligned(trows_ref[g])
        return pltpu.make_async_copy(xs_ref.at[pl.ds(_aligned(tstart_ref[g]), n)],
                                     xbuf.at[s, pl.ds(0, n)], sem_in.at[s])

    def out_copy(g):
        s = g % 2
        n = _aligned(trows_ref[g])
        return pltpu.make_async_copy(ybuf.at[s, pl.ds(0, n)],
                                     ys_ref.at[pl.ds(_aligned(tstart_ref[g]), n)], sem_out.at[s])

    @pl.when(e == 0)
    def _():
        xbuf[...] = jnp.zeros(xbuf.shape, xbuf.dtype)
        for g in range(FFN_LOOKAHEAD):
            @pl.when(g < total)
            def _():
                in_copy(g).start()


    def body(g, carry):
        @pl.when(g + FFN_LOOKAHEAD < total)
        def _():
            in_copy(g + FFN_LOOKAHEAD).start()
        in_copy(g).wait()

        @pl.when(g >= 2)
        def _():
            out_copy(g - 2).wait()
        x = xbuf[g % FFN_IN_SLOTS]
        a = jnp.dot(x, wg_ref[...].astype(BF16), preferred_element_type=F32)
        u = jnp.dot(x, wu_ref[...].astype(BF16), preferred_element_type=F32)
        h = (a * jax.nn.sigmoid(a) * u).astype(BF16)
        ybuf[g % 2] = jnp.dot(h, wd_ref[...].astype(BF16), preferred_element_type=F32).astype(BF16)
        out_copy(g).start()
        return carry
    lax.fori_loop(first_ref[e], first_ref[e] + ntile_ref[e], body, 0)

    @pl.when(e == pl.num_programs(0) - 1)
    def _():
        @pl.when(total >= 2)
        def _():
            out_copy(total - 2).wait()
        out_copy(total - 1).wait()


def _ffn(tabs, xs, wg, wu, wd):
    ew = lambda a: pl.BlockSpec((None,) + a.shape[1:], lambda e, *_: (e, 0, 0))
    return pl.pallas_call(
        _ffn_kernel,
        grid_spec=pltpu.PrefetchScalarGridSpec(
            num_scalar_prefetch=5, grid=(N_EXPERTS,),
            in_specs=[pl.BlockSpec(memory_space=pl.ANY), ew(wg), ew(wu), ew(wd)],
            out_specs=pl.BlockSpec(memory_space=pl.ANY),
            scratch_shapes=[pltpu.VMEM((FFN_IN_SLOTS, FFN_BLOCK, D_MODEL), BF16),
                            pltpu.VMEM((2, FFN_BLOCK, D_MODEL), BF16),
                            pltpu.SemaphoreType.DMA((FFN_IN_SLOTS,)), pltpu.SemaphoreType.DMA((2,))]),
        out_shape=jax.ShapeDtypeStruct(xs.shape, BF16),
        compiler_params=_cparams(("arbitrary",)),
        name="moe_ffn",
    )(tabs["first_tile"], tabs["n_tiles"], tabs["tile_start"], tabs["tile_rows"], tabs["total_tiles"],
      xs, wg, wu, wd)


def _combine_kernel(src_ref, cnt_ref, dst_ref, tot_ref,
                    pos_ref, gs_ref, x1_ref, h_ref, sg_ref, su_ref, sd_ref, g2_ref, nf_ref, ys_ref,
                    yc_ref, yl_ref, ybuf, acc_ref, sem, *, tc, n_chunks, ctx_chunks):
    c = pl.program_id(0)
    slot = c % 2

    def issue(ci, s):
        def e_body(e8, carry):
            for j in range(DMA_UNROLL):
                i = ci * N_EXPERTS + e8 * DMA_UNROLL + j
                n = _aligned(cnt_ref[i])
                pltpu.make_async_copy(ys_ref.at[pl.ds(_aligned(dst_ref[i]), n)],
                                      ybuf.at[s, pl.ds(_aligned(src_ref[i]), n)], sem.at[s]).start()
            return carry
        lax.fori_loop(0, N_EXPERTS // DMA_UNROLL, e_body, 0)

    @pl.when(c == 0)
    def _():
        ybuf[...] = jnp.zeros(ybuf.shape, ybuf.dtype)
        issue(0, 0)

    @pl.when(c + 1 < n_chunks)
    def _():
        issue(c + 1, 1 - slot)

    h = h_ref[...]
    a = jnp.dot(h, sg_ref[...], preferred_element_type=F32)
    u = jnp.dot(h, su_ref[...], preferred_element_type=F32)
    acc_ref[...] = jnp.dot((a * jax.nn.sigmoid(a) * u).astype(BF16), sd_ref[...], preferred_element_type=F32)

    n = _aligned(tot_ref[c])
    pltpu.make_async_copy(ys_ref.at[pl.ds(0, n)], ybuf.at[slot, pl.ds(0, n)], sem.at[slot]).wait()

    tn = (((0,), (0,)), ((), ()))
    rows = _block_rows(tc)
    gates = gs_ref[...].astype(BF16)

    def gate_block(b):
        rel = _block_relative(pos_ref[...], _block_start(b))
        wt = jnp.zeros((PERM_BLOCK, tc), BF16)
        for k in range(TOP_K):
            wt = jnp.where(rel[k:k + 1, :] == rows, gates[k:k + 1, :], wt)
        return wt

    sure_rows = _sure_blocks(tc) * PERM_BLOCK
    wt_sure = jnp.concatenate([gate_block(b) for b in range(_sure_blocks(tc))], axis=0)
    acc_ref[...] += lax.dot_general(wt_sure, ybuf[slot, pl.ds(0, sure_rows), :], tn, preferred_element_type=F32)

    def tail_block(b, carry):
        acc_ref[...] += lax.dot_general(gate_block(b), ybuf[slot, pl.ds(_block_start(b), PERM_BLOCK), :], tn,
                                        preferred_element_type=F32)
        return carry
    lax.fori_loop(_sure_blocks(tc), _perm_blocks(tot_ref[c]), tail_block, 0)

    x2 = x1_ref[...] + g2_ref[...] * acc_ref[...]
    y = _rms(x2, nf_ref[...])

    @pl.when(c < ctx_chunks)
    def _():
        yc_ref[...] = y

    @pl.when(c >= ctx_chunks)
    def _():
        yl_ref[...] = y


def _combine(tabs, pos, gslot, x1, h2, sg, su, sd, mod, norm_f, ys, tc, row_fn, n_ctx):
    n = x1.shape[0]
    n_chunks = n // tc
    ctx_chunks = n_ctx // tc
    tok = lambda w: pl.BlockSpec((tc, w), lambda c, *_: (c, 0))
    slot = pl.BlockSpec((TOP_K, tc), lambda c, *_: (0, c))
    full = lambda a: pl.BlockSpec(a.shape, lambda c, *_: (0,) * a.ndim)
    return pl.pallas_call(
        functools.partial(_combine_kernel, tc=tc, n_chunks=n_chunks, ctx_chunks=ctx_chunks),
        grid_spec=pltpu.PrefetchScalarGridSpec(
            num_scalar_prefetch=4, grid=(n_chunks,),
            in_specs=[slot, slot, tok(D_MODEL), tok(D_MODEL), full(sg), full(su), full(sd),
                      _mod_spec(5, lambda c, *_: row_fn(c)), full(norm_f), pl.BlockSpec(memory_space=pl.ANY)],
            out_specs=[pl.BlockSpec((tc, D_MODEL), lambda c, *_: (jnp.minimum(c, ctx_chunks - 1), 0)),
                       pl.BlockSpec((tc, D_MODEL), lambda c, *_: (jnp.maximum(c - ctx_chunks, 0), 0))],
            scratch_shapes=[pltpu.VMEM((2, _local_cap(tc), D_MODEL), BF16),
                            pltpu.VMEM((tc, D_MODEL), F32),
                            pltpu.SemaphoreType.DMA((2,))]),
        out_shape=[jax.ShapeDtypeStruct((n_ctx, D_MODEL), F32), jax.ShapeDtypeStruct((n - n_ctx, D_MODEL), F32)],
        compiler_params=_cparams(("arbitrary",)),
        name="moe_combine",
    )(tabs["src"], tabs["cnt"], tabs["dst"], tabs["tot"], pos, gslot, x1, h2, sg, su, sd, mod, norm_f, ys)


def _dispatch_tables(pad_cnt, r_max):
    i32 = jnp.int32
    src = jnp.cumsum(pad_cnt, axis=1) - pad_cnt
    tot = jnp.sum(pad_cnt, axis=1)
    e_rows = jnp.sum(pad_cnt, axis=0)
    e_start = jnp.cumsum(e_rows) - e_rows
    dst = e_start[None, :] + jnp.cumsum(pad_cnt, axis=0) - pad_cnt
    n_tiles = (e_rows + FFN_BLOCK - 1) // FFN_BLOCK
    tile_end = jnp.cumsum(n_tiles)
    first_tile = tile_end - n_tiles
    g = jnp.arange(r_max // FFN_BLOCK + N_EXPERTS)
    before = tile_end[None, :] <= g[:, None]
    owner = jnp.concatenate([jnp.ones_like(before[:, :1]), before[:, :-1]], axis=1) & ~before
    offset = (g - jnp.sum(jnp.where(before, n_tiles, 0), axis=1)) * FFN_BLOCK
    tile_start = jnp.clip(jnp.sum(jnp.where(before, e_rows, 0), axis=1) + offset, 0, r_max - FFN_BLOCK)
    tile_rows = jnp.clip(jnp.sum(jnp.where(owner, e_rows, 0), axis=1) - offset, RUN_ALIGN, FFN_BLOCK)
    return {"src": src.reshape(-1).astype(i32), "cnt": pad_cnt.reshape(-1).astype(i32),
            "dst": dst.reshape(-1).astype(i32), "tot": tot.astype(i32),
            "first_tile": first_tile.astype(i32), "n_tiles": n_tiles.astype(i32),
            "tile_start": tile_start.astype(i32), "tile_rows": tile_rows.astype(i32),
            "total_tiles": tile_end[-1:].astype(i32)}


def _moe(x1, h2, logits_t, bias_col, wg, wu, wd, sg, su, sd, mod, norm_f, row_fn, n_ctx):
    n = x1.shape[0]
    tc = MOE_CHUNK
    pos, gslot, pad_cnt = _route(logits_t, bias_col, tc)
    r_max = (n // tc) * _chunk_rows_max(tc)
    tabs = _dispatch_tables(pad_cnt[:, :, 0], r_max)
    xs = _gather(tabs, pos, h2, r_max, tc)
    ys = _ffn(tabs, xs, wg, wu, wd)
    return _combine(tabs, pos, gslot, x1, h2, sg, su, sd, mod, norm_f, ys, tc, row_fn, n_ctx)


def _channel_dft():
    ang = 2.0 * np.pi * ((np.arange(F_HD)[:, None] * np.arange(F_HD)[None, :]) % F_HD) / F_HD
    eye = np.eye(F_GROUPS)
    return np.concatenate([np.kron(eye, np.cos(ang)), np.kron(eye, -np.sin(ang))], axis=1).astype(np.float32)


def _rope_tables(t):
    pos = np.arange(t)
    row, col = pos // GRID_W, pos % GRID_W
    n_freq = HEAD_DIM // 4
    inv = ROPE_THETA ** (-np.arange(n_freq, dtype=np.float64) / n_freq)
    lane = np.arange(LANES)
    hd = lane % HEAD_DIM
    within = hd % (HEAD_DIM // 2)
    freq = within % n_freq
    first = within < n_freq
    p = np.where((hd < HEAD_DIM // 2)[None, :], row[:, None], col[:, None]).astype(np.float64)
    ang = p * inv[freq][None, :]
    cos, sin = np.cos(ang), np.sin(ang)
    sin_up = np.where(first[None, :], -sin, 0.0)
    sin_dn = np.where(first[None, :], 0.0, sin)
    return [jnp.asarray(a.astype(np.float32)) for a in (cos, sin_up, sin_dn)]


def _block_diag_pairs(wf):
    per = LANES // F_HD
    z = jnp.zeros((F_HD, F_HD), wf.dtype)
    blocks = []
    for c in range(F_GROUPS // per):
        rows = [jnp.concatenate([wf[c * per + r] if r == cc else z for cc in range(per)], axis=1) for r in range(per)]
        blocks.append(jnp.concatenate(rows, axis=0))
    return jnp.stack(blocks)


def kernel(x_prompt, x_sample, cache_k, cache_v, c, c_ctx, w_mod, b_mod, norm1, w_in, w_fourier, sink,
           w_out, norm2, w_router, router_bias, w_gate, w_up, w_down, ws_gate, ws_up, ws_down, norm_f):
    nb_ctx, t_ctx, _ = x_prompt.shape
    nb_lat, t_lat, _ = x_sample.shape
    l = 0
    cvec = jnp.concatenate([c_ctx[None, :], c, jnp.zeros((MOD_ROWS - 1 - nb_lat, D_MODEL), F32)], axis=0)
    mod = _modulation(cvec, w_mod[l], b_mod[l]).reshape(MOD_ROWS, 6, 1, D_MODEL)

    w_in_b = w_in[l].astype(BF16)
    dft_c = jnp.asarray(_channel_dft()).astype(BF16)
    wf_bd = _block_diag_pairs(w_fourier[l]).astype(BF16)
    wo_f = w_out[l][:F_DIM].astype(BF16)
    wo_a = w_out[l][F_DIM:].astype(BF16)
    w_router_t = w_router[l].T.astype(BF16)
    bias_col = router_bias[l].reshape(N_EXPERTS, 1)
    n1 = norm1[l].reshape(1, D_MODEL)
    n2 = norm2[l].reshape(1, D_MODEL)
    nf = norm_f.reshape(1, D_MODEL)
    sink_col = sink[l]
    wg, wu, wd = w_gate[l], w_up[l], w_down[l]
    sg, su, sd = ws_gate[l].astype(BF16), ws_up[l].astype(BF16), ws_down[l].astype(BF16)

    def mixers(x, batch, t, latent, kx, vx, factors, tm):
        n = batch * t
        x2d = x.reshape(n, D_MODEL)
        tiles = t // tm
        row_fn = (lambda i: 1 + i // tiles) if latent else (lambda i: 0)
        rope_tabs = _rope_tables(t) if latent else None
        outs = _project(x2d, mod, n1, w_in_b, dft_c, rope_tabs, tm, tiles, row_fn)
        wr, wi, q, k, v = outs[:5]
        mixf = _fourier(wr, wi, wf_bd, batch, t, factors)
        q3, k3, v3 = (a.reshape(batch, t, -1) for a in (q, k, v))
        if latent:
            mixa = _attention(q3, k3, v3, kx, vx, sink_col, batch, t, True)
        else:
            mixa = _attention(q3, None, None, k3, v3, sink_col, batch, t, False)
        return (x2d, mixf.reshape(n, F_DIM), mixa.reshape(n, ATTN_DIM)), outs[5:]

    ctx, (kf, vf) = mixers(x_prompt, nb_ctx, t_ctx, False, None, None, None, CTX_PROJ_TILE)
    kx = cache_k[:, l].reshape(nb_lat, -1, KV_DIM).astype(BF16)
    vx = cache_v[:, l].reshape(nb_lat, -1, KV_DIM).astype(BF16)
    lat, _ = mixers(x_sample, nb_lat, t_lat, True, kx, vx, LAT_DFT_FACTORS, LAT_PROJ_TILE)

    n_ctx = nb_ctx * t_ctx

    def tile_row(tile):
        ctx_tiles, lat_tiles = n_ctx // tile, t_lat // tile
        return lambda i: jnp.where(i < ctx_tiles, 0, 1 + (i - ctx_tiles) // lat_tiles)

    x1, h2, logits_t = _outproj(ctx, lat, wo_f, wo_a, mod, n2, w_router_t, OUTPROJ_TILE, tile_row(OUTPROJ_TILE))
    y_prompt, y_sample = _moe(x1, h2, logits_t, bias_col, wg, wu, wd, sg, su, sd, mod, nf, tile_row(MOE_CHUNK),
                              n_ctx)
    new_k = kf.reshape(nb_ctx, 1, t_ctx, N_KV_HEADS, HEAD_DIM)
    new_v = vf.reshape(nb_ctx, 1, t_ctx, N_KV_HEADS, HEAD_DIM)
    return (y_prompt.reshape(x_prompt.shape), y_sample.reshape(x_sample.shape), new_k, new_v)
```

```python
import functools

import numpy as np
import jax
import jax.numpy as jnp
from jax import lax
from jax.experimental import pallas as pl
from jax.experimental.pallas import tpu as pltpu

F32 = jnp.float32
BF16 = jnp.bfloat16

D_MODEL = 1024
GRID_W = 64
HEAD_DIM = 64
N_HEADS = 12
N_KV_HEADS = 4
GQA_GROUP = N_HEADS // N_KV_HEADS
ATTN_DIM = N_HEADS * HEAD_DIM
KV_DIM = N_KV_HEADS * HEAD_DIM
F_GROUPS = 4
F_HD = 64
F_DIM = F_GROUPS * F_HD
IN_DIM = F_DIM + ATTN_DIM + 2 * KV_DIM
WINDOW = 128
Q_BLOCK = 128
ROPE_THETA = 10000.0
N_EXPERTS = 64
TOP_K = 8
N_EXPERT_GROUPS = 8
TOPK_GROUPS = 4
EXPERT_DIM = 256
SHARED_DIM = 256
ROUTED_SCALE = 2.5
EPS = 1e-6

LANES = 128
MOD_ROWS = 8
RUN_ALIGN = 16
PERM_BLOCK = 256
FFN_BLOCK = 512
FFN_LOOKAHEAD = 5
FFN_IN_SLOTS = FFN_LOOKAHEAD + 1
MOE_CHUNK = 256
CTX_PROJ_TILE = 1024
LAT_PROJ_TILE = 1024
LAT_DFT_FACTORS = (64, 64)
F_SLABS = F_DIM // LANES
OUTPROJ_TILE = 1024
ROUTE_CHUNKS = 4
SOFTMAX_ROWS = 32
ATTN_BLOCKS = 4
DMA_UNROLL = 16
GATHER_SLOTS = 3
STRAIGHT_EXTRA_BLOCKS = 2
VMEM_LIMIT = 56 * 1024 * 1024


def _cparams(sem):
    return pltpu.CompilerParams(dimension_semantics=sem, vmem_limit_bytes=VMEM_LIMIT)


def _bdot(a, b):
    return jnp.dot(a.astype(BF16), b.astype(BF16), preferred_element_type=F32)


def _rms(x, g):
    return x * lax.rsqrt(jnp.mean(x * x, axis=-1, keepdims=True) + EPS) * g


def _mod_kernel(c_ref, w_ref, b_ref, o_ref):
    c = c_ref[...]
    a = c * jax.nn.sigmoid(c)
    o_ref[...] = _bdot(a, w_ref[...]) + b_ref[...]


def _modulation(cvec, w_mod, b_mod):
    n = w_mod.shape[1]
    tn = 1024
    return pl.pallas_call(
        _mod_kernel,
        grid=(n // tn,),
        in_specs=[pl.BlockSpec((MOD_ROWS, D_MODEL), lambda j: (0, 0)),
                  pl.BlockSpec((D_MODEL, tn), lambda j: (0, j)),
                  pl.BlockSpec((1, tn), lambda j: (0, j))],
        out_specs=pl.BlockSpec((MOD_ROWS, tn), lambda j: (0, j)),
        out_shape=jax.ShapeDtypeStruct((MOD_ROWS, n), F32),
        compiler_params=_cparams(("parallel",)),
        name="modulation",
    )(cvec, w_mod, b_mod.reshape(1, n))


def _mod_spec(piece, row_fn):
    return pl.BlockSpec((None, None, 1, D_MODEL), lambda *idx: (row_fn(*idx), piece, 0, 0))


def _rope(x, cos, sin_up, sin_dn):
    outs = []
    for j in range(x.shape[1] // LANES):
        xj = x[:, j * LANES:(j + 1) * LANES]
        up = pltpu.roll(xj, LANES - 16, axis=1)
        dn = pltpu.roll(xj, 16, axis=1)
        outs.append(xj * cos + up * sin_up + dn * sin_dn)
    return jnp.concatenate(outs, axis=1)


def _proj_kernel(*refs, latent):
    if latent:
        (x_ref, sh_ref, sc_ref, g_ref, w_ref, dft_ref, cos_ref, sup_ref, sdn_ref,
         wr_ref, wi_ref, q_ref, k_ref, v_ref) = refs
    else:
        (x_ref, sh_ref, sc_ref, g_ref, w_ref, dft_ref,
         wr_ref, wi_ref, q_ref, k_ref, v_ref, kf_ref, vf_ref) = refs
    x = x_ref[...]
    h = _rms(x, g_ref[...]) * (1.0 + sc_ref[...]) + sh_ref[...]
    p = _bdot(h, w_ref[...])
    u = p[:, :F_DIM]
    q = p[:, F_DIM:F_DIM + ATTN_DIM]
    k = p[:, F_DIM + ATTN_DIM:F_DIM + ATTN_DIM + KV_DIM]
    v = p[:, F_DIM + ATTN_DIM + KV_DIM:]
    w = _bdot(u, dft_ref[...])
    wr_ref[...] = w[:, :F_DIM]
    wi_ref[...] = w[:, F_DIM:]
    if latent:
        cos, sup, sdn = cos_ref[...], sup_ref[...], sdn_ref[...]
        q = _rope(q, cos, sup, sdn)
        k = _rope(k, cos, sup, sdn)
    else:
        kf_ref[...] = k
        vf_ref[...] = v
    q_ref[...] = (q * (HEAD_DIM ** -0.5)).astype(BF16)
    k_ref[...] = k.astype(BF16)
    v_ref[...] = v.astype(BF16)


def _project(x2d, mod, norm1, w_in, dft_c, rope_tabs, tm, tiles_per_batch, row_fn):
    n = x2d.shape[0]
    latent = rope_tabs is not None
    tok = lambda w: pl.BlockSpec((tm, w), lambda i: (i, 0))
    full = lambda a: pl.BlockSpec(a.shape, lambda i: (0,) * a.ndim)
    in_specs = [tok(D_MODEL), _mod_spec(0, row_fn), _mod_spec(1, row_fn), full(norm1), full(w_in), full(dft_c)]
    args = [x2d, mod, mod, norm1, w_in, dft_c]
    out_specs = [tok(F_DIM), tok(F_DIM), tok(ATTN_DIM), tok(KV_DIM), tok(KV_DIM)]
    out_shape = [jax.ShapeDtypeStruct((n, F_DIM), F32), jax.ShapeDtypeStruct((n, F_DIM), F32),
                 jax.ShapeDtypeStruct((n, ATTN_DIM), BF16), jax.ShapeDtypeStruct((n, KV_DIM), BF16),
                 jax.ShapeDtypeStruct((n, KV_DIM), BF16)]
    if latent:
        pos = pl.BlockSpec((tm, LANES), lambda i: (i % tiles_per_batch, 0))
        in_specs += [pos, pos, pos]
        args += list(rope_tabs)
    else:
        out_specs += [tok(KV_DIM), tok(KV_DIM)]
        out_shape += [jax.ShapeDtypeStruct((n, KV_DIM), F32), jax.ShapeDtypeStruct((n, KV_DIM), F32)]
    return pl.pallas_call(
        functools.partial(_proj_kernel, latent=latent),
        grid=(n // tm,),
        in_specs=in_specs, out_specs=out_specs, out_shape=out_shape,
        compiler_params=_cparams(("parallel",)),
        name="project_latent" if latent else "project_context",
    )(*args)


def _fourier_kernel(wr_ref, wi_ref, a_ref, b_ref, wf_ref, o_ref, yr_ref, yi_ref, z_ref, *, t1, t2, scale):
    for j in range(t2):
        rows = pl.ds(j, t1, stride=t2)
        xin = jnp.concatenate([wr_ref[rows, :], wi_ref[rows, :]], axis=0)
        y = _bdot(a_ref[j], xin)
        yr_ref[rows, :] = y[:t1]
        yi_ref[rows, :] = y[t1:]
    bm = b_ref[...]
    for k1 in range(t1):
        rows = pl.ds(k1 * t2, t2)
        yin = jnp.concatenate([yr_ref[rows, :], yi_ref[rows, :]], axis=0)
        z_ref[pl.ds(k1, t2, stride=t1), :] = _bdot(bm, yin)
    o_ref[...] = (_bdot(z_ref[...], wf_ref[...]) * scale).astype(BF16)


def _dense_fourier_kernel(wr_ref, wi_ref, m_ref, wf_ref, o_ref, *, scale):
    f = _bdot(m_ref[...], jnp.concatenate([wr_ref[...], wi_ref[...]], axis=0))
    for h in range(F_SLABS):
        cols = slice(h * LANES, (h + 1) * LANES)
        o_ref[:, cols] = (_bdot(f[:, cols], wf_ref[h]) * scale).astype(BF16)


def _dft_tables(t1, t2):
    t = t1 * t2
    k1 = np.arange(t1)[None, :, None]
    pos = (t2 * np.arange(t1)[None, None, :] + np.arange(t2)[:, None, None])
    ang = 2.0 * np.pi * ((k1 * pos) % t) / t
    c, s = np.cos(ang), np.sin(ang)
    a = np.concatenate([np.concatenate([c, s], axis=2), np.concatenate([-s, c], axis=2)], axis=1)
    ang2 = 2.0 * np.pi * ((np.arange(t2)[:, None] * np.arange(t2)[None, :]) % t2) / t2
    b = np.concatenate([np.cos(ang2), np.sin(ang2)], axis=1)
    return a.astype(np.float32), b.astype(np.float32)


def _fourier(wr, wi, wf_bd, batch, t, factors):
    wr, wi = wr.reshape(batch, t, F_DIM), wi.reshape(batch, t, F_DIM)
    scale = float((t * F_HD) ** -0.5)
    full = lambda a: pl.BlockSpec(a.shape, lambda bi: (0,) * a.ndim)
    whole = pl.BlockSpec((None, t, F_DIM), lambda bi: (bi, 0, 0))
    if factors is None:
        ang = 2.0 * np.pi * ((np.arange(t)[:, None] * np.arange(t)[None, :]) % t) / t
        m = jnp.asarray(np.concatenate([np.cos(ang), np.sin(ang)], axis=1).astype(np.float32)).astype(BF16)
        return pl.pallas_call(
            functools.partial(_dense_fourier_kernel, scale=scale),
            grid=(batch,),
            in_specs=[whole, whole, full(m), full(wf_bd)],
            out_specs=whole,
            out_shape=jax.ShapeDtypeStruct((batch, t, F_DIM), BF16),
            compiler_params=_cparams(("parallel",)),
            name="fourier_dense_%d" % t,
        )(wr, wi, m, wf_bd)
    t1, t2 = factors
    a_np, b_np = _dft_tables(t1, t2)
    a = jnp.asarray(a_np).astype(BF16)
    b = jnp.asarray(b_np).astype(BF16)
    slab = pl.BlockSpec((None, t, LANES), lambda bi, ci: (bi, 0, ci))
    const = lambda x: pl.BlockSpec(x.shape, lambda bi, ci: (0,) * x.ndim)
    return pl.pallas_call(
        functools.partial(_fourier_kernel, t1=t1, t2=t2, scale=scale),
        grid=(batch, F_SLABS),
        in_specs=[slab, slab, const(a), const(b), pl.BlockSpec((None, LANES, LANES), lambda bi, ci: (ci, 0, 0))],
        out_specs=slab,
        out_shape=jax.ShapeDtypeStruct((batch, t, F_DIM), BF16),
        scratch_shapes=[pltpu.VMEM((t, LANES), F32)] * 3,
        compiler_params=_cparams(("parallel", "parallel")),
        name="fourier_%d" % t,
    )(wr, wi, a, b, wf_bd)


def _attend(q_ref, row0, parts, sink_ref, s_ref, p_ref):
    nt = (((1,), (1,)), ((), ()))
    n_keys = sum(k.shape[0] for k, _, _ in parts)
    outs = []
    for kv in range(N_KV_HEADS):
        heads = range(kv * GQA_GROUP, (kv + 1) * GQA_GROUP)
        hs = slice(kv * HEAD_DIM, (kv + 1) * HEAD_DIM)
        qs = jnp.concatenate([q_ref[row0:row0 + Q_BLOCK, h * HEAD_DIM:(h + 1) * HEAD_DIM] for h in heads], axis=0)
        c0 = 0
        for k, _, ok in parts:
            s = lax.dot_general(qs, k[:, hs], nt, preferred_element_type=F32)
            if ok is not None:
                s = jnp.where(jnp.concatenate([ok] * GQA_GROUP, axis=0), s, -jnp.inf)
            s_ref[:, c0:c0 + s.shape[1]] = s
            c0 += s.shape[1]
        invs = []
        for r0 in range(0, GQA_GROUP * Q_BLOCK, SOFTMAX_ROWS):
            s = s_ref[r0:r0 + SOFTMAX_ROWS, :]
            sk = sink_ref[kv * GQA_GROUP + r0 // Q_BLOCK]
            slabs = [s[:, c:c + LANES] for c in range(0, n_keys, LANES)]
            m = jnp.maximum(jnp.max(functools.reduce(jnp.maximum, slabs), axis=1, keepdims=True), sk)
            p = jnp.exp(s - m)
            pslabs = [p[:, c:c + LANES] for c in range(0, n_keys, LANES)]
            den = jnp.sum(functools.reduce(jnp.add, pslabs), axis=1, keepdims=True) + jnp.exp(sk - m)
            p_ref[r0:r0 + SOFTMAX_ROWS, :] = p.astype(BF16)
            invs.append(1.0 / den)
        o = None
        c0 = 0
        for k, v, _ in parts:
            nk = k.shape[0]
            d = jnp.dot(p_ref[:, c0:c0 + nk], v[:, hs], preferred_element_type=F32)
            o = d if o is None else o + d
            c0 += nk
        o = o * jnp.concatenate(invs, axis=0)
        outs += [o[g * Q_BLOCK:(g + 1) * Q_BLOCK] for g in range(GQA_GROUP)]
    return jnp.concatenate(outs, axis=1).astype(BF16)


def _attn_kernel(*refs, windowed, n_steps):
    if not windowed:
        q_ref, kx_ref, vx_ref, sink_ref, o_ref, s0, p0 = refs
        o_ref[...] = _attend(q_ref, 0, [(kx_ref, vx_ref, None)], sink_ref, s0, p0)
        return
    q_ref, kp_ref, kc_ref, kn_ref, vp_ref, vc_ref, vn_ref, kx_ref, vx_ref, sink_ref, o_ref = refs[:11]
    scratch = refs[11:]
    j = pl.program_id(1)
    a = lax.broadcasted_iota(jnp.int32, (Q_BLOCK, Q_BLOCK), 0)
    b = lax.broadcasted_iota(jnp.int32, (Q_BLOCK, Q_BLOCK), 1)
    later, earlier = b >= a, b <= a
    own = [(kc_ref.at[i * Q_BLOCK:(i + 1) * Q_BLOCK], vc_ref.at[i * Q_BLOCK:(i + 1) * Q_BLOCK])
           for i in range(ATTN_BLOCKS)]
    for i in range(ATTN_BLOCKS):
        prev = own[i - 1] + (later,) if i > 0 else (kp_ref, vp_ref, later & (j > 0))
        nxt = own[i + 1] + (earlier,) if i < ATTN_BLOCKS - 1 else (kn_ref, vn_ref, earlier & (j < n_steps - 1))
        parts = [prev, own[i] + (None,), nxt, (kx_ref, vx_ref, None)]
        o_ref[i * Q_BLOCK:(i + 1) * Q_BLOCK, :] = _attend(q_ref, i * Q_BLOCK, parts, sink_ref,
                                                           scratch[2 * i], scratch[2 * i + 1])


def _attention(q, k, v, kx, vx, sink_col, batch, t, windowed):
    nb = t // Q_BLOCK
    per_step = ATTN_BLOCKS if windowed else 1
    n_ctx = kx.shape[1]
    n_keys = n_ctx + (3 * Q_BLOCK if windowed else 0)
    qspec = pl.BlockSpec((None, per_step * Q_BLOCK, ATTN_DIM), lambda b, j: (b, j, 0))
    xspec = pl.BlockSpec((None, n_ctx, KV_DIM), lambda b, j: (b, 0, 0))
    sspec = pl.BlockSpec(memory_space=pltpu.SMEM)
    if windowed:
        prev = pl.BlockSpec((None, Q_BLOCK, KV_DIM), lambda b, j: (b, jnp.maximum(per_step * j - 1, 0), 0))
        own = pl.BlockSpec((None, per_step * Q_BLOCK, KV_DIM), lambda b, j: (b, j, 0))
        nxt = pl.BlockSpec((None, Q_BLOCK, KV_DIM), lambda b, j: (b, jnp.minimum(per_step * (j + 1), nb - 1), 0))
        in_specs = [qspec, prev, own, nxt, prev, own, nxt, xspec, xspec, sspec]
        args = [q, k, k, k, v, v, v, kx, vx, sink_col]
    else:
        in_specs = [qspec, xspec, xspec, sspec]
        args = [q, kx, vx, sink_col]
    scores = pltpu.VMEM((GQA_GROUP * Q_BLOCK, n_keys), F32)
    probs = pltpu.VMEM((GQA_GROUP * Q_BLOCK, n_keys), BF16)
    return pl.pallas_call(
        functools.partial(_attn_kernel, windowed=windowed, n_steps=nb // per_step),
        grid=(batch, nb // per_step),
        in_specs=in_specs, out_specs=qspec,
        out_shape=jax.ShapeDtypeStruct((batch, t, ATTN_DIM), BF16),
        scratch_shapes=[scores, probs] * per_step,
        compiler_params=_cparams(("parallel", "parallel")),
        name="attention_latent" if windowed else "attention_context",
    )(*args)


def _outproj_kernel(xc_ref, mfc_ref, mac_ref, xl_ref, mfl_ref, mal_ref,
                    wof_ref, woa_ref, g1_ref, sh_ref, sc_ref, n2_ref, wr_ref,
                    x1_ref, h_ref, lg_ref, *, ctx_tiles):
    def body(x_ref, mf_ref, ma_ref):
        o = jnp.dot(mf_ref[...], wof_ref[...], preferred_element_type=F32)
        o = o + jnp.dot(ma_ref[...], woa_ref[...], preferred_element_type=F32)
        x1 = x_ref[...] + g1_ref[...] * o
        x1_ref[...] = x1
        h = (_rms(x1, n2_ref[...]) * (1.0 + sc_ref[...]) + sh_ref[...]).astype(BF16)
        h_ref[...] = h
        lg_ref[...] = lax.dot_general(wr_ref[...], h, (((1,), (1,)), ((), ())), preferred_element_type=F32)

    is_ctx = pl.program_id(0) < ctx_tiles
    pl.when(is_ctx)(lambda: body(xc_ref, mfc_ref, mac_ref))
    pl.when(jnp.logical_not(is_ctx))(lambda: body(xl_ref, mfl_ref, mal_ref))


def _outproj(ctx, lat, wo_f, wo_a, mod, norm2, w_router_t, tm, row_fn):
    n_c, n_l = ctx[0].shape[0], lat[0].shape[0]
    ctx_tiles = n_c // tm
    n = n_c + n_l
    ctok = lambda w: pl.BlockSpec((tm, w), lambda i: (jnp.minimum(i, ctx_tiles - 1), 0))
    ltok = lambda w: pl.BlockSpec((tm, w), lambda i: (jnp.maximum(i - ctx_tiles, 0), 0))
    tok = lambda w: pl.BlockSpec((tm, w), lambda i: (i, 0))
    full = lambda a: pl.BlockSpec(a.shape, lambda i: (0,) * a.ndim)
    widths = (D_MODEL, F_DIM, ATTN_DIM)
    return pl.pallas_call(
        functools.partial(_outproj_kernel, ctx_tiles=ctx_tiles),
        grid=(n // tm,),
        in_specs=[ctok(w) for w in widths] + [ltok(w) for w in widths] + [
            full(wo_f), full(wo_a), _mod_spec(2, row_fn), _mod_spec(3, row_fn), _mod_spec(4, row_fn),
            full(norm2), full(w_router_t)],
        out_specs=[tok(D_MODEL), tok(D_MODEL), pl.BlockSpec((N_EXPERTS, tm), lambda i: (0, i))],
        out_shape=[jax.ShapeDtypeStruct((n, D_MODEL), F32), jax.ShapeDtypeStruct((n, D_MODEL), BF16),
                   jax.ShapeDtypeStruct((N_EXPERTS, n), F32)],
        compiler_params=_cparams(("parallel",)),
        name="outproj",
    )(*ctx, *lat, wo_f, wo_a, mod, mod, mod, norm2, w_router_t)


def _top_rows(vals, k):
    n = vals.shape[0]
    idx = lax.broadcasted_iota(jnp.int32, vals.shape, 0)
    picked = jnp.zeros(vals.shape, F32)
    for _ in range(k):
        best = jnp.max(vals, axis=0, keepdims=True)
        first = jnp.min(jnp.where(vals == best, idx, n), axis=0, keepdims=True)
        hit = idx == first
        picked = jnp.where(hit, 1.0, picked)
        vals = jnp.where(hit, -jnp.inf, vals)
    return picked


def _route_chunk(logits, bias, lo, up):
    s = jax.nn.sigmoid(logits)
    sc = s + bias
    tn = s.shape[1]
    per = N_EXPERTS // N_EXPERT_GROUPS
    g3 = sc.reshape(N_EXPERT_GROUPS, per, tn)
    member = lax.broadcasted_iota(jnp.int32, g3.shape, 1)
    m1 = jnp.max(g3, axis=1, keepdims=True)
    first = jnp.min(jnp.where(g3 == m1, member, per), axis=1, keepdims=True)
    m2 = jnp.max(jnp.where(member == first, -jnp.inf, g3), axis=1, keepdims=True)
    gscore = (m1 + m2).reshape(N_EXPERT_GROUPS, tn)
    gsel = _top_rows(gscore, TOPK_GROUPS)
    emask = jnp.broadcast_to(gsel.reshape(N_EXPERT_GROUPS, 1, tn), g3.shape).reshape(N_EXPERTS, tn)
    masked = jnp.where(emask > 0.5, sc, -jnp.inf)
    self = _top_rows(masked, TOP_K)
    sel = self > 0.5
    w = jnp.where(sel, s, 0.0)
    gate = w / jnp.sum(w, axis=0, keepdims=True) * ROUTED_SCALE

    selb = self.astype(BF16)
    cnt = jnp.sum(self, axis=1, keepdims=True)
    pad = jnp.maximum(jnp.floor((cnt + (RUN_ALIGN - 1)) * (1.0 / RUN_ALIGN)), 1.0) * RUN_ALIGN
    soff = jnp.dot(lo, jnp.broadcast_to(pad, (N_EXPERTS, LANES)).astype(BF16), preferred_element_type=F32)[:, :1]
    rank = jnp.dot(selb, up, preferred_element_type=F32)
    kidx = jnp.dot(lo, selb, preferred_element_type=F32)
    pos_e = jnp.where(sel, soff + rank, 0.0)
    rows_p, rows_g = [], []
    for k in range(TOP_K):
        m = kidx == k
        rows_p.append(jnp.sum(jnp.where(m, pos_e, 0.0), axis=0, keepdims=True))
        rows_g.append(jnp.sum(jnp.where(m, gate, 0.0), axis=0, keepdims=True))
    return jnp.concatenate(rows_p, axis=0).astype(jnp.int32), jnp.concatenate(rows_g, axis=0), pad.astype(jnp.int32)


def _route_kernel(lg_ref, bias_ref, lo_ref, up_ref, pos_ref, gs_ref, cnt_ref, *, tc):
    for j in range(ROUTE_CHUNKS):
        cols = slice(j * tc, (j + 1) * tc)
        pos, gates, pad = _route_chunk(lg_ref[:, cols], bias_ref[...], lo_ref[...], up_ref[...])
        pos_ref[:, cols] = pos
        gs_ref[:, cols] = gates
        cnt_ref[j] = pad


def _route(logits_t, bias_col, tc):
    n = logits_t.shape[1]
    lo = jnp.asarray(np.tril(np.ones((N_EXPERTS, N_EXPERTS), np.float32), -1)).astype(BF16)
    up = jnp.asarray(np.triu(np.ones((tc, tc), np.float32), 1)).astype(BF16)
    tn = tc * ROUTE_CHUNKS
    slot = pl.BlockSpec((TOP_K, tn), lambda i: (0, i))
    return pl.pallas_call(
        functools.partial(_route_kernel, tc=tc),
        grid=(n // tn,),
        in_specs=[pl.BlockSpec((N_EXPERTS, tn), lambda i: (0, i)),
                  pl.BlockSpec((N_EXPERTS, 1), lambda i: (0, 0)),
                  pl.BlockSpec(lo.shape, lambda i: (0, 0)),
                  pl.BlockSpec(up.shape, lambda i: (0, 0))],
        out_specs=[slot, slot, pl.BlockSpec((ROUTE_CHUNKS, N_EXPERTS, 1), lambda i: (i, 0, 0))],
        out_shape=[jax.ShapeDtypeStruct((TOP_K, n), jnp.int32), jax.ShapeDtypeStruct((TOP_K, n), F32),
                   jax.ShapeDtypeStruct((n // tc, N_EXPERTS, 1), jnp.int32)],
        compiler_params=_cparams(("parallel",)),
        name="route",
    )(logits_t, bias_col, lo, up)


def _aligned(x):
    return pl.multiple_of(x, RUN_ALIGN)


def _block_rows(tc):
    return lax.broadcasted_iota(jnp.int32, (PERM_BLOCK, tc), 0).astype(F32).astype(BF16)


def _block_relative(pos, r0):
    return (pos - r0).astype(F32).astype(BF16)


def _gather_kernel(src_ref, cnt_ref, dst_ref, tot_ref,
                   pos_ref, x_ref, xs_ref, z_ref, sem, *, tc, n_chunks):
    c = pl.program_id(0)
    slot = c % GATHER_SLOTS

    def wait_chunk(ci, s):
        n = _aligned(tot_ref[ci])
        pltpu.make_async_copy(z_ref.at[s, pl.ds(0, n)], xs_ref.at[pl.ds(0, n)], sem.at[s]).wait()

    def start_runs(ci, s, first, count):
        for j in range(count):
            i = ci * N_EXPERTS + first + j
            n = _aligned(cnt_ref[i])
            pltpu.make_async_copy(z_ref.at[s, pl.ds(_aligned(src_ref[i]), n)],
                                  xs_ref.at[pl.ds(_aligned(dst_ref[i]), n)], sem.at[s]).start()

    @pl.when(c >= GATHER_SLOTS)
    def _():
        wait_chunk(c - GATHER_SLOTS, slot)

    rows = _block_rows(tc)

    def permute_block(b):
        r0 = _block_start(b)
        rel = _block_relative(pos_ref[...], r0)
        onehot = jnp.zeros((PERM_BLOCK, tc), BF16)
        for k in range(TOP_K):
            onehot = jnp.where(rel[k:k + 1, :] == rows, jnp.ones((), BF16), onehot)
        z_ref[slot, pl.ds(r0, PERM_BLOCK), :] = jnp.dot(
            onehot, x_ref[...], preferred_element_type=F32).astype(BF16)

    for b in range(_sure_blocks(tc)):
        permute_block(b)

    def tail_block(b, carry):
        permute_block(b)
        return carry
    lax.fori_loop(_sure_blocks(tc), _perm_blocks(tot_ref[c]), tail_block, 0)

    def e_body(e8, carry):
        start_runs(c, slot, e8 * DMA_UNROLL, DMA_UNROLL)
        return carry
    lax.fori_loop(0, N_EXPERTS // DMA_UNROLL, e_body, 0)

    @pl.when(c == n_chunks - 1)
    def _():
        for back in range(min(GATHER_SLOTS, n_chunks)):
            wait_chunk(c - back, (c - back) % GATHER_SLOTS)


def _perm_blocks(rows):
    return (rows + PERM_BLOCK - 1) // PERM_BLOCK


def _sure_blocks(tc):
    return min(TOP_K * tc // PERM_BLOCK + STRAIGHT_EXTRA_BLOCKS, _local_cap(tc) // PERM_BLOCK)


def _block_start(b):
    return b * PERM_BLOCK if isinstance(b, int) else pl.multiple_of(b * PERM_BLOCK, PERM_BLOCK)


def _chunk_rows_max(tc):
    return TOP_K * tc + N_EXPERTS * RUN_ALIGN


def _local_cap(tc):
    return -(-_chunk_rows_max(tc) // PERM_BLOCK) * PERM_BLOCK


def _gather(tabs, pos, h2, r_max, tc):
    n = h2.shape[0]
    n_chunks = n // tc
    return pl.pallas_call(
        functools.partial(_gather_kernel, tc=tc, n_chunks=n_chunks),
        grid_spec=pltpu.PrefetchScalarGridSpec(
            num_scalar_prefetch=4, grid=(n_chunks,),
            in_specs=[pl.BlockSpec((TOP_K, tc), lambda c, *_: (0, c)),
                      pl.BlockSpec((tc, D_MODEL), lambda c, *_: (c, 0))],
            out_specs=pl.BlockSpec(memory_space=pl.ANY),
            scratch_shapes=[pltpu.VMEM((GATHER_SLOTS, _local_cap(tc), D_MODEL), BF16),
                            pltpu.SemaphoreType.DMA((GATHER_SLOTS,))]),
        out_shape=jax.ShapeDtypeStruct((r_max, D_MODEL), BF16),
        compiler_params=_cparams(("arbitrary",)),
        name="moe_gather",
    )(tabs["src"], tabs["cnt"], tabs["dst"], tabs["tot"], pos, h2)


def _ffn_kernel(first_ref, ntile_ref, tstart_ref, trows_ref, total_ref, xs_ref, wg_ref, wu_ref, wd_ref, ys_ref,
                xbuf, ybuf, sem_in, sem_out):
    e = pl.program_id(0)
    total = total_ref[0]

    def in_copy(g):
        s = g % FFN_IN_SLOTS
        n = _aligned(trows_ref[g])
        return pltpu.make_async_copy(xs_ref.at[pl.ds(_aligned(tstart_ref[g]), n)],
                                     xbuf.at[s, pl.ds(0, n)], sem_in.at[s])

    def out_copy(g):
        s = g % 2
        n = _aligned(trows_ref[g])
        return pltpu.make_async_copy(ybuf.at[s, pl.ds(0, n)],
                                     ys_ref.at[pl.ds(_aligned(tstart_ref[g]), n)], sem_out.at[s])

    @pl.when(e == 0)
    def _():
        xbuf[...] = jnp.zeros(xbuf.shape, xbuf.dtype)
        for g in range(FFN_LOOKAHEAD):
            @pl.when(g < total)
            def _():
                in_copy(g).start()


    def body(g, carry):
        @pl.when(g + FFN_LOOKAHEAD < total)
        def _():
            in_copy(g + FFN_LOOKAHEAD).start()
        in_copy(g).wait()

        @pl.when(g >= 2)
        def _():
            out_copy(g - 2).wait()
        x = xbuf[g % FFN_IN_SLOTS]
        a = jnp.dot(x, wg_ref[...].astype(BF16), preferred_element_type=F32)
        u = jnp.dot(x, wu_ref[...].astype(BF16), preferred_element_type=F32)
        h = (a * jax.nn.sigmoid(a) * u).astype(BF16)
        ybuf[g % 2] = jnp.dot(h, wd_ref[...].astype(BF16), preferred_element_type=F32).astype(BF16)
        out_copy(g).start()
        return carry
    lax.fori_loop(first_ref[e], first_ref[e] + ntile_ref[e], body, 0)

    @pl.when(e == pl.num_programs(0) - 1)
    def _():
        @pl.when(total >= 2)
        def _():
            out_copy(total - 2).wait()
        out_copy(total - 1).wait()


def _ffn(tabs, xs, wg, wu, wd):
    ew = lambda a: pl.BlockSpec((None,) + a.shape[1:], lambda e, *_: (e, 0, 0))
    return pl.pallas_call(
        _ffn_kernel,
        grid_spec=pltpu.PrefetchScalarGridSpec(
            num_scalar_prefetch=5, grid=(N_EXPERTS,),
            in_specs=[pl.BlockSpec(memory_space=pl.ANY), ew(wg), ew(wu), ew(wd)],
            out_specs=pl.BlockSpec(memory_space=pl.ANY),
            scratch_shapes=[pltpu.VMEM((FFN_IN_SLOTS, FFN_BLOCK, D_MODEL), BF16),
                            pltpu.VMEM((2, FFN_BLOCK, D_MODEL), BF16),
                            pltpu.SemaphoreType.DMA((FFN_IN_SLOTS,)), pltpu.SemaphoreType.DMA((2,))]),
        out_shape=jax.ShapeDtypeStruct(xs.shape, BF16),
        compiler_params=_cparams(("arbitrary",)),
        name="moe_ffn",
    )(tabs["first_tile"], tabs["n_tiles"], tabs["tile_start"], tabs["tile_rows"], tabs["total_tiles"],
      xs, wg, wu, wd)


def _combine_kernel(src_ref, cnt_ref, dst_ref, tot_ref,
                    pos_ref, gs_ref, x1_ref, h_ref, sg_ref, su_ref, sd_ref, g2_ref, nf_ref, ys_ref,
                    yc_ref, yl_ref, ybuf, acc_ref, sem, *, tc, n_chunks, ctx_chunks):
    c = pl.program_id(0)
    slot = c % 2

    def issue(ci, s):
        def e_body(e8, carry):
            for j in range(DMA_UNROLL):
                i = ci * N_EXPERTS + e8 * DMA_UNROLL + j
                n = _aligned(cnt_ref[i])
                pltpu.make_async_copy(ys_ref.at[pl.ds(_aligned(dst_ref[i]), n)],
                                      ybuf.at[s, pl.ds(_aligned(src_ref[i]), n)], sem.at[s]).start()
            return carry
        lax.fori_loop(0, N_EXPERTS // DMA_UNROLL, e_body, 0)

    @pl.when(c == 0)
    def _():
        ybuf[...] = jnp.zeros(ybuf.shape, ybuf.dtype)
        issue(0, 0)

    @pl.when(c + 1 < n_chunks)
    def _():
        issue(c + 1, 1 - slot)

    h = h_ref[...]
    a = jnp.dot(h, sg_ref[...], preferred_element_type=F32)
    u = jnp.dot(h, su_ref[...], preferred_element_type=F32)
    acc_ref[...] = jnp.dot((a * jax.nn.sigmoid(a) * u).astype(BF16), sd_ref[...], preferred_element_type=F32)

    n = _aligned(tot_ref[c])
    pltpu.make_async_copy(ys_ref.at[pl.ds(0, n)], ybuf.at[slot, pl.ds(0, n)], sem.at[slot]).wait()

    tn = (((0,), (0,)), ((), ()))
    rows = _block_rows(tc)
    gates = gs_ref[...].astype(BF16)

    def gate_block(b):
        rel = _block_relative(pos_ref[...], _block_start(b))
        wt = jnp.zeros((PERM_BLOCK, tc), BF16)
        for k in range(TOP_K):
            wt = jnp.where(rel[k:k + 1, :] == rows, gates[k:k + 1, :], wt)
        return wt

    sure_rows = _sure_blocks(tc) * PERM_BLOCK
    wt_sure = jnp.concatenate([gate_block(b) for b in range(_sure_blocks(tc))], axis=0)
    acc_ref[...] += lax.dot_general(wt_sure, ybuf[slot, pl.ds(0, sure_rows), :], tn, preferred_element_type=F32)

    def tail_block(b, carry):
        acc_ref[...] += lax.dot_general(gate_block(b), ybuf[slot, pl.ds(_block_start(b), PERM_BLOCK), :], tn,
                                        preferred_element_type=F32)
        return carry
    lax.fori_loop(_sure_blocks(tc), _perm_blocks(tot_ref[c]), tail_block, 0)

    x2 = x1_ref[...] + g2_ref[...] * acc_ref[...]
    y = _rms(x2, nf_ref[...])

    @pl.when(c < ctx_chunks)
    def _():
        yc_ref[...] = y

    @pl.when(c >= ctx_chunks)
    def _():
        yl_ref[...] = y


def _combine(tabs, pos, gslot, x1, h2, sg, su, sd, mod, norm_f, ys, tc, row_fn, n_ctx):
    n = x1.shape[0]
    n_chunks = n // tc
    ctx_chunks = n_ctx // tc
    tok = lambda w: pl.BlockSpec((tc, w), lambda c, *_: (c, 0))
    slot = pl.BlockSpec((TOP_K, tc), lambda c, *_: (0, c))
    full = lambda a: pl.BlockSpec(a.shape, lambda c, *_: (0,) * a.ndim)
    return pl.pallas_call(
        functools.partial(_combine_kernel, tc=tc, n_chunks=n_chunks, ctx_chunks=ctx_chunks),
        grid_spec=pltpu.PrefetchScalarGridSpec(
            num_scalar_prefetch=4, grid=(n_chunks,),
            in_specs=[slot, slot, tok(D_MODEL), tok(D_MODEL), full(sg), full(su), full(sd),
                      _mod_spec(5, lambda c, *_: row_fn(c)), full(norm_f), pl.BlockSpec(memory_space=pl.ANY)],
            out_specs=[pl.BlockSpec((tc, D_MODEL), lambda c, *_: (jnp.minimum(c, ctx_chunks - 1), 0)),
                       pl.BlockSpec((tc, D_MODEL), lambda c, *_: (jnp.maximum(c - ctx_chunks, 0), 0))],
            scratch_shapes=[pltpu.VMEM((2, _local_cap(tc), D_MODEL), BF16),
                            pltpu.VMEM((tc, D_MODEL), F32),
                            pltpu.SemaphoreType.DMA((2,))]),
        out_shape=[jax.ShapeDtypeStruct((n_ctx, D_MODEL), F32), jax.ShapeDtypeStruct((n - n_ctx, D_MODEL), F32)],
        compiler_params=_cparams(("arbitrary",)),
        name="moe_combine",
    )(tabs["src"], tabs["cnt"], tabs["dst"], tabs["tot"], pos, gslot, x1, h2, sg, su, sd, mod, norm_f, ys)


def _dispatch_tables(pad_cnt, r_max):
    i32 = jnp.int32
    src = jnp.cumsum(pad_cnt, axis=1) - pad_cnt
    tot = jnp.sum(pad_cnt, axis=1)
    e_rows = jnp.sum(pad_cnt, axis=0)
    e_start = jnp.cumsum(e_rows) - e_rows
    dst = e_start[None, :] + jnp.cumsum(pad_cnt, axis=0) - pad_cnt
    n_tiles = (e_rows + FFN_BLOCK - 1) // FFN_BLOCK
    tile_end = jnp.cumsum(n_tiles)
    first_tile = tile_end - n_tiles
    g = jnp.arange(r_max // FFN_BLOCK + N_EXPERTS)
    before = tile_end[None, :] <= g[:, None]
    owner = jnp.concatenate([jnp.ones_like(before[:, :1]), before[:, :-1]], axis=1) & ~before
    offset = (g - jnp.sum(jnp.where(before, n_tiles, 0), axis=1)) * FFN_BLOCK
    tile_start = jnp.clip(jnp.sum(jnp.where(before, e_rows, 0), axis=1) + offset, 0, r_max - FFN_BLOCK)
    tile_rows = jnp.clip(jnp.sum(jnp.where(owner, e_rows, 0), axis=1) - offset, RUN_ALIGN, FFN_BLOCK)
    return {"src": src.reshape(-1).astype(i32), "cnt": pad_cnt.reshape(-1).astype(i32),
            "dst": dst.reshape(-1).astype(i32), "tot": tot.astype(i32),
            "first_tile": first_tile.astype(i32), "n_tiles": n_tiles.astype(i32),
            "tile_start": tile_start.astype(i32), "tile_rows": tile_rows.astype(i32),
            "total_tiles": tile_end[-1:].astype(i32)}


def _moe(x1, h2, logits_t, bias_col, wg, wu, wd, sg, su, sd, mod, norm_f, row_fn, n_ctx):
    n = x1.shape[0]
    tc = MOE_CHUNK
    pos, gslot, pad_cnt = _route(logits_t, bias_col, tc)
    r_max = (n // tc) * _chunk_rows_max(tc)
    tabs = _dispatch_tables(pad_cnt[:, :, 0], r_max)
    xs = _gather(tabs, pos, h2, r_max, tc)
    ys = _ffn(tabs, xs, wg, wu, wd)
    return _combine(tabs, pos, gslot, x1, h2, sg, su, sd, mod, norm_f, ys, tc, row_fn, n_ctx)


def _channel_dft():
    ang = 2.0 * np.pi * ((np.arange(F_HD)[:, None] * np.arange(F_HD)[None, :]) % F_HD) / F_HD
    eye = np.eye(F_GROUPS)
    return np.concatenate([np.kron(eye, np.cos(ang)), np.kron(eye, -np.sin(ang))], axis=1).astype(np.float32)


def _rope_tables(t):
    pos = np.arange(t)
    row, col = pos // GRID_W, pos % GRID_W
    n_freq = HEAD_DIM // 4
    inv = ROPE_THETA ** (-np.arange(n_freq, dtype=np.float64) / n_freq)
    lane = np.arange(LANES)
    hd = lane % HEAD_DIM
    within = hd % (HEAD_DIM // 2)
    freq = within % n_freq
    first = within < n_freq
    p = np.where((hd < HEAD_DIM // 2)[None, :], row[:, None], col[:, None]).astype(np.float64)
    ang = p * inv[freq][None, :]
    cos, sin = np.cos(ang), np.sin(ang)
    sin_up = np.where(first[None, :], -sin, 0.0)
    sin_dn = np.where(first[None, :], 0.0, sin)
    return [jnp.asarray(a.astype(np.float32)) for a in (cos, sin_up, sin_dn)]


def _block_diag_pairs(wf):
    per = LANES // F_HD
    z = jnp.zeros((F_HD, F_HD), wf.dtype)
    blocks = []
    for c in range(F_GROUPS // per):
        rows = [jnp.concatenate([wf[c * per + r] if r == cc else z for cc in range(per)], axis=1) for r in range(per)]
        blocks.append(jnp.concatenate(rows, axis=0))
    return jnp.stack(blocks)


def kernel(x_prompt, x_sample, cache_k, cache_v, c, c_ctx, w_mod, b_mod, norm1, w_in, w_fourier, sink,
           w_out, norm2, w_router, router_bias, w_gate, w_up, w_down, ws_gate, ws_up, ws_down, norm_f):
    nb_ctx, t_ctx, _ = x_prompt.shape
    nb_lat, t_lat, _ = x_sample.shape
    l = 0
    cvec = jnp.concatenate([c_ctx[None, :], c, jnp.zeros((MOD_ROWS - 1 - nb_lat, D_MODEL), F32)], axis=0)
    mod = _modulation(cvec, w_mod[l], b_mod[l]).reshape(MOD_ROWS, 6, 1, D_MODEL)

    w_in_b = w_in[l].astype(BF16)
    dft_c = jnp.asarray(_channel_dft()).astype(BF16)
    wf_bd = _block_diag_pairs(w_fourier[l]).astype(BF16)
    wo_f = w_out[l][:F_DIM].astype(BF16)
    wo_a = w_out[l][F_DIM:].astype(BF16)
    w_router_t = w_router[l].T.astype(BF16)
    bias_col = router_bias[l].reshape(N_EXPERTS, 1)
    n1 = norm1[l].reshape(1, D_MODEL)
    n2 = norm2[l].reshape(1, D_MODEL)
    nf = norm_f.reshape(1, D_MODEL)
    sink_col = sink[l]
    wg, wu, wd = w_gate[l], w_up[l], w_down[l]
    sg, su, sd = ws_gate[l].astype(BF16), ws_up[l].astype(BF16), ws_down[l].astype(BF16)

    def mixers(x, batch, t, latent, kx, vx, factors, tm):
        n = batch * t
        x2d = x.reshape(n, D_MODEL)
        tiles = t // tm
        row_fn = (lambda i: 1 + i // tiles) if latent else (lambda i: 0)
        rope_tabs = _rope_tables(t) if latent else None
        outs = _project(x2d, mod, n1, w_in_b, dft_c, rope_tabs, tm, tiles, row_fn)
        wr, wi, q, k, v = outs[:5]
        mixf = _fourier(wr, wi, wf_bd, batch, t, factors)
        q3, k3, v3 = (a.reshape(batch, t, -1) for a in (q, k, v))
        if latent:
            mixa = _attention(q3, k3, v3, kx, vx, sink_col, batch, t, True)
        else:
            mixa = _attention(q3, None, None, k3, v3, sink_col, batch, t, False)
        return (x2d, mixf.reshape(n, F_DIM), mixa.reshape(n, ATTN_DIM)), outs[5:]

    ctx, (kf, vf) = mixers(x_prompt, nb_ctx, t_ctx, False, None, None, None, CTX_PROJ_TILE)
    kx = cache_k[:, l].reshape(nb_lat, -1, KV_DIM).astype(BF16)
    vx = cache_v[:, l].reshape(nb_lat, -1, KV_DIM).astype(BF16)
    lat, _ = mixers(x_sample, nb_lat, t_lat, True, kx, vx, LAT_DFT_FACTORS, LAT_PROJ_TILE)

    n_ctx = nb_ctx * t_ctx

    def tile_row(tile):
        ctx_tiles, lat_tiles = n_ctx // tile, t_lat // tile
        return lambda i: jnp.where(i < ctx_tiles, 0, 1 + (i - ctx_tiles) // lat_tiles)

    x1, h2, logits_t = _outproj(ctx, lat, wo_f, wo_a, mod, n2, w_router_t, OUTPROJ_TILE, tile_row(OUTPROJ_TILE))
    y_prompt, y_sample = _moe(x1, h2, logits_t, bias_col, wg, wu, wd, sg, su, sd, mod, nf, tile_row(MOE_CHUNK),
                              n_ctx)
    new_k = kf.reshape(nb_ctx, 1, t_ctx, N_KV_HEADS, HEAD_DIM)
    new_v = vf.reshape(nb_ctx, 1, t_ctx, N_KV_HEADS, HEAD_DIM)
    return (y_prompt.reshape(x_prompt.shape), y_sample.reshape(x_sample.shape), new_k, new_v)
```

```python
import functools

import numpy as np
import jax
import jax.numpy as jnp
from jax import lax
from jax.experimental import pallas as pl
from jax.experimental.pallas import tpu as pltpu

F32 = jnp.float32
BF16 = jnp.bfloat16

D_MODEL = 1024
GRID_W = 64
HEAD_DIM = 64
N_HEADS = 12
N_KV_HEADS = 4
GQA_GROUP = N_HEADS // N_KV_HEADS
ATTN_DIM = N_HEADS * HEAD_DIM
KV_DIM = N_KV_HEADS * HEAD_DIM
F_GROUPS = 4
F_HD = 64
F_DIM = F_GROUPS * F_HD
IN_DIM = F_DIM + ATTN_DIM + 2 * KV_DIM
WINDOW = 128
Q_BLOCK = 128
ROPE_THETA = 10000.0
N_EXPERTS = 64
TOP_K = 8
N_EXPERT_GROUPS = 8
TOPK_GROUPS = 4
EXPERT_DIM = 256
SHARED_DIM = 256
ROUTED_SCALE = 2.5
EPS = 1e-6

LANES = 128
MOD_ROWS = 8
RUN_ALIGN = 16
PERM_BLOCK = 256
FFN_BLOCK = 512
FFN_LOOKAHEAD = 5
FFN_IN_SLOTS = FFN_LOOKAHEAD + 1
MOE_CHUNK = 256
CTX_PROJ_TILE = 1024
LAT_PROJ_TILE = 1024
LAT_DFT_FACTORS = (64, 64)
F_SLABS = F_DIM // LANES
OUTPROJ_TILE = 1024
ROUTE_CHUNKS = 4
SOFTMAX_ROWS = 32
ATTN_BLOCKS = 4
DMA_UNROLL = 16
GATHER_SLOTS = 3
STRAIGHT_EXTRA_BLOCKS = 2
VMEM_LIMIT = 56 * 1024 * 1024


def _cparams(sem):
    return pltpu.CompilerParams(dimension_semantics=sem, vmem_limit_bytes=VMEM_LIMIT)


def _bdot(a, b):
    return jnp.dot(a.astype(BF16), b.astype(BF16), preferred_element_type=F32)


def _rms(x, g):
    return x * lax.rsqrt(jnp.mean(x * x, axis=-1, keepdims=True) + EPS) * g


def _mod_kernel(c_ref, w_ref, b_ref, o_ref):
    c = c_ref[...]
    a = c * jax.nn.sigmoid(c)
    o_ref[...] = _bdot(a, w_ref[...]) + b_ref[...]


def _modulation(cvec, w_mod, b_mod):
    n = w_mod.shape[1]
    tn = 1024
    return pl.pallas_call(
        _mod_kernel,
        grid=(n // tn,),
        in_specs=[pl.BlockSpec((MOD_ROWS, D_MODEL), lambda j: (0, 0)),
                  pl.BlockSpec((D_MODEL, tn), lambda j: (0, j)),
                  pl.BlockSpec((1, tn), lambda j: (0, j))],
        out_specs=pl.BlockSpec((MOD_ROWS, tn), lambda j: (0, j)),
        out_shape=jax.ShapeDtypeStruct((MOD_ROWS, n), F32),
        compiler_params=_cparams(("parallel",)),
        name="modulation",
    )(cvec, w_mod, b_mod.reshape(1, n))


def _mod_spec(piece, row_fn):
    return pl.BlockSpec((None, None, 1, D_MODEL), lambda *idx: (row_fn(*idx), piece, 0, 0))


def _rope(x, cos, sin_up, sin_dn):
    outs = []
    for j in range(x.shape[1] // LANES):
        xj = x[:, j * LANES:(j + 1) * LANES]
        up = pltpu.roll(xj, LANES - 16, axis=1)
        dn = pltpu.roll(xj, 16, axis=1)
        outs.append(xj * cos + up * sin_up + dn * sin_dn)
    return jnp.concatenate(outs, axis=1)


def _proj_kernel(*refs, latent):
    if latent:
        (x_ref, sh_ref, sc_ref, g_ref, w_ref, dft_ref, cos_ref, sup_ref, sdn_ref,
         wr_ref, wi_ref, q_ref, k_ref, v_ref) = refs
    else:
        (x_ref, sh_ref, sc_ref, g_ref, w_ref, dft_ref,
         wr_ref, wi_ref, q_ref, k_ref, v_ref, kf_ref, vf_ref) = refs
    x = x_ref[...]
    h = _rms(x, g_ref[...]) * (1.0 + sc_ref[...]) + sh_ref[...]
    p = _bdot(h, w_ref[...])
    u = p[:, :F_DIM]
    q = p[:, F_DIM:F_DIM + ATTN_DIM]
    k = p[:, F_DIM + ATTN_DIM:F_DIM + ATTN_DIM + KV_DIM]
    v = p[:, F_DIM + ATTN_DIM + KV_DIM:]
    w = _bdot(u, dft_ref[...])
    wr_ref[...] = w[:, :F_DIM]
    wi_ref[...] = w[:, F_DIM:]
    if latent:
        cos, sup, sdn = cos_ref[...], sup_ref[...], sdn_ref[...]
        q = _rope(q, cos, sup, sdn)
        k = _rope(k, cos, sup, sdn)
    else:
        kf_ref[...] = k
        vf_ref[...] = v
    q_ref[...] = (q * (HEAD_DIM ** -0.5)).astype(BF16)
    k_ref[...] = k.astype(BF16)
    v_ref[...] = v.astype(BF16)


def _project(x2d, mod, norm1, w_in, dft_c, rope_tabs, tm, tiles_per_batch, row_fn):
    n = x2d.shape[0]
    latent = rope_tabs is not None
    tok = lambda w: pl.BlockSpec((tm, w), lambda i: (i, 0))
    full = lambda a: pl.BlockSpec(a.shape, lambda i: (0,) * a.ndim)
    in_specs = [tok(D_MODEL), _mod_spec(0, row_fn), _mod_spec(1, row_fn), full(norm1), full(w_in), full(dft_c)]
    args = [x2d, mod, mod, norm1, w_in, dft_c]
    out_specs = [tok(F_DIM), tok(F_DIM), tok(ATTN_DIM), tok(KV_DIM), tok(KV_DIM)]
    out_shape = [jax.ShapeDtypeStruct((n, F_DIM), F32), jax.ShapeDtypeStruct((n, F_DIM), F32),
                 jax.ShapeDtypeStruct((n, ATTN_DIM), BF16), jax.ShapeDtypeStruct((n, KV_DIM), BF16),
                 jax.ShapeDtypeStruct((n, KV_DIM), BF16)]
    if latent:
        pos = pl.BlockSpec((tm, LANES), lambda i: (i % tiles_per_batch, 0))
        in_specs += [pos, pos, pos]
        args += list(rope_tabs)
    else:
        out_specs += [tok(KV_DIM), tok(KV_DIM)]
        out_shape += [jax.ShapeDtypeStruct((n, KV_DIM), F32), jax.ShapeDtypeStruct((n, KV_DIM), F32)]
    return pl.pallas_call(
        functools.partial(_proj_kernel, latent=latent),
        grid=(n // tm,),
        in_specs=in_specs, out_specs=out_specs, out_shape=out_shape,
        compiler_params=_cparams(("parallel",)),
        name="project_latent" if latent else "project_context",
    )(*args)


def _fourier_kernel(wr_ref, wi_ref, a_ref, b_ref, wf_ref, o_ref, yr_ref, yi_ref, z_ref, *, t1, t2, scale):
    for j in range(t2):
        rows = pl.ds(j, t1, stride=t2)
        xin = jnp.concatenate([wr_ref[rows, :], wi_ref[rows, :]], axis=0)
        y = _bdot(a_ref[j], xin)
        yr_ref[rows, :] = y[:t1]
        yi_ref[rows, :] = y[t1:]
    bm = b_ref[...]
    for k1 in range(t1):
        rows = pl.ds(k1 * t2, t2)
        yin = jnp.concatenate([yr_ref[rows, :], yi_ref[rows, :]], axis=0)
        z_ref[pl.ds(k1, t2, stride=t1), :] = _bdot(bm, yin)
    o_ref[...] = (_bdot(z_ref[...], wf_ref[...]) * scale).astype(BF16)


def _dense_fourier_kernel(wr_ref, wi_ref, m_ref, wf_ref, o_ref, *, scale):
    f = _bdot(m_ref[...], jnp.concatenate([wr_ref[...], wi_ref[...]], axis=0))
    for h in range(F_SLABS):
        cols = slice(h * LANES, (h + 1) * LANES)
        o_ref[:, cols] = (_bdot(f[:, cols], wf_ref[h]) * scale).astype(BF16)


def _dft_tables(t1, t2):
    t = t1 * t2
    k1 = np.arange(t1)[None, :, None]
    pos = (t2 * np.arange(t1)[None, None, :] + np.arange(t2)[:, None, None])
    ang = 2.0 * np.pi * ((k1 * pos) % t) / t
    c, s = np.cos(ang), np.sin(ang)
    a = np.concatenate([np.concatenate([c, s], axis=2), np.concatenate([-s, c], axis=2)], axis=1)
    ang2 = 2.0 * np.pi * ((np.arange(t2)[:, None] * np.arange(t2)[None, :]) % t2) / t2
    b = np.concatenate([np.cos(ang2), np.sin(ang2)], axis=1)
    return a.astype(np.float32), b.astype(np.float32)


def _fourier(wr, wi, wf_bd, batch, t, factors):
    wr, wi = wr.reshape(batch, t, F_DIM), wi.reshape(batch, t, F_DIM)
    scale = float((t * F_HD) ** -0.5)
    full = lambda a: pl.BlockSpec(a.shape, lambda bi: (0,) * a.ndim)
    whole = pl.BlockSpec((None, t, F_DIM), lambda bi: (bi, 0, 0))
    if factors is None:
        ang = 2.0 * np.pi * ((np.arange(t)[:, None] * np.arange(t)[None, :]) % t) / t
        m = jnp.asarray(np.concatenate([np.cos(ang), np.sin(ang)], axis=1).astype(np.float32)).astype(BF16)
        return pl.pallas_call(
            functools.partial(_dense_fourier_kernel, scale=scale),
            grid=(batch,),
            in_specs=[whole, whole, full(m), full(wf_bd)],
            out_specs=whole,
            out_shape=jax.ShapeDtypeStruct((batch, t, F_DIM), BF16),
            compiler_params=_cparams(("parallel",)),
            name="fourier_dense_%d" % t,
        )(wr, wi, m, wf_bd)
    t1, t2 = factors
    a_np, b_np = _dft_tables(t1, t2)
    a = jnp.asarray(a_np).astype(BF16)
    b = jnp.asarray(b_np).astype(BF16)
    slab = pl.BlockSpec((None, t, LANES), lambda bi, ci: (bi, 0, ci))
    const = lambda x: pl.BlockSpec(x.shape, lambda bi, ci: (0,) * x.ndim)
    return pl.pallas_call(
        functools.partial(_fourier_kernel, t1=t1, t2=t2, scale=scale),
        grid=(batch, F_SLABS),
        in_specs=[slab, slab, const(a), const(b), pl.BlockSpec((None, LANES, LANES), lambda bi, ci: (ci, 0, 0))],
        out_specs=slab,
        out_shape=jax.ShapeDtypeStruct((batch, t, F_DIM), BF16),
        scratch_shapes=[pltpu.VMEM((t, LANES), F32)] * 3,
        compiler_params=_cparams(("parallel", "parallel")),
        name="fourier_%d" % t,
    )(wr, wi, a, b, wf_bd)


def _attend(q_ref, row0, parts, sink_ref, s_ref, p_ref):
    nt = (((1,), (1,)), ((), ()))
    n_keys = sum(k.shape[0] for k, _, _ in parts)
    outs = []
    for kv in range(N_KV_HEADS):
        heads = range(kv * GQA_GROUP, (kv + 1) * GQA_GROUP)
        hs = slice(kv * HEAD_DIM, (kv + 1) * HEAD_DIM)
        qs = jnp.concatenate([q_ref[row0:row0 + Q_BLOCK, h * HEAD_DIM:(h + 1) * HEAD_DIM] for h in heads], axis=0)
        c0 = 0
        for k, _, ok in parts:
            s = lax.dot_general(qs, k[:, hs], nt, preferred_element_type=F32)
            if ok is not None:
                s = jnp.where(jnp.concatenate([ok] * GQA_GROUP, axis=0), s, -jnp.inf)
            s_ref[:, c0:c0 + s.shape[1]] = s
            c0 += s.shape[1]
        invs = []
        for r0 in range(0, GQA_GROUP * Q_BLOCK, SOFTMAX_ROWS):
            s = s_ref[r0:r0 + SOFTMAX_ROWS, :]
            sk = sink_ref[kv * GQA_GROUP + r0 // Q_BLOCK]
            slabs = [s[:, c:c + LANES] for c in range(0, n_keys, LANES)]
            m = jnp.maximum(jnp.max(functools.reduce(jnp.maximum, slabs), axis=1, keepdims=True), sk)
            p = jnp.exp(s - m)
            pslabs = [p[:, c:c + LANES] for c in range(0, n_keys, LANES)]
            den = jnp.sum(functools.reduce(jnp.add, pslabs), axis=1, keepdims=True) + jnp.exp(sk - m)
            p_ref[r0:r0 + SOFTMAX_ROWS, :] = p.astype(BF16)
            invs.append(1.0 / den)
        o = None
        c0 = 0
        for k, v, _ in parts:
            nk = k.shape[0]
            d = jnp.dot(p_ref[:, c0:c0 + nk], v[:, hs], preferred_element_type=F32)
            o = d if o is None else o + d
            c0 += nk
        o = o * jnp.concatenate(invs, axis=0)
        outs += [o[g * Q_BLOCK:(g + 1) * Q_BLOCK] for g in range(GQA_GROUP)]
    return jnp.concatenate(outs, axis=1).astype(BF16)


def _attn_kernel(*refs, windowed, n_steps):
    if not windowed:
        q_ref, kx_ref, vx_ref, sink_ref, o_ref, s0, p0 = refs
        o_ref[...] = _attend(q_ref, 0, [(kx_ref, vx_ref, None)], sink_ref, s0, p0)
        return
    q_ref, kp_ref, kc_ref, kn_ref, vp_ref, vc_ref, vn_ref, kx_ref, vx_ref, sink_ref, o_ref = refs[:11]
    scratch = refs[11:]
    j = pl.program_id(1)
    a = lax.broadcasted_iota(jnp.int32, (Q_BLOCK, Q_BLOCK), 0)
    b = lax.broadcasted_iota(jnp.int32, (Q_BLOCK, Q_BLOCK), 1)
    later, earlier = b >= a, b <= a
    own = [(kc_ref.at[i * Q_BLOCK:(i + 1) * Q_BLOCK], vc_ref.at[i * Q_BLOCK:(i + 1) * Q_BLOCK])
           for i in range(ATTN_BLOCKS)]
    for i in range(ATTN_BLOCKS):
        prev = own[i - 1] + (later,) if i > 0 else (kp_ref, vp_ref, later & (j > 0))
        nxt = own[i + 1] + (earlier,) if i < ATTN_BLOCKS - 1 else (kn_ref, vn_ref, earlier & (j < n_steps - 1))
        parts = [prev, own[i] + (None,), nxt, (kx_ref, vx_ref, None)]
        o_ref[i * Q_BLOCK:(i + 1) * Q_BLOCK, :] = _attend(q_ref, i * Q_BLOCK, parts, sink_ref,
                                                           scratch[2 * i], scratch[2 * i + 1])


def _attention(q, k, v, kx, vx, sink_col, batch, t, windowed):
    nb = t // Q_BLOCK
    per_step = ATTN_BLOCKS if windowed else 1
    n_ctx = kx.shape[1]
    n_keys = n_ctx + (3 * Q_BLOCK if windowed else 0)
    qspec = pl.BlockSpec((None, per_step * Q_BLOCK, ATTN_DIM), lambda b, j: (b, j, 0))
    xspec = pl.BlockSpec((None, n_ctx, KV_DIM), lambda b, j: (b, 0, 0))
    sspec = pl.BlockSpec(memory_space=pltpu.SMEM)
    if windowed:
        prev = pl.BlockSpec((None, Q_BLOCK, KV_DIM), lambda b, j: (b, jnp.maximum(per_step * j - 1, 0), 0))
        own = pl.BlockSpec((None, per_step * Q_BLOCK, KV_DIM), lambda b, j: (b, j, 0))
        nxt = pl.BlockSpec((None, Q_BLOCK, KV_DIM), lambda b, j: (b, jnp.minimum(per_step * (j + 1), nb - 1), 0))
        in_specs = [qspec, prev, own, nxt, prev, own, nxt, xspec, xspec, sspec]
        args = [q, k, k, k, v, v, v, kx, vx, sink_col]
    else:
        in_specs = [qspec, xspec, xspec, sspec]
        args = [q, kx, vx, sink_col]
    scores = pltpu.VMEM((GQA_GROUP * Q_BLOCK, n_keys), F32)
    probs = pltpu.VMEM((GQA_GROUP * Q_BLOCK, n_keys), BF16)
    return pl.pallas_call(
        functools.partial(_attn_kernel, windowed=windowed, n_steps=nb // per_step),
        grid=(batch, nb // per_step),
        in_specs=in_specs, out_specs=qspec,
        out_shape=jax.ShapeDtypeStruct((batch, t, ATTN_DIM), BF16),
        scratch_shapes=[scores, probs] * per_step,
        compiler_params=_cparams(("parallel", "parallel")),
        name="attention_latent" if windowed else "attention_context",
    )(*args)


def _outproj_kernel(xc_ref, mfc_ref, mac_ref, xl_ref, mfl_ref, mal_ref,
                    wof_ref, woa_ref, g1_ref, sh_ref, sc_ref, n2_ref, wr_ref,
                    x1_ref, h_ref, lg_ref, *, ctx_tiles):
    def body(x_ref, mf_ref, ma_ref):
        o = jnp.dot(mf_ref[...], wof_ref[...], preferred_element_type=F32)
        o = o + jnp.dot(ma_ref[...], woa_ref[...], preferred_element_type=F32)
        x1 = x_ref[...] + g1_ref[...] * o
        x1_ref[...] = x1
        h = (_rms(x1, n2_ref[...]) * (1.0 + sc_ref[...]) + sh_ref[...]).astype(BF16)
        h_ref[...] = h
        lg_ref[...] = lax.dot_general(wr_ref[...], h, (((1,), (1,)), ((), ())), preferred_element_type=F32)

    is_ctx = pl.program_id(0) < ctx_tiles
    pl.when(is_ctx)(lambda: body(xc_ref, mfc_ref, mac_ref))
    pl.when(jnp.logical_not(is_ctx))(lambda: body(xl_ref, mfl_ref, mal_ref))


def _outproj(ctx, lat, wo_f, wo_a, mod, norm2, w_router_t, tm, row_fn):
    n_c, n_l = ctx[0].shape[0], lat[0].shape[0]
    ctx_tiles = n_c // tm
    n = n_c + n_l
    ctok = lambda w: pl.BlockSpec((tm, w), lambda i: (jnp.minimum(i, ctx_tiles - 1), 0))
    ltok = lambda w: pl.BlockSpec((tm, w), lambda i: (jnp.maximum(i - ctx_tiles, 0), 0))
    tok = lambda w: pl.BlockSpec((tm, w), lambda i: (i, 0))
    full = lambda a: pl.BlockSpec(a.shape, lambda i: (0,) * a.ndim)
    widths = (D_MODEL, F_DIM, ATTN_DIM)
    return pl.pallas_call(
        functools.partial(_outproj_kernel, ctx_tiles=ctx_tiles),
        grid=(n // tm,),
        in_specs=[ctok(w) for w in widths] + [ltok(w) for w in widths] + [
            full(wo_f), full(wo_a), _mod_spec(2, row_fn), _mod_spec(3, row_fn), _mod_spec(4, row_fn),
            full(norm2), full(w_router_t)],
        out_specs=[tok(D_MODEL), tok(D_MODEL), pl.BlockSpec((N_EXPERTS, tm), lambda i: (0, i))],
        out_shape=[jax.ShapeDtypeStruct((n, D_MODEL), F32), jax.ShapeDtypeStruct((n, D_MODEL), BF16),
                   jax.ShapeDtypeStruct((N_EXPERTS, n), F32)],
        compiler_params=_cparams(("parallel",)),
        name="outproj",
    )(*ctx, *lat, wo_f, wo_a, mod, mod, mod, norm2, w_router_t)


def _top_rows(vals, k):
    n = vals.shape[0]
    idx = lax.broadcasted_iota(jnp.int32, vals.shape, 0)
    picked = jnp.zeros(vals.shape, F32)
    for _ in range(k):
        best = jnp.max(vals, axis=0, keepdims=True)
        first = jnp.min(jnp.where(vals == best, idx, n), axis=0, keepdims=True)
        hit = idx == first
        picked = jnp.where(hit, 1.0, picked)
        vals = jnp.where(hit, -jnp.inf, vals)
    return picked


def _route_chunk(logits, bias, lo, up):
    s = jax.nn.sigmoid(logits)
    sc = s + bias
    tn = s.shape[1]
    per = N_EXPERTS // N_EXPERT_GROUPS
    g3 = sc.reshape(N_EXPERT_GROUPS, per, tn)
    member = lax.broadcasted_iota(jnp.int32, g3.shape, 1)
    m1 = jnp.max(g3, axis=1, keepdims=True)
    first = jnp.min(jnp.where(g3 == m1, member, per), axis=1, keepdims=True)
    m2 = jnp.max(jnp.where(member == first, -jnp.inf, g3), axis=1, keepdims=True)
    gscore = (m1 + m2).reshape(N_EXPERT_GROUPS, tn)
    gsel = _top_rows(gscore, TOPK_GROUPS)
    emask = jnp.broadcast_to(gsel.reshape(N_EXPERT_GROUPS, 1, tn), g3.shape).reshape(N_EXPERTS, tn)
    masked = jnp.where(emask > 0.5, sc, -jnp.inf)
    self = _top_rows(masked, TOP_K)
    sel = self > 0.5
    w = jnp.where(sel, s, 0.0)
    gate = w / jnp.sum(w, axis=0, keepdims=True) * ROUTED_SCALE

    selb = self.astype(BF16)
    cnt = jnp.sum(self, axis=1, keepdims=True)
    pad = jnp.maximum(jnp.floor((cnt + (RUN_ALIGN - 1)) * (1.0 / RUN_ALIGN)), 1.0) * RUN_ALIGN
    soff = jnp.dot(lo, jnp.broadcast_to(pad, (N_EXPERTS, LANES)).astype(BF16), preferred_element_type=F32)[:, :1]
    rank = jnp.dot(selb, up, preferred_element_type=F32)
    kidx = jnp.dot(lo, selb, preferred_element_type=F32)
    pos_e = jnp.where(sel, soff + rank, 0.0)
    rows_p, rows_g = [], []
    for k in range(TOP_K):
        m = kidx == k
        rows_p.append(jnp.sum(jnp.where(m, pos_e, 0.0), axis=0, keepdims=True))
        rows_g.append(jnp.sum(jnp.where(m, gate, 0.0), axis=0, keepdims=True))
    return jnp.concatenate(rows_p, axis=0).astype(jnp.int32), jnp.concatenate(rows_g, axis=0), pad.astype(jnp.int32)


def _route_kernel(lg_ref, bias_ref, lo_ref, up_ref, pos_ref, gs_ref, cnt_ref, *, tc):
    for j in range(ROUTE_CHUNKS):
        cols = slice(j * tc, (j + 1) * tc)
        pos, gates, pad = _route_chunk(lg_ref[:, cols], bias_ref[...], lo_ref[...], up_ref[...])
        pos_ref[:, cols] = pos
        gs_ref[:, cols] = gates
        cnt_ref[j] = pad


def _route(logits_t, bias_col, tc):
    n = logits_t.shape[1]
    lo = jnp.asarray(np.tril(np.ones((N_EXPERTS, N_EXPERTS), np.float32), -1)).astype(BF16)
    up = jnp.asarray(np.triu(np.ones((tc, tc), np.float32), 1)).astype(BF16)
    tn = tc * ROUTE_CHUNKS
    slot = pl.BlockSpec((TOP_K, tn), lambda i: (0, i))
    return pl.pallas_call(
        functools.partial(_route_kernel, tc=tc),
        grid=(n // tn,),
        in_specs=[pl.BlockSpec((N_EXPERTS, tn), lambda i: (0, i)),
                  pl.BlockSpec((N_EXPERTS, 1), lambda i: (0, 0)),
                  pl.BlockSpec(lo.shape, lambda i: (0, 0)),
                  pl.BlockSpec(up.shape, lambda i: (0, 0))],
        out_specs=[slot, slot, pl.BlockSpec((ROUTE_CHUNKS, N_EXPERTS, 1), lambda i: (i, 0, 0))],
        out_shape=[jax.ShapeDtypeStruct((TOP_K, n), jnp.int32), jax.ShapeDtypeStruct((TOP_K, n), F32),
                   jax.ShapeDtypeStruct((n // tc, N_EXPERTS, 1), jnp.int32)],
        compiler_params=_cparams(("parallel",)),
        name="route",
    )(logits_t, bias_col, lo, up)


def _aligned(x):
    return pl.multiple_of(x, RUN_ALIGN)


def _block_rows(tc):
    return lax.broadcasted_iota(jnp.int32, (PERM_BLOCK, tc), 0).astype(F32).astype(BF16)


def _block_relative(pos, r0):
    return (pos - r0).astype(F32).astype(BF16)


def _gather_kernel(src_ref, cnt_ref, dst_ref, tot_ref,
                   pos_ref, x_ref, xs_ref, z_ref, sem, *, tc, n_chunks):
    c = pl.program_id(0)
    slot = c % GATHER_SLOTS

    def wait_chunk(ci, s):
        n = _aligned(tot_ref[ci])
        pltpu.make_async_copy(z_ref.at[s, pl.ds(0, n)], xs_ref.at[pl.ds(0, n)], sem.at[s]).wait()

    def start_runs(ci, s, first, count):
        for j in range(count):
            i = ci * N_EXPERTS + first + j
            n = _aligned(cnt_ref[i])
            pltpu.make_async_copy(z_ref.at[s, pl.ds(_aligned(src_ref[i]), n)],
                                  xs_ref.at[pl.ds(_aligned(dst_ref[i]), n)], sem.at[s]).start(priority=j % 2)

    @pl.when(c >= GATHER_SLOTS)
    def _():
        wait_chunk(c - GATHER_SLOTS, slot)

    rows = _block_rows(tc)

    def permute_block(b):
        r0 = _block_start(b)
        rel = _block_relative(pos_ref[...], r0)
        onehot = jnp.zeros((PERM_BLOCK, tc), BF16)
        for k in range(TOP_K):
            onehot = jnp.where(rel[k:k + 1, :] == rows, jnp.ones((), BF16), onehot)
        z_ref[slot, pl.ds(r0, PERM_BLOCK), :] = jnp.dot(
            onehot, x_ref[...], preferred_element_type=F32).astype(BF16)

    for b in range(_sure_blocks(tc)):
        permute_block(b)

    def tail_block(b, carry):
        permute_block(b)
        return carry
    lax.fori_loop(_sure_blocks(tc), _perm_blocks(tot_ref[c]), tail_block, 0)

    def e_body(e8, carry):
        start_runs(c, slot, e8 * DMA_UNROLL, DMA_UNROLL)
        return carry
    lax.fori_loop(0, N_EXPERTS // DMA_UNROLL, e_body, 0)

    @pl.when(c == n_chunks - 1)
    def _():
        for back in range(min(GATHER_SLOTS, n_chunks)):
            wait_chunk(c - back, (c - back) % GATHER_SLOTS)


def _perm_blocks(rows):
    return (rows + PERM_BLOCK - 1) // PERM_BLOCK


def _sure_blocks(tc):
    return min(TOP_K * tc // PERM_BLOCK + STRAIGHT_EXTRA_BLOCKS, _local_cap(tc) // PERM_BLOCK)


def _block_start(b):
    return b * PERM_BLOCK if isinstance(b, int) else pl.multiple_of(b * PERM_BLOCK, PERM_BLOCK)


def _chunk_rows_max(tc):
    return TOP_K * tc + N_EXPERTS * RUN_ALIGN


def _local_cap(tc):
    return -(-_chunk_rows_max(tc) // PERM_BLOCK) * PERM_BLOCK


def _gather(tabs, pos, h2, r_max, tc):
    n = h2.shape[0]
    n_chunks = n // tc
    return pl.pallas_call(
        functools.partial(_gather_kernel, tc=tc, n_chunks=n_chunks),
        grid_spec=pltpu.PrefetchScalarGridSpec(
            num_scalar_prefetch=4, grid=(n_chunks,),
            in_specs=[pl.BlockSpec((TOP_K, tc), lambda c, *_: (0, c)),
                      pl.BlockSpec((tc, D_MODEL), lambda c, *_: (c, 0))],
            out_specs=pl.BlockSpec(memory_space=pl.ANY),
            scratch_shapes=[pltpu.VMEM((GATHER_SLOTS, _local_cap(tc), D_MODEL), BF16),
                            pltpu.SemaphoreType.DMA((GATHER_SLOTS,))]),
        out_shape=jax.ShapeDtypeStruct((r_max, D_MODEL), BF16),
        compiler_params=_cparams(("arbitrary",)),
        name="moe_gather",
    )(tabs["src"], tabs["cnt"], tabs["dst"], tabs["tot"], pos, h2)


def _ffn_kernel(first_ref, ntile_ref, tstart_ref, trows_ref, total_ref, xs_ref, wg_ref, wu_ref, wd_ref, ys_ref,
                xbuf, ybuf, sem_in, sem_out):
    e = pl.program_id(0)
    total = total_ref[0]

    def in_copy(g):
        s = g % FFN_IN_SLOTS
        n = _aligned(trows_ref[g])
        return pltpu.make_async_copy(xs_ref.at[pl.ds(_aligned(tstart_ref[g]), n)],
                                     xbuf.at[s, pl.ds(0, n)], sem_in.at[s])

    def out_copy(g):
        s = g % 2
        n = _aligned(trows_ref[g])
        return pltpu.make_async_copy(ybuf.at[s, pl.ds(0, n)],
                                     ys_ref.at[pl.ds(_aligned(tstart_ref[g]), n)], sem_out.at[s])

    @pl.when(e == 0)
    def _():
        xbuf[...] = jnp.zeros(xbuf.shape, xbuf.dtype)
        for g in range(FFN_LOOKAHEAD):
            @pl.when(g < total)
            def _():
                in_copy(g).start()


    def body(g, carry):
        @pl.when(g + FFN_LOOKAHEAD < total)
        def _():
            in_copy(g + FFN_LOOKAHEAD).start()
        in_copy(g).wait()

        @pl.when(g >= 2)
        def _():
            out_copy(g - 2).wait()
        x = xbuf[g % FFN_IN_SLOTS]
        a = jnp.dot(x, wg_ref[...].astype(BF16), preferred_element_type=F32)
        u = jnp.dot(x, wu_ref[...].astype(BF16), preferred_element_type=F32)
        h = (a * jax.nn.sigmoid(a) * u).astype(BF16)
        ybuf[g % 2] = jnp.dot(h, wd_ref[...].astype(BF16), preferred_element_type=F32).astype(BF16)
        out_copy(g).start()
        return carry
    lax.fori_loop(first_ref[e], first_ref[e] + ntile_ref[e], body, 0)

    @pl.when(e == pl.num_programs(0) - 1)
    def _():
        @pl.when(total >= 2)
        def _():
            out_copy(total - 2).wait()
        out_copy(total - 1).wait()


def _ffn(tabs, xs, wg, wu, wd):
    ew = lambda a: pl.BlockSpec((None,) + a.shape[1:], lambda e, *_: (e, 0, 0))
    return pl.pallas_call(
        _ffn_kernel,
        grid_spec=pltpu.PrefetchScalarGridSpec(
            num_scalar_prefetch=5, grid=(N_EXPERTS,),
            in_specs=[pl.BlockSpec(memory_space=pl.ANY), ew(wg), ew(wu), ew(wd)],
            out_specs=pl.BlockSpec(memory_space=pl.ANY),
            scratch_shapes=[pltpu.VMEM((FFN_IN_SLOTS, FFN_BLOCK, D_MODEL), BF16),
                            pltpu.VMEM((2, FFN_BLOCK, D_MODEL), BF16),
                            pltpu.SemaphoreType.DMA((FFN_IN_SLOTS,)), pltpu.SemaphoreType.DMA((2,))]),
        out_shape=jax.ShapeDtypeStruct(xs.shape, BF16),
        compiler_params=_cparams(("arbitrary",)),
        name="moe_ffn",
    )(tabs["first_tile"], tabs["n_tiles"], tabs["tile_start"], tabs["tile_rows"], tabs["total_tiles"],
      xs, wg, wu, wd)


def _combine_kernel(src_ref, cnt_ref, dst_ref, tot_ref,
                    pos_ref, gs_ref, x1_ref, h_ref, sg_ref, su_ref, sd_ref, g2_ref, nf_ref, ys_ref,
                    yc_ref, yl_ref, ybuf, acc_ref, sem, *, tc, n_chunks, ctx_chunks):
    c = pl.program_id(0)
    slot = c % 2

    def issue(ci, s):
        def e_body(e8, carry):
            for j in range(DMA_UNROLL):
                i = ci * N_EXPERTS + e8 * DMA_UNROLL + j
                n = _aligned(cnt_ref[i])
                pltpu.make_async_copy(ys_ref.at[pl.ds(_aligned(dst_ref[i]), n)],
                                      ybuf.at[s, pl.ds(_aligned(src_ref[i]), n)], sem.at[s]).start(priority=j % 2)
            return carry
        lax.fori_loop(0, N_EXPERTS // DMA_UNROLL, e_body, 0)

    @pl.when(c == 0)
    def _():
        ybuf[...] = jnp.zeros(ybuf.shape, ybuf.dtype)
        issue(0, 0)

    @pl.when(c + 1 < n_chunks)
    def _():
        issue(c + 1, 1 - slot)

    h = h_ref[...]
    a = jnp.dot(h, sg_ref[...], preferred_element_type=F32)
    u = jnp.dot(h, su_ref[...], preferred_element_type=F32)
    acc_ref[...] = jnp.dot((a * jax.nn.sigmoid(a) * u).astype(BF16), sd_ref[...], preferred_element_type=F32)

    n = _aligned(tot_ref[c])
    pltpu.make_async_copy(ys_ref.at[pl.ds(0, n)], ybuf.at[slot, pl.ds(0, n)], sem.at[slot]).wait()

    tn = (((0,), (0,)), ((), ()))
    rows = _block_rows(tc)
    gates = gs_ref[...].astype(BF16)

    def gate_block(b):
        rel = _block_relative(pos_ref[...], _block_start(b))
        wt = jnp.zeros((PERM_BLOCK, tc), BF16)
        for k in range(TOP_K):
            wt = jnp.where(rel[k:k + 1, :] == rows, gates[k:k + 1, :], wt)
        return wt

    sure_rows = _sure_blocks(tc) * PERM_BLOCK
    wt_sure = jnp.concatenate([gate_block(b) for b in range(_sure_blocks(tc))], axis=0)
    acc_ref[...] += lax.dot_general(wt_sure, ybuf[slot, pl.ds(0, sure_rows), :], tn, preferred_element_type=F32)

    def tail_block(b, carry):
        acc_ref[...] += lax.dot_general(gate_block(b), ybuf[slot, pl.ds(_block_start(b), PERM_BLOCK), :], tn,
                                        preferred_element_type=F32)
        return carry
    lax.fori_loop(_sure_blocks(tc), _perm_blocks(tot_ref[c]), tail_block, 0)

    x2 = x1_ref[...] + g2_ref[...] * acc_ref[...]
    y = _rms(x2, nf_ref[...])

    @pl.when(c < ctx_chunks)
    def _():
        yc_ref[...] = y

    @pl.when(c >= ctx_chunks)
    def _():
        yl_ref[...] = y


def _combine(tabs, pos, gslot, x1, h2, sg, su, sd, mod, norm_f, ys, tc, row_fn, n_ctx):
    n = x1.shape[0]
    n_chunks = n // tc
    ctx_chunks = n_ctx // tc
    tok = lambda w: pl.BlockSpec((tc, w), lambda c, *_: (c, 0))
    slot = pl.BlockSpec((TOP_K, tc), lambda c, *_: (0, c))
    full = lambda a: pl.BlockSpec(a.shape, lambda c, *_: (0,) * a.ndim)
    return pl.pallas_call(
        functools.partial(_combine_kernel, tc=tc, n_chunks=n_chunks, ctx_chunks=ctx_chunks),
        grid_spec=pltpu.PrefetchScalarGridSpec(
            num_scalar_prefetch=4, grid=(n_chunks,),
            in_specs=[slot, slot, tok(D_MODEL), tok(D_MODEL), full(sg), full(su), full(sd),
                      _mod_spec(5, lambda c, *_: row_fn(c)), full(norm_f), pl.BlockSpec(memory_space=pl.ANY)],
            out_specs=[pl.BlockSpec((tc, D_MODEL), lambda c, *_: (jnp.minimum(c, ctx_chunks - 1), 0)),
                       pl.BlockSpec((tc, D_MODEL), lambda c, *_: (jnp.maximum(c - ctx_chunks, 0), 0))],
            scratch_shapes=[pltpu.VMEM((2, _local_cap(tc), D_MODEL), BF16),
                            pltpu.VMEM((tc, D_MODEL), F32),
                            pltpu.SemaphoreType.DMA((2,))]),
        out_shape=[jax.ShapeDtypeStruct((n_ctx, D_MODEL), F32), jax.ShapeDtypeStruct((n - n_ctx, D_MODEL), F32)],
        compiler_params=_cparams(("arbitrary",)),
        name="moe_combine",
    )(tabs["src"], tabs["cnt"], tabs["dst"], tabs["tot"], pos, gslot, x1, h2, sg, su, sd, mod, norm_f, ys)


def _dispatch_tables(pad_cnt, r_max):
    i32 = jnp.int32
    src = jnp.cumsum(pad_cnt, axis=1) - pad_cnt
    tot = jnp.sum(pad_cnt, axis=1)
    e_rows = jnp.sum(pad_cnt, axis=0)
    e_start = jnp.cumsum(e_rows) - e_rows
    dst = e_start[None, :] + jnp.cumsum(pad_cnt, axis=0) - pad_cnt
    n_tiles = (e_rows + FFN_BLOCK - 1) // FFN_BLOCK
    tile_end = jnp.cumsum(n_tiles)
    first_tile = tile_end - n_tiles
    g = jnp.arange(r_max // FFN_BLOCK + N_EXPERTS)
    before = tile_end[None, :] <= g[:, None]
    owner = jnp.concatenate([jnp.ones_like(before[:, :1]), before[:, :-1]], axis=1) & ~before
    offset = (g - jnp.sum(jnp.where(before, n_tiles, 0), axis=1)) * FFN_BLOCK
    tile_start = jnp.clip(jnp.sum(jnp.where(before, e_rows, 0), axis=1) + offset, 0, r_max - FFN_BLOCK)
    tile_rows = jnp.clip(jnp.sum(jnp.where(owner, e_rows, 0), axis=1) - offset, RUN_ALIGN, FFN_BLOCK)
    return {"src": src.reshape(-1).astype(i32), "cnt": pad_cnt.reshape(-1).astype(i32),
            "dst": dst.reshape(-1).astype(i32), "tot": tot.astype(i32),
            "first_tile": first_tile.astype(i32), "n_tiles": n_tiles.astype(i32),
            "tile_start": tile_start.astype(i32), "tile_rows": tile_rows.astype(i32),
            "total_tiles": tile_end[-1:].astype(i32)}


def _moe(x1, h2, logits_t, bias_col, wg, wu, wd, sg, su, sd, mod, norm_f, row_fn, n_ctx):
    n = x1.shape[0]
    tc = MOE_CHUNK
    pos, gslot, pad_cnt = _route(logits_t, bias_col, tc)
    r_max = (n // tc) * _chunk_rows_max(tc)
    tabs = _dispatch_tables(pad_cnt[:, :, 0], r_max)
    xs = _gather(tabs, pos, h2, r_max, tc)
    ys = _ffn(tabs, xs, wg, wu, wd)
    return _combine(tabs, pos, gslot, x1, h2, sg, su, sd, mod, norm_f, ys, tc, row_fn, n_ctx)


def _channel_dft():
    ang = 2.0 * np.pi * ((np.arange(F_HD)[:, None] * np.arange(F_HD)[None, :]) % F_HD) / F_HD
    eye = np.eye(F_GROUPS)
    return np.concatenate([np.kron(eye, np.cos(ang)), np.kron(eye, -np.sin(ang))], axis=1).astype(np.float32)


def _rope_tables(t):
    pos = np.arange(t)
    row, col = pos // GRID_W, pos % GRID_W
    n_freq = HEAD_DIM // 4
    inv = ROPE_THETA ** (-np.arange(n_freq, dtype=np.float64) / n_freq)
    lane = np.arange(LANES)
    hd = lane % HEAD_DIM
    within = hd % (HEAD_DIM // 2)
    freq = within % n_freq
    first = within < n_freq
    p = np.where((hd < HEAD_DIM // 2)[None, :], row[:, None], col[:, None]).astype(np.float64)
    ang = p * inv[freq][None, :]
    cos, sin = np.cos(ang), np.sin(ang)
    sin_up = np.where(first[None, :], -sin, 0.0)
    sin_dn = np.where(first[None, :], 0.0, sin)
    return [jnp.asarray(a.astype(np.float32)) for a in (cos, sin_up, sin_dn)]


def _block_diag_pairs(wf):
    per = LANES // F_HD
    z = jnp.zeros((F_HD, F_HD), wf.dtype)
    blocks = []
    for c in range(F_GROUPS // per):
        rows = [jnp.concatenate([wf[c * per + r] if r == cc else z for cc in range(per)], axis=1) for r in range(per)]
        blocks.append(jnp.concatenate(rows, axis=0))
    return jnp.stack(blocks)


def kernel(x_prompt, x_sample, cache_k, cache_v, c, c_ctx, w_mod, b_mod, norm1, w_in, w_fourier, sink,
           w_out, norm2, w_router, router_bias, w_gate, w_up, w_down, ws_gate, ws_up, ws_down, norm_f):
    nb_ctx, t_ctx, _ = x_prompt.shape
    nb_lat, t_lat, _ = x_sample.shape
    l = 0
    cvec = jnp.concatenate([c_ctx[None, :], c, jnp.zeros((MOD_ROWS - 1 - nb_lat, D_MODEL), F32)], axis=0)
    mod = _modulation(cvec, w_mod[l], b_mod[l]).reshape(MOD_ROWS, 6, 1, D_MODEL)

    w_in_b = w_in[l].astype(BF16)
    dft_c = jnp.asarray(_channel_dft()).astype(BF16)
    wf_bd = _block_diag_pairs(w_fourier[l]).astype(BF16)
    wo_f = w_out[l][:F_DIM].astype(BF16)
    wo_a = w_out[l][F_DIM:].astype(BF16)
    w_router_t = w_router[l].T.astype(BF16)
    bias_col = router_bias[l].reshape(N_EXPERTS, 1)
    n1 = norm1[l].reshape(1, D_MODEL)
    n2 = norm2[l].reshape(1, D_MODEL)
    nf = norm_f.reshape(1, D_MODEL)
    sink_col = sink[l]
    wg, wu, wd = w_gate[l], w_up[l], w_down[l]
    sg, su, sd = ws_gate[l].astype(BF16), ws_up[l].astype(BF16), ws_down[l].astype(BF16)

    def mixers(x, batch, t, latent, kx, vx, factors, tm):
        n = batch * t
        x2d = x.reshape(n, D_MODEL)
        tiles = t // tm
        row_fn = (lambda i: 1 + i // tiles) if latent else (lambda i: 0)
        rope_tabs = _rope_tables(t) if latent else None
        outs = _project(x2d, mod, n1, w_in_b, dft_c, rope_tabs, tm, tiles, row_fn)
        wr, wi, q, k, v = outs[:5]
        mixf = _fourier(wr, wi, wf_bd, batch, t, factors)
        q3, k3, v3 = (a.reshape(batch, t, -1) for a in (q, k, v))
        if latent:
            mixa = _attention(q3, k3, v3, kx, vx, sink_col, batch, t, True)
        else:
            mixa = _attention(q3, None, None, k3, v3, sink_col, batch, t, False)
        return (x2d, mixf.reshape(n, F_DIM), mixa.reshape(n, ATTN_DIM)), outs[5:]

    ctx, (kf, vf) = mixers(x_prompt, nb_ctx, t_ctx, False, None, None, None, CTX_PROJ_TILE)
    kx = cache_k[:, l].reshape(nb_lat, -1, KV_DIM).astype(BF16)
    vx = cache_v[:, l].reshape(nb_lat, -1, KV_DIM).astype(BF16)
    lat, _ = mixers(x_sample, nb_lat, t_lat, True, kx, vx, LAT_DFT_FACTORS, LAT_PROJ_TILE)

    n_ctx = nb_ctx * t_ctx

    def tile_row(tile):
        ctx_tiles, lat_tiles = n_ctx // tile, t_lat // tile
        return lambda i: jnp.where(i < ctx_tiles, 0, 1 + (i - ctx_tiles) // lat_tiles)

    x1, h2, logits_t = _outproj(ctx, lat, wo_f, wo_a, mod, n2, w_router_t, OUTPROJ_TILE, tile_row(OUTPROJ_TILE))
    y_prompt, y_sample = _moe(x1, h2, logits_t, bias_col, wg, wu, wd, sg, su, sd, mod, nf, tile_row(MOE_CHUNK),
                              n_ctx)
    new_k = kf.reshape(nb_ctx, 1, t_ctx, N_KV_HEADS, HEAD_DIM)
    new_v = vf.reshape(nb_ctx, 1, t_ctx, N_KV_HEADS, HEAD_DIM)
    return (y_prompt.reshape(x_prompt.shape), y_sample.reshape(x_sample.shape), new_k, new_v)
```
